```python
import jax, jax.numpy as jnp
from jax import lax
import numpy as np

D_MODEL = 1024
BATCH = 8
SEQ = 8192
DEPTH = 4

N_MIXERS = 2
N_A_LAYERS = (DEPTH + 1) // 2
N_B_LAYERS = DEPTH // 2
HGRN_EXPAND = 128
HGRN_HEADS = D_MODEL // HGRN_EXPAND
HGRN_HEAD_K = HGRN_EXPAND
HGRN_HEAD_V = D_MODEL // HGRN_HEADS
HGRN_CHUNK = 32
CONV_WIDTH = 31
D_FF = 4 * D_MODEL
DEEPNORM_ALPHA = (2.0 * DEPTH) ** 0.25
DEEPNORM_BETA = (8.0 * DEPTH) ** -0.25
LN_EPS = 1e-5
RMS_EPS = 1e-6
GATE_EPS = 1e-6

kernel_name = "hgrn2_conformer_interleaved_deepnorm"


def layer_norm(x, g, b):
    x32 = x.astype(jnp.float32)
    mu = jnp.mean(x32, axis=-1, keepdims=True)
    var = jnp.mean(jnp.square(x32 - mu), axis=-1, keepdims=True)
    y = (x32 - mu) * lax.rsqrt(var + LN_EPS) * g.astype(jnp.float32) + b.astype(jnp.float32)
    return y.astype(x.dtype)


def chunkwise_gated_recurrence(q, k, v, b):
    C = q.shape[-2]
    causal = jnp.tril(jnp.ones((C, C), dtype=bool))[:, :, None]

    def step(S, inp):
        qc, kc, vc, bc = inp
        diff = bc[..., :, None, :] - bc[..., None, :, :]
        decay = jnp.where(causal, jnp.exp(jnp.where(causal, diff, 0.0)), 0.0)
        scores = jnp.einsum('bhtd,bhsd,bhtsd->bhts', qc, kc, decay)
        o = (jnp.einsum('bhts,bhsv->bhtv', scores, vc)
             + jnp.einsum('bhtd,bhdv->bhtv', qc * jnp.exp(bc), S))
        b_last = bc[..., -1:, :]
        S = (jnp.exp(b_last)[..., 0, :, None] * S
             + jnp.einsum('bhsd,bhsv->bhdv', kc * jnp.exp(b_last - bc), vc))
        return S, o

    S0 = jnp.zeros(q.shape[1:3] + (q.shape[-1], v.shape[-1]), jnp.float32)
    _, o = lax.scan(step, S0, (q, k, v, b))
    return o


def hgrn2_mixer(h, w_in, lb, norm_g, w_out):
    B_, S_, D = h.shape
    H, dk, dv, C = HGRN_HEADS, HGRN_HEAD_K, HGRN_HEAD_V, HGRN_CHUNK
    nC = S_ // C
    proj = h @ w_in
    q, fz, v, g = jnp.split(proj, 4, axis=-1)
    q = jax.nn.silu(q.astype(jnp.float32))
    lb32 = lb.astype(jnp.float32)
    f = lb32 + (1.0 - lb32) * jax.nn.sigmoid(fz.astype(jnp.float32))
    log_f = jnp.log(jnp.maximum(f, GATE_EPS))
    k = 1.0 - f

    def to_chunks(t, d):
        return t.astype(jnp.float32).reshape(B_, nC, C, H, d).transpose(1, 0, 3, 2, 4)

    qc, kc, vc = to_chunks(q, dk), to_chunks(k, dk), to_chunks(v, dv)
    bc = jnp.cumsum(to_chunks(log_f, dk), axis=-2)
    o = chunkwise_gated_recurrence(qc, kc, vc, bc)
    o = o.transpose(1, 0, 3, 2, 4).reshape(B_, S_, H, dv)
    o = o * lax.rsqrt(jnp.mean(jnp.square(o), axis=-1, keepdims=True) + RMS_EPS)
    o = o * norm_g.astype(jnp.float32).reshape(H, dv)
    o = o.reshape(B_, S_, D) * jax.nn.silu(g.astype(jnp.float32))
    return o.astype(h.dtype) @ w_out


def conformer_conv_mixer(h, w_pw1, b_pw1, w_dw, b_dw, ln_g, ln_b, w_pw2, b_pw2):
    u = h @ w_pw1 + b_pw1
    a, gate = jnp.split(u, 2, axis=-1)
    u = a * jax.nn.sigmoid(gate)
    u = lax.conv_general_dilated(
        u, w_dw[:, None, :].astype(u.dtype), window_strides=(1,), padding=[(CONV_WIDTH - 1, 0)],
        dimension_numbers=('NWC', 'WIO', 'NWC'), feature_group_count=D_MODEL) + b_dw
    u = jax.nn.silu(layer_norm(u, ln_g, ln_b))
    return u @ w_pw2 + b_pw2


def _fwd_setup_inputs(seed: int = 0) -> dict:
    key = jax.random.key(seed)
    ks = jax.random.split(key, 20)
    D, F, K = D_MODEL, D_FF, CONV_WIDTH
    beta = DEEPNORM_BETA

    def nrm(k, shape, scale):
        return jax.random.normal(k, shape, jnp.float32) * scale

    x = nrm(ks[0], (BATCH, SEQ, D), 1.0)
    ln_mix_g = 1.0 + nrm(ks[1], (DEPTH, D), 0.02)
    ln_mix_b = nrm(ks[2], (DEPTH, D), 0.02)
    ln_ffn_g = 1.0 + nrm(ks[3], (DEPTH, D), 0.02)
    ln_ffn_b = nrm(ks[4], (DEPTH, D), 0.02)
    ffn_w1 = nrm(ks[5], (DEPTH, D, F), D ** -0.5 * beta)
    ffn_w2 = nrm(ks[6], (DEPTH, F, D), F ** -0.5 * beta)
    col_scale = jnp.concatenate([jnp.ones((2 * D,), jnp.float32), jnp.full((D,), beta, jnp.float32),
                                 jnp.ones((D,), jnp.float32)])
    a_w_in = nrm(ks[7], (N_A_LAYERS, D, 4 * D), D ** -0.5) * col_scale
    a_lb_logits = nrm(ks[8], (N_A_LAYERS, D), 0.5)
    a_norm_g = 1.0 + nrm(ks[9], (N_A_LAYERS, D), 0.02)
    a_w_out = nrm(ks[10], (N_A_LAYERS, D, D), D ** -0.5 * beta)
    b_w_pw1 = nrm(ks[11], (N_B_LAYERS, D, 2 * D), D ** -0.5)
    b_b_pw1 = nrm(ks[12], (N_B_LAYERS, 2 * D), 0.02)
    b_w_dw = nrm(ks[13], (N_B_LAYERS, K, D), K ** -0.5)
    b_b_dw = nrm(ks[14], (N_B_LAYERS, D), 0.02)
    b_ln_g = 1.0 + nrm(ks[15], (N_B_LAYERS, D), 0.02)
    b_ln_b = nrm(ks[16], (N_B_LAYERS, D), 0.02)
    b_w_pw2 = nrm(ks[17], (N_B_LAYERS, D, D), D ** -0.5 * beta)
    b_b_pw2 = nrm(ks[18], (N_B_LAYERS, D), 0.02)
    return {"x": x, "ln_mix_g": ln_mix_g, "ln_mix_b": ln_mix_b, "ln_ffn_g": ln_ffn_g, "ln_ffn_b": ln_ffn_b,
            "ffn_w1": ffn_w1, "ffn_w2": ffn_w2,
            "a_w_in": a_w_in, "a_lb_logits": a_lb_logits, "a_norm_g": a_norm_g, "a_w_out": a_w_out,
            "b_w_pw1": b_w_pw1, "b_b_pw1": b_b_pw1, "b_w_dw": b_w_dw, "b_b_dw": b_b_dw,
            "b_ln_g": b_ln_g, "b_ln_b": b_ln_b, "b_w_pw2": b_w_pw2, "b_b_pw2": b_b_pw2}


def _fwd_reference(x, ln_mix_g, ln_mix_b, ln_ffn_g, ln_ffn_b, ffn_w1, ffn_w2,
              a_w_in, a_lb_logits, a_norm_g, a_w_out,
              b_w_pw1, b_b_pw1, b_w_dw, b_b_dw, b_ln_g, b_ln_b, b_w_pw2, b_b_pw2):
    lb_soft = jax.nn.softmax(a_lb_logits.astype(jnp.float32), axis=0)
    lb_all = jnp.cumsum(lb_soft, axis=0) - lb_soft[0]
    for i in range(DEPTH):
        j = i // N_MIXERS
        if i % N_MIXERS == 0:
            mix = hgrn2_mixer(x, a_w_in[j], lb_all[j], a_norm_g[j], a_w_out[j])
        else:
            mix = conformer_conv_mixer(x, b_w_pw1[j], b_b_pw1[j], b_w_dw[j], b_b_dw[j],
                                       b_ln_g[j], b_ln_b[j], b_w_pw2[j], b_b_pw2[j])
        x = layer_norm(DEEPNORM_ALPHA * x + mix, ln_mix_g[i], ln_mix_b[i])
        ff = jnp.square(jax.nn.relu(x @ ffn_w1[i])) @ ffn_w2[i]
        x = layer_norm(DEEPNORM_ALPHA * x + ff, ln_ffn_g[i], ln_ffn_b[i])
    return x


import jax as _jax
import jax.numpy as _jnp

TWIN_FORMAT = 'train_step'
FWD_PARAMS = ['x', 'ln_mix_g', 'ln_mix_b', 'ln_ffn_g', 'ln_ffn_b', 'ffn_w1', 'ffn_w2', 'a_w_in', 'a_lb_logits', 'a_norm_g', 'a_w_out', 'b_w_pw1', 'b_b_pw1', 'b_w_dw', 'b_b_dw', 'b_ln_g', 'b_ln_b', 'b_w_pw2', 'b_b_pw2']
TWIN_WEIGHTS = ['ln_mix_g', 'ln_mix_b', 'ln_ffn_g', 'ln_ffn_b', 'ffn_w1', 'ffn_w2', 'a_w_in', 'a_lb_logits', 'a_norm_g', 'a_w_out', 'b_w_pw1', 'b_b_pw1', 'b_w_dw', 'b_b_dw', 'b_ln_g', 'b_ln_b', 'b_w_pw2', 'b_b_pw2']
TWIN_DIFF_INPUT = 'x'
TWIN_INPUTS = ['x', 'ln_mix_g', 'ln_mix_b', 'ln_ffn_g', 'ln_ffn_b', 'ffn_w1', 'ffn_w2', 'a_w_in', 'a_lb_logits', 'a_norm_g', 'a_w_out', 'b_w_pw1', 'b_b_pw1', 'b_w_dw', 'b_b_dw', 'b_ln_g', 'b_ln_b', 'b_w_pw2', 'b_b_pw2', 'loss_target', 'm_ln_mix_g', 'm_ln_mix_b', 'm_ln_ffn_g', 'm_ln_ffn_b', 'm_ffn_w1', 'm_ffn_w2', 'm_a_w_in', 'm_a_lb_logits', 'm_a_norm_g', 'm_a_w_out', 'm_b_w_pw1', 'm_b_b_pw1', 'm_b_w_dw', 'm_b_b_dw', 'm_b_ln_g', 'm_b_ln_b', 'm_b_w_pw2', 'm_b_b_pw2', 'v_ln_mix_g', 'v_ln_mix_b', 'v_ln_ffn_g', 'v_ln_ffn_b', 'v_ffn_w1', 'v_ffn_w2', 'v_a_w_in', 'v_a_lb_logits', 'v_a_norm_g', 'v_a_w_out', 'v_b_w_pw1', 'v_b_b_pw1', 'v_b_w_dw', 'v_b_b_dw', 'v_b_ln_g', 'v_b_ln_b', 'v_b_w_pw2', 'v_b_b_pw2']
TWIN_OUTPUTS = ['loss', 'grad_x', 'grad_ln_mix_g', 'grad_ln_mix_b', 'grad_ln_ffn_g', 'grad_ln_ffn_b', 'grad_ffn_w1', 'grad_ffn_w2', 'grad_a_w_in', 'grad_a_lb_logits', 'grad_a_norm_g', 'grad_a_w_out', 'grad_b_w_pw1', 'grad_b_b_pw1', 'grad_b_w_dw', 'grad_b_b_dw', 'grad_b_ln_g', 'grad_b_ln_b', 'grad_b_w_pw2', 'grad_b_b_pw2', 'delta_ln_mix_g', 'delta_ln_mix_b', 'delta_ln_ffn_g', 'delta_ln_ffn_b', 'delta_ffn_w1', 'delta_ffn_w2', 'delta_a_w_in', 'delta_a_lb_logits', 'delta_a_norm_g', 'delta_a_w_out', 'delta_b_w_pw1', 'delta_b_b_pw1', 'delta_b_w_dw', 'delta_b_b_dw', 'delta_b_ln_g', 'delta_b_ln_b', 'delta_b_w_pw2', 'delta_b_b_pw2', 'new_m_ln_mix_g', 'new_m_ln_mix_b', 'new_m_ln_ffn_g', 'new_m_ln_ffn_b', 'new_m_ffn_w1', 'new_m_ffn_w2', 'new_m_a_w_in', 'new_m_a_lb_logits', 'new_m_a_norm_g', 'new_m_a_w_out', 'new_m_b_w_pw1', 'new_m_b_b_pw1', 'new_m_b_w_dw', 'new_m_b_b_dw', 'new_m_b_ln_g', 'new_m_b_ln_b', 'new_m_b_w_pw2', 'new_m_b_b_pw2', 'new_v_ln_mix_g', 'new_v_ln_mix_b', 'new_v_ln_ffn_g', 'new_v_ln_ffn_b', 'new_v_ffn_w1', 'new_v_ffn_w2', 'new_v_a_w_in', 'new_v_a_lb_logits', 'new_v_a_norm_g', 'new_v_a_w_out', 'new_v_b_w_pw1', 'new_v_b_b_pw1', 'new_v_b_w_dw', 'new_v_b_b_dw', 'new_v_b_ln_g', 'new_v_b_ln_b', 'new_v_b_w_pw2', 'new_v_b_b_pw2']
TWIN_LEAF_KINDS = {'loss': 'loss', 'grad_x': 'grad_x', 'grad_ln_mix_g': 'grad_w', 'grad_ln_mix_b': 'grad_w', 'grad_ln_ffn_g': 'grad_w', 'grad_ln_ffn_b': 'grad_w', 'grad_ffn_w1': 'grad_w', 'grad_ffn_w2': 'grad_w', 'grad_a_w_in': 'grad_w', 'grad_a_lb_logits': 'grad_w', 'grad_a_norm_g': 'grad_w', 'grad_a_w_out': 'grad_w', 'grad_b_w_pw1': 'grad_w', 'grad_b_b_pw1': 'grad_w', 'grad_b_w_dw': 'grad_w', 'grad_b_b_dw': 'grad_w', 'grad_b_ln_g': 'grad_w', 'grad_b_ln_b': 'grad_w', 'grad_b_w_pw2': 'grad_w', 'grad_b_b_pw2': 'grad_w', 'delta_ln_mix_g': 'delta_w', 'delta_ln_mix_b': 'delta_w', 'delta_ln_ffn_g': 'delta_w', 'delta_ln_ffn_b': 'delta_w', 'delta_ffn_w1': 'delta_w', 'delta_ffn_w2': 'delta_w', 'delta_a_w_in': 'delta_w', 'delta_a_lb_logits': 'delta_w', 'delta_a_norm_g': 'delta_w', 'delta_a_w_out': 'delta_w', 'delta_b_w_pw1': 'delta_w', 'delta_b_b_pw1': 'delta_w', 'delta_b_w_dw': 'delta_w', 'delta_b_b_dw': 'delta_w', 'delta_b_ln_g': 'delta_w', 'delta_b_ln_b': 'delta_w', 'delta_b_w_pw2': 'delta_w', 'delta_b_b_pw2': 'delta_w', 'new_m_ln_mix_g': 'new_m', 'new_m_ln_mix_b': 'new_m', 'new_m_ln_ffn_g': 'new_m', 'new_m_ln_ffn_b': 'new_m', 'new_m_ffn_w1': 'new_m', 'new_m_ffn_w2': 'new_m', 'new_m_a_w_in': 'new_m', 'new_m_a_lb_logits': 'new_m', 'new_m_a_norm_g': 'new_m', 'new_m_a_w_out': 'new_m', 'new_m_b_w_pw1': 'new_m', 'new_m_b_b_pw1': 'new_m', 'new_m_b_w_dw': 'new_m', 'new_m_b_b_dw': 'new_m', 'new_m_b_ln_g': 'new_m', 'new_m_b_ln_b': 'new_m', 'new_m_b_w_pw2': 'new_m', 'new_m_b_b_pw2': 'new_m', 'new_v_ln_mix_g': 'new_v', 'new_v_ln_mix_b': 'new_v', 'new_v_ln_ffn_g': 'new_v', 'new_v_ln_ffn_b': 'new_v', 'new_v_ffn_w1': 'new_v', 'new_v_ffn_w2': 'new_v', 'new_v_a_w_in': 'new_v', 'new_v_a_lb_logits': 'new_v', 'new_v_a_norm_g': 'new_v', 'new_v_a_w_out': 'new_v', 'new_v_b_w_pw1': 'new_v', 'new_v_b_b_pw1': 'new_v', 'new_v_b_w_dw': 'new_v', 'new_v_b_b_dw': 'new_v', 'new_v_b_ln_g': 'new_v', 'new_v_b_ln_b': 'new_v', 'new_v_b_w_pw2': 'new_v', 'new_v_b_b_pw2': 'new_v'}


def _forward(args):
    return _fwd_reference(*[args[k] for k in FWD_PARAMS])


def _output_shape():
    def fwd():
        inp = _fwd_setup_inputs(0)
        return _fwd_reference(*[inp[k] for k in FWD_PARAMS])
    out = _jax.eval_shape(fwd)
    return out.shape, out.dtype

N_MICROBATCH = 1
ADAM_LR = 0.001
ADAM_B1 = 0.9
ADAM_B2 = 0.999
ADAM_EPS = 1e-08
ADAM_WD = 0.01
ADAM_STEP = 10
PER_EXAMPLE_BATCH_AXIS = {'x': 0, 'loss_target': 0}
SHARED_INPUTS = []
_WEIGHT_DTYPES = {'ln_mix_g': _jnp.float32, 'ln_mix_b': _jnp.float32, 'ln_ffn_g': _jnp.float32, 'ln_ffn_b': _jnp.float32, 'ffn_w1': _jnp.float32, 'ffn_w2': _jnp.float32, 'a_w_in': _jnp.float32, 'a_lb_logits': _jnp.float32, 'a_norm_g': _jnp.float32, 'a_w_out': _jnp.float32, 'b_w_pw1': _jnp.float32, 'b_b_pw1': _jnp.float32, 'b_w_dw': _jnp.float32, 'b_b_dw': _jnp.float32, 'b_ln_g': _jnp.float32, 'b_ln_b': _jnp.float32, 'b_w_pw2': _jnp.float32, 'b_b_pw2': _jnp.float32}
MOMENT_SCALE = {'ln_mix_g': 2.203859e+00, 'ln_mix_b': 1.008523e+00, 'ln_ffn_g': 3.223039e+01, 'ln_ffn_b': 3.214326e+00, 'ffn_w1': 1.895017e-02, 'ffn_w2': 5.136133e-02, 'a_w_in': 4.896255e-02, 'a_lb_logits': 3.227378e-03, 'a_norm_g': 3.794561e-02, 'a_w_out': 8.974542e-02, 'b_w_pw1': 2.951369e-02, 'b_b_pw1': 7.772095e-02, 'b_w_dw': 3.948733e-02, 'b_b_dw': 1.911850e-01, 'b_ln_g': 7.605341e-02, 'b_ln_b': 1.182273e-01, 'b_w_pw2': 1.271688e-01, 'b_b_pw2': 5.967439e-01}


def _to_microbatches(a, axis):
    t = _jnp.moveaxis(a, axis, 0)
    t = t.reshape((N_MICROBATCH, t.shape[0] // N_MICROBATCH) + t.shape[1:])
    return _jnp.moveaxis(t, 1, axis + 1)


def setup_inputs(seed: int = 0) -> dict:
    inp = _fwd_setup_inputs(seed)
    key = _jax.random.fold_in(_jax.random.key(seed), 7919)
    shape, _ = _output_shape()
    out = dict(inp)
    out["loss_target"] = _jax.random.normal(_jax.random.fold_in(key, 0), shape, _jnp.float32)
    for i, name in enumerate(TWIN_WEIGHTS):
        w = inp[name].astype(_jnp.float32)
        if MOMENT_SCALE is None:
            s = _jnp.sqrt(_jnp.mean(_jnp.square(w)) + 1e-30)
        else:
            s = MOMENT_SCALE[name]
        km, kv = _jax.random.split(_jax.random.fold_in(key, i + 1))
        out[name] = w
        out["m_" + name] = s * _jax.random.normal(km, w.shape, _jnp.float32)
        out["v_" + name] = (s * s) * _jax.random.uniform(kv, w.shape, _jnp.float32, 0.5, 1.5)
    if N_MICROBATCH > 1:
        for name, axis in PER_EXAMPLE_BATCH_AXIS.items():
            out[name] = _to_microbatches(out[name], axis)
    return {'x': out['x'], 'ln_mix_g': out['ln_mix_g'], 'ln_mix_b': out['ln_mix_b'], 'ln_ffn_g': out['ln_ffn_g'], 'ln_ffn_b': out['ln_ffn_b'], 'ffn_w1': out['ffn_w1'], 'ffn_w2': out['ffn_w2'], 'a_w_in': out['a_w_in'], 'a_lb_logits': out['a_lb_logits'], 'a_norm_g': out['a_norm_g'], 'a_w_out': out['a_w_out'], 'b_w_pw1': out['b_w_pw1'], 'b_b_pw1': out['b_b_pw1'], 'b_w_dw': out['b_w_dw'], 'b_b_dw': out['b_b_dw'], 'b_ln_g': out['b_ln_g'], 'b_ln_b': out['b_ln_b'], 'b_w_pw2': out['b_w_pw2'], 'b_b_pw2': out['b_b_pw2'], 'loss_target': out['loss_target'], 'm_ln_mix_g': out['m_ln_mix_g'], 'm_ln_mix_b': out['m_ln_mix_b'], 'm_ln_ffn_g': out['m_ln_ffn_g'], 'm_ln_ffn_b': out['m_ln_ffn_b'], 'm_ffn_w1': out['m_ffn_w1'], 'm_ffn_w2': out['m_ffn_w2'], 'm_a_w_in': out['m_a_w_in'], 'm_a_lb_logits': out['m_a_lb_logits'], 'm_a_norm_g': out['m_a_norm_g'], 'm_a_w_out': out['m_a_w_out'], 'm_b_w_pw1': out['m_b_w_pw1'], 'm_b_b_pw1': out['m_b_b_pw1'], 'm_b_w_dw': out['m_b_w_dw'], 'm_b_b_dw': out['m_b_b_dw'], 'm_b_ln_g': out['m_b_ln_g'], 'm_b_ln_b': out['m_b_ln_b'], 'm_b_w_pw2': out['m_b_w_pw2'], 'm_b_b_pw2': out['m_b_b_pw2'], 'v_ln_mix_g': out['v_ln_mix_g'], 'v_ln_mix_b': out['v_ln_mix_b'], 'v_ln_ffn_g': out['v_ln_ffn_g'], 'v_ln_ffn_b': out['v_ln_ffn_b'], 'v_ffn_w1': out['v_ffn_w1'], 'v_ffn_w2': out['v_ffn_w2'], 'v_a_w_in': out['v_a_w_in'], 'v_a_lb_logits': out['v_a_lb_logits'], 'v_a_norm_g': out['v_a_norm_g'], 'v_a_w_out': out['v_a_w_out'], 'v_b_w_pw1': out['v_b_w_pw1'], 'v_b_b_pw1': out['v_b_b_pw1'], 'v_b_w_dw': out['v_b_w_dw'], 'v_b_b_dw': out['v_b_b_dw'], 'v_b_ln_g': out['v_b_ln_g'], 'v_b_ln_b': out['v_b_ln_b'], 'v_b_w_pw2': out['v_b_w_pw2'], 'v_b_b_pw2': out['v_b_b_pw2']}


def _loss(weights, diff, rest, loss_target):
    with _jax.named_scope("forward"):
        args = {**rest, TWIN_DIFF_INPUT: diff, **{k: w.astype(_WEIGHT_DTYPES[k]) for k, w in weights.items()}}
        y = _forward(args)
    with _jax.named_scope("loss_head"):
        err = _jnp.square(y.astype(_jnp.float32) - loss_target)
        return 0.5 * _jnp.sum(_jnp.mean(err, axis=-1)) if err.ndim else 0.5 * err


def _adamw(w, g, m, v):
    m = ADAM_B1 * m + (1.0 - ADAM_B1) * g
    v = ADAM_B2 * v + (1.0 - ADAM_B2) * _jnp.square(g)
    m_hat = m / (1.0 - ADAM_B1 ** ADAM_STEP)
    v_hat = v / (1.0 - ADAM_B2 ** ADAM_STEP)
    delta = -ADAM_LR * (m_hat / (_jnp.sqrt(v_hat) + ADAM_EPS) + ADAM_WD * w)
    return delta, m, v


def reference(x, ln_mix_g, ln_mix_b, ln_ffn_g, ln_ffn_b, ffn_w1, ffn_w2, a_w_in, a_lb_logits, a_norm_g, a_w_out, b_w_pw1, b_b_pw1, b_w_dw, b_b_dw, b_ln_g, b_ln_b, b_w_pw2, b_b_pw2, loss_target, m_ln_mix_g, m_ln_mix_b, m_ln_ffn_g, m_ln_ffn_b, m_ffn_w1, m_ffn_w2, m_a_w_in, m_a_lb_logits, m_a_norm_g, m_a_w_out, m_b_w_pw1, m_b_b_pw1, m_b_w_dw, m_b_b_dw, m_b_ln_g, m_b_ln_b, m_b_w_pw2, m_b_b_pw2, v_ln_mix_g, v_ln_mix_b, v_ln_ffn_g, v_ln_ffn_b, v_ffn_w1, v_ffn_w2, v_a_w_in, v_a_lb_logits, v_a_norm_g, v_a_w_out, v_b_w_pw1, v_b_b_pw1, v_b_w_dw, v_b_b_dw, v_b_ln_g, v_b_ln_b, v_b_w_pw2, v_b_b_pw2):
    given = dict(x=x, ln_mix_g=ln_mix_g, ln_mix_b=ln_mix_b, ln_ffn_g=ln_ffn_g, ln_ffn_b=ln_ffn_b, ffn_w1=ffn_w1, ffn_w2=ffn_w2, a_w_in=a_w_in, a_lb_logits=a_lb_logits, a_norm_g=a_norm_g, a_w_out=a_w_out, b_w_pw1=b_w_pw1, b_b_pw1=b_b_pw1, b_w_dw=b_w_dw, b_b_dw=b_b_dw, b_ln_g=b_ln_g, b_ln_b=b_ln_b, b_w_pw2=b_w_pw2, b_b_pw2=b_b_pw2, loss_target=loss_target, m_ln_mix_g=m_ln_mix_g, m_ln_mix_b=m_ln_mix_b, m_ln_ffn_g=m_ln_ffn_g, m_ln_ffn_b=m_ln_ffn_b, m_ffn_w1=m_ffn_w1, m_ffn_w2=m_ffn_w2, m_a_w_in=m_a_w_in, m_a_lb_logits=m_a_lb_logits, m_a_norm_g=m_a_norm_g, m_a_w_out=m_a_w_out, m_b_w_pw1=m_b_w_pw1, m_b_b_pw1=m_b_b_pw1, m_b_w_dw=m_b_w_dw, m_b_b_dw=m_b_b_dw, m_b_ln_g=m_b_ln_g, m_b_ln_b=m_b_ln_b, m_b_w_pw2=m_b_w_pw2, m_b_b_pw2=m_b_b_pw2, v_ln_mix_g=v_ln_mix_g, v_ln_mix_b=v_ln_mix_b, v_ln_ffn_g=v_ln_ffn_g, v_ln_ffn_b=v_ln_ffn_b, v_ffn_w1=v_ffn_w1, v_ffn_w2=v_ffn_w2, v_a_w_in=v_a_w_in, v_a_lb_logits=v_a_lb_logits, v_a_norm_g=v_a_norm_g, v_a_w_out=v_a_w_out, v_b_w_pw1=v_b_w_pw1, v_b_b_pw1=v_b_b_pw1, v_b_w_dw=v_b_w_dw, v_b_b_dw=v_b_b_dw, v_b_ln_g=v_b_ln_g, v_b_ln_b=v_b_ln_b, v_b_w_pw2=v_b_w_pw2, v_b_b_pw2=v_b_b_pw2)
    weights = {n: given[n] for n in TWIN_WEIGHTS}
    shared = {n: given[n] for n in SHARED_INPUTS}
    per_example = {n: given[n] for n in ['x']}
    grad_fn = _jax.value_and_grad(_loss, argnums=(0, 1))

    def one_microbatch(ex, loss_target):
        ex = dict(ex)
        diff = ex.pop(TWIN_DIFF_INPUT)
        return grad_fn(weights, diff, {**shared, **ex}, loss_target)

    if N_MICROBATCH == 1:
        loss, (grad_w, grad_x) = one_microbatch(per_example, given["loss_target"])
    else:
        def body(carry, xs):
            loss_sum, grad_sum = carry
            l_k, (gw_k, gx_k) = one_microbatch(xs[0], xs[1])
            with _jax.named_scope("update"):
                return (loss_sum + l_k, _jax.tree.map(_jnp.add, grad_sum, gw_k)), gx_k

        init = (_jnp.zeros((), _jnp.float32), _jax.tree.map(_jnp.zeros_like, weights))
        (loss, grad_w), grad_x = _jax.lax.scan(body, init, (per_example, given["loss_target"]))
    with _jax.named_scope("update"):
        delta_w, new_m, new_v = {}, {}, {}
        for n in TWIN_WEIGHTS:
            delta_w[n], new_m[n], new_v[n] = _adamw(weights[n], grad_w[n], given["m_" + n], given["v_" + n])
    return (loss, grad_x, *[grad_w[n] for n in TWIN_WEIGHTS], *[delta_w[n] for n in TWIN_WEIGHTS],
            *[new_m[n] for n in TWIN_WEIGHTS], *[new_v[n] for n in TWIN_WEIGHTS])
```

```python
import functools

import jax
import jax.numpy as jnp
from jax import lax
from jax.experimental import pallas as pl
from jax.experimental.pallas import tpu as pltpu

F32 = jnp.float32
BF16 = jnp.bfloat16
MESH = pl.DeviceIdType.MESH

DEPTH = 4
ALPHA = (2.0 * DEPTH) ** 0.25
LN_EPS = 1e-5
RMS_EPS = 1e-6
GATE_EPS = 1e-6
HEAD = 128
CHUNK = 64
SUB = 16
CONV_W = 31
HALO = 32
ADAM_LR, ADAM_B1, ADAM_B2, ADAM_EPS, ADAM_WD, ADAM_STEP = 0.001, 0.9, 0.999, 1e-08, 0.01, 10
VMEM_LIMIT = 56 * 1024 * 1024


def _params(sem):
    return pltpu.CompilerParams(dimension_semantics=sem, vmem_limit_bytes=VMEM_LIMIT)


def _dot(a, b):
    return jnp.dot(a, b, preferred_element_type=F32)


def _dot_nt(a, b):
    return lax.dot_general(a, b, (((1,), (1,)), ((), ())), preferred_element_type=F32)


def _dot_tn(a, b):
    return lax.dot_general(a, b, (((0,), (0,)), ((), ())), preferred_element_type=F32)


def _sigmoid(x):
    return 1.0 / (1.0 + jnp.exp(-x))


def _ln_stats(r):
    mu = jnp.mean(r, axis=-1, keepdims=True)
    xc = r - mu
    var = jnp.mean(xc * xc, axis=-1, keepdims=True)
    rstd = lax.rsqrt(var + LN_EPS)
    return xc * rstd, rstd


def _ln_bwd(dy, xhat, rstd, g):
    dyg = dy * g
    m1 = jnp.mean(dyg, axis=-1, keepdims=True)
    m2 = jnp.mean(dyg * xhat, axis=-1, keepdims=True)
    return rstd * (dyg - m1 - xhat * m2)


def mm_groups(a, w, bias, *, tm, name):
    T, K = a.shape
    G, _, N = w.shape

    def body(a_ref, w_ref, b_ref, o_ref):
        o_ref[...] = _dot(a_ref[...].astype(BF16), w_ref[...]) + b_ref[...]

    return pl.pallas_call(
        body, name=name, grid=(G, T // tm),
        in_specs=[pl.BlockSpec((tm, K), lambda g, i: (i, 0)),
                  pl.BlockSpec((None, K, N), lambda g, i: (g, 0, 0)),
                  pl.BlockSpec((None, 1, N), lambda g, i: (g, 0, 0))],
        out_specs=pl.BlockSpec((None, tm, N), lambda g, i: (g, i, 0)),
        out_shape=jax.ShapeDtypeStruct((G, T, N), F32),
        compiler_params=_params(("parallel", "parallel")),
    )(a, w, bias)


def mm_res_ln(a, w, bias, res, g, b, *, tm, name):
    T, K = a.shape
    N = w.shape[1]

    def body(a_ref, w_ref, bias_ref, res_ref, g_ref, b_ref, r_ref, y_ref):
        r = ALPHA * res_ref[...] + _dot(a_ref[...], w_ref[...]) + bias_ref[...]
        r_ref[...] = r
        xhat, _ = _ln_stats(r)
        y_ref[...] = xhat * g_ref[...] + b_ref[...]

    row = lambda i: (i, 0)
    fix = lambda i: (0, 0)
    return pl.pallas_call(
        body, name=name, grid=(T // tm,),
        in_specs=[pl.BlockSpec((tm, K), row), pl.BlockSpec((K, N), fix), pl.BlockSpec((1, N), fix),
                  pl.BlockSpec((tm, N), row), pl.BlockSpec((1, N), fix), pl.BlockSpec((1, N), fix)],
        out_specs=[pl.BlockSpec((tm, N), row), pl.BlockSpec((tm, N), row)],
        out_shape=[jax.ShapeDtypeStruct((T, N), F32), jax.ShapeDtypeStruct((T, N), F32)],
        compiler_params=_params(("parallel",)),
    )(a, w, bias, res, g, b)


def ffn_fwd(x, w1, w2, g, b, *, tm, tf, name):
    T, D = x.shape
    NC, _, FC = w1.shape
    F = NC * FC
    per = FC // tf
    nf = F // tf

    def body(x_ref, w1_ref, w2_ref, g_ref, b_ref, z_ref, r_ref, y_ref, acc_ref, xb_ref):
        f = pl.program_id(1)

        @pl.when(f == 0)
        def _():
            acc_ref[...] = jnp.zeros_like(acc_ref)
            xb_ref[...] = x_ref[...].astype(BF16)

        z = _dot(xb_ref[...], w1_ref[...])
        z_ref[...] = z.astype(BF16)
        h = jnp.square(jnp.maximum(z, 0.0)).astype(BF16)
        acc_ref[...] += _dot(h, w2_ref[...])

        @pl.when(f == nf - 1)
        def _():
            r = ALPHA * x_ref[...] + acc_ref[...]
            r_ref[...] = r
            xhat, _ = _ln_stats(r)
            y_ref[...] = xhat * g_ref[...] + b_ref[...]

    return pl.pallas_call(
        body, name=name, grid=(T // tm, nf),
        in_specs=[pl.BlockSpec((tm, D), lambda i, f: (i, 0)),
                  pl.BlockSpec((None, D, tf), lambda i, f: (f // per, 0, f % per)),
                  pl.BlockSpec((tf, D), lambda i, f: (f, 0)),
                  pl.BlockSpec((1, D), lambda i, f: (0, 0)),
                  pl.BlockSpec((1, D), lambda i, f: (0, 0))],
        out_specs=[pl.BlockSpec((tm, tf), lambda i, f: (i, f)),
                   pl.BlockSpec((tm, D), lambda i, f: (i, 0)),
                   pl.BlockSpec((tm, D), lambda i, f: (i, 0))],
        out_shape=[jax.ShapeDtypeStruct((T, F), BF16), jax.ShapeDtypeStruct((T, D), F32),
                   jax.ShapeDtypeStruct((T, D), F32)],
        scratch_shapes=[pltpu.VMEM((tm, D), F32), pltpu.VMEM((tm, D), BF16)],
        compiler_params=_params(("parallel", "arbitrary")),
    )(x, w1, w2, g, b)


def ln_bwd_mm(dy, r, g, w, *, tm, name):
    T, N = dy.shape
    Ko = w.shape[0]

    def body(dy_ref, r_ref, g_ref, w_ref, dr_ref, drb_ref, o_ref, s_ref):
        @pl.when(pl.program_id(0) == 0)
        def _():
            s_ref[...] = jnp.zeros_like(s_ref)

        dy_ = dy_ref[...]
        xhat, rstd = _ln_stats(r_ref[...])
        dr = _ln_bwd(dy_, xhat, rstd, g_ref[...])
        dr_ref[...] = dr
        drb = dr.astype(BF16)
        drb_ref[...] = drb
        o_ref[...] = _dot_nt(drb, w_ref[...])
        s_ref[0:1, :] += jnp.sum(dy_ * xhat, axis=0, keepdims=True)
        s_ref[1:2, :] += jnp.sum(dy_, axis=0, keepdims=True)
        s_ref[2:3, :] += jnp.sum(dr, axis=0, keepdims=True)

    row = lambda i: (i, 0)
    fix = lambda i: (0, 0)
    return pl.pallas_call(
        body, name=name, grid=(T // tm,),
        in_specs=[pl.BlockSpec((tm, N), row), pl.BlockSpec((tm, N), row), pl.BlockSpec((1, N), fix),
                  pl.BlockSpec((Ko, N), fix)],
        out_specs=[pl.BlockSpec((tm, N), row), pl.BlockSpec((tm, N), row), pl.BlockSpec((tm, Ko), row),
                   pl.BlockSpec((8, N), fix)],
        out_shape=[jax.ShapeDtypeStruct((T, N), F32), jax.ShapeDtypeStruct((T, N), BF16),
                   jax.ShapeDtypeStruct((T, Ko), F32), jax.ShapeDtypeStruct((8, N), F32)],
        compiler_params=_params(("arbitrary",)),
    )(dy, r, g, w)


def ffn_bwd_dx(dy, r, g, z, w1, w2, *, tm, tf, name):
    T, D = dy.shape
    NC, _, FC = w1.shape
    F = NC * FC
    per = FC // tf
    nf = F // tf

    def body(dy_ref, r_ref, g_ref, z_ref, w1_ref, w2_ref, dz_ref, dx_ref, drb_ref, s_ref, dr_scr, acc_ref):
        i = pl.program_id(0)
        f = pl.program_id(1)

        @pl.when((i == 0) & (f == 0))
        def _():
            s_ref[...] = jnp.zeros_like(s_ref)

        @pl.when(f == 0)
        def _():
            dy_ = dy_ref[...]
            xhat, rstd = _ln_stats(r_ref[...])
            dr = _ln_bwd(dy_, xhat, rstd, g_ref[...])
            dr_scr[...] = dr
            drb_ref[...] = dr.astype(BF16)
            acc_ref[...] = jnp.zeros_like(acc_ref)
            s_ref[0:1, :] += jnp.sum(dy_ * xhat, axis=0, keepdims=True)
            s_ref[1:2, :] += jnp.sum(dy_, axis=0, keepdims=True)

        dh = _dot_nt(drb_ref[...], w2_ref[...])
        dz = (dh * (2.0 * jnp.maximum(z_ref[...].astype(F32), 0.0))).astype(BF16)
        dz_ref[...] = dz
        acc_ref[...] += _dot_nt(dz, w1_ref[...])

        @pl.when(f == nf - 1)
        def _():
            dx_ref[...] = ALPHA * dr_scr[...] + acc_ref[...]

    return pl.pallas_call(
        body, name=name, grid=(T // tm, nf),
        in_specs=[pl.BlockSpec((tm, D), lambda i, f: (i, 0)),
                  pl.BlockSpec((tm, D), lambda i, f: (i, 0)),
                  pl.BlockSpec((1, D), lambda i, f: (0, 0)),
                  pl.BlockSpec((tm, tf), lambda i, f: (i, f)),
                  pl.BlockSpec((None, D, tf), lambda i, f: (f // per, 0, f % per)),
                  pl.BlockSpec((tf, D), lambda i, f: (f, 0))],
        out_specs=[pl.BlockSpec((tm, tf), lambda i, f: (i, f)),
                   pl.BlockSpec((tm, D), lambda i, f: (i, 0)),
                   pl.BlockSpec((tm, D), lambda i, f: (i, 0)),
                   pl.BlockSpec((8, D), lambda i, f: (0, 0))],
        out_shape=[jax.ShapeDtypeStruct((T, F), BF16), jax.ShapeDtypeStruct((T, D), F32),
                   jax.ShapeDtypeStruct((T, D), BF16), jax.ShapeDtypeStruct((8, D), F32)],
        scratch_shapes=[pltpu.VMEM((tm, D), F32), pltpu.VMEM((tm, D), F32)],
        compiler_params=_params(("arbitrary", "arbitrary")),
    )(dy, r, g, z, w1, w2)


def mm_tn(a, b, *, tm, tk, tn, relu2=False, colsum=False, name):
    T, K = a.shape
    G, _, N = b.shape
    nt = T // tm

    def body(a_ref, b_ref, o_ref, *rest):
        t = pl.program_id(3)

        @pl.when(t == 0)
        def _():
            o_ref[...] = jnp.zeros_like(o_ref)

        av = a_ref[...]
        if relu2:
            av = jnp.square(jnp.maximum(av.astype(F32), 0.0))
        bv = b_ref[...]
        o_ref[...] += _dot_tn(av.astype(BF16), bv)
        if colsum:
            c_ref = rest[0]

            @pl.when(t == 0)
            def _():
                c_ref[...] = jnp.zeros_like(c_ref)

            c_ref[...] += jnp.sum(bv.astype(F32), axis=0, keepdims=True)

    out_specs = [pl.BlockSpec((None, None, tk, tn), lambda g, k, n, t: (g, n, k, 0))]
    out_shape = [jax.ShapeDtypeStruct((G, N // tn, K, tn), F32)]
    if colsum:
        out_specs.append(pl.BlockSpec((None, 1, tn), lambda g, k, n, t: (g, 0, n)))
        out_shape.append(jax.ShapeDtypeStruct((G, 1, N), F32))
    res = pl.pallas_call(
        body, name=name, grid=(G, K // tk, N // tn, nt),
        in_specs=[pl.BlockSpec((tm, tk), lambda g, k, n, t: (t, k)),
                  pl.BlockSpec((None, tm, tn), lambda g, k, n, t: (g, t, n))],
        out_specs=out_specs, out_shape=out_shape,
        compiler_params=_params(("parallel", "arbitrary", "arbitrary", "arbitrary")),
    )(a, b)
    return res if colsum else res[0]


def mm_nt_acc(dy, w, base, *, tm, name):
    G, T, N = dy.shape
    K = w.shape[1]

    def body(dy_ref, w_ref, base_ref, o_ref):
        g = pl.program_id(1)

        @pl.when(g == 0)
        def _():
            o_ref[...] = ALPHA * base_ref[...]

        o_ref[...] += _dot_nt(dy_ref[...], w_ref[...])

    return pl.pallas_call(
        body, name=name, grid=(T // tm, G),
        in_specs=[pl.BlockSpec((None, tm, N), lambda i, g: (g, i, 0)),
                  pl.BlockSpec((None, K, N), lambda i, g: (g, 0, 0)),
                  pl.BlockSpec((tm, K), lambda i, g: (i, 0))],
        out_specs=pl.BlockSpec((tm, K), lambda i, g: (i, 0)),
        out_shape=jax.ShapeDtypeStruct((T, K), F32),
        compiler_params=_params(("parallel", "arbitrary")),
    )(dy, w, base)


def _split3(x):
    x1 = x.astype(BF16)
    r1 = x - x1.astype(F32)
    x2 = r1.astype(BF16)
    x3 = (r1 - x2.astype(F32)).astype(BF16)
    return x1, x2, x3


def _tri_dot(tri, x):
    x1, x2, x3 = _split3(x)
    return _dot(tri, x1) + _dot(tri, x2) + _dot(tri, x3)


def _gates(pq, fz, lb):
    sg = _sigmoid(fz)
    f = lb + (1.0 - lb) * sg
    logf = jnp.log(jnp.maximum(f, GATE_EPS))
    sq = _sigmoid(pq)
    return pq * sq, 1.0 - f, logf, f, sg, sq


def _row_ge(s):
    return lax.broadcasted_iota(jnp.int32, (SUB, HEAD), 0) >= s


def hgrn_fwd(proj, lb, norm_g, *, rb, name):
    _, T, D = proj.shape
    H = D // HEAD
    nb = T // rb
    nck = rb // CHUNK
    nsub = CHUNK // SUB

    def body(pq_ref, fz_ref, pv_ref, pg_ref, lb_ref, ng_ref, o_ref, og_ref, st_ref,
             S, q_s, k_s, v_s, b_s, o_s, p_s):
        @pl.when(pl.program_id(1) == 0)
        def _():
            S[...] = jnp.zeros_like(S)

        lb_ = lb_ref[...]
        ng = ng_ref[...]
        ri = lax.broadcasted_iota(jnp.int32, (CHUNK, CHUNK), 0)
        ci = lax.broadcasted_iota(jnp.int32, (CHUNK, CHUNK), 1)
        tri = (ci <= ri).astype(BF16)
        ones = jnp.ones((HEAD, HEAD), BF16)

        def chunk(c, carry):
            rows = pl.ds(pl.multiple_of(c * CHUNK, CHUNK), CHUNK)
            q, k, logf, _, _, _ = _gates(pq_ref[rows, :], fz_ref[rows, :], lb_)
            v = pv_ref[rows, :]
            b = _tri_dot(tri, logf)
            q_s[...] = q
            k_s[...] = k
            v_s[...] = v
            b_s[...] = b
            st_ref[c] = S[...]
            Sb = S[...].astype(BF16)
            o_s[...] = _dot_nt((q * jnp.exp(b)).astype(BF16), Sb)
            for i in range(1, nsub):
                lo = i * SUB
                ref = b_s[lo - 1:lo, :]
                qt = (q_s[lo:lo + SUB, :] * jnp.exp(b_s[lo:lo + SUB, :] - ref)).astype(BF16)
                kt = (k_s[0:lo, :] * jnp.exp(ref - b_s[0:lo, :])).astype(BF16)
                a = _dot_nt(qt, kt)
                o_s[lo:lo + SUB, :] += _dot(a.astype(BF16), v_s[0:lo, :].astype(BF16))
            for i in range(nsub):
                lo = i * SUB
                qi = q_s[lo:lo + SUB, :]
                bi = b_s[lo:lo + SUB, :]
                for s in range(SUB):
                    m = _row_ge(s)
                    e = jnp.exp(jnp.where(m, bi - b_s[lo + s:lo + s + 1, :], 0.0))
                    p = jnp.where(m, qi * (k_s[lo + s:lo + s + 1, :] * e), 0.0)
                    p_s[s * SUB:(s + 1) * SUB, :] = p.astype(BF16)
                rs = _dot(p_s[...], ones)
                acc = o_s[lo:lo + SUB, :]
                for s in range(SUB):
                    acc = acc + rs[s * SUB:(s + 1) * SUB, :] * v_s[lo + s:lo + s + 1, :]
                o_s[lo:lo + SUB, :] = acc
            bl = b_s[CHUNK - 1:CHUNK, :]
            kk = (k * jnp.exp(bl - b)).astype(BF16)
            S[...] = S[...] * jnp.exp(bl) + _dot_tn(v.astype(BF16), kk)
            o = o_s[...]
            o_ref[rows, :] = o
            rinv = lax.rsqrt(jnp.mean(o * o, axis=-1, keepdims=True) + RMS_EPS)
            pg = pg_ref[rows, :]
            og_ref[rows, :] = (o * rinv * ng * (pg * _sigmoid(pg))).astype(BF16)
            return carry

        lax.fori_loop(0, nck, chunk, 0)

    def grp(gi):
        return pl.BlockSpec((None, rb, HEAD), lambda h, r: (gi, r, h))

    vec = pl.BlockSpec((1, HEAD), lambda h, r: (0, h))
    return pl.pallas_call(
        body, name=name, grid=(H, nb),
        in_specs=[grp(0), grp(1), grp(2), grp(3), vec, vec],
        out_specs=[pl.BlockSpec((rb, HEAD), lambda h, r: (r, h)),
                   pl.BlockSpec((rb, HEAD), lambda h, r: (r, h)),
                   pl.BlockSpec((None, nck, HEAD, HEAD), lambda h, r: (h, r, 0, 0))],
        out_shape=[jax.ShapeDtypeStruct((T, D), F32), jax.ShapeDtypeStruct((T, D), BF16),
                   jax.ShapeDtypeStruct((H, T // CHUNK, HEAD, HEAD), F32)],
        scratch_shapes=[pltpu.VMEM((HEAD, HEAD), F32)] + [pltpu.VMEM((CHUNK, HEAD), F32)] * 5
        + [pltpu.VMEM((SUB * SUB, HEAD), BF16)],
        compiler_params=_params(("parallel", "arbitrary")),
    )(proj, proj, proj, proj, lb, norm_g)


def hgrn_bwd(proj, o, dog, states, lb, norm_g, *, rb, name):
    _, T, D = proj.shape
    H = D // HEAD
    nb = T // rb
    nck = rb // CHUNK
    nsub = CHUNK // SUB

    def body(pq_ref, fz_ref, pv_ref, pg_ref, o_ref, dog_ref, st_ref, lb_ref, ng_ref, dp_ref, s_ref,
             dS, q_s, k_s, v_s, b_s, do_s, dq_s, dk_s, dv_s, ke_s, qe_s, p_s):
        @pl.when(pl.program_id(1) == 0)
        def _():
            dS[...] = jnp.zeros_like(dS)
            s_ref[...] = jnp.zeros_like(s_ref)

        lb_ = lb_ref[...]
        ng = ng_ref[...]
        ri = lax.broadcasted_iota(jnp.int32, (CHUNK, CHUNK), 0)
        ci = lax.broadcasted_iota(jnp.int32, (CHUNK, CHUNK), 1)
        tri = (ci <= ri).astype(BF16)
        triu = (ci >= ri).astype(BF16)
        ones = jnp.ones((HEAD, HEAD), BF16)
        last_row = lax.broadcasted_iota(jnp.int32, (CHUNK, HEAD), 0) == CHUNK - 1

        def chunk(cc, carry):
            c = nck - 1 - cc
            rows = pl.ds(pl.multiple_of(c * CHUNK, CHUNK), CHUNK)
            pq = pq_ref[rows, :]
            q, k, logf, f, sg, sq = _gates(pq, fz_ref[rows, :], lb_)
            v = pv_ref[rows, :]
            b = _tri_dot(tri, logf)
            o = o_ref[rows, :]
            dog_ = dog_ref[rows, :]
            pg = pg_ref[rows, :]
            spg = _sigmoid(pg)
            rinv = lax.rsqrt(jnp.mean(o * o, axis=-1, keepdims=True) + RMS_EPS)
            on = o * rinv
            dpg = dog_ * (on * ng) * (spg * (1.0 + pg * (1.0 - spg)))
            don = dog_ * (pg * spg)
            s_ref[0:1, :] += jnp.sum(don * on, axis=0, keepdims=True)
            dxn = don * ng
            do = rinv * (dxn - on * jnp.mean(dxn * on, axis=-1, keepdims=True))

            q_s[...] = q
            k_s[...] = k
            v_s[...] = v
            b_s[...] = b
            do_s[...] = do
            S0 = st_ref[c]
            dSv = dS[...]
            dSb = dSv.astype(BF16)
            eb = jnp.exp(b)
            bl = b_s[CHUNK - 1:CHUNK, :]
            ebl = jnp.exp(bl)
            ekk = jnp.exp(bl - b)
            qe = q * eb
            kk = k * ekk
            dob = do.astype(BF16)
            vb = v.astype(BF16)
            dq_s[...] = _dot(dob, S0.astype(BF16)) * eb
            dk_state = _dot(vb, dSb) * ekk
            dk_s[...] = dk_state
            dv_s[...] = _dot_nt(kk.astype(BF16), dSb)
            extra = jnp.sum(k * dk_state, axis=0, keepdims=True) + ebl * jnp.sum(S0 * dSv, axis=0, keepdims=True)
            dS[...] = dSv * ebl + _dot_tn(dob, qe.astype(BF16))

            for i in range(1, nsub):
                lo = i * SUB
                ref = b_s[lo - 1:lo, :]
                eq = jnp.exp(b_s[lo:lo + SUB, :] - ref)
                ek = jnp.exp(ref - b_s[0:lo, :])
                qt = (q_s[lo:lo + SUB, :] * eq).astype(BF16)
                kt = (k_s[0:lo, :] * ek).astype(BF16)
                doi = do_s[lo:lo + SUB, :].astype(BF16)
                vp = v_s[0:lo, :].astype(BF16)
                at = _dot_nt(kt, qt)
                dv_s[0:lo, :] += _dot(at.astype(BF16), doi)
                da = _dot_nt(doi, vp)
                dq_s[lo:lo + SUB, :] += _dot(da.astype(BF16), kt) * eq
                dat = _dot_nt(vp, doi)
                dk_s[0:lo, :] += _dot(dat.astype(BF16), qt) * ek

            for i in range(nsub):
                lo = i * SUB
                qi = q_s[lo:lo + SUB, :]
                bi = b_s[lo:lo + SUB, :]
                doi = do_s[lo:lo + SUB, :]
                for s in range(SUB):
                    m = _row_ge(s)
                    e = jnp.exp(jnp.where(m, bi - b_s[lo + s:lo + s + 1, :], 0.0))
                    ke = k_s[lo + s:lo + s + 1, :] * e
                    ke_s[s * SUB:(s + 1) * SUB, :] = ke
                    qe_s[s * SUB:(s + 1) * SUB, :] = qi * e
                    p_s[s * SUB:(s + 1) * SUB, :] = jnp.where(m, qi * ke, 0.0).astype(BF16)
                    p_s[(SUB + s) * SUB:(SUB + s + 1) * SUB, :] = jnp.where(
                        m, doi * v_s[lo + s:lo + s + 1, :], 0.0).astype(BF16)
                rs = _dot(p_s[...], ones)
                dqa = dq_s[lo:lo + SUB, :]
                for s in range(SUB):
                    acol = rs[s * SUB:(s + 1) * SUB, :]
                    dacol = rs[(SUB + s) * SUB:(SUB + s + 1) * SUB, :]
                    dqa = dqa + dacol * ke_s[s * SUB:(s + 1) * SUB, :]
                    dk_s[lo + s:lo + s + 1, :] += jnp.sum(dacol * qe_s[s * SUB:(s + 1) * SUB, :], axis=0, keepdims=True)
                    dv_s[lo + s:lo + s + 1, :] += jnp.sum(acol * doi, axis=0, keepdims=True)
                dq_s[lo:lo + SUB, :] = dqa

            dq = dq_s[...]
            dk = dk_s[...]
            db = q * dq - k * dk + jnp.where(last_row, extra, 0.0)
            dlogf = _tri_dot(triu, db)
            df = jnp.where(f > GATE_EPS, dlogf / jnp.maximum(f, GATE_EPS), 0.0) - dk
            s_ref[1:2, :] += jnp.sum(df * (1.0 - sg), axis=0, keepdims=True)
            dp_ref[0, rows, :] = (dq * (sq * (1.0 + pq * (1.0 - sq)))).astype(BF16)
            dp_ref[1, rows, :] = (df * (1.0 - lb_) * sg * (1.0 - sg)).astype(BF16)
            dp_ref[2, rows, :] = dv_s[...].astype(BF16)
            dp_ref[3, rows, :] = dpg.astype(BF16)
            return carry

        lax.fori_loop(0, nck, chunk, 0)

    def grp(gi):
        return pl.BlockSpec((None, rb, HEAD), lambda h, r: (gi, nb - 1 - r, h))

    rowsp = pl.BlockSpec((rb, HEAD), lambda h, r: (nb - 1 - r, h))
    vec = pl.BlockSpec((1, HEAD), lambda h, r: (0, h))
    return pl.pallas_call(
        body, name=name, grid=(H, nb),
        in_specs=[grp(0), grp(1), grp(2), grp(3), rowsp, rowsp,
                  pl.BlockSpec((None, nck, HEAD, HEAD), lambda h, r: (h, nb - 1 - r, 0, 0)), vec, vec],
        out_specs=[pl.BlockSpec((4, rb, HEAD), lambda h, r: (0, nb - 1 - r, h)),
                   pl.BlockSpec((8, HEAD), lambda h, r: (0, h))],
        out_shape=[jax.ShapeDtypeStruct((4, T, D), BF16), jax.ShapeDtypeStruct((8, D), F32)],
        scratch_shapes=[pltpu.VMEM((HEAD, HEAD), F32)] + [pltpu.VMEM((CHUNK, HEAD), F32)] * 8
        + [pltpu.VMEM((SUB * SUB, HEAD), F32)] * 2 + [pltpu.VMEM((2 * SUB * SUB, HEAD), BF16)],
        compiler_params=_params(("parallel", "arbitrary")),
    )(proj, proj, proj, proj, o, dog, states, lb, norm_g)


def lb_fwd(logits):
    def body(l_ref, o_ref):
        l = l_ref[...]
        mx = jnp.max(l, axis=0, keepdims=True)
        e = jnp.exp(l - mx)
        sm = e / jnp.sum(e, axis=0, keepdims=True)
        o_ref[0:1, :] = jnp.zeros_like(sm[0:1, :])
        o_ref[1:2, :] = sm[1:2, :]

    return pl.pallas_call(body, name="lb_fwd", out_shape=jax.ShapeDtypeStruct(logits.shape, F32))(logits)


def lb_bwd(logits, dlb):
    def body(l_ref, d_ref, o_ref):
        l = l_ref[...]
        mx = jnp.max(l, axis=0, keepdims=True)
        e = jnp.exp(l - mx)
        sm = e / jnp.sum(e, axis=0, keepdims=True)
        inner = d_ref[1:2, :] * sm[1:2, :]
        o_ref[0:1, :] = sm[0:1, :] * (0.0 - inner)
        o_ref[1:2, :] = sm[1:2, :] * (d_ref[1:2, :] - inner)

    return pl.pallas_call(body, name="lb_bwd", out_shape=jax.ShapeDtypeStruct(logits.shape, F32))(logits, dlb)


def conv_fwd(u, wdw, bdw, g, b, *, tm, name):
    _, T, D = u.shape
    hb = tm // HALO

    def body(a_ref, gt_ref, ap_ref, gp_ref, w_ref, bdw_ref, g_ref, b_ref, c_ref, v_ref, buf):
        i = pl.program_id(0)
        buf[HALO:HALO + tm, :] = a_ref[...] * _sigmoid(gt_ref[...])
        prev = ap_ref[...] * _sigmoid(gp_ref[...])
        buf[0:HALO, :] = jnp.where(i > 0, prev, 0.0)
        acc = jnp.zeros((tm, D), F32) + bdw_ref[...]
        for j in range(CONV_W):
            acc = acc + w_ref[j:j + 1, :] * buf[j + 2:j + 2 + tm, :]
        c_ref[...] = acc
        xhat, _ = _ln_stats(acc)
        y = xhat * g_ref[...] + b_ref[...]
        v_ref[...] = (y * _sigmoid(y)).astype(BF16)

    cur = lambda gi: pl.BlockSpec((None, tm, D), lambda i: (gi, i, 0))
    prv = lambda gi: pl.BlockSpec((None, HALO, D), lambda i: (gi, jnp.maximum(i * hb - 1, 0), 0))
    fix = lambda i: (0, 0)
    return pl.pallas_call(
        body, name=name, grid=(T // tm,),
        in_specs=[cur(0), cur(1), prv(0), prv(1), pl.BlockSpec((HALO, D), fix), pl.BlockSpec((1, D), fix),
                  pl.BlockSpec((1, D), fix), pl.BlockSpec((1, D), fix)],
        out_specs=[pl.BlockSpec((tm, D), lambda i: (i, 0)), pl.BlockSpec((tm, D), lambda i: (i, 0))],
        out_shape=[jax.ShapeDtypeStruct((T, D), F32), jax.ShapeDtypeStruct((T, D), BF16)],
        scratch_shapes=[pltpu.VMEM((tm + HALO, D), F32)],
        compiler_params=_params(("parallel",)),
    )(u, u, u, u, wdw, bdw, g, b)


def conv_bwd(dv2, c, u, wdw, g, b, *, tm, name):
    _, T, D = u.shape
    hb = tm // HALO
    nt = T // tm
    nh = T // HALO

    def body(dv_ref, c_ref, dvn_ref, cn_ref, a_ref, gt_ref, ap_ref, gp_ref, w_ref, g_ref, b_ref,
             du_ref, s_ref, gbuf, dbuf):
        i = pl.program_id(0)

        @pl.when(i == 0)
        def _():
            s_ref[...] = jnp.zeros_like(s_ref)

        gam = g_ref[...]
        bet = b_ref[...]

        def dconv(dv, cc):
            xhat, rstd = _ln_stats(cc)
            y = xhat * gam + bet
            sy = _sigmoid(y)
            dy = dv * (sy * (1.0 + y * (1.0 - sy)))
            return _ln_bwd(dy, xhat, rstd, gam), dy, xhat

        dc, dy, xhat = dconv(dv_ref[...], c_ref[...])
        dcn, _, _ = dconv(dvn_ref[...], cn_ref[...])
        dbuf[0:tm, :] = dc
        dbuf[tm:tm + HALO, :] = jnp.where(i < nt - 1, dcn, 0.0)
        a = a_ref[...]
        sgt = _sigmoid(gt_ref[...])
        gbuf[HALO:HALO + tm, :] = a * sgt
        gbuf[0:HALO, :] = jnp.where(i > 0, ap_ref[...] * _sigmoid(gp_ref[...]), 0.0)
        s_ref[32:33, :] += jnp.sum(dc, axis=0, keepdims=True)
        s_ref[33:34, :] += jnp.sum(dy * xhat, axis=0, keepdims=True)
        s_ref[34:35, :] += jnp.sum(dy, axis=0, keepdims=True)
        dglu = jnp.zeros((tm, D), F32)
        for j in range(CONV_W):
            dglu = dglu + w_ref[j:j + 1, :] * dbuf[30 - j:30 - j + tm, :]
            s_ref[j:j + 1, :] += jnp.sum(dc * gbuf[j + 2:j + 2 + tm, :], axis=0, keepdims=True)
        da = (dglu * sgt).astype(BF16)
        dg = (dglu * a * sgt * (1.0 - sgt)).astype(BF16)
        du_ref[0] = da
        du_ref[1] = dg
        s_ref[35:36, :] += jnp.sum(da.astype(F32), axis=0, keepdims=True)
        s_ref[36:37, :] += jnp.sum(dg.astype(F32), axis=0, keepdims=True)

    row = lambda i: (i, 0)
    nxt = lambda i: (jnp.minimum((i + 1) * hb, nh - 1), 0)
    cur = lambda gi: pl.BlockSpec((None, tm, D), lambda i: (gi, i, 0))
    prv = lambda gi: pl.BlockSpec((None, HALO, D), lambda i: (gi, jnp.maximum(i * hb - 1, 0), 0))
    fix = lambda i: (0, 0)
    return pl.pallas_call(
        body, name=name, grid=(nt,),
        in_specs=[pl.BlockSpec((tm, D), row), pl.BlockSpec((tm, D), row),
                  pl.BlockSpec((HALO, D), nxt), pl.BlockSpec((HALO, D), nxt),
                  cur(0), cur(1), prv(0), prv(1), pl.BlockSpec((HALO, D), fix),
                  pl.BlockSpec((1, D), fix), pl.BlockSpec((1, D), fix)],
        out_specs=[pl.BlockSpec((2, tm, D), lambda i: (0, i, 0)), pl.BlockSpec((40, D), fix)],
        out_shape=[jax.ShapeDtypeStruct((2, T, D), BF16), jax.ShapeDtypeStruct((40, D), F32)],
        scratch_shapes=[pltpu.VMEM((tm + HALO, D), F32), pltpu.VMEM((tm + HALO, D), F32)],
        compiler_params=_params(("arbitrary",)),
    )(dv2, c, dv2, c, u, u, u, u, wdw, g, b)


def loss_grad(y, target, *, tm):
    T, D = y.shape
    nt = T // tm

    def body(y_ref, t_ref, l_ref, d_ref, acc):
        i = pl.program_id(0)

        @pl.when(i == 0)
        def _():
            acc[...] = jnp.zeros_like(acc)

        e = y_ref[...] - t_ref[...]
        d_ref[...] = e * (1.0 / D)
        acc[...] += jnp.sum(e * e, axis=0, keepdims=True)

        @pl.when(i == nt - 1)
        def _():
            l_ref[...] = 0.5 * jnp.sum(acc[...], axis=1, keepdims=True) * (1.0 / D)

    row = lambda i: (i, 0)
    return pl.pallas_call(
        body, name="loss_grad", grid=(nt,),
        in_specs=[pl.BlockSpec((tm, D), row), pl.BlockSpec((tm, D), row)],
        out_specs=[pl.BlockSpec((1, 1), lambda i: (0, 0)), pl.BlockSpec((tm, D), row)],
        out_shape=[jax.ShapeDtypeStruct((1, 1), F32), jax.ShapeDtypeStruct((T, D), F32)],
        scratch_shapes=[pltpu.VMEM((1, D), F32)],
        compiler_params=_params(("arbitrary",)),
    )(y, target)


def _rows_block(R, C, budget=1 << 20):
    tr = R
    while tr * C * 4 > budget and tr % 16 == 0:
        tr //= 2
    return tr


def adamw(w, g, m, v, *, name):
    R, C = w.shape
    tr = _rows_block(R, C)

    def body(w_ref, g_ref, m_ref, v_ref, d_ref, mo_ref, vo_ref):
        g_ = g_ref[...]
        mn = ADAM_B1 * m_ref[...] + (1.0 - ADAM_B1) * g_
        vn = ADAM_B2 * v_ref[...] + (1.0 - ADAM_B2) * jnp.square(g_)
        m_hat = mn / (1.0 - ADAM_B1 ** ADAM_STEP)
        v_hat = vn / (1.0 - ADAM_B2 ** ADAM_STEP)
        d_ref[...] = -ADAM_LR * (m_hat / (jnp.sqrt(v_hat) + ADAM_EPS) + ADAM_WD * w_ref[...])
        mo_ref[...] = mn
        vo_ref[...] = vn

    spec = pl.BlockSpec((tr, C), lambda i: (i, 0))
    sd = jax.ShapeDtypeStruct((R, C), F32)
    return pl.pallas_call(
        body, name=name, grid=(R // tr,), in_specs=[spec] * 4, out_specs=[spec] * 3, out_shape=[sd] * 3,
        compiler_params=_params(("parallel",)),
    )(w, g, m, v)


def add_n(xs, *, name):
    R, C = xs[0].shape
    tr = _rows_block(R, C)
    n = len(xs)

    def body(*refs):
        acc = refs[0][...]
        for r in refs[1:n]:
            acc = acc + r[...]
        refs[n][...] = acc

    spec = pl.BlockSpec((tr, C), lambda i: (i, 0))
    return pl.pallas_call(
        body, name=name, grid=(R // tr,), in_specs=[spec] * n, out_specs=spec,
        out_shape=jax.ShapeDtypeStruct((R, C), F32), compiler_params=_params(("parallel",)),
    )(*xs)


def _place():
    x, y, c = lax.axis_index("x"), lax.axis_index("y"), lax.axis_index("c")
    return x, y, c


def _other_chips(x, y):
    return [(1 - x, y), (x, 1 - y), (1 - x, 1 - y)]


ANY = pl.BlockSpec(memory_space=pl.ANY)


def gather_chips(arrs):
    n = len(arrs)

    def body(*refs):
        ins, outs = refs[:n], refs[n:2 * n]
        send, recv, loc = refs[2 * n:]
        x, y, c = _place()
        me = 2 * x + y
        local = [pltpu.make_async_copy(ins[a], outs[a].at[me], loc.at[a]) for a in range(n)]
        for cp in local:
            cp.start()
        remote = []
        for a in range(n):
            for j, (px, py) in enumerate(_other_chips(x, y)):
                remote.append(pltpu.make_async_remote_copy(
                    src_ref=ins[a], dst_ref=outs[a].at[me], send_sem=send.at[3 * a + j], recv_sem=recv.at[3 * a + j],
                    device_id=(px, py, c), device_id_type=MESH))
        for cp in remote:
            cp.start()
        for cp in remote:
            cp.wait()
        for cp in local:
            cp.wait()

    return pl.pallas_call(
        body, name="gather_chips", in_specs=[ANY] * n, out_specs=[ANY] * n,
        out_shape=[jax.ShapeDtypeStruct((4,) + a.shape, a.dtype) for a in arrs],
        scratch_shapes=[pltpu.SemaphoreType.DMA((3 * n,)), pltpu.SemaphoreType.DMA((3 * n,)),
                        pltpu.SemaphoreType.DMA((n,))],
        compiler_params=pltpu.CompilerParams(has_side_effects=True),
    )(*arrs)


def sibling_swap_halves(arrs):
    n = len(arrs)

    def body(*refs):
        ins, outs = refs[:n], refs[n:2 * n]
        send, recv = refs[2 * n:]
        x, y, c = _place()
        cps = []
        for a in range(n):
            lh = arrs[a].shape[0] // 2
            cps.append(pltpu.make_async_remote_copy(
                src_ref=ins[a].at[pl.ds((1 - c) * lh, lh)], dst_ref=outs[a], send_sem=send.at[a], recv_sem=recv.at[a],
                device_id=(x, y, 1 - c), device_id_type=MESH))
        for cp in cps:
            cp.start()
        for cp in cps:
            cp.wait()

    return pl.pallas_call(
        body, name="sibling_swap_halves", in_specs=[ANY] * n, out_specs=[ANY] * n,
        out_shape=[jax.ShapeDtypeStruct((a.shape[0] // 2,) + a.shape[1:], a.dtype) for a in arrs],
        scratch_shapes=[pltpu.SemaphoreType.DMA((n,)), pltpu.SemaphoreType.DMA((n,))],
        compiler_params=pltpu.CompilerParams(has_side_effects=True),
    )(*arrs)


def scatter_chips(arrs):
    n = len(arrs)

    def body(*refs):
        ins, outs = refs[:n], refs[n:2 * n]
        send, recv, loc = refs[2 * n:]
        x, y, c = _place()
        me = 2 * x + y
        local = [pltpu.make_async_copy(ins[a].at[me], outs[a].at[me], loc.at[a]) for a in range(n)]
        for cp in local:
            cp.start()
        remote = []
        for a in range(n):
            for j, (px, py) in enumerate(_other_chips(x, y)):
                remote.append(pltpu.make_async_remote_copy(
                    src_ref=ins[a].at[2 * px + py], dst_ref=outs[a].at[me], send_sem=send.at[3 * a + j],
                    recv_sem=recv.at[3 * a + j], device_id=(px, py, c), device_id_type=MESH))
        for cp in remote:
            cp.start()
        for cp in remote:
            cp.wait()
        for cp in local:
            cp.wait()

    return pl.pallas_call(
        body, name="scatter_chips", in_specs=[ANY] * n, out_specs=[ANY] * n,
        out_shape=[jax.ShapeDtypeStruct(a.shape, a.dtype) for a in arrs],
        scratch_shapes=[pltpu.SemaphoreType.DMA((3 * n,)), pltpu.SemaphoreType.DMA((3 * n,)),
                        pltpu.SemaphoreType.DMA((n,))],
        compiler_params=pltpu.CompilerParams(has_side_effects=True),
    )(*arrs)


def sibling_join_halves(arrs):
    n = len(arrs)

    def body(*refs):
        ins, outs = refs[:n], refs[n:2 * n]
        send, recv, loc = refs[2 * n:]
        x, y, c = _place()
        local, remote = [], []
        for a in range(n):
            lh = arrs[a].shape[0]
            mine = outs[a].at[pl.ds(c * lh, lh)]
            local.append(pltpu.make_async_copy(ins[a], mine, loc.at[a]))
            remote.append(pltpu.make_async_remote_copy(
                src_ref=ins[a], dst_ref=mine, send_sem=send.at[a], recv_sem=recv.at[a],
                device_id=(x, y, 1 - c), device_id_type=MESH))
        for cp in local + remote:
            cp.start()
        for cp in remote:
            cp.wait()
        for cp in local:
            cp.wait()

    return pl.pallas_call(
        body, name="sibling_join_halves", in_specs=[ANY] * n, out_specs=[ANY] * n,
        out_shape=[jax.ShapeDtypeStruct((2 * a.shape[0],) + a.shape[1:], a.dtype) for a in arrs],
        scratch_shapes=[pltpu.SemaphoreType.DMA((n,)), pltpu.SemaphoreType.DMA((n,)), pltpu.SemaphoreType.DMA((n,))],
        compiler_params=pltpu.CompilerParams(has_side_effects=True),
    )(*arrs)


def all_reduce_small(v):
    R, C = v.shape

    def body(v_ref, o_ref, slots, send, recv):
        x, y, c = _place()
        me = 4 * x + 2 * y + c
        slots[me] = v_ref[...]
        cps = []
        k = 0
        for fx in (0, 1):
            for fy in (0, 1):
                for fc in (0, 1):
                    if fx or fy or fc:
                        cps.append(pltpu.make_async_remote_copy(
                            src_ref=v_ref, dst_ref=slots.at[me], send_sem=send.at[k], recv_sem=recv.at[k],
                            device_id=(x ^ fx, y ^ fy, c ^ fc), device_id_type=MESH))
                        k += 1
        for cp in cps:
            cp.start()
        for cp in cps:
            cp.wait()
        acc = slots[0]
        for d in range(1, 8):
            acc = acc + slots[d]
        o_ref[...] = acc

    vm = pl.BlockSpec(memory_space=pltpu.VMEM)
    return pl.pallas_call(
        body, name="all_reduce_small", in_specs=[vm], out_specs=vm,
        out_shape=jax.ShapeDtypeStruct((R, C), F32),
        scratch_shapes=[pltpu.VMEM((8, R, C), F32), pltpu.SemaphoreType.DMA((7,)), pltpu.SemaphoreType.DMA((7,))],
        compiler_params=pltpu.CompilerParams(has_side_effects=True, vmem_limit_bytes=VMEM_LIMIT),
    )(v)


def _tile(T, want):
    return min(T, want)


def local_step(x, target, p):
    T, D = x.shape
    tm = _tile(T, 512)
    tmc = _tile(T, 256)
    rb = _tile(T, 512)
    F = p["w2"].shape[1]
    tf = min(F // p["w1"].shape[1], 1024)
    zeros_bias = jnp.zeros((1, D), F32)
    lb_all = lb_fwd(p["a_lb_logits"])

    saved = []
    h = x
    for i in range(DEPTH):
        j = i // 2
        s = {"x": h}
        if i % 2 == 0:
            s["proj"] = mm_groups(h, p["a_w_in"][j], jnp.zeros((4, 1, D), F32), tm=tm, name="a_in_proj")
            s["o"], s["og"], s["st"] = hgrn_fwd(s["proj"], lb_all[j:j + 1], p["a_norm_g"][j:j + 1], rb=rb, name="hgrn_fwd")
            s["r1"], s["x1"] = mm_res_ln(s["og"], p["a_w_out"][j], zeros_bias, h, p["ln_mix_g"][i:i + 1],
                                         p["ln_mix_b"][i:i + 1], tm=tm, name="a_out_ln")
        else:
            s["u"] = mm_groups(h, p["b_w_pw1"][j], p["b_b_pw1"][j], tm=tm, name="b_pw1")
            s["c"], s["v2"] = conv_fwd(s["u"], p["b_w_dw"][j], p["b_b_dw"][j:j + 1], p["b_ln_g"][j:j + 1],
                                       p["b_ln_b"][j:j + 1], tm=tmc, name="conv_fwd")
            s["r1"], s["x1"] = mm_res_ln(s["v2"], p["b_w_pw2"][j], p["b_b_pw2"][j:j + 1], h, p["ln_mix_g"][i:i + 1],
                                         p["ln_mix_b"][i:i + 1], tm=tm, name="b_pw2_ln")
        s["z"], s["r2"], h = ffn_fwd(s["x1"], p["w1"][i], p["w2"][i], p["ln_ffn_g"][i:i + 1], p["ln_ffn_b"][i:i + 1],
                                     tm=tm, tf=tf, name="ffn_fwd")
        saved.append(s)

    loss_part, dh = loss_grad(h, target, tm=tm)

    gr = {k: [None] * DEPTH for k in ("ln_mix_g", "ln_mix_b", "ln_ffn_g", "ln_ffn_b", "w1", "w2")}
    for k in ("a_w_in", "a_w_out", "a_norm_g", "a_dlb", "b_w_pw1", "b_w_pw2", "b_small"):
        gr[k] = [None] * 2
    for i in reversed(range(DEPTH)):
        j = i // 2
        s = saved[i]
        dz, dx1, drb2, sums2 = ffn_bwd_dx(dh, s["r2"], p["ln_ffn_g"][i:i + 1], s["z"], p["w1"][i], p["w2"][i],
                                          tm=tm, tf=tf, name="ffn_bwd_dx")
        gr["ln_ffn_g"][i], gr["ln_ffn_b"][i] = sums2[0], sums2[1]
        tkn = F // 4
        gr["w1"][i] = mm_tn(s["x1"], dz[None], tm=tm, tk=D, tn=tkn, name="ffn_dw1")[0]
        gr["w2"][i] = mm_tn(s["z"], drb2[None], tm=tm, tk=tkn, tn=D, relu2=True, name="ffn_dw2")[0, 0]
        wmix = p["a_w_out"][j] if i % 2 == 0 else p["b_w_pw2"][j]
        dr1, drb1, dmo, sums1 = ln_bwd_mm(dx1, s["r1"], p["ln_mix_g"][i:i + 1], wmix, tm=tm, name="mix_ln_bwd")
        gr["ln_mix_g"][i], gr["ln_mix_b"][i] = sums1[0], sums1[1]
        if i % 2 == 0:
            gr["a_w_out"][j] = mm_tn(s["og"], drb1[None], tm=tm, tk=D, tn=D, name="a_dw_out")[0, 0]
            dproj, hs = hgrn_bwd(s["proj"], s["o"], dmo, s["st"], lb_all[j:j + 1], p["a_norm_g"][j:j + 1], rb=rb,
                                 name="hgrn_bwd")
            gr["a_norm_g"][j], gr["a_dlb"][j] = hs[0], hs[1]
            gr["a_w_in"][j] = mm_tn(s["x"], dproj, tm=tm, tk=D, tn=D, name="a_dw_in")[:, 0]
            dh = mm_nt_acc(dproj, p["a_w_in"][j], dr1, tm=tm, name="a_dx")
        else:
            gr["b_w_pw2"][j] = mm_tn(s["v2"], drb1[None], tm=tm, tk=D, tn=D, name="b_dw_pw2")[0, 0]
            du, cs = conv_bwd(dmo, s["c"], s["u"], p["b_w_dw"][j], p["b_ln_g"][j:j + 1], p["b_ln_b"][j:j + 1],
                              tm=tmc, name="conv_bwd")
            gr["b_small"][j] = (cs, sums1[2])
            gr["b_w_pw1"][j] = mm_tn(s["x"], du, tm=tm, tk=D, tn=D, name="b_dw_pw1")[:, 0]
            dh = mm_nt_acc(du, p["b_w_pw1"][j], dr1, tm=tm, name="b_dx")

    g = {}
    for k in ("ln_mix_g", "ln_mix_b", "ln_ffn_g", "ln_ffn_b", "w1", "w2", "a_w_in", "a_w_out", "a_norm_g",
              "b_w_pw1", "b_w_pw2"):
        g[k] = jnp.stack(gr[k])
    g["a_lb_logits"] = lb_bwd(p["a_lb_logits"], jnp.stack(gr["a_dlb"]))
    g["b_w_dw"] = jnp.stack([cs[0:CONV_W] for cs, _ in gr["b_small"]])
    g["b_b_dw"] = jnp.stack([cs[32] for cs, _ in gr["b_small"]])
    g["b_ln_g"] = jnp.stack([cs[33] for cs, _ in gr["b_small"]])
    g["b_ln_b"] = jnp.stack([cs[34] for cs, _ in gr["b_small"]])
    g["b_b_pw1"] = jnp.stack([cs[35:37] for cs, _ in gr["b_small"]])
    g["b_b_pw2"] = jnp.stack([bs for _, bs in gr["b_small"]])
    return loss_part, dh, g


def _adam_nd(w, g, m, v, name):
    shp = w.shape
    c = shp[-1]
    f2 = lambda a: a.reshape(-1, c)
    d, mn, vn = adamw(f2(w), f2(g), f2(m), f2(v), name=name)
    return d.reshape(shp), mn.reshape(shp), vn.reshape(shp)


def kernel(x, ln_mix_g, ln_mix_b, ln_ffn_g, ln_ffn_b, ffn_w1, ffn_w2, a_w_in, a_lb_logits, a_norm_g, a_w_out, b_w_pw1, b_b_pw1, b_w_dw, b_b_dw, b_ln_g, b_ln_b, b_w_pw2, b_b_pw2, loss_target, m_ln_mix_g, m_ln_mix_b, m_ln_ffn_g, m_ln_ffn_b, m_ffn_w1, m_ffn_w2, m_a_w_in, m_a_lb_logits, m_a_norm_g, m_a_w_out, m_b_w_pw1, m_b_b_pw1, m_b_w_dw, m_b_b_dw, m_b_ln_g, m_b_ln_b, m_b_w_pw2, m_b_b_pw2, v_ln_mix_g, v_ln_mix_b, v_ln_ffn_g, v_ln_ffn_b, v_ffn_w1, v_ffn_w2, v_a_w_in, v_a_lb_logits, v_a_norm_g, v_a_w_out, v_b_w_pw1, v_b_b_pw1, v_b_w_dw, v_b_b_dw, v_b_ln_g, v_b_ln_b, v_b_w_pw2, v_b_b_pw2):
    names = ["ln_mix_g", "ln_mix_b", "ln_ffn_g", "ln_ffn_b", "ffn_w1", "ffn_w2", "a_w_in", "a_lb_logits", "a_norm_g",
             "a_w_out", "b_w_pw1", "b_b_pw1", "b_w_dw", "b_b_dw", "b_ln_g", "b_ln_b", "b_w_pw2", "b_b_pw2"]
    w = dict(zip(names, [ln_mix_g, ln_mix_b, ln_ffn_g, ln_ffn_b, ffn_w1, ffn_w2, a_w_in, a_lb_logits, a_norm_g, a_w_out,
                         b_w_pw1, b_b_pw1, b_w_dw, b_b_dw, b_ln_g, b_ln_b, b_w_pw2, b_b_pw2]))
    m = dict(zip(names, [m_ln_mix_g, m_ln_mix_b, m_ln_ffn_g, m_ln_ffn_b, m_ffn_w1, m_ffn_w2, m_a_w_in, m_a_lb_logits,
                         m_a_norm_g, m_a_w_out, m_b_w_pw1, m_b_b_pw1, m_b_w_dw, m_b_b_dw, m_b_ln_g, m_b_ln_b, m_b_w_pw2,
                         m_b_b_pw2]))
    v = dict(zip(names, [v_ln_mix_g, v_ln_mix_b, v_ln_ffn_g, v_ln_ffn_b, v_ffn_w1, v_ffn_w2, v_a_w_in, v_a_lb_logits,
                         v_a_norm_g, v_a_w_out, v_b_w_pw1, v_b_b_pw1, v_b_w_dw, v_b_b_dw, v_b_ln_g, v_b_ln_b, v_b_w_pw2,
                         v_b_b_pw2]))
    T, D = x.shape[1], x.shape[2]
    chip = 2 * lax.axis_index("x") + lax.axis_index("y")

    big = ["ffn_w1", "ffn_w2", "a_w_in", "a_w_out", "b_w_pw1", "b_w_pw2"]
    vecs = ["b_b_pw1", "b_w_dw", "b_b_dw", "b_ln_g", "b_ln_b", "b_b_pw2"]
    gathered = gather_chips([w[k].astype(BF16) for k in big] + [w[k] for k in vecs])
    gw = dict(zip(big + vecs, gathered))
    DS = D // 4
    p = {k: w[k] for k in ("ln_mix_g", "ln_mix_b", "ln_ffn_g", "ln_ffn_b", "a_lb_logits", "a_norm_g")}
    p["w1"] = jnp.swapaxes(gw["ffn_w1"], 0, 1)
    p["w2"] = jnp.swapaxes(gw["ffn_w2"], 0, 1).reshape(DEPTH, -1, D)
    p["a_w_in"] = jnp.swapaxes(gw["a_w_in"], 0, 1)
    p["a_w_out"] = jnp.swapaxes(gw["a_w_out"], 0, 1).reshape(2, D, D)
    pw1 = jnp.swapaxes(gw["b_w_pw1"], 0, 1)
    p["b_w_pw1"] = jnp.transpose(pw1.reshape(2, 2, 2, D, D // 2), (0, 1, 3, 2, 4)).reshape(2, 2, D, D)
    p["b_w_pw2"] = jnp.swapaxes(gw["b_w_pw2"], 0, 1).reshape(2, D, D)
    p["b_b_pw1"] = jnp.swapaxes(gw["b_b_pw1"], 0, 1).reshape(2, 2, 1, D)
    dwf = jnp.transpose(gw["b_w_dw"], (1, 2, 0, 3)).reshape(2, CONV_W, D)
    p["b_w_dw"] = jnp.pad(dwf, ((0, 0), (0, HALO - CONV_W), (0, 0)))
    for k in ("b_b_dw", "b_ln_g", "b_ln_b", "b_b_pw2"):
        p[k] = jnp.swapaxes(gw[k], 0, 1).reshape(2, D)

    loss_part, dx, g = local_step(x[0], loss_target[0], p)
    loss = lax.psum(loss_part[0, 0], ("x", "y", "c"))
    grad_x = dx[None]

    NC = g["w1"].shape[1]
    part = {
        "ffn_w1": g["w1"],
        "ffn_w2": g["w2"].reshape(DEPTH, 4, -1, D),
        "a_w_in": g["a_w_in"],
        "a_w_out": g["a_w_out"].reshape(2, 4, DS, D),
        "b_w_pw1": jnp.transpose(g["b_w_pw1"].reshape(2, 2, D, 2, D // 2), (0, 1, 3, 2, 4)).reshape(2, 4, D, D // 2),
        "b_w_pw2": g["b_w_pw2"].reshape(2, 4, DS, D),
    }
    del NC
    full = [part[k] for k in big]
    got = sibling_swap_halves(full)
    c_idx = lax.axis_index("c")
    chip_sums = []
    for a, r in zip(full, got):
        lh = a.shape[0] // 2
        mine = lax.dynamic_slice_in_dim(a, c_idx * lh, lh, axis=0)
        cc = a.shape[-1]
        chip_sums.append(add_n([mine.reshape(-1, cc), r.reshape(-1, cc)], name="add_sibling").reshape(r.shape))
    pieces = scatter_chips([jnp.swapaxes(a, 0, 1) for a in chip_sums])
    halves = []
    for q in pieces:
        cc = q.shape[-1]
        halves.append(add_n([q[k].reshape(-1, cc) for k in range(4)], name="add_chips").reshape(q.shape[1:]))
    reduced = dict(zip(big, sibling_join_halves(halves)))

    small_names = ["ln_mix_g", "ln_mix_b", "ln_ffn_g", "ln_ffn_b", "a_lb_logits", "a_norm_g",
                   "b_b_pw1", "b_w_dw", "b_b_dw", "b_ln_g", "b_ln_b", "b_b_pw2"]
    rows = [g[k].reshape(-1, D) for k in small_names]
    counts = [r.shape[0] for r in rows]
    tot = sum(counts)
    pad = (-tot) % 8
    packed = jnp.concatenate(rows + ([jnp.zeros((pad, D), F32)] if pad else []), axis=0)
    summed = all_reduce_small(packed)
    sm = {}
    off = 0
    for k, n in zip(small_names, counts):
        sm[k] = summed[off:off + n]
        off += n

    def shard_cols(a):
        return lax.dynamic_slice_in_dim(a, chip * DS, DS, axis=a.ndim - 1)

    grads = dict(reduced)
    for k in ("ln_mix_g", "ln_mix_b", "ln_ffn_g", "ln_ffn_b", "a_lb_logits", "a_norm_g"):
        grads[k] = sm[k]
    grads["b_b_pw1"] = lax.dynamic_slice_in_dim(sm["b_b_pw1"].reshape(2, 2 * D), chip * (D // 2), D // 2, axis=1)
    grads["b_w_dw"] = shard_cols(sm["b_w_dw"].reshape(2, CONV_W, D))
    for k in ("b_b_dw", "b_ln_g", "b_ln_b", "b_b_pw2"):
        grads[k] = shard_cols(sm[k])

    delta, new_m, new_v = {}, {}, {}
    for k in names:
        delta[k], new_m[k], new_v[k] = _adam_nd(w[k], grads[k], m[k], v[k], "adamw_" + k)
    return (loss, grad_x, *[grads[k] for k in names], *[delta[k] for k in names],
            *[new_m[k] for k in names], *[new_v[k] for k in names])
```

```python
import jax
import jax.numpy as jnp
from jax import lax
from jax.experimental import pallas as pl
from jax.experimental.pallas import tpu as pltpu

F32 = jnp.float32
BF16 = jnp.bfloat16
MESH = pl.DeviceIdType.MESH

DEPTH = 4
ALPHA = (2.0 * DEPTH) ** 0.25
LN_EPS = 1e-5
RMS_EPS = 1e-6
GATE_EPS = 1e-6
HEAD = 128
CHUNK = 64
SUB = 16
HEADS_PER_STEP = 2
CHUNKS_PER_TRIP = 2
CONV_W = 31
HALO = 32
VEC_ROWS = 40
ADAM_LR, ADAM_B1, ADAM_B2, ADAM_EPS, ADAM_WD, ADAM_STEP = 0.001, 0.9, 0.999, 1e-08, 0.01, 10
VMEM_LIMIT = 56 * 1024 * 1024
ANY = pl.BlockSpec(memory_space=pl.ANY)


def _dot(a, b):
    return jnp.dot(a, b, preferred_element_type=F32)


def _dot_nt(a, b):
    return lax.dot_general(a, b, (((1,), (1,)), ((), ())), preferred_element_type=F32)


def _dot_tn(a, b):
    return lax.dot_general(a, b, (((0,), (0,)), ((), ())), preferred_element_type=F32)


def _sigmoid(x):
    return 1.0 / (1.0 + jnp.exp(-x))


def _ln_stats(r):
    mu = jnp.mean(r, axis=-1, keepdims=True)
    xc = r - mu
    var = jnp.mean(xc * xc, axis=-1, keepdims=True)
    rstd = lax.rsqrt(var + LN_EPS)
    return xc * rstd, rstd


def _ln_bwd(dy, xhat, rstd, g):
    dyg = dy * g
    m1 = jnp.mean(dyg, axis=-1, keepdims=True)
    m2 = jnp.mean(dyg * xhat, axis=-1, keepdims=True)
    return rstd * (dyg - m1 - xhat * m2)


def _place():
    return lax.axis_index("x"), lax.axis_index("y"), lax.axis_index("c")


class GatherChips:
    def __init__(self, arrs):
        self.ins = list(arrs)
        n = len(arrs)
        self.out_shapes = [jax.ShapeDtypeStruct((4,) + a.shape, a.dtype) for a in arrs]
        self.sems = [pltpu.SemaphoreType.DMA((3 * n,)), pltpu.SemaphoreType.DMA((3 * n,)),
                     pltpu.SemaphoreType.DMA((n,))]

    def copies(self, ins, outs, send, recv, loc):
        x, y, c = _place()
        me = 2 * x + y
        local, remote = [], []
        for a in range(len(ins)):
            local.append(pltpu.make_async_copy(ins[a], outs[a].at[me], loc.at[a]))
            for j, (px, py) in enumerate([(1 - x, y), (x, 1 - y), (1 - x, 1 - y)]):
                remote.append(pltpu.make_async_remote_copy(
                    src_ref=ins[a], dst_ref=outs[a].at[me], send_sem=send.at[3 * a + j], recv_sem=recv.at[3 * a + j],
                    device_id=(px, py, c), device_id_type=MESH))
        return local + remote


class ScatterPieces:
    def __init__(self, arrs):
        self.ins = list(arrs)
        n = len(arrs)
        self.out_shapes = [jax.ShapeDtypeStruct((8,) + a.shape[1:], a.dtype) for a in arrs]
        self.sems = [pltpu.SemaphoreType.DMA((7 * n,)), pltpu.SemaphoreType.DMA((7 * n,)),
                     pltpu.SemaphoreType.DMA((n,))]

    def copies(self, ins, outs, send, recv, loc):
        x, y, c = _place()
        me = 4 * x + 2 * y + c
        local, remote = [], []
        for a in range(len(ins)):
            local.append(pltpu.make_async_copy(ins[a].at[2 * x + y], outs[a].at[me], loc.at[a]))
            k = 0
            for fx in (0, 1):
                for fy in (0, 1):
                    for fc in (0, 1):
                        if fx or fy or fc:
                            tx, ty = x ^ fx, y ^ fy
                            remote.append(pltpu.make_async_remote_copy(
                                src_ref=ins[a].at[2 * tx + ty], dst_ref=outs[a].at[me],
                                send_sem=send.at[7 * a + k], recv_sem=recv.at[7 * a + k],
                                device_id=(tx, ty, c ^ fc), device_id_type=MESH))
                            k += 1
        return local + remote


def carried_call(body, comm, *, name, grid, in_specs, out_specs, out_shape, scratch_shapes, args):
    sem = ("arbitrary",) * len(grid)
    params = pltpu.CompilerParams(dimension_semantics=sem, vmem_limit_bytes=VMEM_LIMIT)
    if comm is None:
        res = pl.pallas_call(body, name=name, grid=grid, in_specs=in_specs, out_specs=out_specs, out_shape=out_shape,
                             scratch_shapes=scratch_shapes, compiler_params=params)(*args)
        return res, []
    ni, no, nscr = len(in_specs), len(out_specs), len(scratch_shapes)
    ci, co = len(comm.ins), len(comm.out_shapes)

    def both(*refs):
        ins, refs = refs[:ni], refs[ni:]
        cins, refs = refs[:ci], refs[ci:]
        outs, refs = refs[:no], refs[no:]
        couts, refs = refs[:co], refs[co:]
        scr, sems = refs[:nscr], refs[nscr:]
        first = pl.program_id(0) == 0
        last = pl.program_id(0) == grid[0] - 1
        for d in range(1, len(grid)):
            first = first & (pl.program_id(d) == 0)
            last = last & (pl.program_id(d) == grid[d] - 1)

        @pl.when(first)
        def _():
            for cp in comm.copies(cins, couts, *sems):
                cp.start()

        body(*ins, *outs, *scr)

        @pl.when(last)
        def _():
            for cp in comm.copies(cins, couts, *sems):
                cp.wait()

    res = pl.pallas_call(
        both, name=name, grid=grid, in_specs=list(in_specs) + [ANY] * ci, out_specs=list(out_specs) + [ANY] * co,
        out_shape=list(out_shape) + comm.out_shapes, scratch_shapes=list(scratch_shapes) + comm.sems,
        compiler_params=params)(*args, *comm.ins)
    return res[:no], res[no:]


def comm_call(comm, *, name):
    ci, co = len(comm.ins), len(comm.out_shapes)

    def body(*refs):
        cps = comm.copies(refs[:ci], refs[ci:ci + co], *refs[ci + co:])
        for cp in cps:
            cp.start()
        for cp in cps:
            cp.wait()

    return pl.pallas_call(body, name=name, in_specs=[ANY] * ci, out_specs=[ANY] * co, out_shape=comm.out_shapes,
                          scratch_shapes=comm.sems, compiler_params=pltpu.CompilerParams(has_side_effects=True))(*comm.ins)


def all_reduce_small(v):
    R, C = v.shape

    def body(v_ref, o_ref, slots, send, recv):
        x, y, c = _place()
        me = 4 * x + 2 * y + c
        slots[me] = v_ref[...]
        cps = []
        k = 0
        for fx in (0, 1):
            for fy in (0, 1):
                for fc in (0, 1):
                    if fx or fy or fc:
                        cps.append(pltpu.make_async_remote_copy(
                            src_ref=v_ref, dst_ref=slots.at[me], send_sem=send.at[k], recv_sem=recv.at[k],
                            device_id=(x ^ fx, y ^ fy, c ^ fc), device_id_type=MESH))
                        k += 1
        for cp in cps:
            cp.start()
        for cp in cps:
            cp.wait()
        acc = slots[0]
        for d in range(1, 8):
            acc = acc + slots[d]
        o_ref[...] = acc

    vm = pl.BlockSpec(memory_space=pltpu.VMEM)
    return pl.pallas_call(
        body, name="all_reduce_small", in_specs=[vm], out_specs=vm,
        out_shape=jax.ShapeDtypeStruct((R, C), F32),
        scratch_shapes=[pltpu.VMEM((8, R, C), F32), pltpu.SemaphoreType.DMA((7,)), pltpu.SemaphoreType.DMA((7,))],
        compiler_params=pltpu.CompilerParams(has_side_effects=True, vmem_limit_bytes=VMEM_LIMIT),
    )(v)


def _call(body, *, name, grid, in_specs, out_specs, out_shape, scratch_shapes=(), args):
    res, _ = carried_call(body, None, name=name, grid=grid, in_specs=in_specs, out_specs=out_specs,
                          out_shape=out_shape, scratch_shapes=list(scratch_shapes), args=args)
    return res


def mm_groups(a, w, bias, *, tm, name):
    T, K = a.shape
    G, _, N = w.shape

    def body(a_ref, w_ref, b_ref, o_ref):
        o_ref[...] = _dot(a_ref[...].astype(BF16), w_ref[...]) + b_ref[...]

    return _call(
        body, name=name, grid=(G, T // tm),
        in_specs=[pl.BlockSpec((tm, K), lambda g, i: (i, 0)),
                  pl.BlockSpec((None, K, N), lambda g, i: (g, 0, 0)),
                  pl.BlockSpec((None, 1, N), lambda g, i: (g, 0, 0))],
        out_specs=[pl.BlockSpec((None, tm, N), lambda g, i: (g, i, 0))],
        out_shape=[jax.ShapeDtypeStruct((G, T, N), F32)], args=(a, w, bias))[0]


def mm_res_ln(a, w, bias, res, g, b, *, tm, name):
    T, K = a.shape
    N = w.shape[1]

    def body(a_ref, w_ref, bias_ref, res_ref, g_ref, b_ref, r_ref, y_ref):
        r = ALPHA * res_ref[...] + _dot(a_ref[...], w_ref[...]) + bias_ref[...]
        r_ref[...] = r
        xhat, _ = _ln_stats(r)
        y_ref[...] = xhat * g_ref[...] + b_ref[...]

    row = lambda i: (i, 0)
    fix = lambda i: (0, 0)
    return _call(
        body, name=name, grid=(T // tm,),
        in_specs=[pl.BlockSpec((tm, K), row), pl.BlockSpec((K, N), fix), pl.BlockSpec((1, N), fix),
                  pl.BlockSpec((tm, N), row), pl.BlockSpec((1, N), fix), pl.BlockSpec((1, N), fix)],
        out_specs=[pl.BlockSpec((tm, N), row), pl.BlockSpec((tm, N), row)],
        out_shape=[jax.ShapeDtypeStruct((T, N), F32), jax.ShapeDtypeStruct((T, N), F32)],
        args=(a, w, bias, res, g, b))


def ffn_fwd(x, w1, w2, g, b, *, tm, tf, name, comm=None):
    T, D = x.shape
    NC, _, FC = w1.shape
    F = NC * FC
    per = FC // tf
    nf = F // tf

    def body(x_ref, w1_ref, w2_ref, g_ref, b_ref, z_ref, r_ref, y_ref, acc_ref, xb_ref):
        f = pl.program_id(1)

        @pl.when(f == 0)
        def _():
            acc_ref[...] = jnp.zeros_like(acc_ref)
            xb_ref[...] = x_ref[...].astype(BF16)

        z = _dot(xb_ref[...], w1_ref[...])
        z_ref[...] = z.astype(BF16)
        h = jnp.square(jnp.maximum(z, 0.0)).astype(BF16)
        acc_ref[...] += _dot(h, w2_ref[...])

        @pl.when(f == nf - 1)
        def _():
            r = ALPHA * x_ref[...] + acc_ref[...]
            r_ref[...] = r
            xhat, _ = _ln_stats(r)
            y_ref[...] = xhat * g_ref[...] + b_ref[...]

    return carried_call(
        body, comm, name=name, grid=(T // tm, nf),
        in_specs=[pl.BlockSpec((tm, D), lambda i, f: (i, 0)),
                  pl.BlockSpec((None, D, tf), lambda i, f: (f // per, 0, f % per)),
                  pl.BlockSpec((tf, D), lambda i, f: (f, 0)),
                  pl.BlockSpec((1, D), lambda i, f: (0, 0)),
                  pl.BlockSpec((1, D), lambda i, f: (0, 0))],
        out_specs=[pl.BlockSpec((tm, tf), lambda i, f: (i, f)),
                   pl.BlockSpec((tm, D), lambda i, f: (i, 0)),
                   pl.BlockSpec((tm, D), lambda i, f: (i, 0))],
        out_shape=[jax.ShapeDtypeStruct((T, F), BF16), jax.ShapeDtypeStruct((T, D), F32),
                   jax.ShapeDtypeStruct((T, D), F32)],
        scratch_shapes=[pltpu.VMEM((tm, D), F32), pltpu.VMEM((tm, D), BF16)],
        args=(x, w1, w2, g, b))


def ln_bwd_mm(dy, r, g, w, *, tm, name):
    T, N = dy.shape
    Ko = w.shape[0]

    def body(dy_ref, r_ref, g_ref, w_ref, dr_ref, drb_ref, o_ref, s_ref):
        @pl.when(pl.program_id(0) == 0)
        def _():
            s_ref[...] = jnp.zeros_like(s_ref)

        dy_ = dy_ref[...]
        xhat, rstd = _ln_stats(r_ref[...])
        dr = _ln_bwd(dy_, xhat, rstd, g_ref[...])
        dr_ref[...] = dr
        drb = dr.astype(BF16)
        drb_ref[...] = drb
        o_ref[...] = _dot_nt(drb, w_ref[...])
        s_ref[0:1, :] += jnp.sum(dy_ * xhat, axis=0, keepdims=True)
        s_ref[1:2, :] += jnp.sum(dy_, axis=0, keepdims=True)
        s_ref[2:3, :] += jnp.sum(dr, axis=0, keepdims=True)

    row = lambda i: (i, 0)
    fix = lambda i: (0, 0)
    return _call(
        body, name=name, grid=(T // tm,),
        in_specs=[pl.BlockSpec((tm, N), row), pl.BlockSpec((tm, N), row), pl.BlockSpec((1, N), fix),
                  pl.BlockSpec((Ko, N), fix)],
        out_specs=[pl.BlockSpec((tm, N), row), pl.BlockSpec((tm, N), row), pl.BlockSpec((tm, Ko), row),
                   pl.BlockSpec((8, N), fix)],
        out_shape=[jax.ShapeDtypeStruct((T, N), F32), jax.ShapeDtypeStruct((T, N), BF16),
                   jax.ShapeDtypeStruct((T, Ko), F32), jax.ShapeDtypeStruct((8, N), F32)],
        args=(dy, r, g, w))


def ffn_bwd_dx(dy, r, g, z, w1, w2, *, tm, tf, name, comm=None):
    T, D = dy.shape
    NC, _, FC = w1.shape
    F = NC * FC
    per = FC // tf
    nf = F // tf

    def body(dy_ref, r_ref, g_ref, z_ref, w1_ref, w2_ref, dz_ref, dx_ref, drb_ref, s_ref, dr_scr, acc_ref):
        i = pl.program_id(0)
        f = pl.program_id(1)

        @pl.when((i == 0) & (f == 0))
        def _():
            s_ref[...] = jnp.zeros_like(s_ref)

        @pl.when(f == 0)
        def _():
            dy_ = dy_ref[...]
            xhat, rstd = _ln_stats(r_ref[...])
            dr = _ln_bwd(dy_, xhat, rstd, g_ref[...])
            dr_scr[...] = dr
            drb_ref[...] = dr.astype(BF16)
            acc_ref[...] = jnp.zeros_like(acc_ref)
            s_ref[0:1, :] += jnp.sum(dy_ * xhat, axis=0, keepdims=True)
            s_ref[1:2, :] += jnp.sum(dy_, axis=0, keepdims=True)

        dh = _dot_nt(drb_ref[...], w2_ref[...])
        dz = (dh * (2.0 * jnp.maximum(z_ref[...].astype(F32), 0.0))).astype(BF16)
        dz_ref[...] = dz
        acc_ref[...] += _dot_nt(dz, w1_ref[...])

        @pl.when(f == nf - 1)
        def _():
            dx_ref[...] = ALPHA * dr_scr[...] + acc_ref[...]

    return carried_call(
        body, comm, name=name, grid=(T // tm, nf),
        in_specs=[pl.BlockSpec((tm, D), lambda i, f: (i, 0)),
                  pl.BlockSpec((tm, D), lambda i, f: (i, 0)),
                  pl.BlockSpec((1, D), lambda i, f: (0, 0)),
                  pl.BlockSpec((tm, tf), lambda i, f: (i, f)),
                  pl.BlockSpec((None, D, tf), lambda i, f: (f // per, 0, f % per)),
                  pl.BlockSpec((tf, D), lambda i, f: (f, 0))],
        out_specs=[pl.BlockSpec((tm, tf), lambda i, f: (i, f)),
                   pl.BlockSpec((tm, D), lambda i, f: (i, 0)),
                   pl.BlockSpec((tm, D), lambda i, f: (i, 0)),
                   pl.BlockSpec((8, D), lambda i, f: (0, 0))],
        out_shape=[jax.ShapeDtypeStruct((T, F), BF16), jax.ShapeDtypeStruct((T, D), F32),
                   jax.ShapeDtypeStruct((T, D), BF16), jax.ShapeDtypeStruct((8, D), F32)],
        scratch_shapes=[pltpu.VMEM((tm, D), F32), pltpu.VMEM((tm, D), F32)],
        args=(dy, r, g, z, w1, w2))


def mm_tn(a, b, *, tm, tk, tn, relu2=False, name):
    T, K = a.shape
    G, _, N = b.shape
    nt = T // tm

    def body(a_ref, b_ref, o_ref, acc_ref):
        t = pl.program_id(3)

        @pl.when(t == 0)
        def _():
            acc_ref[...] = jnp.zeros_like(acc_ref)

        av = a_ref[...]
        if relu2:
            av = jnp.square(jnp.maximum(av.astype(F32), 0.0))
        acc_ref[...] += _dot_tn(av.astype(BF16), b_ref[...])

        @pl.when(t == nt - 1)
        def _():
            o_ref[...] = acc_ref[...].astype(BF16)

    return _call(
        body, name=name, grid=(G, K // tk, N // tn, nt),
        in_specs=[pl.BlockSpec((tm, tk), lambda g, k, n, t: (t, k)),
                  pl.BlockSpec((None, tm, tn), lambda g, k, n, t: (g, t, n))],
        out_specs=[pl.BlockSpec((None, None, tk, tn), lambda g, k, n, t: (g, n, k, 0))],
        out_shape=[jax.ShapeDtypeStruct((G, N // tn, K, tn), BF16)],
        scratch_shapes=[pltpu.VMEM((tk, tn), F32)], args=(a, b))[0]


def mm_nt_acc(dy, w, base, *, tm, name):
    G, T, N = dy.shape
    K = w.shape[1]

    def body(dy_ref, w_ref, base_ref, o_ref):
        g = pl.program_id(1)

        @pl.when(g == 0)
        def _():
            o_ref[...] = ALPHA * base_ref[...]

        o_ref[...] += _dot_nt(dy_ref[...], w_ref[...])

    return _call(
        body, name=name, grid=(T // tm, G),
        in_specs=[pl.BlockSpec((None, tm, N), lambda i, g: (g, i, 0)),
                  pl.BlockSpec((None, K, N), lambda i, g: (g, 0, 0)),
                  pl.BlockSpec((tm, K), lambda i, g: (i, 0))],
        out_specs=[pl.BlockSpec((tm, K), lambda i, g: (i, 0))],
        out_shape=[jax.ShapeDtypeStruct((T, K), F32)], args=(dy, w, base))[0]


def _split3(x):
    x1 = x.astype(BF16)
    r1 = x - x1.astype(F32)
    x2 = r1.astype(BF16)
    x3 = (r1 - x2.astype(F32)).astype(BF16)
    return x1, x2, x3


def _tri_dot(tri, x):
    x1, x2, x3 = _split3(x)
    return _dot(tri, x1) + _dot(tri, x2) + _dot(tri, x3)


def _gates(pq, fz, lb):
    sg = _sigmoid(fz)
    f = lb + (1.0 - lb) * sg
    logf = jnp.log(jnp.maximum(f, GATE_EPS))
    sq = _sigmoid(pq)
    return pq * sq, 1.0 - f, logf, f, sg, sq


def _row_ge(s):
    return lax.broadcasted_iota(jnp.int32, (SUB, HEAD), 0) >= s


def hgrn_fwd(proj, lb, norm_g, *, rb, hp, cu, name, comm=None):
    _, T, D = proj.shape
    H = D // HEAD
    nb = T // rb
    nck = rb // CHUNK
    nsub = CHUNK // SUB
    ns = 5
    streams = [(h, j) for j in range(cu) for h in range(hp)]
    sid = {hj: n for n, hj in enumerate(streams)}
    nst = len(streams)
    lns = [slice(h * HEAD, (h + 1) * HEAD) for h in range(hp)]

    def body(pq_ref, fz_ref, pv_ref, pg_ref, lb_ref, ng_ref, o_ref, og_ref, st_ref, *scr):
        S = scr[0:hp]
        q_s, k_s, v_s, b_s, p_s = [[scr[hp + ns * n + j] for n in range(nst)] for j in range(ns)]

        @pl.when(pl.program_id(1) == 0)
        def _():
            for h in range(hp):
                S[h][...] = jnp.zeros((HEAD, HEAD), F32)

        ri = lax.broadcasted_iota(jnp.int32, (CHUNK, CHUNK), 0)
        ci = lax.broadcasted_iota(jnp.int32, (CHUNK, CHUNK), 1)
        tri = (ci <= ri).astype(BF16)
        ones = jnp.ones((HEAD, HEAD), BF16)

        def trip(g, carry):
            base = [(g * cu + j) * CHUNK for j in range(cu)]
            rows = [pl.ds(pl.multiple_of(base[j], CHUNK), CHUNK) for j in range(cu)]
            q, k, v, b = {}, {}, {}, {}
            for (h, j) in streams:
                n = sid[(h, j)]
                qh, kh, logf, _, _, _ = _gates(pq_ref[rows[j], lns[h]], fz_ref[rows[j], lns[h]], lb_ref[:, lns[h]])
                vh = pv_ref[rows[j], lns[h]]
                bh = _tri_dot(tri, logf)
                q_s[n][...] = qh
                k_s[n][...] = kh
                v_s[n][...] = vh
                b_s[n][...] = bh
                q[n], k[n], v[n], b[n] = qh, kh, vh, bh
            upd, ebl, o_int = {}, {}, {}
            for n in range(nst):
                bl = b_s[n][CHUNK - 1:CHUNK, :]
                ebl[n] = jnp.exp(bl)
                upd[n] = _dot_tn(v[n].astype(BF16), (k[n] * jnp.exp(bl - b[n])).astype(BF16))
            for h in range(hp):
                for j in range(cu):
                    n = sid[(h, j)]
                    Sv = S[h][...]
                    st_ref[h, g * cu + j] = Sv
                    o_int[n] = _dot_nt((q[n] * jnp.exp(b[n])).astype(BF16), Sv.astype(BF16))
                    S[h][...] = Sv * ebl[n] + upd[n]
            off = {}
            for i in range(1, nsub):
                lo = i * SUB
                for n in range(nst):
                    ref = b_s[n][lo - 1:lo, :]
                    qt = (q_s[n][lo:lo + SUB, :] * jnp.exp(b_s[n][lo:lo + SUB, :] - ref)).astype(BF16)
                    kt = (k_s[n][0:lo, :] * jnp.exp(ref - b_s[n][0:lo, :])).astype(BF16)
                    a = _dot_nt(qt, kt)
                    off[(n, i)] = _dot(a.astype(BF16), v_s[n][0:lo, :].astype(BF16))
            for i in range(nsub):
                lo = i * SUB
                for s in range(SUB):
                    m = _row_ge(s)
                    at = (i * SUB + s) * SUB
                    for n in range(nst):
                        e = jnp.exp(b_s[n][lo:lo + SUB, :] - b_s[n][lo + s:lo + s + 1, :])
                        p = jnp.where(m, q_s[n][lo:lo + SUB, :] * (k_s[n][lo + s:lo + s + 1, :] * e), 0.0)
                        p_s[n][at:at + SUB, :] = p.astype(BF16)
            rs = [_dot(p_s[n][...], ones) for n in range(nst)]
            for i in range(nsub):
                lo = i * SUB
                for (h, j) in streams:
                    n = sid[(h, j)]
                    blk = pl.ds(pl.multiple_of(base[j] + lo, SUB), SUB)
                    acc = o_int[n][lo:lo + SUB, :]
                    if i > 0:
                        acc = acc + off[(n, i)]
                    for s in range(SUB):
                        at = (i * SUB + s) * SUB
                        acc = acc + rs[n][at:at + SUB, :] * v_s[n][lo + s:lo + s + 1, :]
                    o_ref[blk, lns[h]] = acc
                    rinv = lax.rsqrt(jnp.mean(acc * acc, axis=-1, keepdims=True) + RMS_EPS)
                    pg = pg_ref[blk, lns[h]]
                    og_ref[blk, lns[h]] = (acc * rinv * ng_ref[:, lns[h]] * (pg * _sigmoid(pg))).astype(BF16)
            return carry

        lax.fori_loop(0, nck // cu, trip, 0)

    W = hp * HEAD

    def grp(gi):
        return pl.BlockSpec((None, rb, W), lambda h, r: (gi, r, h))

    vec = pl.BlockSpec((1, W), lambda h, r: (0, h))
    per_stream = [pltpu.VMEM((CHUNK, HEAD), F32)] * 4 + [pltpu.VMEM((CHUNK * SUB, HEAD), BF16)]
    return carried_call(
        body, comm, name=name, grid=(H // hp, nb),
        in_specs=[grp(0), grp(1), grp(2), grp(3), vec, vec],
        out_specs=[pl.BlockSpec((rb, W), lambda h, r: (r, h)),
                   pl.BlockSpec((rb, W), lambda h, r: (r, h)),
                   pl.BlockSpec((hp, nck, HEAD, HEAD), lambda h, r: (h, r, 0, 0))],
        out_shape=[jax.ShapeDtypeStruct((T, D), F32), jax.ShapeDtypeStruct((T, D), BF16),
                   jax.ShapeDtypeStruct((H, T // CHUNK, HEAD, HEAD), F32)],
        scratch_shapes=[pltpu.VMEM((HEAD, HEAD), F32)] * hp + per_stream * nst,
        args=(proj, proj, proj, proj, lb, norm_g))


def hgrn_bwd(proj, o, dog, states, lb, norm_g, *, rb, hp, cu, name, comm=None):
    _, T, D = proj.shape
    H = D // HEAD
    nb = T // rb
    nck = rb // CHUNK
    nsub = CHUNK // SUB
    ns = 11
    streams = [(h, j) for j in range(cu) for h in range(hp)]
    sid = {hj: n for n, hj in enumerate(streams)}
    nst = len(streams)
    lns = [slice(h * HEAD, (h + 1) * HEAD) for h in range(hp)]
    half = CHUNK * SUB

    def body(pq_ref, fz_ref, pv_ref, pg_ref, o_ref, dog_ref, st_ref, lb_ref, ng_ref, dp_ref, s_ref, *scr):
        dS = scr[0:hp]
        q_s, k_s, v_s, b_s, do_s, dq_s, dk_s, dv_s, ke_s, qe_s, p_s = [
            [scr[hp + ns * n + j] for n in range(nst)] for j in range(ns)]

        @pl.when(pl.program_id(1) == 0)
        def _():
            for h in range(hp):
                dS[h][...] = jnp.zeros((HEAD, HEAD), F32)
            s_ref[...] = jnp.zeros_like(s_ref)

        ri = lax.broadcasted_iota(jnp.int32, (CHUNK, CHUNK), 0)
        ci = lax.broadcasted_iota(jnp.int32, (CHUNK, CHUNK), 1)
        tri = (ci <= ri).astype(BF16)
        triu = (ci >= ri).astype(BF16)
        ones = jnp.ones((HEAD, HEAD), BF16)
        last_row = lax.broadcasted_iota(jnp.int32, (CHUNK, HEAD), 0) == CHUNK - 1

        def trip(g, carry):
            cidx = [nck - 1 - (g * cu + j) for j in range(cu)]
            rows = [pl.ds(pl.multiple_of(cidx[j] * CHUNK, CHUNK), CHUNK) for j in range(cu)]
            pq, q, k, f, sg, sq, b, dpg, extra = {}, {}, {}, {}, {}, {}, {}, {}, {}
            for (h, j) in streams:
                n = sid[(h, j)]
                ln = lns[h]
                pqh = pq_ref[rows[j], ln]
                qh, kh, logf, fh, sgh, sqh = _gates(pqh, fz_ref[rows[j], ln], lb_ref[:, ln])
                vh = pv_ref[rows[j], ln]
                bh = _tri_dot(tri, logf)
                oh = o_ref[rows[j], ln]
                dog_ = dog_ref[rows[j], ln]
                pg = pg_ref[rows[j], ln]
                ng = ng_ref[:, ln]
                spg = _sigmoid(pg)
                rinv = lax.rsqrt(jnp.mean(oh * oh, axis=-1, keepdims=True) + RMS_EPS)
                on = oh * rinv
                dpg[n] = dog_ * (on * ng) * (spg * (1.0 + pg * (1.0 - spg)))
                don = dog_ * (pg * spg)
                s_ref[0:1, ln] += jnp.sum(don * on, axis=0, keepdims=True)
                dxn = don * ng
                do = rinv * (dxn - on * jnp.mean(dxn * on, axis=-1, keepdims=True))
                q_s[n][...] = qh
                k_s[n][...] = kh
                v_s[n][...] = vh
                b_s[n][...] = bh
                do_s[n][...] = do
                pq[n], q[n], k[n], f[n], sg[n], sq[n], b[n] = pqh, qh, kh, fh, sgh, sqh, bh
            upd, eb, ebl, ekk = {}, {}, {}, {}
            for n in range(nst):
                eb[n] = jnp.exp(b[n])
                bl = b_s[n][CHUNK - 1:CHUNK, :]
                ebl[n] = jnp.exp(bl)
                ekk[n] = jnp.exp(bl - b[n])
                upd[n] = _dot_tn(do_s[n][...].astype(BF16), (q[n] * eb[n]).astype(BF16))
            for h in range(hp):
                for j in range(cu):
                    n = sid[(h, j)]
                    S0 = st_ref[h, cidx[j]]
                    dSv = dS[h][...]
                    dSb = dSv.astype(BF16)
                    dq_s[n][...] = _dot(do_s[n][...].astype(BF16), S0.astype(BF16)) * eb[n]
                    dk_state = _dot(v_s[n][...].astype(BF16), dSb) * ekk[n]
                    dk_s[n][...] = dk_state
                    dv_s[n][...] = _dot_nt((k[n] * ekk[n]).astype(BF16), dSb)
                    extra[n] = (jnp.sum(k[n] * dk_state, axis=0, keepdims=True)
                                + ebl[n] * jnp.sum(S0 * dSv, axis=0, keepdims=True))
                    dS[h][...] = dSv * ebl[n] + upd[n]

            for i in range(1, nsub):
                lo = i * SUB
                for n in range(nst):
                    ref = b_s[n][lo - 1:lo, :]
                    eq = jnp.exp(b_s[n][lo:lo + SUB, :] - ref)
                    ek = jnp.exp(ref - b_s[n][0:lo, :])
                    qt = (q_s[n][lo:lo + SUB, :] * eq).astype(BF16)
                    kt = (k_s[n][0:lo, :] * ek).astype(BF16)
                    doi = do_s[n][lo:lo + SUB, :].astype(BF16)
                    vp = v_s[n][0:lo, :].astype(BF16)
                    at = _dot_nt(kt, qt)
                    dv_s[n][0:lo, :] += _dot(at.astype(BF16), doi)
                    da = _dot_nt(doi, vp)
                    dq_s[n][lo:lo + SUB, :] += _dot(da.astype(BF16), kt) * eq
                    dat = _dot_nt(vp, doi)
                    dk_s[n][0:lo, :] += _dot(dat.astype(BF16), qt) * ek

            for i in range(nsub):
                lo = i * SUB
                for s in range(SUB):
                    m = _row_ge(s)
                    at = (i * SUB + s) * SUB
                    for n in range(nst):
                        qi = q_s[n][lo:lo + SUB, :]
                        e = jnp.where(m, jnp.exp(b_s[n][lo:lo + SUB, :] - b_s[n][lo + s:lo + s + 1, :]), 0.0)
                        ke = k_s[n][lo + s:lo + s + 1, :] * e
                        ke_s[n][at:at + SUB, :] = ke
                        qe_s[n][at:at + SUB, :] = qi * e
                        p_s[n][at:at + SUB, :] = (qi * ke).astype(BF16)
                        p_s[n][half + at:half + at + SUB, :] = jnp.where(
                            m, do_s[n][lo:lo + SUB, :] * v_s[n][lo + s:lo + s + 1, :], 0.0).astype(BF16)
            rs = [_dot(p_s[n][...], ones) for n in range(nst)]
            for i in range(nsub):
                lo = i * SUB
                for n in range(nst):
                    doi = do_s[n][lo:lo + SUB, :]
                    dqa = dq_s[n][lo:lo + SUB, :]
                    for s in range(SUB):
                        at = (i * SUB + s) * SUB
                        acol = rs[n][at:at + SUB, :]
                        dacol = rs[n][half + at:half + at + SUB, :]
                        dqa = dqa + dacol * ke_s[n][at:at + SUB, :]
                        dk_s[n][lo + s:lo + s + 1, :] += jnp.sum(dacol * qe_s[n][at:at + SUB, :], axis=0, keepdims=True)
                        dv_s[n][lo + s:lo + s + 1, :] += jnp.sum(acol * doi, axis=0, keepdims=True)
                    dq_s[n][lo:lo + SUB, :] = dqa

            for (h, j) in streams:
                n = sid[(h, j)]
                ln = lns[h]
                lb_ = lb_ref[:, ln]
                dq = dq_s[n][...]
                dk = dk_s[n][...]
                db = q[n] * dq - k[n] * dk + jnp.where(last_row, extra[n], 0.0)
                dlogf = _tri_dot(triu, db)
                df = jnp.where(f[n] > GATE_EPS, dlogf / jnp.maximum(f[n], GATE_EPS), 0.0) - dk
                s_ref[1:2, ln] += jnp.sum(df * (1.0 - sg[n]), axis=0, keepdims=True)
                dp_ref[0, rows[j], ln] = (dq * (sq[n] * (1.0 + pq[n] * (1.0 - sq[n])))).astype(BF16)
                dp_ref[1, rows[j], ln] = (df * (1.0 - lb_) * sg[n] * (1.0 - sg[n])).astype(BF16)
                dp_ref[2, rows[j], ln] = dv_s[n][...].astype(BF16)
                dp_ref[3, rows[j], ln] = dpg[n].astype(BF16)
            return carry

        lax.fori_loop(0, nck // cu, trip, 0)

    W = hp * HEAD

    def grp(gi):
        return pl.BlockSpec((None, rb, W), lambda h, r: (gi, nb - 1 - r, h))

    rowsp = pl.BlockSpec((rb, W), lambda h, r: (nb - 1 - r, h))
    vec = pl.BlockSpec((1, W), lambda h, r: (0, h))
    per_stream = ([pltpu.VMEM((CHUNK, HEAD), F32)] * 8 + [pltpu.VMEM((half, HEAD), F32)] * 2
                  + [pltpu.VMEM((2 * half, HEAD), BF16)])
    return carried_call(
        body, comm, name=name, grid=(H // hp, nb),
        in_specs=[grp(0), grp(1), grp(2), grp(3), rowsp, rowsp,
                  pl.BlockSpec((hp, nck, HEAD, HEAD), lambda h, r: (h, nb - 1 - r, 0, 0)), vec, vec],
        out_specs=[pl.BlockSpec((4, rb, W), lambda h, r: (0, nb - 1 - r, h)),
                   pl.BlockSpec((8, W), lambda h, r: (0, h))],
        out_shape=[jax.ShapeDtypeStruct((4, T, D), BF16), jax.ShapeDtypeStruct((8, D), F32)],
        scratch_shapes=[pltpu.VMEM((HEAD, HEAD), F32)] * hp + per_stream * nst,
        args=(proj, proj, proj, proj, o, dog, states, lb, norm_g))


def lb_fwd(logits):
    def body(l_ref, o_ref):
        l = l_ref[...]
        mx = jnp.max(l, axis=0, keepdims=True)
        e = jnp.exp(l - mx)
        sm = e / jnp.sum(e, axis=0, keepdims=True)
        o_ref[0:1, :] = jnp.zeros_like(sm[0:1, :])
        o_ref[1:2, :] = sm[1:2, :]

    return pl.pallas_call(body, name="lb_fwd", out_shape=jax.ShapeDtypeStruct(logits.shape, F32))(logits)


def lb_bwd(logits, dlb):
    def body(l_ref, d_ref, o_ref):
        l = l_ref[...]
        mx = jnp.max(l, axis=0, keepdims=True)
        e = jnp.exp(l - mx)
        sm = e / jnp.sum(e, axis=0, keepdims=True)
        inner = d_ref[1:2, :] * sm[1:2, :]
        o_ref[0:1, :] = sm[0:1, :] * (0.0 - inner)
        o_ref[1:2, :] = sm[1:2, :] * (d_ref[1:2, :] - inner)

    return pl.pallas_call(body, name="lb_bwd", out_shape=jax.ShapeDtypeStruct(logits.shape, F32))(logits, dlb)


def conv_fwd(u, vec, *, tm, name, comm=None):
    _, T, D = u.shape
    hb = tm // HALO

    def body(a_ref, gt_ref, ap_ref, gp_ref, w_ref, c_ref, v_ref, buf):
        i = pl.program_id(0)
        buf[HALO:HALO + tm, :] = a_ref[...] * _sigmoid(gt_ref[...])
        prev = ap_ref[...] * _sigmoid(gp_ref[...])
        buf[0:HALO, :] = jnp.where(i > 0, prev, 0.0)
        acc = jnp.zeros((tm, D), F32) + w_ref[32:33, :]
        for j in range(CONV_W):
            acc = acc + w_ref[j:j + 1, :] * buf[j + 2:j + 2 + tm, :]
        c_ref[...] = acc
        xhat, _ = _ln_stats(acc)
        y = xhat * w_ref[33:34, :] + w_ref[34:35, :]
        v_ref[...] = (y * _sigmoid(y)).astype(BF16)

    cur = lambda gi: pl.BlockSpec((None, tm, D), lambda i: (gi, i, 0))
    prv = lambda gi: pl.BlockSpec((None, HALO, D), lambda i: (gi, jnp.maximum(i * hb - 1, 0), 0))
    return carried_call(
        body, comm, name=name, grid=(T // tm,),
        in_specs=[cur(0), cur(1), prv(0), prv(1), pl.BlockSpec((VEC_ROWS, D), lambda i: (0, 0))],
        out_specs=[pl.BlockSpec((tm, D), lambda i: (i, 0)), pl.BlockSpec((tm, D), lambda i: (i, 0))],
        out_shape=[jax.ShapeDtypeStruct((T, D), F32), jax.ShapeDtypeStruct((T, D), BF16)],
        scratch_shapes=[pltpu.VMEM((tm + HALO, D), F32)],
        args=(u, u, u, u, vec))


def conv_bwd(dv2, c, u, vec, *, tm, name, comm=None):
    _, T, D = u.shape
    hb = tm // HALO
    nt = T // tm
    nh = T // HALO

    def body(dv_ref, c_ref, dvn_ref, cn_ref, a_ref, gt_ref, ap_ref, gp_ref, w_ref, du_ref, s_ref, gbuf, dbuf):
        i = pl.program_id(0)

        @pl.when(i == 0)
        def _():
            s_ref[...] = jnp.zeros_like(s_ref)

        gam = w_ref[33:34, :]
        bet = w_ref[34:35, :]

        def dconv(dv, cc):
            xhat, rstd = _ln_stats(cc)
            y = xhat * gam + bet
            sy = _sigmoid(y)
            dy = dv * (sy * (1.0 + y * (1.0 - sy)))
            return _ln_bwd(dy, xhat, rstd, gam), dy, xhat

        dc, dy, xhat = dconv(dv_ref[...], c_ref[...])
        dcn, _, _ = dconv(dvn_ref[...], cn_ref[...])
        dbuf[0:tm, :] = dc
        dbuf[tm:tm + HALO, :] = jnp.where(i < nt - 1, dcn, 0.0)
        a = a_ref[...]
        sgt = _sigmoid(gt_ref[...])
        gbuf[HALO:HALO + tm, :] = a * sgt
        gbuf[0:HALO, :] = jnp.where(i > 0, ap_ref[...] * _sigmoid(gp_ref[...]), 0.0)
        s_ref[32:33, :] += jnp.sum(dc, axis=0, keepdims=True)
        s_ref[33:34, :] += jnp.sum(dy * xhat, axis=0, keepdims=True)
        s_ref[34:35, :] += jnp.sum(dy, axis=0, keepdims=True)
        dglu = jnp.zeros((tm, D), F32)
        for j in range(CONV_W):
            dglu = dglu + w_ref[j:j + 1, :] * dbuf[30 - j:30 - j + tm, :]
            s_ref[j:j + 1, :] += jnp.sum(dc * gbuf[j + 2:j + 2 + tm, :], axis=0, keepdims=True)
        da = (dglu * sgt).astype(BF16)
        dg = (dglu * a * sgt * (1.0 - sgt)).astype(BF16)
        du_ref[0] = da
        du_ref[1] = dg
        s_ref[36:37, :] += jnp.sum(da.astype(F32), axis=0, keepdims=True)
        s_ref[37:38, :] += jnp.sum(dg.astype(F32), axis=0, keepdims=True)

    row = lambda i: (i, 0)
    nxt = lambda i: (jnp.minimum((i + 1) * hb, nh - 1), 0)
    cur = lambda gi: pl.BlockSpec((None, tm, D), lambda i: (gi, i, 0))
    prv = lambda gi: pl.BlockSpec((None, HALO, D), lambda i: (gi, jnp.maximum(i * hb - 1, 0), 0))
    fix = lambda i: (0, 0)
    return carried_call(
        body, comm, name=name, grid=(nt,),
        in_specs=[pl.BlockSpec((tm, D), row), pl.BlockSpec((tm, D), row),
                  pl.BlockSpec((HALO, D), nxt), pl.BlockSpec((HALO, D), nxt),
                  cur(0), cur(1), prv(0), prv(1), pl.BlockSpec((VEC_ROWS, D), fix)],
        out_specs=[pl.BlockSpec((2, tm, D), lambda i: (0, i, 0)), pl.BlockSpec((VEC_ROWS, D), fix)],
        out_shape=[jax.ShapeDtypeStruct((2, T, D), BF16), jax.ShapeDtypeStruct((VEC_ROWS, D), F32)],
        scratch_shapes=[pltpu.VMEM((tm + HALO, D), F32), pltpu.VMEM((tm + HALO, D), F32)],
        args=(dv2, c, dv2, c, u, u, u, u, vec))


def loss_grad(y, target, *, tm):
    T, D = y.shape
    nt = T // tm

    def body(y_ref, t_ref, l_ref, d_ref, acc):
        i = pl.program_id(0)

        @pl.when(i == 0)
        def _():
            acc[...] = jnp.zeros_like(acc)

        e = y_ref[...] - t_ref[...]
        d_ref[...] = e * (1.0 / D)
        acc[...] += jnp.sum(e * e, axis=0, keepdims=True)

        @pl.when(i == nt - 1)
        def _():
            l_ref[...] = 0.5 * jnp.sum(acc[...], axis=1, keepdims=True) * (1.0 / D)

    row = lambda i: (i, 0)
    return _call(
        body, name="loss_grad", grid=(nt,),
        in_specs=[pl.BlockSpec((tm, D), row), pl.BlockSpec((tm, D), row)],
        out_specs=[pl.BlockSpec((1, 1), lambda i: (0, 0)), pl.BlockSpec((tm, D), row)],
        out_shape=[jax.ShapeDtypeStruct((1, 1), F32), jax.ShapeDtypeStruct((T, D), F32)],
        scratch_shapes=[pltpu.VMEM((1, D), F32)], args=(y, target))


def _rows_block(R, C, budget=1 << 20):
    tr = R
    while tr * C * 4 > budget and tr % 32 == 0:
        tr //= 2
    return tr


def adamw(w, g, m, v, *, name):
    R, C = w.shape
    tr = _rows_block(R, C)

    def body(w_ref, g_ref, m_ref, v_ref, d_ref, mo_ref, vo_ref):
        g_ = g_ref[...]
        mn = ADAM_B1 * m_ref[...] + (1.0 - ADAM_B1) * g_
        vn = ADAM_B2 * v_ref[...] + (1.0 - ADAM_B2) * jnp.square(g_)
        m_hat = mn / (1.0 - ADAM_B1 ** ADAM_STEP)
        v_hat = vn / (1.0 - ADAM_B2 ** ADAM_STEP)
        d_ref[...] = -ADAM_LR * (m_hat / (jnp.sqrt(v_hat) + ADAM_EPS) + ADAM_WD * w_ref[...])
        mo_ref[...] = mn
        vo_ref[...] = vn

    spec = pl.BlockSpec((tr, C), lambda i: (i, 0))
    sd = jax.ShapeDtypeStruct((R, C), F32)
    return _call(body, name=name, grid=(R // tr,), in_specs=[spec] * 4, out_specs=[spec] * 3, out_shape=[sd] * 3,
                 args=(w, g, m, v))


def sum_slots(slots, *, name):
    _, R, C = slots.shape
    tr = _rows_block(R, C, budget=1 << 19)

    def body(s_ref, o_ref):
        acc = s_ref[0].astype(F32)
        for d in range(1, 8):
            acc = acc + s_ref[d].astype(F32)
        o_ref[...] = acc

    return _call(body, name=name, grid=(R // tr,), in_specs=[pl.BlockSpec((8, tr, C), lambda i: (0, i, 0))],
                 out_specs=[pl.BlockSpec((tr, C), lambda i: (i, 0))], out_shape=[jax.ShapeDtypeStruct((R, C), F32)],
                 args=(slots,))[0]


def _adam_nd(w, g, m, v, name):
    shp = w.shape
    c = shp[-1]
    f2 = lambda a: a.reshape(-1, c)
    d, mn, vn = adamw(f2(w), f2(g), f2(m), f2(v), name=name)
    return d.reshape(shp), mn.reshape(shp), vn.reshape(shp)


def _reduced(slots, name):
    out = []
    for s in slots:
        c = s.shape[-1]
        out.append(sum_slots(s.reshape(8, -1, c), name=name).reshape(s.shape[1:]))
    return out


def kernel(x, ln_mix_g, ln_mix_b, ln_ffn_g, ln_ffn_b, ffn_w1, ffn_w2, a_w_in, a_lb_logits, a_norm_g, a_w_out, b_w_pw1, b_b_pw1, b_w_dw, b_b_dw, b_ln_g, b_ln_b, b_w_pw2, b_b_pw2, loss_target, m_ln_mix_g, m_ln_mix_b, m_ln_ffn_g, m_ln_ffn_b, m_ffn_w1, m_ffn_w2, m_a_w_in, m_a_lb_logits, m_a_norm_g, m_a_w_out, m_b_w_pw1, m_b_b_pw1, m_b_w_dw, m_b_b_dw, m_b_ln_g, m_b_ln_b, m_b_w_pw2, m_b_b_pw2, v_ln_mix_g, v_ln_mix_b, v_ln_ffn_g, v_ln_ffn_b, v_ffn_w1, v_ffn_w2, v_a_w_in, v_a_lb_logits, v_a_norm_g, v_a_w_out, v_b_w_pw1, v_b_b_pw1, v_b_w_dw, v_b_b_dw, v_b_ln_g, v_b_ln_b, v_b_w_pw2, v_b_b_pw2):
    names = ["ln_mix_g", "ln_mix_b", "ln_ffn_g", "ln_ffn_b", "ffn_w1", "ffn_w2", "a_w_in", "a_lb_logits", "a_norm_g",
             "a_w_out", "b_w_pw1", "b_b_pw1", "b_w_dw", "b_b_dw", "b_ln_g", "b_ln_b", "b_w_pw2", "b_b_pw2"]
    w = dict(zip(names, [ln_mix_g, ln_mix_b, ln_ffn_g, ln_ffn_b, ffn_w1, ffn_w2, a_w_in, a_lb_logits, a_norm_g, a_w_out,
                         b_w_pw1, b_b_pw1, b_w_dw, b_b_dw, b_ln_g, b_ln_b, b_w_pw2, b_b_pw2]))
    m = dict(zip(names, [m_ln_mix_g, m_ln_mix_b, m_ln_ffn_g, m_ln_ffn_b, m_ffn_w1, m_ffn_w2, m_a_w_in, m_a_lb_logits,
                         m_a_norm_g, m_a_w_out, m_b_w_pw1, m_b_b_pw1, m_b_w_dw, m_b_b_dw, m_b_ln_g, m_b_ln_b, m_b_w_pw2,
                         m_b_b_pw2]))
    v = dict(zip(names, [v_ln_mix_g, v_ln_mix_b, v_ln_ffn_g, v_ln_ffn_b, v_ffn_w1, v_ffn_w2, v_a_w_in, v_a_lb_logits,
                         v_a_norm_g, v_a_w_out, v_b_w_pw1, v_b_b_pw1, v_b_w_dw, v_b_b_dw, v_b_ln_g, v_b_ln_b, v_b_w_pw2,
                         v_b_b_pw2]))
    T, D = x.shape[1], x.shape[2]
    DS = D // 4
    F = 4 * ffn_w1.shape[2]
    chip = 2 * lax.axis_index("x") + lax.axis_index("y")
    tm = min(T, 512)
    tmc = min(T, 256)
    rb = min(T, 512)
    tf = min(F // 4, 1024)
    hp, cu = HEADS_PER_STEP, CHUNKS_PER_TRIP
    xin, target = x[0], loss_target[0]

    def mix_shards(i):
        j = i // 2
        if i % 2 == 0:
            return [a_w_in[j].astype(BF16), a_w_out[j].astype(BF16)]
        vec = jnp.concatenate([b_w_dw[j], jnp.zeros((1, DS), F32), b_b_dw[j][None], b_ln_g[j][None], b_ln_b[j][None],
                               b_b_pw2[j][None], b_b_pw1[j].reshape(2, DS), jnp.zeros((2, DS), F32)], axis=0)
        return [b_w_pw1[j].astype(BF16), b_w_pw2[j].astype(BF16), vec]

    def ffn_shards(i):
        return [ffn_w1[i].astype(BF16), ffn_w2[i].astype(BF16)]

    def mix_weights(i, got):
        if i % 2 == 0:
            return {"w_in": got[0], "w_out": got[1].reshape(D, D)}
        pw1 = jnp.transpose(got[0].reshape(2, 2, D, D // 2), (0, 2, 1, 3)).reshape(2, D, D)
        vec = jnp.transpose(got[2], (1, 0, 2)).reshape(VEC_ROWS, D)
        return {"pw1": pw1, "pw2": got[1].reshape(D, D), "vec": vec,
                "b_pw1": got[2][:, 36:38, :].reshape(2, 1, D)}

    lb_all = lb_fwd(a_lb_logits)
    zeros_bias = jnp.zeros((1, D), F32)

    mixw = mix_weights(0, comm_call(GatherChips(mix_shards(0)), name="gather_first"))
    saved = []
    h = xin
    for i in range(DEPTH):
        j = i // 2
        s = {"x": h, "mixw": mixw}
        gf = GatherChips(ffn_shards(i))
        if i % 2 == 0:
            s["proj"] = mm_groups(h, mixw["w_in"], jnp.zeros((4, 1, D), F32), tm=tm, name="a_in_proj")
            (s["o"], s["og"], s["st"]), got = hgrn_fwd(s["proj"], lb_all[j:j + 1], a_norm_g[j:j + 1], rb=rb, hp=hp, cu=cu,
                                                       name="hgrn_fwd", comm=gf)
            s["r1"], s["x1"] = mm_res_ln(s["og"], mixw["w_out"], zeros_bias, h, ln_mix_g[i:i + 1], ln_mix_b[i:i + 1],
                                         tm=tm, name="a_out_ln")
        else:
            s["u"] = mm_groups(h, mixw["pw1"], mixw["b_pw1"], tm=tm, name="b_pw1")
            (s["c"], s["v2"]), got = conv_fwd(s["u"], mixw["vec"], tm=tmc, name="conv_fwd", comm=gf)
            s["r1"], s["x1"] = mm_res_ln(s["v2"], mixw["pw2"], mixw["vec"][35:36], h, ln_mix_g[i:i + 1],
                                         ln_mix_b[i:i + 1], tm=tm, name="b_pw2_ln")
        s["w1"], s["w2"] = got[0], got[1].reshape(F, D)
        gm = GatherChips(mix_shards(i + 1)) if i + 1 < DEPTH else None
        (s["z"], s["r2"], h), got = ffn_fwd(s["x1"], s["w1"], s["w2"], ln_ffn_g[i:i + 1], ln_ffn_b[i:i + 1],
                                            tm=tm, tf=tf, name="ffn_fwd", comm=gm)
        if gm is not None:
            mixw = mix_weights(i + 1, got)
        saved.append(s)

    loss_part, dh = loss_grad(h, target, tm=tm)
    loss = lax.psum(loss_part[0, 0], ("x", "y", "c"))

    gr = {k: [None] * DEPTH for k in ("ln_mix_g", "ln_mix_b", "ln_ffn_g", "ln_ffn_b", "ffn_w1", "ffn_w2")}
    for k in ("a_w_in", "a_w_out", "a_norm_g", "a_dlb", "b_w_pw1", "b_w_pw2", "b_vec", "b_b_pw2"):
        gr[k] = [None] * 2
    pending = None

    def take_mixer(i, slots):
        got = _reduced(slots, "sum_mix_grads")
        j = i // 2
        if i % 2 == 0:
            gr["a_w_in"][j], gr["a_w_out"][j] = got
        else:
            gr["b_w_pw1"][j], gr["b_w_pw2"][j] = got

    for i in reversed(range(DEPTH)):
        j = i // 2
        s = saved[i]
        mixw = s["mixw"]
        sm = ScatterPieces(pending[1]) if pending is not None else None
        (dz, dx1, drb2, sums2), slots = ffn_bwd_dx(dh, s["r2"], ln_ffn_g[i:i + 1], s["z"], s["w1"], s["w2"],
                                                   tm=tm, tf=tf, name="ffn_bwd_dx", comm=sm)
        if pending is not None:
            take_mixer(pending[0], slots)
        gr["ln_ffn_g"][i], gr["ln_ffn_b"][i] = sums2[0], sums2[1]
        dw1 = mm_tn(s["x1"], dz[None], tm=tm, tk=D, tn=F // 4, name="ffn_dw1")[0]
        dw2 = mm_tn(s["z"], drb2[None], tm=tm, tk=F // 4, tn=D, relu2=True, name="ffn_dw2")[0, 0]
        sf = ScatterPieces([dw1, dw2.reshape(4, F // 4, D)])
        wmix = mixw["w_out"] if i % 2 == 0 else mixw["pw2"]
        dr1, drb1, dmo, sums1 = ln_bwd_mm(dx1, s["r1"], ln_mix_g[i:i + 1], wmix, tm=tm, name="mix_ln_bwd")
        gr["ln_mix_g"][i], gr["ln_mix_b"][i] = sums1[0], sums1[1]
        if i % 2 == 0:
            dwo = mm_tn(s["og"], drb1[None], tm=tm, tk=D, tn=D, name="a_dw_out")[0, 0].reshape(4, DS, D)
            (dproj, hs), slots = hgrn_bwd(s["proj"], s["o"], dmo, s["st"], lb_all[j:j + 1], a_norm_g[j:j + 1], rb=rb,
                                          hp=hp, cu=cu, name="hgrn_bwd", comm=sf)
            gr["a_norm_g"][j], gr["a_dlb"][j] = hs[0], hs[1]
            dwi = mm_tn(s["x"], dproj, tm=tm, tk=D, tn=D, name="a_dw_in")[:, 0]
            dh = mm_nt_acc(dproj, mixw["w_in"], dr1, tm=tm, name="a_dx")
        else:
            dwo = mm_tn(s["v2"], drb1[None], tm=tm, tk=D, tn=D, name="b_dw_pw2")[0, 0].reshape(4, DS, D)
            (du, cs), slots = conv_bwd(dmo, s["c"], s["u"], mixw["vec"], tm=tmc, name="conv_bwd", comm=sf)
            gr["b_vec"][j], gr["b_b_pw2"][j] = cs, sums1[2]
            dwi = mm_tn(s["x"], du, tm=tm, tk=D, tn=D // 2, name="b_dw_pw1").reshape(4, D, D // 2)
            dh = mm_nt_acc(du, mixw["pw1"], dr1, tm=tm, name="b_dx")
        gr["ffn_w1"][i], gr["ffn_w2"][i] = _reduced(slots, "sum_ffn_grads")
        pending = (i, [dwi, dwo])
    take_mixer(pending[0], comm_call(ScatterPieces(pending[1]), name="scatter_last"))
    grad_x = dh[None]

    small = {k: jnp.stack(gr[k]) for k in ("ln_mix_g", "ln_mix_b", "ln_ffn_g", "ln_ffn_b", "a_norm_g", "b_vec", "b_b_pw2")}
    small["a_lb_logits"] = lb_bwd(a_lb_logits, jnp.stack(gr["a_dlb"]))
    small_names = ["ln_mix_g", "ln_mix_b", "ln_ffn_g", "ln_ffn_b", "a_lb_logits", "a_norm_g", "b_b_pw2", "b_vec"]
    rows = [small[k].reshape(-1, D) for k in small_names]
    counts = [r.shape[0] for r in rows]
    pad = (-sum(counts)) % 8
    summed = all_reduce_small(jnp.concatenate(rows + ([jnp.zeros((pad, D), F32)] if pad else []), axis=0))
    sm = {}
    off = 0
    for k, n in zip(small_names, counts):
        sm[k] = summed[off:off + n]
        off += n
    bvec = sm["b_vec"].reshape(2, VEC_ROWS, D)

    def shard_cols(a):
        return lax.dynamic_slice_in_dim(a, chip * DS, DS, axis=a.ndim - 1)

    grads = {k: jnp.stack(gr[k]) for k in ("ffn_w1", "ffn_w2", "a_w_in", "a_w_out", "b_w_pw1", "b_w_pw2")}
    for k in ("ln_mix_g", "ln_mix_b", "ln_ffn_g", "ln_ffn_b", "a_lb_logits", "a_norm_g"):
        grads[k] = sm[k]
    grads["b_b_pw1"] = lax.dynamic_slice_in_dim(bvec[:, 36:38, :].reshape(2, 2 * D), chip * (D // 2), D // 2, axis=1)
    grads["b_w_dw"] = shard_cols(bvec[:, 0:CONV_W, :])
    grads["b_b_dw"] = shard_cols(bvec[:, 32, :])
    grads["b_ln_g"] = shard_cols(bvec[:, 33, :])
    grads["b_ln_b"] = shard_cols(bvec[:, 34, :])
    grads["b_b_pw2"] = shard_cols(sm["b_b_pw2"])

    delta, new_m, new_v = {}, {}, {}
    for k in names:
        delta[k], new_m[k], new_v[k] = _adam_nd(w[k], grads[k], m[k], v[k], "adamw_" + k)
    return (loss, grad_x, *[grads[k] for k in names], *[delta[k] for k in names],
            *[new_m[k] for k in names], *[new_v[k] for k in names])
```

```python
import jax
import jax.numpy as jnp
from jax import lax
from jax.experimental import pallas as pl
from jax.experimental.pallas import tpu as pltpu

F32 = jnp.float32
BF16 = jnp.bfloat16
MESH = pl.DeviceIdType.MESH

DEPTH = 4
ALPHA = (2.0 * DEPTH) ** 0.25
LN_EPS = 1e-5
RMS_EPS = 1e-6
GATE_EPS = 1e-6
HEAD = 128
CHUNK = 128
SUB = 16
PAIR = 2
CONV_W = 31
HALO = 32
VEC_ROWS = 40
CONV_RB = 32
ADAM_LR, ADAM_B1, ADAM_B2, ADAM_EPS, ADAM_WD, ADAM_STEP = 0.001, 0.9, 0.999, 1e-08, 0.01, 10
VMEM_LIMIT = 56 * 1024 * 1024
ANY = pl.BlockSpec(memory_space=pl.ANY)


def _dot(a, b):
    return jnp.dot(a, b, preferred_element_type=F32)


def _dot_nt(a, b):
    return lax.dot_general(a, b, (((1,), (1,)), ((), ())), preferred_element_type=F32)


def _dot_tn(a, b):
    return lax.dot_general(a, b, (((0,), (0,)), ((), ())), preferred_element_type=F32)


def _sigmoid(x):
    return 1.0 / (1.0 + jnp.exp(-x))


def _ln_stats(r):
    mu = jnp.mean(r, axis=-1, keepdims=True)
    xc = r - mu
    var = jnp.mean(xc * xc, axis=-1, keepdims=True)
    rstd = lax.rsqrt(var + LN_EPS)
    return xc * rstd, rstd


def _ln_bwd(dy, xhat, rstd, g):
    dyg = dy * g
    m1 = jnp.mean(dyg, axis=-1, keepdims=True)
    m2 = jnp.mean(dyg * xhat, axis=-1, keepdims=True)
    return rstd * (dyg - m1 - xhat * m2)


def _place():
    return lax.axis_index("x"), lax.axis_index("y"), lax.axis_index("c")


class GatherChips:
    def __init__(self, arrs):
        self.ins = list(arrs)
        n = len(arrs)
        self.out_shapes = [jax.ShapeDtypeStruct((4,) + a.shape, a.dtype) for a in arrs]
        self.sems = [pltpu.SemaphoreType.DMA((3 * n,)), pltpu.SemaphoreType.DMA((3 * n,)),
                     pltpu.SemaphoreType.DMA((n,))]

    def copies(self, ins, outs, send, recv, loc):
        x, y, c = _place()
        me = 2 * x + y
        local, remote = [], []
        for a in range(len(ins)):
            local.append(pltpu.make_async_copy(ins[a], outs[a].at[me], loc.at[a]))
            for j, (px, py) in enumerate([(1 - x, y), (x, 1 - y), (1 - x, 1 - y)]):
                remote.append(pltpu.make_async_remote_copy(
                    src_ref=ins[a], dst_ref=outs[a].at[me], send_sem=send.at[3 * a + j], recv_sem=recv.at[3 * a + j],
                    device_id=(px, py, c), device_id_type=MESH))
        return local + remote


class ScatterPieces:
    def __init__(self, arrs):
        self.ins = list(arrs)
        n = len(arrs)
        self.out_shapes = [jax.ShapeDtypeStruct((8,) + a.shape[1:], a.dtype) for a in arrs]
        self.sems = [pltpu.SemaphoreType.DMA((7 * n,)), pltpu.SemaphoreType.DMA((7 * n,)),
                     pltpu.SemaphoreType.DMA((n,))]

    def copies(self, ins, outs, send, recv, loc):
        x, y, c = _place()
        me = 4 * x + 2 * y + c
        local, remote = [], []
        for a in range(len(ins)):
            local.append(pltpu.make_async_copy(ins[a].at[2 * x + y], outs[a].at[me], loc.at[a]))
            k = 0
            for fx in (0, 1):
                for fy in (0, 1):
                    for fc in (0, 1):
                        if fx or fy or fc:
                            tx, ty = x ^ fx, y ^ fy
                            remote.append(pltpu.make_async_remote_copy(
                                src_ref=ins[a].at[2 * tx + ty], dst_ref=outs[a].at[me],
                                send_sem=send.at[7 * a + k], recv_sem=recv.at[7 * a + k],
                                device_id=(tx, ty, c ^ fc), device_id_type=MESH))
                            k += 1
        return local + remote


def carried_call(body, comm, *, name, grid, in_specs, out_specs, out_shape, scratch_shapes, args):
    sem = ("arbitrary",) * len(grid)
    params = pltpu.CompilerParams(dimension_semantics=sem, vmem_limit_bytes=VMEM_LIMIT)
    if comm is None:
        res = pl.pallas_call(body, name=name, grid=grid, in_specs=in_specs, out_specs=out_specs, out_shape=out_shape,
                             scratch_shapes=scratch_shapes, compiler_params=params)(*args)
        return res, []
    ni, no, nscr = len(in_specs), len(out_specs), len(scratch_shapes)
    ci, co = len(comm.ins), len(comm.out_shapes)

    def both(*refs):
        ins, refs = refs[:ni], refs[ni:]
        cins, refs = refs[:ci], refs[ci:]
        outs, refs = refs[:no], refs[no:]
        couts, refs = refs[:co], refs[co:]
        scr, sems = refs[:nscr], refs[nscr:]
        first = pl.program_id(0) == 0
        last = pl.program_id(0) == grid[0] - 1
        for d in range(1, len(grid)):
            first = first & (pl.program_id(d) == 0)
            last = last & (pl.program_id(d) == grid[d] - 1)

        @pl.when(first)
        def _():
            for cp in comm.copies(cins, couts, *sems):
                cp.start()

        body(*ins, *outs, *scr)

        @pl.when(last)
        def _():
            for cp in comm.copies(cins, couts, *sems):
                cp.wait()

    res = pl.pallas_call(
        both, name=name, grid=grid, in_specs=list(in_specs) + [ANY] * ci, out_specs=list(out_specs) + [ANY] * co,
        out_shape=list(out_shape) + comm.out_shapes, scratch_shapes=list(scratch_shapes) + comm.sems,
        compiler_params=params)(*args, *comm.ins)
    return res[:no], res[no:]


def comm_call(comm, *, name):
    ci, co = len(comm.ins), len(comm.out_shapes)

    def body(*refs):
        cps = comm.copies(refs[:ci], refs[ci:ci + co], *refs[ci + co:])
        for cp in cps:
            cp.start()
        for cp in cps:
            cp.wait()

    return pl.pallas_call(body, name=name, in_specs=[ANY] * ci, out_specs=[ANY] * co, out_shape=comm.out_shapes,
                          scratch_shapes=comm.sems, compiler_params=pltpu.CompilerParams(has_side_effects=True))(*comm.ins)


def all_reduce_small(v):
    R, C = v.shape

    def body(v_ref, o_ref, slots, send, recv):
        x, y, c = _place()
        me = 4 * x + 2 * y + c
        slots[me] = v_ref[...]
        cps = []
        k = 0
        for fx in (0, 1):
            for fy in (0, 1):
                for fc in (0, 1):
                    if fx or fy or fc:
                        cps.append(pltpu.make_async_remote_copy(
                            src_ref=v_ref, dst_ref=slots.at[me], send_sem=send.at[k], recv_sem=recv.at[k],
                            device_id=(x ^ fx, y ^ fy, c ^ fc), device_id_type=MESH))
                        k += 1
        for cp in cps:
            cp.start()
        for cp in cps:
            cp.wait()
        acc = slots[0]
        for d in range(1, 8):
            acc = acc + slots[d]
        o_ref[...] = acc

    vm = pl.BlockSpec(memory_space=pltpu.VMEM)
    return pl.pallas_call(
        body, name="all_reduce_small", in_specs=[vm], out_specs=vm,
        out_shape=jax.ShapeDtypeStruct((R, C), F32),
        scratch_shapes=[pltpu.VMEM((8, R, C), F32), pltpu.SemaphoreType.DMA((7,)), pltpu.SemaphoreType.DMA((7,))],
        compiler_params=pltpu.CompilerParams(has_side_effects=True, vmem_limit_bytes=VMEM_LIMIT),
    )(v)


def _call(body, *, name, grid, in_specs, out_specs, out_shape, scratch_shapes=(), args):
    res, _ = carried_call(body, None, name=name, grid=grid, in_specs=in_specs, out_specs=out_specs,
                          out_shape=out_shape, scratch_shapes=list(scratch_shapes), args=args)
    return res


def mm_groups(a, w, bias, *, tm, name):
    T, K = a.shape
    G, _, N = w.shape

    def body(a_ref, w_ref, b_ref, o_ref):
        o_ref[...] = _dot(a_ref[...].astype(BF16), w_ref[...]) + b_ref[...]

    return _call(
        body, name=name, grid=(G, T // tm),
        in_specs=[pl.BlockSpec((tm, K), lambda g, i: (i, 0)),
                  pl.BlockSpec((None, K, N), lambda g, i: (g, 0, 0)),
                  pl.BlockSpec((None, 1, N), lambda g, i: (g, 0, 0))],
        out_specs=[pl.BlockSpec((None, tm, N), lambda g, i: (g, i, 0))],
        out_shape=[jax.ShapeDtypeStruct((G, T, N), F32)], args=(a, w, bias))[0]


def mm_res_ln(a, w, bias, res, g, b, *, tm, name):
    T, K = a.shape
    N = w.shape[1]

    def body(a_ref, w_ref, bias_ref, res_ref, g_ref, b_ref, r_ref, y_ref):
        r = ALPHA * res_ref[...] + _dot(a_ref[...], w_ref[...]) + bias_ref[...]
        r_ref[...] = r
        xhat, _ = _ln_stats(r)
        y_ref[...] = xhat * g_ref[...] + b_ref[...]

    row = lambda i: (i, 0)
    fix = lambda i: (0, 0)
    return _call(
        body, name=name, grid=(T // tm,),
        in_specs=[pl.BlockSpec((tm, K), row), pl.BlockSpec((K, N), fix), pl.BlockSpec((1, N), fix),
                  pl.BlockSpec((tm, N), row), pl.BlockSpec((1, N), fix), pl.BlockSpec((1, N), fix)],
        out_specs=[pl.BlockSpec((tm, N), row), pl.BlockSpec((tm, N), row)],
        out_shape=[jax.ShapeDtypeStruct((T, N), F32), jax.ShapeDtypeStruct((T, N), F32)],
        args=(a, w, bias, res, g, b))


def ffn_fwd(x, w1, w2, g, b, *, tm, tf, name, comm=None):
    T, D = x.shape
    NC, _, FC = w1.shape
    F = NC * FC
    per = FC // tf
    nf = F // tf

    def body(x_ref, w1_ref, w2_ref, g_ref, b_ref, z_ref, r_ref, y_ref, acc_ref, xb_ref):
        f = pl.program_id(1)

        @pl.when(f == 0)
        def _():
            acc_ref[...] = jnp.zeros_like(acc_ref)
            xb_ref[...] = x_ref[...].astype(BF16)

        z = _dot(xb_ref[...], w1_ref[...])
        z_ref[...] = z.astype(BF16)
        h = jnp.square(jnp.maximum(z, 0.0)).astype(BF16)
        acc_ref[...] += _dot(h, w2_ref[...])

        @pl.when(f == nf - 1)
        def _():
            r = ALPHA * x_ref[...] + acc_ref[...]
            r_ref[...] = r
            xhat, _ = _ln_stats(r)
            y_ref[...] = xhat * g_ref[...] + b_ref[...]

    return carried_call(
        body, comm, name=name, grid=(T // tm, nf),
        in_specs=[pl.BlockSpec((tm, D), lambda i, f: (i, 0)),
                  pl.BlockSpec((None, D, tf), lambda i, f: (f // per, 0, f % per)),
                  pl.BlockSpec((tf, D), lambda i, f: (f, 0)),
                  pl.BlockSpec((1, D), lambda i, f: (0, 0)),
                  pl.BlockSpec((1, D), lambda i, f: (0, 0))],
        out_specs=[pl.BlockSpec((tm, tf), lambda i, f: (i, f)),
                   pl.BlockSpec((tm, D), lambda i, f: (i, 0)),
                   pl.BlockSpec((tm, D), lambda i, f: (i, 0))],
        out_shape=[jax.ShapeDtypeStruct((T, F), BF16), jax.ShapeDtypeStruct((T, D), F32),
                   jax.ShapeDtypeStruct((T, D), F32)],
        scratch_shapes=[pltpu.VMEM((tm, D), F32), pltpu.VMEM((tm, D), BF16)],
        args=(x, w1, w2, g, b))


def ln_bwd_mm(dy, r, g, w, *, tm, name):
    T, N = dy.shape
    Ko = w.shape[0]

    def body(dy_ref, r_ref, g_ref, w_ref, dr_ref, drb_ref, o_ref, s_ref):
        @pl.when(pl.program_id(0) == 0)
        def _():
            s_ref[...] = jnp.zeros_like(s_ref)

        dy_ = dy_ref[...]
        xhat, rstd = _ln_stats(r_ref[...])
        dr = _ln_bwd(dy_, xhat, rstd, g_ref[...])
        dr_ref[...] = dr
        drb = dr.astype(BF16)
        drb_ref[...] = drb
        o_ref[...] = _dot_nt(drb, w_ref[...])
        s_ref[0:1, :] += jnp.sum(dy_ * xhat, axis=0, keepdims=True)
        s_ref[1:2, :] += jnp.sum(dy_, axis=0, keepdims=True)
        s_ref[2:3, :] += jnp.sum(dr, axis=0, keepdims=True)

    row = lambda i: (i, 0)
    fix = lambda i: (0, 0)
    return _call(
        body, name=name, grid=(T // tm,),
        in_specs=[pl.BlockSpec((tm, N), row), pl.BlockSpec((tm, N), row), pl.BlockSpec((1, N), fix),
                  pl.BlockSpec((Ko, N), fix)],
        out_specs=[pl.BlockSpec((tm, N), row), pl.BlockSpec((tm, N), row), pl.BlockSpec((tm, Ko), row),
                   pl.BlockSpec((8, N), fix)],
        out_shape=[jax.ShapeDtypeStruct((T, N), F32), jax.ShapeDtypeStruct((T, N), BF16),
                   jax.ShapeDtypeStruct((T, Ko), F32), jax.ShapeDtypeStruct((8, N), F32)],
        args=(dy, r, g, w))


def ffn_bwd_dx(dy, r, g, z, w1, w2, *, tm, tf, name, comm=None):
    T, D = dy.shape
    NC, _, FC = w1.shape
    F = NC * FC
    per = FC // tf
    nf = F // tf

    def body(dy_ref, r_ref, g_ref, z_ref, w1_ref, w2_ref, dz_ref, dx_ref, drb_ref, s_ref, dr_scr, acc_ref):
        i = pl.program_id(0)
        f = pl.program_id(1)

        @pl.when((i == 0) & (f == 0))
        def _():
            s_ref[...] = jnp.zeros_like(s_ref)

        @pl.when(f == 0)
        def _():
            dy_ = dy_ref[...]
            xhat, rstd = _ln_stats(r_ref[...])
            dr = _ln_bwd(dy_, xhat, rstd, g_ref[...])
            dr_scr[...] = dr
            drb_ref[...] = dr.astype(BF16)
            acc_ref[...] = jnp.zeros_like(acc_ref)
            s_ref[0:1, :] += jnp.sum(dy_ * xhat, axis=0, keepdims=True)
            s_ref[1:2, :] += jnp.sum(dy_, axis=0, keepdims=True)

        dh = _dot_nt(drb_ref[...], w2_ref[...])
        dz = (dh * (2.0 * jnp.maximum(z_ref[...].astype(F32), 0.0))).astype(BF16)
        dz_ref[...] = dz
        acc_ref[...] += _dot_nt(dz, w1_ref[...])

        @pl.when(f == nf - 1)
        def _():
            dx_ref[...] = ALPHA * dr_scr[...] + acc_ref[...]

    return carried_call(
        body, comm, name=name, grid=(T // tm, nf),
        in_specs=[pl.BlockSpec((tm, D), lambda i, f: (i, 0)),
                  pl.BlockSpec((tm, D), lambda i, f: (i, 0)),
                  pl.BlockSpec((1, D), lambda i, f: (0, 0)),
                  pl.BlockSpec((tm, tf), lambda i, f: (i, f)),
                  pl.BlockSpec((None, D, tf), lambda i, f: (f // per, 0, f % per)),
                  pl.BlockSpec((tf, D), lambda i, f: (f, 0))],
        out_specs=[pl.BlockSpec((tm, tf), lambda i, f: (i, f)),
                   pl.BlockSpec((tm, D), lambda i, f: (i, 0)),
                   pl.BlockSpec((tm, D), lambda i, f: (i, 0)),
                   pl.BlockSpec((8, D), lambda i, f: (0, 0))],
        out_shape=[jax.ShapeDtypeStruct((T, F), BF16), jax.ShapeDtypeStruct((T, D), F32),
                   jax.ShapeDtypeStruct((T, D), BF16), jax.ShapeDtypeStruct((8, D), F32)],
        scratch_shapes=[pltpu.VMEM((tm, D), F32), pltpu.VMEM((tm, D), F32)],
        args=(dy, r, g, z, w1, w2))


def mm_tn(a, b, *, tm, tk, tn, relu2=False, name):
    T, K = a.shape
    G, _, N = b.shape
    nt = T // tm

    def body(a_ref, b_ref, o_ref, acc_ref):
        t = pl.program_id(3)

        @pl.when(t == 0)
        def _():
            acc_ref[...] = jnp.zeros_like(acc_ref)

        av = a_ref[...]
        if relu2:
            av = jnp.square(jnp.maximum(av.astype(F32), 0.0))
        acc_ref[...] += _dot_tn(av.astype(BF16), b_ref[...])

        @pl.when(t == nt - 1)
        def _():
            o_ref[...] = acc_ref[...].astype(BF16)

    return _call(
        body, name=name, grid=(G, K // tk, N // tn, nt),
        in_specs=[pl.BlockSpec((tm, tk), lambda g, k, n, t: (t, k)),
                  pl.BlockSpec((None, tm, tn), lambda g, k, n, t: (g, t, n))],
        out_specs=[pl.BlockSpec((None, None, tk, tn), lambda g, k, n, t: (g, n, k, 0))],
        out_shape=[jax.ShapeDtypeStruct((G, N // tn, K, tn), BF16)],
        scratch_shapes=[pltpu.VMEM((tk, tn), F32)], args=(a, b))[0]


def mm_nt_acc(dy, w, base, *, tm, name):
    G, T, N = dy.shape
    K = w.shape[1]

    def body(dy_ref, w_ref, base_ref, o_ref):
        g = pl.program_id(1)

        @pl.when(g == 0)
        def _():
            o_ref[...] = ALPHA * base_ref[...]

        o_ref[...] += _dot_nt(dy_ref[...], w_ref[...])

    return _call(
        body, name=name, grid=(T // tm, G),
        in_specs=[pl.BlockSpec((None, tm, N), lambda i, g: (g, i, 0)),
                  pl.BlockSpec((None, K, N), lambda i, g: (g, 0, 0)),
                  pl.BlockSpec((tm, K), lambda i, g: (i, 0))],
        out_specs=[pl.BlockSpec((tm, K), lambda i, g: (i, 0))],
        out_shape=[jax.ShapeDtypeStruct((T, K), F32)], args=(dy, w, base))[0]


def _split3(x):
    x1 = x.astype(BF16)
    r1 = x - x1.astype(F32)
    x2 = r1.astype(BF16)
    x3 = (r1 - x2.astype(F32)).astype(BF16)
    return x1, x2, x3


def _tri_dot(tri, x):
    x1, x2, x3 = _split3(x)
    return _dot(tri, x1) + _dot(tri, x2) + _dot(tri, x3)


def _gates(pq, fz, lb):
    sg = _sigmoid(fz)
    f = lb + (1.0 - lb) * sg
    logf = jnp.log(jnp.maximum(f, GATE_EPS))
    sq = _sigmoid(pq)
    return pq * sq, 1.0 - f, logf, f, sg, sq


def _block_diag_mask():
    ri = lax.broadcasted_iota(jnp.int32, (PAIR * HEAD, PAIR * HEAD), 0) // HEAD
    ci = lax.broadcasted_iota(jnp.int32, (PAIR * HEAD, PAIR * HEAD), 1) // HEAD
    return ri == ci


def _fill_off_diagonal(q_s, k_s, b_s, lhs, rhs):
    for i in range(1, CHUNK // SUB):
        lo = i * SUB
        ref = b_s[lo - 1:lo, :]
        qt = (q_s[lo:lo + SUB, :] * jnp.exp(b_s[lo:lo + SUB, :] - ref)).astype(BF16)
        kt = (k_s[0:lo, :] * jnp.exp(ref - b_s[0:lo, :])).astype(BF16)
        for h in range(PAIR):
            hl = slice(h * HEAD, (h + 1) * HEAD)
            lhs[h, lo:lo + SUB, (i - 1) * HEAD:i * HEAD] = qt[:, hl]
            rhs[h, 0:lo, (i - 1) * HEAD:i * HEAD] = kt[:, hl]


def hgrn_fwd(proj, lb, norm_g, *, rb, name, comm=None):
    _, T, D = proj.shape
    H = D // HEAD
    nb = T // rb
    nck = rb // CHUNK
    nsub = CHUNK // SUB
    W = PAIR * HEAD
    fam = CHUNK * SUB

    def body(pq_ref, fz_ref, pv_ref, pg_ref, lb_ref, ng_ref, o_ref, og_ref, st_ref, S, q_s, k_s, v_s, b_s, lhs, rhs, p_s):
        @pl.when(pl.program_id(1) == 0)
        def _():
            S[...] = jnp.zeros_like(S)

        @pl.when((pl.program_id(0) == 0) & (pl.program_id(1) == 0))
        def _():
            lhs[...] = jnp.zeros_like(lhs)
            rhs[...] = jnp.zeros_like(rhs)

        ri = lax.broadcasted_iota(jnp.int32, (CHUNK, CHUNK), 0)
        ci = lax.broadcasted_iota(jnp.int32, (CHUNK, CHUNK), 1)
        tri = (ci <= ri).astype(BF16)
        ones = jnp.ones((HEAD, HEAD), BF16)
        bd = _block_diag_mask()

        def trip(c, carry):
            base = c * CHUNK
            rows = pl.ds(pl.multiple_of(base, CHUNK), CHUNK)
            q, k, logf, _, _, _ = _gates(pq_ref[rows, :], fz_ref[rows, :], lb_ref[...])
            v = pv_ref[rows, :]
            b = _tri_dot(tri, logf)
            q_s[...] = q
            k_s[...] = k
            v_s[...] = v
            b_s[...] = b
            bl = b_s[CHUNK - 1:CHUNK, :]
            upd = _dot_tn(v.astype(BF16), (k * jnp.exp(bl - b)).astype(BF16))
            Sv = S[...]
            for h in range(PAIR):
                st_ref[h, c] = Sv[h * HEAD:(h + 1) * HEAD, h * HEAD:(h + 1) * HEAD]
            o_int = _dot_nt((q * jnp.exp(b)).astype(BF16), Sv.astype(BF16))
            S[...] = Sv * jnp.exp(bl) + jnp.where(bd, upd, 0.0)
            _fill_off_diagonal(q_s, k_s, b_s, lhs, rhs)
            off = []
            for h in range(PAIR):
                a = _dot_nt(lhs[h], rhs[h])
                off.append(_dot(a.astype(BF16), v_s[:, h * HEAD:(h + 1) * HEAD].astype(BF16)))
            for i in range(nsub):
                lo = i * SUB
                for s in range(SUB):
                    m = lax.broadcasted_iota(jnp.int32, (SUB, W), 0) >= s
                    at = (i * SUB + s) * SUB
                    e = jnp.exp(b_s[lo:lo + SUB, :] - b_s[lo + s:lo + s + 1, :])
                    p = jnp.where(m, q_s[lo:lo + SUB, :] * (k_s[lo + s:lo + s + 1, :] * e), 0.0).astype(BF16)
                    for h in range(PAIR):
                        p_s[h * fam + at:h * fam + at + SUB, :] = p[:, h * HEAD:(h + 1) * HEAD]
            rs = _dot(p_s[...], ones)
            for i in range(nsub):
                lo = i * SUB
                blk = pl.ds(pl.multiple_of(base + lo, SUB), SUB)
                for h in range(PAIR):
                    hl = slice(h * HEAD, (h + 1) * HEAD)
                    acc = o_int[lo:lo + SUB, hl] + off[h][lo:lo + SUB, :]
                    for s in range(SUB):
                        at = h * fam + (i * SUB + s) * SUB
                        acc = acc + rs[at:at + SUB, :] * v_s[lo + s:lo + s + 1, hl]
                    o_ref[blk, hl] = acc
                    rinv = lax.rsqrt(jnp.mean(acc * acc, axis=-1, keepdims=True) + RMS_EPS)
                    pg = pg_ref[blk, hl]
                    og_ref[blk, hl] = (acc * rinv * ng_ref[:, hl] * (pg * _sigmoid(pg))).astype(BF16)
            return carry

        lax.fori_loop(0, nck, trip, 0)

    def grp(gi):
        return pl.BlockSpec((None, rb, W), lambda h, r: (gi, r, h))

    vec = pl.BlockSpec((1, W), lambda h, r: (0, h))
    return carried_call(
        body, comm, name=name, grid=(H // PAIR, nb),
        in_specs=[grp(0), grp(1), grp(2), grp(3), vec, vec],
        out_specs=[pl.BlockSpec((rb, W), lambda h, r: (r, h)),
                   pl.BlockSpec((rb, W), lambda h, r: (r, h)),
                   pl.BlockSpec((PAIR, nck, HEAD, HEAD), lambda h, r: (h, r, 0, 0))],
        out_shape=[jax.ShapeDtypeStruct((T, D), F32), jax.ShapeDtypeStruct((T, D), BF16),
                   jax.ShapeDtypeStruct((H, T // CHUNK, HEAD, HEAD), F32)],
        scratch_shapes=[pltpu.VMEM((W, W), F32)] + [pltpu.VMEM((CHUNK, W), F32)] * 4
        + [pltpu.VMEM((PAIR, CHUNK, (nsub - 1) * HEAD), BF16)] * 2 + [pltpu.VMEM((PAIR * fam, HEAD), BF16)],
        args=(proj, proj, proj, proj, lb, norm_g))


def hgrn_bwd(proj, o, dog, states, lb, norm_g, *, rb, name, comm=None):
    _, T, D = proj.shape
    H = D // HEAD
    nb = T // rb
    nck = rb // CHUNK
    nsub = CHUNK // SUB
    W = PAIR * HEAD
    fam = CHUNK * SUB

    def body(pq_ref, fz_ref, pv_ref, pg_ref, o_ref, dog_ref, st_ref, lb_ref, ng_ref, dp_ref, s_ref,
             dS, S0, q_s, k_s, v_s, b_s, do_s, dq_s, dk_s, dv_s, lhs, rhs, ke_s, qe_s, p_s):
        @pl.when(pl.program_id(1) == 0)
        def _():
            dS[...] = jnp.zeros_like(dS)
            s_ref[...] = jnp.zeros_like(s_ref)

        @pl.when((pl.program_id(0) == 0) & (pl.program_id(1) == 0))
        def _():
            lhs[...] = jnp.zeros_like(lhs)
            rhs[...] = jnp.zeros_like(rhs)
            S0[...] = jnp.zeros_like(S0)

        ri = lax.broadcasted_iota(jnp.int32, (CHUNK, CHUNK), 0)
        ci = lax.broadcasted_iota(jnp.int32, (CHUNK, CHUNK), 1)
        tri = (ci <= ri).astype(BF16)
        triu = (ci >= ri).astype(BF16)
        below = (ri // SUB) > (ci // SUB)
        above = (ri // SUB) < (ci // SUB)
        ones = jnp.ones((HEAD, HEAD), BF16)
        bd = _block_diag_mask()
        last_row = lax.broadcasted_iota(jnp.int32, (CHUNK, W), 0) == CHUNK - 1

        def trip(cc, carry):
            c = nck - 1 - cc
            rows = pl.ds(pl.multiple_of(c * CHUNK, CHUNK), CHUNK)
            lb_ = lb_ref[...]
            pq = pq_ref[rows, :]
            q, k, logf, f, sg, sq = _gates(pq, fz_ref[rows, :], lb_)
            v = pv_ref[rows, :]
            b = _tri_dot(tri, logf)
            dpg = []
            for h in range(PAIR):
                hl = slice(h * HEAD, (h + 1) * HEAD)
                oh = o_ref[rows, hl]
                dog_ = dog_ref[rows, hl]
                pg = pg_ref[rows, hl]
                ng = ng_ref[:, hl]
                spg = _sigmoid(pg)
                rinv = lax.rsqrt(jnp.mean(oh * oh, axis=-1, keepdims=True) + RMS_EPS)
                on = oh * rinv
                dpg.append(dog_ * (on * ng) * (spg * (1.0 + pg * (1.0 - spg))))
                don = dog_ * (pg * spg)
                s_ref[0:1, hl] += jnp.sum(don * on, axis=0, keepdims=True)
                dxn = don * ng
                do_s[:, hl] = rinv * (dxn - on * jnp.mean(dxn * on, axis=-1, keepdims=True))
                S0[hl, hl] = st_ref[h, c]
            q_s[...] = q
            k_s[...] = k
            v_s[...] = v
            b_s[...] = b
            do = do_s[...]
            dob = do.astype(BF16)
            vb = v.astype(BF16)
            eb = jnp.exp(b)
            bl = b_s[CHUNK - 1:CHUNK, :]
            ebl = jnp.exp(bl)
            ekk = jnp.exp(bl - b)
            upd = _dot_tn(dob, (q * eb).astype(BF16))
            S0v = S0[...]
            dSv = dS[...]
            dSb = dSv.astype(BF16)
            dq_s[...] = _dot(dob, S0v.astype(BF16)) * eb
            dk_state = _dot(vb, dSb) * ekk
            dk_s[...] = dk_state
            dv_s[...] = _dot_nt((k * ekk).astype(BF16), dSb)
            extra = jnp.sum(k * dk_state, axis=0, keepdims=True) + ebl * jnp.sum(S0v * dSv, axis=0, keepdims=True)
            dS[...] = dSv * ebl + jnp.where(bd, upd, 0.0)

            _fill_off_diagonal(q_s, k_s, b_s, lhs, rhs)
            dqb, dkb = [], []
            for h in range(PAIR):
                hl = slice(h * HEAD, (h + 1) * HEAD)
                doh = dob[:, hl]
                vh = vb[:, hl]
                at = _dot_nt(rhs[h], lhs[h])
                dv_s[:, hl] += _dot(at.astype(BF16), doh)
                da = jnp.where(below, _dot_nt(doh, vh), 0.0).astype(BF16)
                dat = jnp.where(above, _dot_nt(vh, doh), 0.0).astype(BF16)
                dqb.append(_dot(da, rhs[h]))
                dkb.append(_dot(dat, lhs[h]))
            for i in range(1, nsub):
                lo = i * SUB
                ref = b_s[lo - 1:lo, :]
                eq = jnp.exp(b_s[lo:lo + SUB, :] - ref)
                ek = jnp.exp(ref - b_s[0:lo, :])
                cb = slice((i - 1) * HEAD, i * HEAD)
                for h in range(PAIR):
                    hl = slice(h * HEAD, (h + 1) * HEAD)
                    dq_s[lo:lo + SUB, hl] += dqb[h][lo:lo + SUB, cb] * eq[:, hl]
                    dk_s[0:lo, hl] += dkb[h][0:lo, cb] * ek[:, hl]

            for i in range(nsub):
                lo = i * SUB
                for s in range(SUB):
                    m = lax.broadcasted_iota(jnp.int32, (SUB, W), 0) >= s
                    at = (i * SUB + s) * SUB
                    qi = q_s[lo:lo + SUB, :]
                    e = jnp.where(m, jnp.exp(b_s[lo:lo + SUB, :] - b_s[lo + s:lo + s + 1, :]), 0.0)
                    ke = k_s[lo + s:lo + s + 1, :] * e
                    ke_s[at:at + SUB, :] = ke
                    qe_s[at:at + SUB, :] = qi * e
                    pa = (qi * ke).astype(BF16)
                    pd = jnp.where(m, do_s[lo:lo + SUB, :] * v_s[lo + s:lo + s + 1, :], 0.0).astype(BF16)
                    for h in range(PAIR):
                        hl = slice(h * HEAD, (h + 1) * HEAD)
                        p_s[(2 * h) * fam + at:(2 * h) * fam + at + SUB, :] = pa[:, hl]
                        p_s[(2 * h + 1) * fam + at:(2 * h + 1) * fam + at + SUB, :] = pd[:, hl]
            rs = _dot(p_s[...], ones)
            for i in range(nsub):
                lo = i * SUB
                for h in range(PAIR):
                    hl = slice(h * HEAD, (h + 1) * HEAD)
                    doi = do_s[lo:lo + SUB, hl]
                    dqa = dq_s[lo:lo + SUB, hl]
                    for s in range(SUB):
                        at = (i * SUB + s) * SUB
                        acol = rs[(2 * h) * fam + at:(2 * h) * fam + at + SUB, :]
                        dacol = rs[(2 * h + 1) * fam + at:(2 * h + 1) * fam + at + SUB, :]
                        dqa = dqa + dacol * ke_s[at:at + SUB, hl]
                        dk_s[lo + s:lo + s + 1, hl] += jnp.sum(dacol * qe_s[at:at + SUB, hl], axis=0, keepdims=True)
                        dv_s[lo + s:lo + s + 1, hl] += jnp.sum(acol * doi, axis=0, keepdims=True)
                    dq_s[lo:lo + SUB, hl] = dqa

            dq = dq_s[...]
            dk = dk_s[...]
            db = q * dq - k * dk + jnp.where(last_row, extra, 0.0)
            dlogf = _tri_dot(triu, db)
            df = jnp.where(f > GATE_EPS, dlogf / jnp.maximum(f, GATE_EPS), 0.0) - dk
            s_ref[1:2, :] += jnp.sum(df * (1.0 - sg), axis=0, keepdims=True)
            dp_ref[0, rows, :] = (dq * (sq * (1.0 + pq * (1.0 - sq)))).astype(BF16)
            dp_ref[1, rows, :] = (df * (1.0 - lb_) * sg * (1.0 - sg)).astype(BF16)
            dp_ref[2, rows, :] = dv_s[...].astype(BF16)
            for h in range(PAIR):
                dp_ref[3, rows, h * HEAD:(h + 1) * HEAD] = dpg[h].astype(BF16)
            return carry

        lax.fori_loop(0, nck, trip, 0)

    def grp(gi):
        return pl.BlockSpec((None, rb, W), lambda h, r: (gi, nb - 1 - r, h))

    rowsp = pl.BlockSpec((rb, W), lambda h, r: (nb - 1 - r, h))
    vec = pl.BlockSpec((1, W), lambda h, r: (0, h))
    return carried_call(
        body, comm, name=name, grid=(H // PAIR, nb),
        in_specs=[grp(0), grp(1), grp(2), grp(3), rowsp, rowsp,
                  pl.BlockSpec((PAIR, nck, HEAD, HEAD), lambda h, r: (h, nb - 1 - r, 0, 0)), vec, vec],
        out_specs=[pl.BlockSpec((4, rb, W), lambda h, r: (0, nb - 1 - r, h)),
                   pl.BlockSpec((8, W), lambda h, r: (0, h))],
        out_shape=[jax.ShapeDtypeStruct((4, T, D), BF16), jax.ShapeDtypeStruct((8, D), F32)],
        scratch_shapes=[pltpu.VMEM((W, W), F32)] * 2 + [pltpu.VMEM((CHUNK, W), F32)] * 8
        + [pltpu.VMEM((PAIR, CHUNK, (nsub - 1) * HEAD), BF16)] * 2 + [pltpu.VMEM((fam, W), F32)] * 2
        + [pltpu.VMEM((2 * PAIR * fam, HEAD), BF16)],
        args=(proj, proj, proj, proj, o, dog, states, lb, norm_g))


def lb_fwd(logits):
    def body(l_ref, o_ref):
        l = l_ref[...]
        mx = jnp.max(l, axis=0, keepdims=True)
        e = jnp.exp(l - mx)
        sm = e / jnp.sum(e, axis=0, keepdims=True)
        o_ref[0:1, :] = jnp.zeros_like(sm[0:1, :])
        o_ref[1:2, :] = sm[1:2, :]

    return pl.pallas_call(body, name="lb_fwd", out_shape=jax.ShapeDtypeStruct(logits.shape, F32))(logits)


def lb_bwd(logits, dlb):
    def body(l_ref, d_ref, o_ref):
        l = l_ref[...]
        mx = jnp.max(l, axis=0, keepdims=True)
        e = jnp.exp(l - mx)
        sm = e / jnp.sum(e, axis=0, keepdims=True)
        inner = d_ref[1:2, :] * sm[1:2, :]
        o_ref[0:1, :] = sm[0:1, :] * (0.0 - inner)
        o_ref[1:2, :] = sm[1:2, :] * (d_ref[1:2, :] - inner)

    return pl.pallas_call(body, name="lb_bwd", out_shape=jax.ShapeDtypeStruct(logits.shape, F32))(logits, dlb)


def _shifted_copies(sh, rows):
    for b in range(1, 8):
        sh[b, 0:rows, :] = sh[0, b:b + rows, :]


def conv_fwd(u, vec, *, tm, name, comm=None):
    _, T, D = u.shape
    hb = tm // HALO
    nlc = D // HEAD

    def body(a_ref, gt_ref, ap_ref, gp_ref, w_ref, c_ref, v_ref, sh):
        i = pl.program_id(0)
        sh[0, HALO:HALO + tm, :] = a_ref[...] * _sigmoid(gt_ref[...])
        prev = ap_ref[...] * _sigmoid(gp_ref[...])
        sh[0, 0:HALO, :] = jnp.where(i > 0, prev, 0.0)
        _shifted_copies(sh, tm + HALO - 8)

        def rowblock(r, carry):
            r0 = r * CONV_RB
            for cl in range(nlc):
                ls = slice(cl * HEAD, (cl + 1) * HEAD)
                acc = jnp.zeros((CONV_RB, HEAD), F32) + w_ref[32:33, ls]
                for j in range(CONV_W):
                    o = j + 2
                    at = pl.ds(pl.multiple_of(r0 + o - o % 8, 8), CONV_RB)
                    acc = acc + w_ref[j:j + 1, ls] * sh[o % 8, at, ls]
                c_ref[pl.ds(pl.multiple_of(r0, CONV_RB), CONV_RB), ls] = acc
            return carry

        lax.fori_loop(0, tm // CONV_RB, rowblock, 0)
        xhat, _ = _ln_stats(c_ref[...])
        y = xhat * w_ref[33:34, :] + w_ref[34:35, :]
        v_ref[...] = (y * _sigmoid(y)).astype(BF16)

    cur = lambda gi: pl.BlockSpec((None, tm, D), lambda i: (gi, i, 0))
    prv = lambda gi: pl.BlockSpec((None, HALO, D), lambda i: (gi, jnp.maximum(i * hb - 1, 0), 0))
    return carried_call(
        body, comm, name=name, grid=(T // tm,),
        in_specs=[cur(0), cur(1), prv(0), prv(1), pl.BlockSpec((VEC_ROWS, D), lambda i: (0, 0))],
        out_specs=[pl.BlockSpec((tm, D), lambda i: (i, 0)), pl.BlockSpec((tm, D), lambda i: (i, 0))],
        out_shape=[jax.ShapeDtypeStruct((T, D), F32), jax.ShapeDtypeStruct((T, D), BF16)],
        scratch_shapes=[pltpu.VMEM((8, tm + HALO, D), F32)],
        args=(u, u, u, u, vec))


def conv_bwd(dv2, c, u, vec, *, tm, name, comm=None):
    _, T, D = u.shape
    hb = tm // HALO
    nt = T // tm
    nh = T // HALO
    nlc = D // HEAD
    acc_rows = {j: j for j in range(CONV_W)}
    acc_rows.update({36: CONV_W, 37: CONV_W + 1})

    def body(dv_ref, c_ref, dvn_ref, cn_ref, a_ref, gt_ref, ap_ref, gp_ref, w_ref, du_ref, s_ref, gsh, dsh, part):
        i = pl.program_id(0)

        @pl.when(i == 0)
        def _():
            s_ref[...] = jnp.zeros_like(s_ref)
            part[...] = jnp.zeros_like(part)

        gam = w_ref[33:34, :]
        bet = w_ref[34:35, :]

        def dconv(dv, cc):
            xhat, rstd = _ln_stats(cc)
            y = xhat * gam + bet
            sy = _sigmoid(y)
            dy = dv * (sy * (1.0 + y * (1.0 - sy)))
            return _ln_bwd(dy, xhat, rstd, gam), dy, xhat

        dc, dy, xhat = dconv(dv_ref[...], c_ref[...])
        dcn, _, _ = dconv(dvn_ref[...], cn_ref[...])
        dsh[0, 0:tm, :] = dc
        dsh[0, tm:tm + HALO, :] = jnp.where(i < nt - 1, dcn, 0.0)
        gsh[0, HALO:HALO + tm, :] = a_ref[...] * _sigmoid(gt_ref[...])
        gsh[0, 0:HALO, :] = jnp.where(i > 0, ap_ref[...] * _sigmoid(gp_ref[...]), 0.0)
        s_ref[32:33, :] += jnp.sum(dc, axis=0, keepdims=True)
        s_ref[33:34, :] += jnp.sum(dy * xhat, axis=0, keepdims=True)
        s_ref[34:35, :] += jnp.sum(dy, axis=0, keepdims=True)
        _shifted_copies(dsh, tm + HALO - 8)
        _shifted_copies(gsh, tm + HALO - 8)

        def fold8(x):
            acc = x[0:8, :]
            for g in range(1, CONV_RB // 8):
                acc = acc + x[8 * g:8 * g + 8, :]
            return acc

        for cl in range(nlc):
            ls = slice(cl * HEAD, (cl + 1) * HEAD)

            def rowblock(r, sums, ls=ls):
                r0 = r * CONV_RB
                rows = pl.ds(pl.multiple_of(r0, CONV_RB), CONV_RB)
                dcb = dsh[0, rows, ls]
                dglu = jnp.zeros((CONV_RB, HEAD), F32)
                new = []
                for j in range(CONV_W):
                    od = 30 - j
                    og = j + 2
                    atd = pl.ds(pl.multiple_of(r0 + od - od % 8, 8), CONV_RB)
                    atg = pl.ds(pl.multiple_of(r0 + og - og % 8, 8), CONV_RB)
                    dglu = dglu + w_ref[j:j + 1, ls] * dsh[od % 8, atd, ls]
                    new.append(sums[j] + fold8(dcb * gsh[og % 8, atg, ls]))
                a = a_ref[rows, ls]
                sgt = _sigmoid(gt_ref[rows, ls])
                da = (dglu * sgt).astype(BF16)
                dg = (dglu * a * sgt * (1.0 - sgt)).astype(BF16)
                du_ref[0, rows, ls] = da
                du_ref[1, rows, ls] = dg
                new.append(sums[CONV_W] + fold8(da.astype(F32)))
                new.append(sums[CONV_W + 1] + fold8(dg.astype(F32)))
                return tuple(new)

            zero = jnp.zeros((8, HEAD), F32)
            sums = lax.fori_loop(0, tm // CONV_RB, rowblock, (zero,) * (CONV_W + 2))
            for k in range(CONV_W + 2):
                part[8 * k:8 * k + 8, ls] += sums[k]

        @pl.when(i == nt - 1)
        def _():
            for row, k in acc_rows.items():
                s_ref[row:row + 1, :] = jnp.sum(part[8 * k:8 * k + 8, :], axis=0, keepdims=True)

    row = lambda i: (i, 0)
    nxt = lambda i: (jnp.minimum((i + 1) * hb, nh - 1), 0)
    cur = lambda gi: pl.BlockSpec((None, tm, D), lambda i: (gi, i, 0))
    prv = lambda gi: pl.BlockSpec((None, HALO, D), lambda i: (gi, jnp.maximum(i * hb - 1, 0), 0))
    fix = lambda i: (0, 0)
    return carried_call(
        body, comm, name=name, grid=(nt,),
        in_specs=[pl.BlockSpec((tm, D), row), pl.BlockSpec((tm, D), row),
                  pl.BlockSpec((HALO, D), nxt), pl.BlockSpec((HALO, D), nxt),
                  cur(0), cur(1), prv(0), prv(1), pl.BlockSpec((VEC_ROWS, D), fix)],
        out_specs=[pl.BlockSpec((2, tm, D), lambda i: (0, i, 0)), pl.BlockSpec((VEC_ROWS, D), fix)],
        out_shape=[jax.ShapeDtypeStruct((2, T, D), BF16), jax.ShapeDtypeStruct((VEC_ROWS, D), F32)],
        scratch_shapes=[pltpu.VMEM((8, tm + HALO, D), F32), pltpu.VMEM((8, tm + HALO, D), F32),
                        pltpu.VMEM((8 * (CONV_W + 2), D), F32)],
        args=(dv2, c, dv2, c, u, u, u, u, vec))


def loss_grad(y, target, *, tm):
    T, D = y.shape
    nt = T // tm

    def body(y_ref, t_ref, l_ref, d_ref, acc):
        i = pl.program_id(0)

        @pl.when(i == 0)
        def _():
            acc[...] = jnp.zeros_like(acc)

        e = y_ref[...] - t_ref[...]
        d_ref[...] = e * (1.0 / D)
        acc[...] += jnp.sum(e * e, axis=0, keepdims=True)

        @pl.when(i == nt - 1)
        def _():
            l_ref[...] = 0.5 * jnp.sum(acc[...], axis=1, keepdims=True) * (1.0 / D)

    row = lambda i: (i, 0)
    return _call(
        body, name="loss_grad", grid=(nt,),
        in_specs=[pl.BlockSpec((tm, D), row), pl.BlockSpec((tm, D), row)],
        out_specs=[pl.BlockSpec((1, 1), lambda i: (0, 0)), pl.BlockSpec((tm, D), row)],
        out_shape=[jax.ShapeDtypeStruct((1, 1), F32), jax.ShapeDtypeStruct((T, D), F32)],
        scratch_shapes=[pltpu.VMEM((1, D), F32)], args=(y, target))


def _rows_block(R, C, budget=1 << 20):
    tr = R
    while tr * C * 4 > budget and tr % 32 == 0:
        tr //= 2
    return tr


def adamw(w, g, m, v, *, name):
    R, C = w.shape
    tr = _rows_block(R, C)

    def body(w_ref, g_ref, m_ref, v_ref, d_ref, mo_ref, vo_ref):
        g_ = g_ref[...]
        mn = ADAM_B1 * m_ref[...] + (1.0 - ADAM_B1) * g_
        vn = ADAM_B2 * v_ref[...] + (1.0 - ADAM_B2) * jnp.square(g_)
        m_hat = mn / (1.0 - ADAM_B1 ** ADAM_STEP)
        v_hat = vn / (1.0 - ADAM_B2 ** ADAM_STEP)
        d_ref[...] = -ADAM_LR * (m_hat / (jnp.sqrt(v_hat) + ADAM_EPS) + ADAM_WD * w_ref[...])
        mo_ref[...] = mn
        vo_ref[...] = vn

    spec = pl.BlockSpec((tr, C), lambda i: (i, 0))
    sd = jax.ShapeDtypeStruct((R, C), F32)
    return _call(body, name=name, grid=(R // tr,), in_specs=[spec] * 4, out_specs=[spec] * 3, out_shape=[sd] * 3,
                 args=(w, g, m, v))


def sum_slots(slots, *, name):
    _, R, C = slots.shape
    tr = _rows_block(R, C, budget=1 << 19)

    def body(s_ref, o_ref):
        acc = s_ref[0].astype(F32)
        for d in range(1, 8):
            acc = acc + s_ref[d].astype(F32)
        o_ref[...] = acc

    return _call(body, name=name, grid=(R // tr,), in_specs=[pl.BlockSpec((8, tr, C), lambda i: (0, i, 0))],
                 out_specs=[pl.BlockSpec((tr, C), lambda i: (i, 0))], out_shape=[jax.ShapeDtypeStruct((R, C), F32)],
                 args=(slots,))[0]


def _adam_nd(w, g, m, v, name):
    shp = w.shape
    c = shp[-1]
    f2 = lambda a: a.reshape(-1, c)
    d, mn, vn = adamw(f2(w), f2(g), f2(m), f2(v), name=name)
    return d.reshape(shp), mn.reshape(shp), vn.reshape(shp)


def _reduced(slots, name):
    out = []
    for s in slots:
        c = s.shape[-1]
        out.append(sum_slots(s.reshape(8, -1, c), name=name).reshape(s.shape[1:]))
    return out


def kernel(x, ln_mix_g, ln_mix_b, ln_ffn_g, ln_ffn_b, ffn_w1, ffn_w2, a_w_in, a_lb_logits, a_norm_g, a_w_out, b_w_pw1, b_b_pw1, b_w_dw, b_b_dw, b_ln_g, b_ln_b, b_w_pw2, b_b_pw2, loss_target, m_ln_mix_g, m_ln_mix_b, m_ln_ffn_g, m_ln_ffn_b, m_ffn_w1, m_ffn_w2, m_a_w_in, m_a_lb_logits, m_a_norm_g, m_a_w_out, m_b_w_pw1, m_b_b_pw1, m_b_w_dw, m_b_b_dw, m_b_ln_g, m_b_ln_b, m_b_w_pw2, m_b_b_pw2, v_ln_mix_g, v_ln_mix_b, v_ln_ffn_g, v_ln_ffn_b, v_ffn_w1, v_ffn_w2, v_a_w_in, v_a_lb_logits, v_a_norm_g, v_a_w_out, v_b_w_pw1, v_b_b_pw1, v_b_w_dw, v_b_b_dw, v_b_ln_g, v_b_ln_b, v_b_w_pw2, v_b_b_pw2):
    names = ["ln_mix_g", "ln_mix_b", "ln_ffn_g", "ln_ffn_b", "ffn_w1", "ffn_w2", "a_w_in", "a_lb_logits", "a_norm_g",
             "a_w_out", "b_w_pw1", "b_b_pw1", "b_w_dw", "b_b_dw", "b_ln_g", "b_ln_b", "b_w_pw2", "b_b_pw2"]
    w = dict(zip(names, [ln_mix_g, ln_mix_b, ln_ffn_g, ln_ffn_b, ffn_w1, ffn_w2, a_w_in, a_lb_logits, a_norm_g, a_w_out,
                         b_w_pw1, b_b_pw1, b_w_dw, b_b_dw, b_ln_g, b_ln_b, b_w_pw2, b_b_pw2]))
    m = dict(zip(names, [m_ln_mix_g, m_ln_mix_b, m_ln_ffn_g, m_ln_ffn_b, m_ffn_w1, m_ffn_w2, m_a_w_in, m_a_lb_logits,
                         m_a_norm_g, m_a_w_out, m_b_w_pw1, m_b_b_pw1, m_b_w_dw, m_b_b_dw, m_b_ln_g, m_b_ln_b, m_b_w_pw2,
                         m_b_b_pw2]))
    v = dict(zip(names, [v_ln_mix_g, v_ln_mix_b, v_ln_ffn_g, v_ln_ffn_b, v_ffn_w1, v_ffn_w2, v_a_w_in, v_a_lb_logits,
                         v_a_norm_g, v_a_w_out, v_b_w_pw1, v_b_b_pw1, v_b_w_dw, v_b_b_dw, v_b_ln_g, v_b_ln_b, v_b_w_pw2,
                         v_b_b_pw2]))
    T, D = x.shape[1], x.shape[2]
    DS = D // 4
    F = 4 * ffn_w1.shape[2]
    chip = 2 * lax.axis_index("x") + lax.axis_index("y")
    tm = min(T, 512)
    tmw = min(T, 1024)
    tmc = min(T, 256)
    rb = min(T, 512)
    tf = min(F // 4, 1024)
    xin, target = x[0], loss_target[0]

    def mix_shards(i):
        j = i // 2
        if i % 2 == 0:
            return [a_w_in[j].astype(BF16), a_w_out[j].astype(BF16)]
        vec = jnp.concatenate([b_w_dw[j], jnp.zeros((1, DS), F32), b_b_dw[j][None], b_ln_g[j][None], b_ln_b[j][None],
                               b_b_pw2[j][None], b_b_pw1[j].reshape(2, DS), jnp.zeros((2, DS), F32)], axis=0)
        return [b_w_pw1[j].astype(BF16), b_w_pw2[j].astype(BF16), vec]

    def ffn_shards(i):
        return [ffn_w1[i].astype(BF16), ffn_w2[i].astype(BF16)]

    def mix_weights(i, got):
        if i % 2 == 0:
            return {"w_in": got[0], "w_out": got[1].reshape(D, D)}
        pw1 = jnp.transpose(got[0].reshape(2, 2, D, D // 2), (0, 2, 1, 3)).reshape(2, D, D)
        vec = jnp.transpose(got[2], (1, 0, 2)).reshape(VEC_ROWS, D)
        return {"pw1": pw1, "pw2": got[1].reshape(D, D), "vec": vec,
                "b_pw1": got[2][:, 36:38, :].reshape(2, 1, D)}

    lb_all = lb_fwd(a_lb_logits)
    zeros_bias = jnp.zeros((1, D), F32)

    mixw = mix_weights(0, comm_call(GatherChips(mix_shards(0)), name="gather_first"))
    saved = []
    h = xin
    for i in range(DEPTH):
        j = i // 2
        s = {"x": h, "mixw": mixw}
        gf = GatherChips(ffn_shards(i))
        if i % 2 == 0:
            s["proj"] = mm_groups(h, mixw["w_in"], jnp.zeros((4, 1, D), F32), tm=tm, name="a_in_proj")
            (s["o"], s["og"], s["st"]), got = hgrn_fwd(s["proj"], lb_all[j:j + 1], a_norm_g[j:j + 1], rb=rb,
                                                       name="hgrn_fwd", comm=gf)
            s["r1"], s["x1"] = mm_res_ln(s["og"], mixw["w_out"], zeros_bias, h, ln_mix_g[i:i + 1], ln_mix_b[i:i + 1],
                                         tm=tm, name="a_out_ln")
        else:
            s["u"] = mm_groups(h, mixw["pw1"], mixw["b_pw1"], tm=tm, name="b_pw1")
            (s["c"], s["v2"]), got = conv_fwd(s["u"], mixw["vec"], tm=tmc, name="conv_fwd", comm=gf)
            s["r1"], s["x1"] = mm_res_ln(s["v2"], mixw["pw2"], mixw["vec"][35:36], h, ln_mix_g[i:i + 1],
                                         ln_mix_b[i:i + 1], tm=tm, name="b_pw2_ln")
        s["w1"], s["w2"] = got[0], got[1].reshape(F, D)
        gm = GatherChips(mix_shards(i + 1)) if i + 1 < DEPTH else None
        (s["z"], s["r2"], h), got = ffn_fwd(s["x1"], s["w1"], s["w2"], ln_ffn_g[i:i + 1], ln_ffn_b[i:i + 1],
                                            tm=tm, tf=tf, name="ffn_fwd", comm=gm)
        if gm is not None:
            mixw = mix_weights(i + 1, got)
        saved.append(s)

    loss_part, dh = loss_grad(h, target, tm=tm)
    loss = lax.psum(loss_part[0, 0], ("x", "y", "c"))

    gr = {k: [None] * DEPTH for k in ("ln_mix_g", "ln_mix_b", "ln_ffn_g", "ln_ffn_b", "ffn_w1", "ffn_w2")}
    for k in ("a_w_in", "a_w_out", "a_norm_g", "a_dlb", "b_w_pw1", "b_w_pw2", "b_vec", "b_b_pw2"):
        gr[k] = [None] * 2
    pending = None

    def take_mixer(i, slots):
        got = _reduced(slots, "sum_mix_grads")
        j = i // 2
        if i % 2 == 0:
            gr["a_w_in"][j], gr["a_w_out"][j] = got
        else:
            gr["b_w_pw1"][j], gr["b_w_pw2"][j] = got

    for i in reversed(range(DEPTH)):
        j = i // 2
        s = saved[i]
        mixw = s["mixw"]
        sm = ScatterPieces(pending[1]) if pending is not None else None
        (dz, dx1, drb2, sums2), slots = ffn_bwd_dx(dh, s["r2"], ln_ffn_g[i:i + 1], s["z"], s["w1"], s["w2"],
                                                   tm=tm, tf=tf, name="ffn_bwd_dx", comm=sm)
        if pending is not None:
            take_mixer(pending[0], slots)
        gr["ln_ffn_g"][i], gr["ln_ffn_b"][i] = sums2[0], sums2[1]
        dw1 = mm_tn(s["x1"], dz[None], tm=tmw, tk=D, tn=F // 4, name="ffn_dw1")[0]
        dw2 = mm_tn(s["z"], drb2[None], tm=tmw, tk=F // 4, tn=D, relu2=True, name="ffn_dw2")[0, 0]
        sf = ScatterPieces([dw1, dw2.reshape(4, F // 4, D)])
        wmix = mixw["w_out"] if i % 2 == 0 else mixw["pw2"]
        dr1, drb1, dmo, sums1 = ln_bwd_mm(dx1, s["r1"], ln_mix_g[i:i + 1], wmix, tm=tm, name="mix_ln_bwd")
        gr["ln_mix_g"][i], gr["ln_mix_b"][i] = sums1[0], sums1[1]
        if i % 2 == 0:
            dwo = mm_tn(s["og"], drb1[None], tm=tmw, tk=D, tn=D, name="a_dw_out")[0, 0].reshape(4, DS, D)
            (dproj, hs), slots = hgrn_bwd(s["proj"], s["o"], dmo, s["st"], lb_all[j:j + 1], a_norm_g[j:j + 1], rb=rb,
                                          name="hgrn_bwd", comm=sf)
            gr["a_norm_g"][j], gr["a_dlb"][j] = hs[0], hs[1]
            dwi = mm_tn(s["x"], dproj, tm=tmw, tk=D, tn=D, name="a_dw_in")[:, 0]
            dh = mm_nt_acc(dproj, mixw["w_in"], dr1, tm=tm, name="a_dx")
        else:
            dwo = mm_tn(s["v2"], drb1[None], tm=tmw, tk=D, tn=D, name="b_dw_pw2")[0, 0].reshape(4, DS, D)
            (du, cs), slots = conv_bwd(dmo, s["c"], s["u"], mixw["vec"], tm=tmc, name="conv_bwd", comm=sf)
            gr["b_vec"][j], gr["b_b_pw2"][j] = cs, sums1[2]
            dwi = mm_tn(s["x"], du, tm=tmw, tk=D, tn=D // 2, name="b_dw_pw1").reshape(4, D, D // 2)
            dh = mm_nt_acc(du, mixw["pw1"], dr1, tm=tm, name="b_dx")
        gr["ffn_w1"][i], gr["ffn_w2"][i] = _reduced(slots, "sum_ffn_grads")
        pending = (i, [dwi, dwo])
    take_mixer(pending[0], comm_call(ScatterPieces(pending[1]), name="scatter_last"))
    grad_x = dh[None]

    small = {k: jnp.stack(gr[k]) for k in ("ln_mix_g", "ln_mix_b", "ln_ffn_g", "ln_ffn_b", "a_norm_g", "b_vec", "b_b_pw2")}
    small["a_lb_logits"] = lb_bwd(a_lb_logits, jnp.stack(gr["a_dlb"]))
    small_names = ["ln_mix_g", "ln_mix_b", "ln_ffn_g", "ln_ffn_b", "a_lb_logits", "a_norm_g", "b_b_pw2", "b_vec"]
    rows = [small[k].reshape(-1, D) for k in small_names]
    counts = [r.shape[0] for r in rows]
    rows = [jnp.pad(r, ((0, (-r.shape[0]) % 8), (0, 0))) for r in rows]
    summed = all_reduce_small(jnp.concatenate(rows, axis=0))
    sm = {}
    off = 0
    for k, n, r in zip(small_names, counts, rows):
        sm[k] = summed[off:off + n]
        off += r.shape[0]
    bvec = sm["b_vec"].reshape(2, VEC_ROWS, D)

    def shard_cols(a):
        return lax.dynamic_slice_in_dim(a, chip * DS, DS, axis=a.ndim - 1)

    grads = {k: jnp.stack(gr[k]) for k in ("ffn_w1", "ffn_w2", "a_w_in", "a_w_out", "b_w_pw1", "b_w_pw2")}
    for k in ("ln_mix_g", "ln_mix_b", "ln_ffn_g", "ln_ffn_b", "a_lb_logits", "a_norm_g"):
        grads[k] = sm[k]
    grads["b_b_pw1"] = lax.dynamic_slice_in_dim(bvec[:, 36:38, :].reshape(2, 2 * D), chip * (D // 2), D // 2, axis=1)
    grads["b_w_dw"] = shard_cols(bvec[:, 0:CONV_W, :])
    grads["b_b_dw"] = shard_cols(bvec[:, 32, :])
    grads["b_ln_g"] = shard_cols(bvec[:, 33, :])
    grads["b_ln_b"] = shard_cols(bvec[:, 34, :])
    grads["b_b_pw2"] = shard_cols(sm["b_b_pw2"])

    delta, new_m, new_v = {}, {}, {}
    for k in names:
        delta[k], new_m[k], new_v[k] = _adam_nd(w[k], grads[k], m[k], v[k], "adamw_" + k)
    return (loss, grad_x, *[grads[k] for k in names], *[delta[k] for k in names],
            *[new_m[k] for k in names], *[new_v[k] for k in names])
```

```python
import jax
import jax.numpy as jnp
from jax import lax
from jax.experimental import pallas as pl
from jax.experimental.pallas import tpu as pltpu

F32 = jnp.float32
BF16 = jnp.bfloat16
MESH = pl.DeviceIdType.MESH

DEPTH = 4
ALPHA = (2.0 * DEPTH) ** 0.25
LN_EPS = 1e-5
RMS_EPS = 1e-6
GATE_EPS = 1e-6
HEAD = 128
CHUNK = 128
SUB = 16
PAIR = 2
CONV_W = 31
HALO = 32
VEC_ROWS = 40
CONV_RB = 32
ADAM_LR, ADAM_B1, ADAM_B2, ADAM_EPS, ADAM_WD, ADAM_STEP = 0.001, 0.9, 0.999, 1e-08, 0.01, 10
VMEM_LIMIT = 56 * 1024 * 1024
ANY = pl.BlockSpec(memory_space=pl.ANY)


def _dot(a, b):
    return jnp.dot(a, b, preferred_element_type=F32)


def _dot_nt(a, b):
    return lax.dot_general(a, b, (((1,), (1,)), ((), ())), preferred_element_type=F32)


def _dot_tn(a, b):
    return lax.dot_general(a, b, (((0,), (0,)), ((), ())), preferred_element_type=F32)


def _sigmoid(x):
    return 1.0 / (1.0 + jnp.exp(-x))


def _ln_stats(r):
    mu = jnp.mean(r, axis=-1, keepdims=True)
    xc = r - mu
    var = jnp.mean(xc * xc, axis=-1, keepdims=True)
    rstd = lax.rsqrt(var + LN_EPS)
    return xc * rstd, rstd


def _ln_bwd(dy, xhat, rstd, g):
    dyg = dy * g
    m1 = jnp.mean(dyg, axis=-1, keepdims=True)
    m2 = jnp.mean(dyg * xhat, axis=-1, keepdims=True)
    return rstd * (dyg - m1 - xhat * m2)


def _place():
    return lax.axis_index("x"), lax.axis_index("y"), lax.axis_index("c")


class GatherChips:
    def __init__(self, arrs):
        self.ins = list(arrs)
        n = len(arrs)
        self.out_shapes = [jax.ShapeDtypeStruct((4,) + a.shape, a.dtype) for a in arrs]
        self.sems = [pltpu.SemaphoreType.DMA((3 * n,)), pltpu.SemaphoreType.DMA((3 * n,)),
                     pltpu.SemaphoreType.DMA((n,))]

    def copies(self, ins, outs, send, recv, loc):
        x, y, c = _place()
        me = 2 * x + y
        local, remote = [], []
        for a in range(len(ins)):
            local.append(pltpu.make_async_copy(ins[a], outs[a].at[me], loc.at[a]))
            for j, (px, py) in enumerate([(1 - x, y), (x, 1 - y), (1 - x, 1 - y)]):
                remote.append(pltpu.make_async_remote_copy(
                    src_ref=ins[a], dst_ref=outs[a].at[me], send_sem=send.at[3 * a + j], recv_sem=recv.at[3 * a + j],
                    device_id=(px, py, c), device_id_type=MESH))
        return local + remote


class ScatterPieces:
    def __init__(self, arrs):
        self.ins = list(arrs)
        n = len(arrs)
        self.out_shapes = [jax.ShapeDtypeStruct((8,) + a.shape[1:], a.dtype) for a in arrs]
        self.sems = [pltpu.SemaphoreType.DMA((7 * n,)), pltpu.SemaphoreType.DMA((7 * n,)),
                     pltpu.SemaphoreType.DMA((n,))]

    def copies(self, ins, outs, send, recv, loc):
        x, y, c = _place()
        me = 4 * x + 2 * y + c
        local, remote = [], []
        for a in range(len(ins)):
            local.append(pltpu.make_async_copy(ins[a].at[2 * x + y], outs[a].at[me], loc.at[a]))
            k = 0
            for fx in (0, 1):
                for fy in (0, 1):
                    for fc in (0, 1):
                        if fx or fy or fc:
                            tx, ty = x ^ fx, y ^ fy
                            remote.append(pltpu.make_async_remote_copy(
                                src_ref=ins[a].at[2 * tx + ty], dst_ref=outs[a].at[me],
                                send_sem=send.at[7 * a + k], recv_sem=recv.at[7 * a + k],
                                device_id=(tx, ty, c ^ fc), device_id_type=MESH))
                            k += 1
        return local + remote


def carried_call(body, comm, *, name, grid, in_specs, out_specs, out_shape, scratch_shapes, args):
    sem = ("arbitrary",) * len(grid)
    params = pltpu.CompilerParams(dimension_semantics=sem, vmem_limit_bytes=VMEM_LIMIT)
    if comm is None:
        res = pl.pallas_call(body, name=name, grid=grid, in_specs=in_specs, out_specs=out_specs, out_shape=out_shape,
                             scratch_shapes=scratch_shapes, compiler_params=params)(*args)
        return res, []
    ni, no, nscr = len(in_specs), len(out_specs), len(scratch_shapes)
    ci, co = len(comm.ins), len(comm.out_shapes)

    def both(*refs):
        ins, refs = refs[:ni], refs[ni:]
        cins, refs = refs[:ci], refs[ci:]
        outs, refs = refs[:no], refs[no:]
        couts, refs = refs[:co], refs[co:]
        scr, sems = refs[:nscr], refs[nscr:]
        first = pl.program_id(0) == 0
        last = pl.program_id(0) == grid[0] - 1
        for d in range(1, len(grid)):
            first = first & (pl.program_id(d) == 0)
            last = last & (pl.program_id(d) == grid[d] - 1)

        @pl.when(first)
        def _():
            for cp in comm.copies(cins, couts, *sems):
                cp.start()

        body(*ins, *outs, *scr)

        @pl.when(last)
        def _():
            for cp in comm.copies(cins, couts, *sems):
                cp.wait()

    res = pl.pallas_call(
        both, name=name, grid=grid, in_specs=list(in_specs) + [ANY] * ci, out_specs=list(out_specs) + [ANY] * co,
        out_shape=list(out_shape) + comm.out_shapes, scratch_shapes=list(scratch_shapes) + comm.sems,
        compiler_params=params)(*args, *comm.ins)
    return res[:no], res[no:]


def comm_call(comm, *, name):
    ci, co = len(comm.ins), len(comm.out_shapes)

    def body(*refs):
        cps = comm.copies(refs[:ci], refs[ci:ci + co], *refs[ci + co:])
        for cp in cps:
            cp.start()
        for cp in cps:
            cp.wait()

    return pl.pallas_call(body, name=name, in_specs=[ANY] * ci, out_specs=[ANY] * co, out_shape=comm.out_shapes,
                          scratch_shapes=comm.sems, compiler_params=pltpu.CompilerParams(has_side_effects=True))(*comm.ins)


def all_reduce_small(v):
    R, C = v.shape

    def body(v_ref, o_ref, slots, send, recv):
        x, y, c = _place()
        me = 4 * x + 2 * y + c
        slots[me] = v_ref[...]
        cps = []
        k = 0
        for fx in (0, 1):
            for fy in (0, 1):
                for fc in (0, 1):
                    if fx or fy or fc:
                        cps.append(pltpu.make_async_remote_copy(
                            src_ref=v_ref, dst_ref=slots.at[me], send_sem=send.at[k], recv_sem=recv.at[k],
                            device_id=(x ^ fx, y ^ fy, c ^ fc), device_id_type=MESH))
                        k += 1
        for cp in cps:
            cp.start()
        for cp in cps:
            cp.wait()
        acc = slots[0]
        for d in range(1, 8):
            acc = acc + slots[d]
        o_ref[...] = acc

    vm = pl.BlockSpec(memory_space=pltpu.VMEM)
    return pl.pallas_call(
        body, name="all_reduce_small", in_specs=[vm], out_specs=vm,
        out_shape=jax.ShapeDtypeStruct((R, C), F32),
        scratch_shapes=[pltpu.VMEM((8, R, C), F32), pltpu.SemaphoreType.DMA((7,)), pltpu.SemaphoreType.DMA((7,))],
        compiler_params=pltpu.CompilerParams(has_side_effects=True, vmem_limit_bytes=VMEM_LIMIT),
    )(v)


def _call(body, *, name, grid, in_specs, out_specs, out_shape, scratch_shapes=(), args):
    res, _ = carried_call(body, None, name=name, grid=grid, in_specs=in_specs, out_specs=out_specs,
                          out_shape=out_shape, scratch_shapes=list(scratch_shapes), args=args)
    return res


def mm_groups(a, w, bias, *, tm, name):
    T, K = a.shape
    G, _, N = w.shape

    def body(a_ref, w_ref, b_ref, o_ref):
        o_ref[...] = _dot(a_ref[...].astype(BF16), w_ref[...]) + b_ref[...]

    return _call(
        body, name=name, grid=(G, T // tm),
        in_specs=[pl.BlockSpec((tm, K), lambda g, i: (i, 0)),
                  pl.BlockSpec((None, K, N), lambda g, i: (g, 0, 0)),
                  pl.BlockSpec((None, 1, N), lambda g, i: (g, 0, 0))],
        out_specs=[pl.BlockSpec((None, tm, N), lambda g, i: (g, i, 0))],
        out_shape=[jax.ShapeDtypeStruct((G, T, N), F32)], args=(a, w, bias))[0]


def mm_res_ln(a, w, bias, res, g, b, *, tm, name):
    T, K = a.shape
    N = w.shape[1]

    def body(a_ref, w_ref, bias_ref, res_ref, g_ref, b_ref, r_ref, y_ref):
        r = ALPHA * res_ref[...] + _dot(a_ref[...], w_ref[...]) + bias_ref[...]
        r_ref[...] = r
        xhat, _ = _ln_stats(r)
        y_ref[...] = xhat * g_ref[...] + b_ref[...]

    row = lambda i: (i, 0)
    fix = lambda i: (0, 0)
    return _call(
        body, name=name, grid=(T // tm,),
        in_specs=[pl.BlockSpec((tm, K), row), pl.BlockSpec((K, N), fix), pl.BlockSpec((1, N), fix),
                  pl.BlockSpec((tm, N), row), pl.BlockSpec((1, N), fix), pl.BlockSpec((1, N), fix)],
        out_specs=[pl.BlockSpec((tm, N), row), pl.BlockSpec((tm, N), row)],
        out_shape=[jax.ShapeDtypeStruct((T, N), F32), jax.ShapeDtypeStruct((T, N), F32)],
        args=(a, w, bias, res, g, b))


def ffn_fwd(x, w1, w2, g, b, *, tm, tf, name, comm=None):
    T, D = x.shape
    NC, _, FC = w1.shape
    F = NC * FC
    per = FC // tf
    nf = F // tf

    def body(x_ref, w1_ref, w2_ref, g_ref, b_ref, z_ref, r_ref, y_ref, acc_ref, xb_ref):
        f = pl.program_id(1)

        @pl.when(f == 0)
        def _():
            acc_ref[...] = jnp.zeros_like(acc_ref)
            xb_ref[...] = x_ref[...].astype(BF16)

        z = _dot(xb_ref[...], w1_ref[...])
        z_ref[...] = z.astype(BF16)
        h = jnp.square(jnp.maximum(z, 0.0)).astype(BF16)
        acc_ref[...] += _dot(h, w2_ref[...])

        @pl.when(f == nf - 1)
        def _():
            r = ALPHA * x_ref[...] + acc_ref[...]
            r_ref[...] = r
            xhat, _ = _ln_stats(r)
            y_ref[...] = xhat * g_ref[...] + b_ref[...]

    return carried_call(
        body, comm, name=name, grid=(T // tm, nf),
        in_specs=[pl.BlockSpec((tm, D), lambda i, f: (i, 0)),
                  pl.BlockSpec((None, D, tf), lambda i, f: (f // per, 0, f % per)),
                  pl.BlockSpec((tf, D), lambda i, f: (f, 0)),
                  pl.BlockSpec((1, D), lambda i, f: (0, 0)),
                  pl.BlockSpec((1, D), lambda i, f: (0, 0))],
        out_specs=[pl.BlockSpec((tm, tf), lambda i, f: (i, f)),
                   pl.BlockSpec((tm, D), lambda i, f: (i, 0)),
                   pl.BlockSpec((tm, D), lambda i, f: (i, 0))],
        out_shape=[jax.ShapeDtypeStruct((T, F), BF16), jax.ShapeDtypeStruct((T, D), F32),
                   jax.ShapeDtypeStruct((T, D), F32)],
        scratch_shapes=[pltpu.VMEM((tm, D), F32), pltpu.VMEM((tm, D), BF16)],
        args=(x, w1, w2, g, b))


def ln_bwd_mm(dy, r, g, w, *, tm, name):
    T, N = dy.shape
    Ko = w.shape[0]

    def body(dy_ref, r_ref, g_ref, w_ref, dr_ref, drb_ref, o_ref, s_ref):
        @pl.when(pl.program_id(0) == 0)
        def _():
            s_ref[...] = jnp.zeros_like(s_ref)

        dy_ = dy_ref[...]
        xhat, rstd = _ln_stats(r_ref[...])
        dr = _ln_bwd(dy_, xhat, rstd, g_ref[...])
        dr_ref[...] = dr
        drb = dr.astype(BF16)
        drb_ref[...] = drb
        o_ref[...] = _dot_nt(drb, w_ref[...])
        s_ref[0:1, :] += jnp.sum(dy_ * xhat, axis=0, keepdims=True)
        s_ref[1:2, :] += jnp.sum(dy_, axis=0, keepdims=True)
        s_ref[2:3, :] += jnp.sum(dr, axis=0, keepdims=True)

    row = lambda i: (i, 0)
    fix = lambda i: (0, 0)
    return _call(
        body, name=name, grid=(T // tm,),
        in_specs=[pl.BlockSpec((tm, N), row), pl.BlockSpec((tm, N), row), pl.BlockSpec((1, N), fix),
                  pl.BlockSpec((Ko, N), fix)],
        out_specs=[pl.BlockSpec((tm, N), row), pl.BlockSpec((tm, N), row), pl.BlockSpec((tm, Ko), row),
                   pl.BlockSpec((8, N), fix)],
        out_shape=[jax.ShapeDtypeStruct((T, N), F32), jax.ShapeDtypeStruct((T, N), BF16),
                   jax.ShapeDtypeStruct((T, Ko), F32), jax.ShapeDtypeStruct((8, N), F32)],
        args=(dy, r, g, w))


def ffn_bwd_dx(dy, r, g, z, w1, w2, *, tm, tf, name, comm=None):
    T, D = dy.shape
    NC, _, FC = w1.shape
    F = NC * FC
    per = FC // tf
    nf = F // tf

    def body(dy_ref, r_ref, g_ref, z_ref, w1_ref, w2_ref, dz_ref, dx_ref, drb_ref, s_ref, dr_scr, acc_ref):
        i = pl.program_id(0)
        f = pl.program_id(1)

        @pl.when((i == 0) & (f == 0))
        def _():
            s_ref[...] = jnp.zeros_like(s_ref)

        @pl.when(f == 0)
        def _():
            dy_ = dy_ref[...]
            xhat, rstd = _ln_stats(r_ref[...])
            dr = _ln_bwd(dy_, xhat, rstd, g_ref[...])
            dr_scr[...] = dr
            drb_ref[...] = dr.astype(BF16)
            acc_ref[...] = jnp.zeros_like(acc_ref)
            s_ref[0:1, :] += jnp.sum(dy_ * xhat, axis=0, keepdims=True)
            s_ref[1:2, :] += jnp.sum(dy_, axis=0, keepdims=True)

        dh = _dot_nt(drb_ref[...], w2_ref[...])
        dz = (dh * (2.0 * jnp.maximum(z_ref[...].astype(F32), 0.0))).astype(BF16)
        dz_ref[...] = dz
        acc_ref[...] += _dot_nt(dz, w1_ref[...])

        @pl.when(f == nf - 1)
        def _():
            dx_ref[...] = ALPHA * dr_scr[...] + acc_ref[...]

    return carried_call(
        body, comm, name=name, grid=(T // tm, nf),
        in_specs=[pl.BlockSpec((tm, D), lambda i, f: (i, 0)),
                  pl.BlockSpec((tm, D), lambda i, f: (i, 0)),
                  pl.BlockSpec((1, D), lambda i, f: (0, 0)),
                  pl.BlockSpec((tm, tf), lambda i, f: (i, f)),
                  pl.BlockSpec((None, D, tf), lambda i, f: (f // per, 0, f % per)),
                  pl.BlockSpec((tf, D), lambda i, f: (f, 0))],
        out_specs=[pl.BlockSpec((tm, tf), lambda i, f: (i, f)),
                   pl.BlockSpec((tm, D), lambda i, f: (i, 0)),
                   pl.BlockSpec((tm, D), lambda i, f: (i, 0)),
                   pl.BlockSpec((8, D), lambda i, f: (0, 0))],
        out_shape=[jax.ShapeDtypeStruct((T, F), BF16), jax.ShapeDtypeStruct((T, D), F32),
                   jax.ShapeDtypeStruct((T, D), BF16), jax.ShapeDtypeStruct((8, D), F32)],
        scratch_shapes=[pltpu.VMEM((tm, D), F32), pltpu.VMEM((tm, D), F32)],
        args=(dy, r, g, z, w1, w2))


def mm_tn(a, b, *, tm, tk, tn, relu2=False, name):
    T, K = a.shape
    G, _, N = b.shape
    nt = T // tm

    def body(a_ref, b_ref, o_ref, acc_ref):
        t = pl.program_id(3)

        @pl.when(t == 0)
        def _():
            acc_ref[...] = jnp.zeros_like(acc_ref)

        av = a_ref[...]
        if relu2:
            av = jnp.square(jnp.maximum(av.astype(F32), 0.0))
        acc_ref[...] += _dot_tn(av.astype(BF16), b_ref[...])

        @pl.when(t == nt - 1)
        def _():
            o_ref[...] = acc_ref[...].astype(BF16)

    return _call(
        body, name=name, grid=(G, K // tk, N // tn, nt),
        in_specs=[pl.BlockSpec((tm, tk), lambda g, k, n, t: (t, k)),
                  pl.BlockSpec((None, tm, tn), lambda g, k, n, t: (g, t, n))],
        out_specs=[pl.BlockSpec((None, None, tk, tn), lambda g, k, n, t: (g, n, k, 0))],
        out_shape=[jax.ShapeDtypeStruct((G, N // tn, K, tn), BF16)],
        scratch_shapes=[pltpu.VMEM((tk, tn), F32)], args=(a, b))[0]


def mm_nt_acc(dy, w, base, *, tm, name, comm=None):
    G, T, N = dy.shape
    K = w.shape[1]

    def body(dy_ref, w_ref, base_ref, o_ref):
        g = pl.program_id(1)

        @pl.when(g == 0)
        def _():
            o_ref[...] = ALPHA * base_ref[...]

        o_ref[...] += _dot_nt(dy_ref[...], w_ref[...])

    res, got = carried_call(
        body, comm, name=name, grid=(T // tm, G),
        in_specs=[pl.BlockSpec((None, tm, N), lambda i, g: (g, i, 0)),
                  pl.BlockSpec((None, K, N), lambda i, g: (g, 0, 0)),
                  pl.BlockSpec((tm, K), lambda i, g: (i, 0))],
        out_specs=[pl.BlockSpec((tm, K), lambda i, g: (i, 0))],
        out_shape=[jax.ShapeDtypeStruct((T, K), F32)], scratch_shapes=[], args=(dy, w, base))
    return res[0], got


def _split3(x):
    x1 = x.astype(BF16)
    r1 = x - x1.astype(F32)
    x2 = r1.astype(BF16)
    x3 = (r1 - x2.astype(F32)).astype(BF16)
    return x1, x2, x3


def _tri_dot(tri, x):
    x1, x2, x3 = _split3(x)
    return _dot(tri, x1) + _dot(tri, x2) + _dot(tri, x3)


def _gates(pq, fz, lb):
    sg = _sigmoid(fz)
    f = lb + (1.0 - lb) * sg
    logf = jnp.log(jnp.maximum(f, GATE_EPS))
    sq = _sigmoid(pq)
    return pq * sq, 1.0 - f, logf, f, sg, sq


def _block_diag_mask():
    ri = lax.broadcasted_iota(jnp.int32, (PAIR * HEAD, PAIR * HEAD), 0) // HEAD
    ci = lax.broadcasted_iota(jnp.int32, (PAIR * HEAD, PAIR * HEAD), 1) // HEAD
    return ri == ci


def _fill_off_diagonal(q_s, k_s, b_s, lhs, rhs):
    for i in range(1, CHUNK // SUB):
        lo = i * SUB
        ref = b_s[lo - 1:lo, :]
        qt = (q_s[lo:lo + SUB, :] * jnp.exp(b_s[lo:lo + SUB, :] - ref)).astype(BF16)
        kt = (k_s[0:lo, :] * jnp.exp(ref - b_s[0:lo, :])).astype(BF16)
        for h in range(PAIR):
            hl = slice(h * HEAD, (h + 1) * HEAD)
            lhs[h, lo:lo + SUB, (i - 1) * HEAD:i * HEAD] = qt[:, hl]
            rhs[h, 0:lo, (i - 1) * HEAD:i * HEAD] = kt[:, hl]


def hgrn_fwd(proj, lb, norm_g, *, rb, name, comm=None):
    _, T, D = proj.shape
    H = D // HEAD
    nb = T // rb
    nck = rb // CHUNK
    nsub = CHUNK // SUB
    W = PAIR * HEAD
    fam = CHUNK * SUB

    def body(pq_ref, fz_ref, pv_ref, pg_ref, lb_ref, ng_ref, o_ref, og_ref, st_ref, S, q_s, k_s, v_s, b_s, lhs, rhs, p_s):
        @pl.when(pl.program_id(1) == 0)
        def _():
            S[...] = jnp.zeros_like(S)

        @pl.when((pl.program_id(0) == 0) & (pl.program_id(1) == 0))
        def _():
            lhs[...] = jnp.zeros_like(lhs)
            rhs[...] = jnp.zeros_like(rhs)

        ri = lax.broadcasted_iota(jnp.int32, (CHUNK, CHUNK), 0)
        ci = lax.broadcasted_iota(jnp.int32, (CHUNK, CHUNK), 1)
        tri = (ci <= ri).astype(BF16)
        ones = jnp.ones((HEAD, HEAD), BF16)
        bd = _block_diag_mask()

        def trip(c, carry):
            base = c * CHUNK
            rows = pl.ds(pl.multiple_of(base, CHUNK), CHUNK)
            q, k, logf, _, _, _ = _gates(pq_ref[rows, :], fz_ref[rows, :], lb_ref[...])
            v = pv_ref[rows, :]
            b = _tri_dot(tri, logf)
            q_s[...] = q
            k_s[...] = k
            v_s[...] = v
            b_s[...] = b
            bl = b_s[CHUNK - 1:CHUNK, :]
            upd = _dot_tn(v.astype(BF16), (k * jnp.exp(bl - b)).astype(BF16))
            Sv = S[...]
            for h in range(PAIR):
                st_ref[h, c] = Sv[h * HEAD:(h + 1) * HEAD, h * HEAD:(h + 1) * HEAD]
            o_int = _dot_nt((q * jnp.exp(b)).astype(BF16), Sv.astype(BF16))
            S[...] = Sv * jnp.exp(bl) + jnp.where(bd, upd, 0.0)
            _fill_off_diagonal(q_s, k_s, b_s, lhs, rhs)
            off = []
            for h in range(PAIR):
                a = _dot_nt(lhs[h], rhs[h])
                off.append(_dot(a.astype(BF16), v_s[:, h * HEAD:(h + 1) * HEAD].astype(BF16)))
            for i in range(nsub):
                lo = i * SUB
                for s in range(SUB):
                    m = lax.broadcasted_iota(jnp.int32, (SUB, W), 0) >= s
                    at = (i * SUB + s) * SUB
                    e = jnp.exp(b_s[lo:lo + SUB, :] - b_s[lo + s:lo + s + 1, :])
                    p = jnp.where(m, q_s[lo:lo + SUB, :] * (k_s[lo + s:lo + s + 1, :] * e), 0.0).astype(BF16)
                    for h in range(PAIR):
                        p_s[h * fam + at:h * fam + at + SUB, :] = p[:, h * HEAD:(h + 1) * HEAD]
            rs = _dot(p_s[...], ones)
            for i in range(nsub):
                lo = i * SUB
                blk = pl.ds(pl.multiple_of(base + lo, SUB), SUB)
                for h in range(PAIR):
                    hl = slice(h * HEAD, (h + 1) * HEAD)
                    acc = o_int[lo:lo + SUB, hl] + off[h][lo:lo + SUB, :]
                    for s in range(SUB):
                        at = h * fam + (i * SUB + s) * SUB
                        acc = acc + rs[at:at + SUB, :] * v_s[lo + s:lo + s + 1, hl]
                    o_ref[blk, hl] = acc
                    rinv = lax.rsqrt(jnp.mean(acc * acc, axis=-1, keepdims=True) + RMS_EPS)
                    pg = pg_ref[blk, hl]
                    og_ref[blk, hl] = (acc * rinv * ng_ref[:, hl] * (pg * _sigmoid(pg))).astype(BF16)
            return carry

        lax.fori_loop(0, nck, trip, 0)

    def grp(gi):
        return pl.BlockSpec((None, rb, W), lambda h, r: (gi, r, h))

    vec = pl.BlockSpec((1, W), lambda h, r: (0, h))
    return carried_call(
        body, comm, name=name, grid=(H // PAIR, nb),
        in_specs=[grp(0), grp(1), grp(2), grp(3), vec, vec],
        out_specs=[pl.BlockSpec((rb, W), lambda h, r: (r, h)),
                   pl.BlockSpec((rb, W), lambda h, r: (r, h)),
                   pl.BlockSpec((PAIR, nck, HEAD, HEAD), lambda h, r: (h, r, 0, 0))],
        out_shape=[jax.ShapeDtypeStruct((T, D), F32), jax.ShapeDtypeStruct((T, D), BF16),
                   jax.ShapeDtypeStruct((H, T // CHUNK, HEAD, HEAD), F32)],
        scratch_shapes=[pltpu.VMEM((W, W), F32)] + [pltpu.VMEM((CHUNK, W), F32)] * 4
        + [pltpu.VMEM((PAIR, CHUNK, (nsub - 1) * HEAD), BF16)] * 2 + [pltpu.VMEM((PAIR * fam, HEAD), BF16)],
        args=(proj, proj, proj, proj, lb, norm_g))


def hgrn_bwd(proj, o, dog, states, lb, norm_g, *, rb, name, comm=None):
    _, T, D = proj.shape
    H = D // HEAD
    nb = T // rb
    nck = rb // CHUNK
    nsub = CHUNK // SUB
    W = PAIR * HEAD
    fam = CHUNK * SUB

    def body(pq_ref, fz_ref, pv_ref, pg_ref, o_ref, dog_ref, st_ref, lb_ref, ng_ref, dp_ref, s_ref,
             dS, S0, q_s, k_s, v_s, b_s, do_s, dq_s, dk_s, dv_s, cr_s, lhs, rhs, ke_s, qe_s, p_s):
        @pl.when(pl.program_id(1) == 0)
        def _():
            dS[...] = jnp.zeros_like(dS)
            s_ref[...] = jnp.zeros_like(s_ref)

        @pl.when((pl.program_id(0) == 0) & (pl.program_id(1) == 0))
        def _():
            lhs[...] = jnp.zeros_like(lhs)
            rhs[...] = jnp.zeros_like(rhs)
            S0[...] = jnp.zeros_like(S0)

        ri = lax.broadcasted_iota(jnp.int32, (CHUNK, CHUNK), 0)
        ci = lax.broadcasted_iota(jnp.int32, (CHUNK, CHUNK), 1)
        tri = (ci <= ri).astype(BF16)
        triu = (ci >= ri).astype(BF16)
        below = (ri // SUB) > (ci // SUB)
        ones = jnp.ones((HEAD, HEAD), BF16)
        bd = _block_diag_mask()
        last_row = lax.broadcasted_iota(jnp.int32, (CHUNK, W), 0) == CHUNK - 1

        def trip(cc, carry):
            c = nck - 1 - cc
            rows = pl.ds(pl.multiple_of(c * CHUNK, CHUNK), CHUNK)
            lb_ = lb_ref[...]
            pq = pq_ref[rows, :]
            q, k, logf, f, sg, sq = _gates(pq, fz_ref[rows, :], lb_)
            v = pv_ref[rows, :]
            b = _tri_dot(tri, logf)
            dpg = []
            for h in range(PAIR):
                hl = slice(h * HEAD, (h + 1) * HEAD)
                oh = o_ref[rows, hl]
                dog_ = dog_ref[rows, hl]
                pg = pg_ref[rows, hl]
                ng = ng_ref[:, hl]
                spg = _sigmoid(pg)
                rinv = lax.rsqrt(jnp.mean(oh * oh, axis=-1, keepdims=True) + RMS_EPS)
                on = oh * rinv
                dpg.append(dog_ * (on * ng) * (spg * (1.0 + pg * (1.0 - spg))))
                don = dog_ * (pg * spg)
                s_ref[0:1, hl] += jnp.sum(don * on, axis=0, keepdims=True)
                dxn = don * ng
                do_s[:, hl] = rinv * (dxn - on * jnp.mean(dxn * on, axis=-1, keepdims=True))
                S0[hl, hl] = st_ref[h, c]
            q_s[...] = q
            k_s[...] = k
            v_s[...] = v
            b_s[...] = b
            do = do_s[...]
            dob = do.astype(BF16)
            vb = v.astype(BF16)
            eb = jnp.exp(b)
            bl = b_s[CHUNK - 1:CHUNK, :]
            ebl = jnp.exp(bl)
            ekk = jnp.exp(bl - b)
            upd = _dot_tn(dob, (q * eb).astype(BF16))
            S0v = S0[...]
            dSv = dS[...]
            dSb = dSv.astype(BF16)
            dq_s[...] = _dot(dob, S0v.astype(BF16)) * eb
            dk_state = _dot(vb, dSb) * ekk
            dk_s[...] = dk_state
            dv_s[...] = _dot_nt((k * ekk).astype(BF16), dSb)
            extra = jnp.sum(k * dk_state, axis=0, keepdims=True) + ebl * jnp.sum(S0v * dSv, axis=0, keepdims=True)
            dS[...] = dSv * ebl + jnp.where(bd, upd, 0.0)

            _fill_off_diagonal(q_s, k_s, b_s, lhs, rhs)
            dqb, dkb = [], []
            for h in range(PAIR):
                hl = slice(h * HEAD, (h + 1) * HEAD)
                doh = dob[:, hl]
                vh = vb[:, hl]
                at = _dot_nt(rhs[h], lhs[h])
                dv_s[:, hl] += _dot(at.astype(BF16), doh)
                daf = jnp.where(below, _dot_nt(doh, vh), 0.0)
                dqb.append(_dot(daf.astype(BF16), rhs[h]))
                dkb.append(_dot(daf.T.astype(BF16), lhs[h]))
            cr_s[...] = jnp.zeros_like(cr_s)
            for i in range(1, nsub):
                lo = i * SUB
                ref = b_s[lo - 1:lo, :]
                eq = jnp.exp(b_s[lo:lo + SUB, :] - ref)
                ek = jnp.exp(ref - b_s[0:lo, :])
                qtf = q_s[lo:lo + SUB, :] * eq
                ktf = k_s[0:lo, :] * ek
                cb = slice((i - 1) * HEAD, i * HEAD)
                for h in range(PAIR):
                    hl = slice(h * HEAD, (h + 1) * HEAD)
                    dqi = dqb[h][lo:lo + SUB, cb]
                    dki = dkb[h][0:lo, cb]
                    dq_s[lo:lo + SUB, hl] += dqi * eq[:, hl]
                    dk_s[0:lo, hl] += dki * ek[:, hl]
                    cr_s[lo:lo + SUB, hl] += (lhs[h, lo:lo + SUB, cb].astype(F32) - qtf[:, hl]) * dqi
                    cr_s[0:lo, hl] -= (rhs[h, 0:lo, cb].astype(F32) - ktf[:, hl]) * dki

            for i in range(nsub):
                lo = i * SUB
                for s in range(SUB):
                    m = lax.broadcasted_iota(jnp.int32, (SUB, W), 0) >= s
                    at = (i * SUB + s) * SUB
                    qi = q_s[lo:lo + SUB, :]
                    e = jnp.where(m, jnp.exp(b_s[lo:lo + SUB, :] - b_s[lo + s:lo + s + 1, :]), 0.0)
                    ke = k_s[lo + s:lo + s + 1, :] * e
                    ke_s[at:at + SUB, :] = ke
                    qe_s[at:at + SUB, :] = qi * e
                    pa = (qi * ke).astype(BF16)
                    pd = jnp.where(m, do_s[lo:lo + SUB, :] * v_s[lo + s:lo + s + 1, :], 0.0).astype(BF16)
                    for h in range(PAIR):
                        hl = slice(h * HEAD, (h + 1) * HEAD)
                        p_s[(2 * h) * fam + at:(2 * h) * fam + at + SUB, :] = pa[:, hl]
                        p_s[(2 * h + 1) * fam + at:(2 * h + 1) * fam + at + SUB, :] = pd[:, hl]
            rs = _dot(p_s[...], ones)
            for i in range(nsub):
                lo = i * SUB
                for h in range(PAIR):
                    hl = slice(h * HEAD, (h + 1) * HEAD)
                    doi = do_s[lo:lo + SUB, hl]
                    dqa = dq_s[lo:lo + SUB, hl]
                    for s in range(SUB):
                        at = (i * SUB + s) * SUB
                        acol = rs[(2 * h) * fam + at:(2 * h) * fam + at + SUB, :]
                        dacol = rs[(2 * h + 1) * fam + at:(2 * h + 1) * fam + at + SUB, :]
                        dqa = dqa + dacol * ke_s[at:at + SUB, hl]
                        dk_s[lo + s:lo + s + 1, hl] += jnp.sum(dacol * qe_s[at:at + SUB, hl], axis=0, keepdims=True)
                        dv_s[lo + s:lo + s + 1, hl] += jnp.sum(acol * doi, axis=0, keepdims=True)
                    dq_s[lo:lo + SUB, hl] = dqa

            dq = dq_s[...]
            dk = dk_s[...]
            db = q * dq - k * dk + cr_s[...] + jnp.where(last_row, extra, 0.0)
            dlogf = _tri_dot(triu, db)
            df = jnp.where(f > GATE_EPS, dlogf / jnp.maximum(f, GATE_EPS), 0.0) - dk
            s_ref[1:2, :] += jnp.sum(df * (1.0 - sg), axis=0, keepdims=True)
            dp_ref[0, rows, :] = (dq * (sq * (1.0 + pq * (1.0 - sq)))).astype(BF16)
            dp_ref[1, rows, :] = (df * (1.0 - lb_) * sg * (1.0 - sg)).astype(BF16)
            dp_ref[2, rows, :] = dv_s[...].astype(BF16)
            for h in range(PAIR):
                dp_ref[3, rows, h * HEAD:(h + 1) * HEAD] = dpg[h].astype(BF16)
            return carry

        lax.fori_loop(0, nck, trip, 0)

    def grp(gi):
        return pl.BlockSpec((None, rb, W), lambda h, r: (gi, nb - 1 - r, h))

    rowsp = pl.BlockSpec((rb, W), lambda h, r: (nb - 1 - r, h))
    vec = pl.BlockSpec((1, W), lambda h, r: (0, h))
    return carried_call(
        body, comm, name=name, grid=(H // PAIR, nb),
        in_specs=[grp(0), grp(1), grp(2), grp(3), rowsp, rowsp,
                  pl.BlockSpec((PAIR, nck, HEAD, HEAD), lambda h, r: (h, nb - 1 - r, 0, 0)), vec, vec],
        out_specs=[pl.BlockSpec((4, rb, W), lambda h, r: (0, nb - 1 - r, h)),
                   pl.BlockSpec((8, W), lambda h, r: (0, h))],
        out_shape=[jax.ShapeDtypeStruct((4, T, D), BF16), jax.ShapeDtypeStruct((8, D), F32)],
        scratch_shapes=[pltpu.VMEM((W, W), F32)] * 2 + [pltpu.VMEM((CHUNK, W), F32)] * 9
        + [pltpu.VMEM((PAIR, CHUNK, (nsub - 1) * HEAD), BF16)] * 2 + [pltpu.VMEM((fam, W), F32)] * 2
        + [pltpu.VMEM((2 * PAIR * fam, HEAD), BF16)],
        args=(proj, proj, proj, proj, o, dog, states, lb, norm_g))


def lb_fwd(logits):
    def body(l_ref, o_ref):
        l = l_ref[...]
        mx = jnp.max(l, axis=0, keepdims=True)
        e = jnp.exp(l - mx)
        sm = e / jnp.sum(e, axis=0, keepdims=True)
        o_ref[0:1, :] = jnp.zeros_like(sm[0:1, :])
        o_ref[1:2, :] = sm[1:2, :]

    return pl.pallas_call(body, name="lb_fwd", out_shape=jax.ShapeDtypeStruct(logits.shape, F32))(logits)


def lb_bwd(logits, dlb):
    def body(l_ref, d_ref, o_ref):
        l = l_ref[...]
        mx = jnp.max(l, axis=0, keepdims=True)
        e = jnp.exp(l - mx)
        sm = e / jnp.sum(e, axis=0, keepdims=True)
        inner = d_ref[1:2, :] * sm[1:2, :]
        o_ref[0:1, :] = sm[0:1, :] * (0.0 - inner)
        o_ref[1:2, :] = sm[1:2, :] * (d_ref[1:2, :] - inner)

    return pl.pallas_call(body, name="lb_bwd", out_shape=jax.ShapeDtypeStruct(logits.shape, F32))(logits, dlb)


def _shifted_copies(sh, rows):
    for b in range(1, 8):
        sh[b, 0:rows, :] = sh[0, b:b + rows, :]


def conv_fwd(u, vec, *, tm, name, comm=None):
    _, T, D = u.shape
    hb = tm // HALO
    nlc = D // HEAD

    def body(a_ref, gt_ref, ap_ref, gp_ref, w_ref, c_ref, v_ref, sh):
        i = pl.program_id(0)
        sh[0, HALO:HALO + tm, :] = a_ref[...] * _sigmoid(gt_ref[...])
        prev = ap_ref[...] * _sigmoid(gp_ref[...])
        sh[0, 0:HALO, :] = jnp.where(i > 0, prev, 0.0)
        _shifted_copies(sh, tm + HALO - 8)

        def rowblock(r, carry):
            r0 = r * CONV_RB
            for cl in range(nlc):
                ls = slice(cl * HEAD, (cl + 1) * HEAD)
                acc = jnp.zeros((CONV_RB, HEAD), F32) + w_ref[32:33, ls]
                for j in range(CONV_W):
                    o = j + 2
                    at = pl.ds(pl.multiple_of(r0 + o - o % 8, 8), CONV_RB)
                    acc = acc + w_ref[j:j + 1, ls] * sh[o % 8, at, ls]
                c_ref[pl.ds(pl.multiple_of(r0, CONV_RB), CONV_RB), ls] = acc
            return carry

        lax.fori_loop(0, tm // CONV_RB, rowblock, 0)
        xhat, _ = _ln_stats(c_ref[...])
        y = xhat * w_ref[33:34, :] + w_ref[34:35, :]
        v_ref[...] = (y * _sigmoid(y)).astype(BF16)

    cur = lambda gi: pl.BlockSpec((None, tm, D), lambda i: (gi, i, 0))
    prv = lambda gi: pl.BlockSpec((None, HALO, D), lambda i: (gi, jnp.maximum(i * hb - 1, 0), 0))
    return carried_call(
        body, comm, name=name, grid=(T // tm,),
        in_specs=[cur(0), cur(1), prv(0), prv(1), pl.BlockSpec((VEC_ROWS, D), lambda i: (0, 0))],
        out_specs=[pl.BlockSpec((tm, D), lambda i: (i, 0)), pl.BlockSpec((tm, D), lambda i: (i, 0))],
        out_shape=[jax.ShapeDtypeStruct((T, D), F32), jax.ShapeDtypeStruct((T, D), BF16)],
        scratch_shapes=[pltpu.VMEM((8, tm + HALO, D), F32)],
        args=(u, u, u, u, vec))


def conv_bwd(dv2, c, u, vec, *, tm, name, comm=None):
    _, T, D = u.shape
    hb = tm // HALO
    nt = T // tm
    nh = T // HALO
    nlc = D // HEAD
    acc_rows = {j: j for j in range(CONV_W)}
    acc_rows.update({36: CONV_W, 37: CONV_W + 1})

    def body(dv_ref, c_ref, dvn_ref, cn_ref, a_ref, gt_ref, ap_ref, gp_ref, w_ref, du_ref, s_ref, gsh, dsh, part):
        i = pl.program_id(0)

        @pl.when(i == 0)
        def _():
            s_ref[...] = jnp.zeros_like(s_ref)
            part[...] = jnp.zeros_like(part)

        gam = w_ref[33:34, :]
        bet = w_ref[34:35, :]

        def dconv(dv, cc):
            xhat, rstd = _ln_stats(cc)
            y = xhat * gam + bet
            sy = _sigmoid(y)
            dy = dv * (sy * (1.0 + y * (1.0 - sy)))
            return _ln_bwd(dy, xhat, rstd, gam), dy, xhat

        dc, dy, xhat = dconv(dv_ref[...], c_ref[...])
        dcn, _, _ = dconv(dvn_ref[...], cn_ref[...])
        dsh[0, 0:tm, :] = dc
        dsh[0, tm:tm + HALO, :] = jnp.where(i < nt - 1, dcn, 0.0)
        gsh[0, HALO:HALO + tm, :] = a_ref[...] * _sigmoid(gt_ref[...])
        gsh[0, 0:HALO, :] = jnp.where(i > 0, ap_ref[...] * _sigmoid(gp_ref[...]), 0.0)
        s_ref[32:33, :] += jnp.sum(dc, axis=0, keepdims=True)
        s_ref[33:34, :] += jnp.sum(dy * xhat, axis=0, keepdims=True)
        s_ref[34:35, :] += jnp.sum(dy, axis=0, keepdims=True)
        _shifted_copies(dsh, tm + HALO - 8)
        _shifted_copies(gsh, tm + HALO - 8)

        def fold8(x):
            acc = x[0:8, :]
            for g in range(1, CONV_RB // 8):
                acc = acc + x[8 * g:8 * g + 8, :]
            return acc

        for cl in range(nlc):
            ls = slice(cl * HEAD, (cl + 1) * HEAD)

            def rowblock(r, sums, ls=ls):
                r0 = r * CONV_RB
                rows = pl.ds(pl.multiple_of(r0, CONV_RB), CONV_RB)
                dcb = dsh[0, rows, ls]
                dglu = jnp.zeros((CONV_RB, HEAD), F32)
                new = []
                for j in range(CONV_W):
                    od = 30 - j
                    og = j + 2
                    atd = pl.ds(pl.multiple_of(r0 + od - od % 8, 8), CONV_RB)
                    atg = pl.ds(pl.multiple_of(r0 + og - og % 8, 8), CONV_RB)
                    dglu = dglu + w_ref[j:j + 1, ls] * dsh[od % 8, atd, ls]
                    new.append(sums[j] + fold8(dcb * gsh[og % 8, atg, ls]))
                a = a_ref[rows, ls]
                sgt = _sigmoid(gt_ref[rows, ls])
                da = (dglu * sgt).astype(BF16)
                dg = (dglu * a * sgt * (1.0 - sgt)).astype(BF16)
                du_ref[0, rows, ls] = da
                du_ref[1, rows, ls] = dg
                new.append(sums[CONV_W] + fold8(da.astype(F32)))
                new.append(sums[CONV_W + 1] + fold8(dg.astype(F32)))
                return tuple(new)

            zero = jnp.zeros((8, HEAD), F32)
            sums = lax.fori_loop(0, tm // CONV_RB, rowblock, (zero,) * (CONV_W + 2))
            for k in range(CONV_W + 2):
                part[8 * k:8 * k + 8, ls] += sums[k]

        @pl.when(i == nt - 1)
        def _():
            for row, k in acc_rows.items():
                s_ref[row:row + 1, :] = jnp.sum(part[8 * k:8 * k + 8, :], axis=0, keepdims=True)

    row = lambda i: (i, 0)
    nxt = lambda i: (jnp.minimum((i + 1) * hb, nh - 1), 0)
    cur = lambda gi: pl.BlockSpec((None, tm, D), lambda i: (gi, i, 0))
    prv = lambda gi: pl.BlockSpec((None, HALO, D), lambda i: (gi, jnp.maximum(i * hb - 1, 0), 0))
    fix = lambda i: (0, 0)
    return carried_call(
        body, comm, name=name, grid=(nt,),
        in_specs=[pl.BlockSpec((tm, D), row), pl.BlockSpec((tm, D), row),
                  pl.BlockSpec((HALO, D), nxt), pl.BlockSpec((HALO, D), nxt),
                  cur(0), cur(1), prv(0), prv(1), pl.BlockSpec((VEC_ROWS, D), fix)],
        out_specs=[pl.BlockSpec((2, tm, D), lambda i: (0, i, 0)), pl.BlockSpec((VEC_ROWS, D), fix)],
        out_shape=[jax.ShapeDtypeStruct((2, T, D), BF16), jax.ShapeDtypeStruct((VEC_ROWS, D), F32)],
        scratch_shapes=[pltpu.VMEM((8, tm + HALO, D), F32), pltpu.VMEM((8, tm + HALO, D), F32),
                        pltpu.VMEM((8 * (CONV_W + 2), D), F32)],
        args=(dv2, c, dv2, c, u, u, u, u, vec))


def loss_grad(y, target, *, tm):
    T, D = y.shape
    nt = T // tm

    def body(y_ref, t_ref, l_ref, d_ref, acc):
        i = pl.program_id(0)

        @pl.when(i == 0)
        def _():
            acc[...] = jnp.zeros_like(acc)

        e = y_ref[...] - t_ref[...]
        d_ref[...] = e * (1.0 / D)
        acc[...] += jnp.sum(e * e, axis=0, keepdims=True)

        @pl.when(i == nt - 1)
        def _():
            l_ref[...] = 0.5 * jnp.sum(acc[...], axis=1, keepdims=True) * (1.0 / D)

    row = lambda i: (i, 0)
    return _call(
        body, name="loss_grad", grid=(nt,),
        in_specs=[pl.BlockSpec((tm, D), row), pl.BlockSpec((tm, D), row)],
        out_specs=[pl.BlockSpec((1, 1), lambda i: (0, 0)), pl.BlockSpec((tm, D), row)],
        out_shape=[jax.ShapeDtypeStruct((1, 1), F32), jax.ShapeDtypeStruct((T, D), F32)],
        scratch_shapes=[pltpu.VMEM((1, D), F32)], args=(y, target))


def _rows_block(R, C, budget=1 << 20):
    tr = R
    while tr * C * 4 > budget and tr % 32 == 0:
        tr //= 2
    return tr


def adamw(w, g, m, v, *, name):
    R, C = w.shape
    tr = _rows_block(R, C)

    def body(w_ref, g_ref, m_ref, v_ref, d_ref, mo_ref, vo_ref):
        g_ = g_ref[...]
        mn = ADAM_B1 * m_ref[...] + (1.0 - ADAM_B1) * g_
        vn = ADAM_B2 * v_ref[...] + (1.0 - ADAM_B2) * jnp.square(g_)
        m_hat = mn / (1.0 - ADAM_B1 ** ADAM_STEP)
        v_hat = vn / (1.0 - ADAM_B2 ** ADAM_STEP)
        d_ref[...] = -ADAM_LR * (m_hat / (jnp.sqrt(v_hat) + ADAM_EPS) + ADAM_WD * w_ref[...])
        mo_ref[...] = mn
        vo_ref[...] = vn

    spec = pl.BlockSpec((tr, C), lambda i: (i, 0))
    sd = jax.ShapeDtypeStruct((R, C), F32)
    return _call(body, name=name, grid=(R // tr,), in_specs=[spec] * 4, out_specs=[spec] * 3, out_shape=[sd] * 3,
                 args=(w, g, m, v))


def sum_slots(slots, *, name):
    _, R, C = slots.shape
    tr = _rows_block(R, C, budget=1 << 19)

    def body(s_ref, o_ref):
        acc = s_ref[0].astype(F32)
        for d in range(1, 8):
            acc = acc + s_ref[d].astype(F32)
        o_ref[...] = acc

    return _call(body, name=name, grid=(R // tr,), in_specs=[pl.BlockSpec((8, tr, C), lambda i: (0, i, 0))],
                 out_specs=[pl.BlockSpec((tr, C), lambda i: (i, 0))], out_shape=[jax.ShapeDtypeStruct((R, C), F32)],
                 args=(slots,))[0]


def _adam_nd(w, g, m, v, name):
    shp = w.shape
    c = shp[-1]
    f2 = lambda a: a.reshape(-1, c)
    d, mn, vn = adamw(f2(w), f2(g), f2(m), f2(v), name=name)
    return d.reshape(shp), mn.reshape(shp), vn.reshape(shp)


def _reduced(slots, name):
    out = []
    for s in slots:
        c = s.shape[-1]
        out.append(sum_slots(s.reshape(8, -1, c), name=name).reshape(s.shape[1:]))
    return out


def kernel(x, ln_mix_g, ln_mix_b, ln_ffn_g, ln_ffn_b, ffn_w1, ffn_w2, a_w_in, a_lb_logits, a_norm_g, a_w_out, b_w_pw1, b_b_pw1, b_w_dw, b_b_dw, b_ln_g, b_ln_b, b_w_pw2, b_b_pw2, loss_target, m_ln_mix_g, m_ln_mix_b, m_ln_ffn_g, m_ln_ffn_b, m_ffn_w1, m_ffn_w2, m_a_w_in, m_a_lb_logits, m_a_norm_g, m_a_w_out, m_b_w_pw1, m_b_b_pw1, m_b_w_dw, m_b_b_dw, m_b_ln_g, m_b_ln_b, m_b_w_pw2, m_b_b_pw2, v_ln_mix_g, v_ln_mix_b, v_ln_ffn_g, v_ln_ffn_b, v_ffn_w1, v_ffn_w2, v_a_w_in, v_a_lb_logits, v_a_norm_g, v_a_w_out, v_b_w_pw1, v_b_b_pw1, v_b_w_dw, v_b_b_dw, v_b_ln_g, v_b_ln_b, v_b_w_pw2, v_b_b_pw2):
    names = ["ln_mix_g", "ln_mix_b", "ln_ffn_g", "ln_ffn_b", "ffn_w1", "ffn_w2", "a_w_in", "a_lb_logits", "a_norm_g",
             "a_w_out", "b_w_pw1", "b_b_pw1", "b_w_dw", "b_b_dw", "b_ln_g", "b_ln_b", "b_w_pw2", "b_b_pw2"]
    w = dict(zip(names, [ln_mix_g, ln_mix_b, ln_ffn_g, ln_ffn_b, ffn_w1, ffn_w2, a_w_in, a_lb_logits, a_norm_g, a_w_out,
                         b_w_pw1, b_b_pw1, b_w_dw, b_b_dw, b_ln_g, b_ln_b, b_w_pw2, b_b_pw2]))
    m = dict(zip(names, [m_ln_mix_g, m_ln_mix_b, m_ln_ffn_g, m_ln_ffn_b, m_ffn_w1, m_ffn_w2, m_a_w_in, m_a_lb_logits,
                         m_a_norm_g, m_a_w_out, m_b_w_pw1, m_b_b_pw1, m_b_w_dw, m_b_b_dw, m_b_ln_g, m_b_ln_b, m_b_w_pw2,
                         m_b_b_pw2]))
    v = dict(zip(names, [v_ln_mix_g, v_ln_mix_b, v_ln_ffn_g, v_ln_ffn_b, v_ffn_w1, v_ffn_w2, v_a_w_in, v_a_lb_logits,
                         v_a_norm_g, v_a_w_out, v_b_w_pw1, v_b_b_pw1, v_b_w_dw, v_b_b_dw, v_b_ln_g, v_b_ln_b, v_b_w_pw2,
                         v_b_b_pw2]))
    T, D = x.shape[1], x.shape[2]
    DS = D // 4
    F = 4 * ffn_w1.shape[2]
    chip = 2 * lax.axis_index("x") + lax.axis_index("y")
    tm = min(T, 512)
    tmw = min(T, 1024)
    tmc = min(T, 256)
    rb = min(T, 512)
    tf = min(F // 4, 1024)
    xin, target = x[0], loss_target[0]

    def mix_shards(i):
        j = i // 2
        if i % 2 == 0:
            return [a_w_in[j].astype(BF16), a_w_out[j].astype(BF16)]
        vec = jnp.concatenate([b_w_dw[j], jnp.zeros((1, DS), F32), b_b_dw[j][None], b_ln_g[j][None], b_ln_b[j][None],
                               b_b_pw2[j][None], b_b_pw1[j].reshape(2, DS), jnp.zeros((2, DS), F32)], axis=0)
        return [b_w_pw1[j].astype(BF16), b_w_pw2[j].astype(BF16), vec]

    def ffn_shards(i):
        return [ffn_w1[i].astype(BF16), ffn_w2[i].astype(BF16)]

    def mix_weights(i, got):
        if i % 2 == 0:
            return {"w_in": got[0], "w_out": got[1].reshape(D, D)}
        pw1 = jnp.transpose(got[0].reshape(2, 2, D, D // 2), (0, 2, 1, 3)).reshape(2, D, D)
        vec = jnp.transpose(got[2], (1, 0, 2)).reshape(VEC_ROWS, D)
        return {"pw1": pw1, "pw2": got[1].reshape(D, D), "vec": vec,
                "b_pw1": got[2][:, 36:38, :].reshape(2, 1, D)}

    lb_all = lb_fwd(a_lb_logits)
    zeros_bias = jnp.zeros((1, D), F32)

    first = mix_shards(0)
    mixw = {"w_in": comm_call(GatherChips(first[:1]), name="gather_first")[0]}
    saved = []
    h = xin
    for i in range(DEPTH):
        j = i // 2
        s = {"x": h, "mixw": mixw}
        gf = GatherChips(ffn_shards(i) + (first[1:] if i == 0 else []))
        if i % 2 == 0:
            s["proj"] = mm_groups(h, mixw["w_in"], jnp.zeros((4, 1, D), F32), tm=tm, name="a_in_proj")
            (s["o"], s["og"], s["st"]), got = hgrn_fwd(s["proj"], lb_all[j:j + 1], a_norm_g[j:j + 1], rb=rb,
                                                       name="hgrn_fwd", comm=gf)
            if i == 0:
                mixw["w_out"] = got[2].reshape(D, D)
            s["r1"], s["x1"] = mm_res_ln(s["og"], mixw["w_out"], zeros_bias, h, ln_mix_g[i:i + 1], ln_mix_b[i:i + 1],
                                         tm=tm, name="a_out_ln")
        else:
            s["u"] = mm_groups(h, mixw["pw1"], mixw["b_pw1"], tm=tm, name="b_pw1")
            (s["c"], s["v2"]), got = conv_fwd(s["u"], mixw["vec"], tm=tmc, name="conv_fwd", comm=gf)
            s["r1"], s["x1"] = mm_res_ln(s["v2"], mixw["pw2"], mixw["vec"][35:36], h, ln_mix_g[i:i + 1],
                                         ln_mix_b[i:i + 1], tm=tm, name="b_pw2_ln")
        s["w1"], s["w2"] = got[0], got[1].reshape(F, D)
        gm = GatherChips(mix_shards(i + 1)) if i + 1 < DEPTH else None
        (s["z"], s["r2"], h), got = ffn_fwd(s["x1"], s["w1"], s["w2"], ln_ffn_g[i:i + 1], ln_ffn_b[i:i + 1],
                                            tm=tm, tf=tf, name="ffn_fwd", comm=gm)
        if gm is not None:
            mixw = mix_weights(i + 1, got)
        saved.append(s)

    loss_part, dh = loss_grad(h, target, tm=tm)
    loss = lax.psum(loss_part[0, 0], ("x", "y", "c"))

    gr = {k: [None] * DEPTH for k in ("ln_mix_g", "ln_mix_b", "ln_ffn_g", "ln_ffn_b", "ffn_w1", "ffn_w2")}
    for k in ("a_w_in", "a_w_out", "a_norm_g", "a_dlb", "b_w_pw1", "b_w_pw2", "b_vec", "b_b_pw2"):
        gr[k] = [None] * 2
    w_in_name = ("a_w_in", "b_w_pw1")
    w_out_name = ("a_w_out", "b_w_pw2")
    pending = None

    for i in reversed(range(DEPTH)):
        j = i // 2
        s = saved[i]
        mixw = s["mixw"]
        sm = ScatterPieces([pending[1]]) if pending is not None else None
        (dz, dx1, drb2, sums2), slots = ffn_bwd_dx(dh, s["r2"], ln_ffn_g[i:i + 1], s["z"], s["w1"], s["w2"],
                                                   tm=tm, tf=tf, name="ffn_bwd_dx", comm=sm)
        if pending is not None:
            gr[w_in_name[pending[0] % 2]][pending[0] // 2] = _reduced(slots, "sum_mix_grads")[0]
        gr["ln_ffn_g"][i], gr["ln_ffn_b"][i] = sums2[0], sums2[1]
        dw1 = mm_tn(s["x1"], dz[None], tm=tmw, tk=D, tn=F // 4, name="ffn_dw1")[0]
        dw2 = mm_tn(s["z"], drb2[None], tm=tmw, tk=F // 4, tn=D, relu2=True, name="ffn_dw2")[0, 0]
        wmix = mixw["w_out"] if i % 2 == 0 else mixw["pw2"]
        dr1, drb1, dmo, sums1 = ln_bwd_mm(dx1, s["r1"], ln_mix_g[i:i + 1], wmix, tm=tm, name="mix_ln_bwd")
        gr["ln_mix_g"][i], gr["ln_mix_b"][i] = sums1[0], sums1[1]
        if i % 2 == 0:
            dwo = mm_tn(s["og"], drb1[None], tm=tmw, tk=D, tn=D, name="a_dw_out")[0, 0].reshape(4, DS, D)
            sf = ScatterPieces([dw1, dw2.reshape(4, F // 4, D), dwo])
            (dproj, hs), slots = hgrn_bwd(s["proj"], s["o"], dmo, s["st"], lb_all[j:j + 1], a_norm_g[j:j + 1], rb=rb,
                                          name="hgrn_bwd", comm=sf)
            gr["a_norm_g"][j], gr["a_dlb"][j] = hs[0], hs[1]
            dwi = mm_tn(s["x"], dproj, tm=tmw, tk=D, tn=D, name="a_dw_in")[:, 0]
            dy_in, w_in_t, dx_name = dproj, mixw["w_in"], "a_dx"
        else:
            dwo = mm_tn(s["v2"], drb1[None], tm=tmw, tk=D, tn=D, name="b_dw_pw2")[0, 0].reshape(4, DS, D)
            sf = ScatterPieces([dw1, dw2.reshape(4, F // 4, D), dwo])
            (du, cs), slots = conv_bwd(dmo, s["c"], s["u"], mixw["vec"], tm=tmc, name="conv_bwd", comm=sf)
            gr["b_vec"][j], gr["b_b_pw2"][j] = cs, sums1[2]
            dwi = mm_tn(s["x"], du, tm=tmw, tk=D, tn=D // 2, name="b_dw_pw1").reshape(4, D, D // 2)
            dy_in, w_in_t, dx_name = du, mixw["pw1"], "b_dx"
        gr["ffn_w1"][i], gr["ffn_w2"][i], gr[w_out_name[i % 2]][j] = _reduced(slots, "sum_ffn_grads")
        dh, slots = mm_nt_acc(dy_in, w_in_t, dr1, tm=tm, name=dx_name, comm=ScatterPieces([dwi]) if i == 0 else None)
        if i == 0:
            gr[w_in_name[0]][0] = _reduced(slots, "sum_mix_grads")[0]
        pending = (i, dwi)
    grad_x = dh[None]

    small = {k: jnp.stack(gr[k]) for k in ("ln_mix_g", "ln_mix_b", "ln_ffn_g", "ln_ffn_b", "a_norm_g", "b_vec", "b_b_pw2")}
    small["a_lb_logits"] = lb_bwd(a_lb_logits, jnp.stack(gr["a_dlb"]))
    small_names = ["ln_mix_g", "ln_mix_b", "ln_ffn_g", "ln_ffn_b", "a_lb_logits", "a_norm_g", "b_b_pw2", "b_vec"]
    rows = [small[k].reshape(-1, D) for k in small_names]
    counts = [r.shape[0] for r in rows]
    rows = [jnp.pad(r, ((0, (-r.shape[0]) % 8), (0, 0))) for r in rows]
    summed = all_reduce_small(jnp.concatenate(rows, axis=0))
    sm = {}
    off = 0
    for k, n, r in zip(small_names, counts, rows):
        sm[k] = summed[off:off + n]
        off += r.shape[0]
    bvec = sm["b_vec"].reshape(2, VEC_ROWS, D)

    def shard_cols(a):
        return lax.dynamic_slice_in_dim(a, chip * DS, DS, axis=a.ndim - 1)

    grads = {k: jnp.stack(gr[k]) for k in ("ffn_w1", "ffn_w2", "a_w_in", "a_w_out", "b_w_pw1", "b_w_pw2")}
    for k in ("ln_mix_g", "ln_mix_b", "ln_ffn_g", "ln_ffn_b", "a_lb_logits", "a_norm_g"):
        grads[k] = sm[k]
    grads["b_b_pw1"] = lax.dynamic_slice_in_dim(bvec[:, 36:38, :].reshape(2, 2 * D), chip * (D // 2), D // 2, axis=1)
    grads["b_w_dw"] = shard_cols(bvec[:, 0:CONV_W, :])
    grads["b_b_dw"] = shard_cols(bvec[:, 32, :])
    grads["b_ln_g"] = shard_cols(bvec[:, 33, :])
    grads["b_ln_b"] = shard_cols(bvec[:, 34, :])
    grads["b_b_pw2"] = shard_cols(sm["b_b_pw2"])

    delta, new_m, new_v = {}, {}, {}
    for k in names:
        delta[k], new_m[k], new_v[k] = _adam_nd(w[k], grads[k], m[k], v[k], "adamw_" + k)
    return (loss, grad_x, *[grads[k] for k in names], *[delta[k] for k in names],
            *[new_m[k] for k in names], *[new_v[k] for k in names])
```

```python
import jax
import jax.numpy as jnp
from jax import lax
from jax.experimental import pallas as pl
from jax.experimental.pallas import tpu as pltpu

F32 = jnp.float32
BF16 = jnp.bfloat16
MESH = pl.DeviceIdType.MESH

DEPTH = 4
ALPHA = (2.0 * DEPTH) ** 0.25
LN_EPS = 1e-5
RMS_EPS = 1e-6
GATE_EPS = 1e-6
HEAD = 128
CHUNK = 128
SUB = 16
PAIR = 2
TRIP_CHUNKS = 2
CONV_W = 31
HALO = 32
VEC_ROWS = 40
CONV_RB = 32
ADAM_LR, ADAM_B1, ADAM_B2, ADAM_EPS, ADAM_WD, ADAM_STEP = 0.001, 0.9, 0.999, 1e-08, 0.01, 10
VMEM_LIMIT = 56 * 1024 * 1024
ANY = pl.BlockSpec(memory_space=pl.ANY)


def _dot(a, b):
    return jnp.dot(a, b, preferred_element_type=F32)


def _dot_nt(a, b):
    return lax.dot_general(a, b, (((1,), (1,)), ((), ())), preferred_element_type=F32)


def _dot_tn(a, b):
    return lax.dot_general(a, b, (((0,), (0,)), ((), ())), preferred_element_type=F32)


def _sigmoid(x):
    return 1.0 / (1.0 + jnp.exp(-x))


def _ln_stats(r):
    mu = jnp.mean(r, axis=-1, keepdims=True)
    xc = r - mu
    var = jnp.mean(xc * xc, axis=-1, keepdims=True)
    rstd = lax.rsqrt(var + LN_EPS)
    return xc * rstd, rstd


def _ln_bwd(dy, xhat, rstd, g):
    dyg = dy * g
    m1 = jnp.mean(dyg, axis=-1, keepdims=True)
    m2 = jnp.mean(dyg * xhat, axis=-1, keepdims=True)
    return rstd * (dyg - m1 - xhat * m2)


def _place():
    return lax.axis_index("x"), lax.axis_index("y"), lax.axis_index("c")


class GatherChips:
    def __init__(self, arrs):
        self.ins = list(arrs)
        n = len(arrs)
        self.out_shapes = [jax.ShapeDtypeStruct((4,) + a.shape, a.dtype) for a in arrs]
        self.sems = [pltpu.SemaphoreType.DMA((3 * n,)), pltpu.SemaphoreType.DMA((3 * n,)),
                     pltpu.SemaphoreType.DMA((n,))]

    def copies(self, ins, outs, send, recv, loc):
        x, y, c = _place()
        me = 2 * x + y
        local, remote = [], []
        for a in range(len(ins)):
            local.append(pltpu.make_async_copy(ins[a], outs[a].at[me], loc.at[a]))
            for j, (px, py) in enumerate([(1 - x, y), (x, 1 - y), (1 - x, 1 - y)]):
                remote.append(pltpu.make_async_remote_copy(
                    src_ref=ins[a], dst_ref=outs[a].at[me], send_sem=send.at[3 * a + j], recv_sem=recv.at[3 * a + j],
                    device_id=(px, py, c), device_id_type=MESH))
        return local + remote


class ScatterPieces:
    def __init__(self, arrs):
        self.ins = list(arrs)
        n = len(arrs)
        self.out_shapes = [jax.ShapeDtypeStruct((8,) + a.shape[1:], a.dtype) for a in arrs]
        self.sems = [pltpu.SemaphoreType.DMA((7 * n,)), pltpu.SemaphoreType.DMA((7 * n,)),
                     pltpu.SemaphoreType.DMA((n,))]

    def copies(self, ins, outs, send, recv, loc):
        x, y, c = _place()
        me = 4 * x + 2 * y + c
        local, remote = [], []
        for a in range(len(ins)):
            local.append(pltpu.make_async_copy(ins[a].at[2 * x + y], outs[a].at[me], loc.at[a]))
            k = 0
            for fx in (0, 1):
                for fy in (0, 1):
                    for fc in (0, 1):
                        if fx or fy or fc:
                            tx, ty = x ^ fx, y ^ fy
                            remote.append(pltpu.make_async_remote_copy(
                                src_ref=ins[a].at[2 * tx + ty], dst_ref=outs[a].at[me],
                                send_sem=send.at[7 * a + k], recv_sem=recv.at[7 * a + k],
                                device_id=(tx, ty, c ^ fc), device_id_type=MESH))
                            k += 1
        return local + remote


def carried_call(body, comm, *, name, grid, in_specs, out_specs, out_shape, scratch_shapes, args):
    sem = ("arbitrary",) * len(grid)
    params = pltpu.CompilerParams(dimension_semantics=sem, vmem_limit_bytes=VMEM_LIMIT)
    if comm is None:
        res = pl.pallas_call(body, name=name, grid=grid, in_specs=in_specs, out_specs=out_specs, out_shape=out_shape,
                             scratch_shapes=scratch_shapes, compiler_params=params)(*args)
        return res, []
    ni, no, nscr = len(in_specs), len(out_specs), len(scratch_shapes)
    ci, co = len(comm.ins), len(comm.out_shapes)

    def both(*refs):
        ins, refs = refs[:ni], refs[ni:]
        cins, refs = refs[:ci], refs[ci:]
        outs, refs = refs[:no], refs[no:]
        couts, refs = refs[:co], refs[co:]
        scr, sems = refs[:nscr], refs[nscr:]
        first = pl.program_id(0) == 0
        last = pl.program_id(0) == grid[0] - 1
        for d in range(1, len(grid)):
            first = first & (pl.program_id(d) == 0)
            last = last & (pl.program_id(d) == grid[d] - 1)

        @pl.when(first)
        def _():
            for cp in comm.copies(cins, couts, *sems):
                cp.start()

        body(*ins, *outs, *scr)

        @pl.when(last)
        def _():
            for cp in comm.copies(cins, couts, *sems):
                cp.wait()

    res = pl.pallas_call(
        both, name=name, grid=grid, in_specs=list(in_specs) + [ANY] * ci, out_specs=list(out_specs) + [ANY] * co,
        out_shape=list(out_shape) + comm.out_shapes, scratch_shapes=list(scratch_shapes) + comm.sems,
        compiler_params=params)(*args, *comm.ins)
    return res[:no], res[no:]


def comm_call(comm, *, name):
    ci, co = len(comm.ins), len(comm.out_shapes)

    def body(*refs):
        cps = comm.copies(refs[:ci], refs[ci:ci + co], *refs[ci + co:])
        for cp in cps:
            cp.start()
        for cp in cps:
            cp.wait()

    return pl.pallas_call(body, name=name, in_specs=[ANY] * ci, out_specs=[ANY] * co, out_shape=comm.out_shapes,
                          scratch_shapes=comm.sems, compiler_params=pltpu.CompilerParams(has_side_effects=True))(*comm.ins)


def all_reduce_small(v):
    R, C = v.shape

    def body(v_ref, o_ref, slots, send, recv):
        x, y, c = _place()
        me = 4 * x + 2 * y + c
        slots[me] = v_ref[...]
        cps = []
        k = 0
        for fx in (0, 1):
            for fy in (0, 1):
                for fc in (0, 1):
                    if fx or fy or fc:
                        cps.append(pltpu.make_async_remote_copy(
                            src_ref=v_ref, dst_ref=slots.at[me], send_sem=send.at[k], recv_sem=recv.at[k],
                            device_id=(x ^ fx, y ^ fy, c ^ fc), device_id_type=MESH))
                        k += 1
        for cp in cps:
            cp.start()
        for cp in cps:
            cp.wait()
        acc = slots[0]
        for d in range(1, 8):
            acc = acc + slots[d]
        o_ref[...] = acc

    vm = pl.BlockSpec(memory_space=pltpu.VMEM)
    return pl.pallas_call(
        body, name="all_reduce_small", in_specs=[vm], out_specs=vm,
        out_shape=jax.ShapeDtypeStruct((R, C), F32),
        scratch_shapes=[pltpu.VMEM((8, R, C), F32), pltpu.SemaphoreType.DMA((7,)), pltpu.SemaphoreType.DMA((7,))],
        compiler_params=pltpu.CompilerParams(has_side_effects=True, vmem_limit_bytes=VMEM_LIMIT),
    )(v)


def _call(body, *, name, grid, in_specs, out_specs, out_shape, scratch_shapes=(), args):
    res, _ = carried_call(body, None, name=name, grid=grid, in_specs=in_specs, out_specs=out_specs,
                          out_shape=out_shape, scratch_shapes=list(scratch_shapes), args=args)
    return res


def mm_groups(a, w, bias, *, tm, name):
    T, K = a.shape
    G, _, N = w.shape

    def body(a_ref, w_ref, b_ref, o_ref):
        o_ref[...] = _dot(a_ref[...].astype(BF16), w_ref[...]) + b_ref[...]

    return _call(
        body, name=name, grid=(G, T // tm),
        in_specs=[pl.BlockSpec((tm, K), lambda g, i: (i, 0)),
                  pl.BlockSpec((None, K, N), lambda g, i: (g, 0, 0)),
                  pl.BlockSpec((None, 1, N), lambda g, i: (g, 0, 0))],
        out_specs=[pl.BlockSpec((None, tm, N), lambda g, i: (g, i, 0))],
        out_shape=[jax.ShapeDtypeStruct((G, T, N), F32)], args=(a, w, bias))[0]


def mm_res_ln(a, w, bias, res, g, b, *, tm, name):
    T, K = a.shape
    N = w.shape[1]

    def body(a_ref, w_ref, bias_ref, res_ref, g_ref, b_ref, r_ref, y_ref):
        r = ALPHA * res_ref[...] + _dot(a_ref[...], w_ref[...]) + bias_ref[...]
        r_ref[...] = r
        xhat, _ = _ln_stats(r)
        y_ref[...] = xhat * g_ref[...] + b_ref[...]

    row = lambda i: (i, 0)
    fix = lambda i: (0, 0)
    return _call(
        body, name=name, grid=(T // tm,),
        in_specs=[pl.BlockSpec((tm, K), row), pl.BlockSpec((K, N), fix), pl.BlockSpec((1, N), fix),
                  pl.BlockSpec((tm, N), row), pl.BlockSpec((1, N), fix), pl.BlockSpec((1, N), fix)],
        out_specs=[pl.BlockSpec((tm, N), row), pl.BlockSpec((tm, N), row)],
        out_shape=[jax.ShapeDtypeStruct((T, N), F32), jax.ShapeDtypeStruct((T, N), F32)],
        args=(a, w, bias, res, g, b))


def ffn_fwd(x, w1, w2, g, b, *, tm, tf, name, comm=None):
    T, D = x.shape
    NC, _, FC = w1.shape
    F = NC * FC
    per = FC // tf
    nf = F // tf

    def body(x_ref, w1_ref, w2_ref, g_ref, b_ref, z_ref, r_ref, y_ref, acc_ref, xb_ref):
        f = pl.program_id(1)

        @pl.when(f == 0)
        def _():
            acc_ref[...] = jnp.zeros_like(acc_ref)
            xb_ref[...] = x_ref[...].astype(BF16)

        z = _dot(xb_ref[...], w1_ref[...])
        z_ref[...] = z.astype(BF16)
        h = jnp.square(jnp.maximum(z, 0.0)).astype(BF16)
        acc_ref[...] += _dot(h, w2_ref[...])

        @pl.when(f == nf - 1)
        def _():
            r = ALPHA * x_ref[...] + acc_ref[...]
            r_ref[...] = r
            xhat, _ = _ln_stats(r)
            y_ref[...] = xhat * g_ref[...] + b_ref[...]

    return carried_call(
        body, comm, name=name, grid=(T // tm, nf),
        in_specs=[pl.BlockSpec((tm, D), lambda i, f: (i, 0)),
                  pl.BlockSpec((None, D, tf), lambda i, f: (f // per, 0, f % per)),
                  pl.BlockSpec((tf, D), lambda i, f: (f, 0)),
                  pl.BlockSpec((1, D), lambda i, f: (0, 0)),
                  pl.BlockSpec((1, D), lambda i, f: (0, 0))],
        out_specs=[pl.BlockSpec((tm, tf), lambda i, f: (i, f)),
                   pl.BlockSpec((tm, D), lambda i, f: (i, 0)),
                   pl.BlockSpec((tm, D), lambda i, f: (i, 0))],
        out_shape=[jax.ShapeDtypeStruct((T, F), BF16), jax.ShapeDtypeStruct((T, D), F32),
                   jax.ShapeDtypeStruct((T, D), F32)],
        scratch_shapes=[pltpu.VMEM((tm, D), F32), pltpu.VMEM((tm, D), BF16)],
        args=(x, w1, w2, g, b))


def ln_bwd_mm(dy, r, g, w, *, tm, name):
    T, N = dy.shape
    Ko = w.shape[0]

    def body(dy_ref, r_ref, g_ref, w_ref, dr_ref, drb_ref, o_ref, s_ref):
        @pl.when(pl.program_id(0) == 0)
        def _():
            s_ref[...] = jnp.zeros_like(s_ref)

        dy_ = dy_ref[...]
        xhat, rstd = _ln_stats(r_ref[...])
        dr = _ln_bwd(dy_, xhat, rstd, g_ref[...])
        dr_ref[...] = dr
        drb = dr.astype(BF16)
        drb_ref[...] = drb
        o_ref[...] = _dot_nt(drb, w_ref[...])
        s_ref[0:1, :] += jnp.sum(dy_ * xhat, axis=0, keepdims=True)
        s_ref[1:2, :] += jnp.sum(dy_, axis=0, keepdims=True)
        s_ref[2:3, :] += jnp.sum(dr, axis=0, keepdims=True)

    row = lambda i: (i, 0)
    fix = lambda i: (0, 0)
    return _call(
        body, name=name, grid=(T // tm,),
        in_specs=[pl.BlockSpec((tm, N), row), pl.BlockSpec((tm, N), row), pl.BlockSpec((1, N), fix),
                  pl.BlockSpec((Ko, N), fix)],
        out_specs=[pl.BlockSpec((tm, N), row), pl.BlockSpec((tm, N), row), pl.BlockSpec((tm, Ko), row),
                   pl.BlockSpec((8, N), fix)],
        out_shape=[jax.ShapeDtypeStruct((T, N), F32), jax.ShapeDtypeStruct((T, N), BF16),
                   jax.ShapeDtypeStruct((T, Ko), F32), jax.ShapeDtypeStruct((8, N), F32)],
        args=(dy, r, g, w))


def ffn_bwd_dx(dy, r, g, z, w1, w2, *, tm, tf, name, comm=None):
    T, D = dy.shape
    NC, _, FC = w1.shape
    F = NC * FC
    per = FC // tf
    nf = F // tf

    def body(dy_ref, r_ref, g_ref, z_ref, w1_ref, w2_ref, dz_ref, dx_ref, drb_ref, s_ref, dr_scr, acc_ref):
        i = pl.program_id(0)
        f = pl.program_id(1)

        @pl.when((i == 0) & (f == 0))
        def _():
            s_ref[...] = jnp.zeros_like(s_ref)

        @pl.when(f == 0)
        def _():
            dy_ = dy_ref[...]
            xhat, rstd = _ln_stats(r_ref[...])
            dr = _ln_bwd(dy_, xhat, rstd, g_ref[...])
            dr_scr[...] = dr
            drb_ref[...] = dr.astype(BF16)
            acc_ref[...] = jnp.zeros_like(acc_ref)
            s_ref[0:1, :] += jnp.sum(dy_ * xhat, axis=0, keepdims=True)
            s_ref[1:2, :] += jnp.sum(dy_, axis=0, keepdims=True)

        dh = _dot_nt(drb_ref[...], w2_ref[...])
        dz = (dh * (2.0 * jnp.maximum(z_ref[...].astype(F32), 0.0))).astype(BF16)
        dz_ref[...] = dz
        acc_ref[...] += _dot_nt(dz, w1_ref[...])

        @pl.when(f == nf - 1)
        def _():
            dx_ref[...] = ALPHA * dr_scr[...] + acc_ref[...]

    return carried_call(
        body, comm, name=name, grid=(T // tm, nf),
        in_specs=[pl.BlockSpec((tm, D), lambda i, f: (i, 0)),
                  pl.BlockSpec((tm, D), lambda i, f: (i, 0)),
                  pl.BlockSpec((1, D), lambda i, f: (0, 0)),
                  pl.BlockSpec((tm, tf), lambda i, f: (i, f)),
                  pl.BlockSpec((None, D, tf), lambda i, f: (f // per, 0, f % per)),
                  pl.BlockSpec((tf, D), lambda i, f: (f, 0))],
        out_specs=[pl.BlockSpec((tm, tf), lambda i, f: (i, f)),
                   pl.BlockSpec((tm, D), lambda i, f: (i, 0)),
                   pl.BlockSpec((tm, D), lambda i, f: (i, 0)),
                   pl.BlockSpec((8, D), lambda i, f: (0, 0))],
        out_shape=[jax.ShapeDtypeStruct((T, F), BF16), jax.ShapeDtypeStruct((T, D), F32),
                   jax.ShapeDtypeStruct((T, D), BF16), jax.ShapeDtypeStruct((8, D), F32)],
        scratch_shapes=[pltpu.VMEM((tm, D), F32), pltpu.VMEM((tm, D), F32)],
        args=(dy, r, g, z, w1, w2))


def mm_tn(a, b, *, tm, tk, tn, relu2=False, name):
    T, K = a.shape
    G, _, N = b.shape
    nt = T // tm

    def body(a_ref, b_ref, o_ref, acc_ref):
        t = pl.program_id(3)

        @pl.when(t == 0)
        def _():
            acc_ref[...] = jnp.zeros_like(acc_ref)

        av = a_ref[...]
        if relu2:
            av = jnp.square(jnp.maximum(av.astype(F32), 0.0))
        acc_ref[...] += _dot_tn(av.astype(BF16), b_ref[...])

        @pl.when(t == nt - 1)
        def _():
            o_ref[...] = acc_ref[...].astype(BF16)

    return _call(
        body, name=name, grid=(G, K // tk, N // tn, nt),
        in_specs=[pl.BlockSpec((tm, tk), lambda g, k, n, t: (t, k)),
                  pl.BlockSpec((None, tm, tn), lambda g, k, n, t: (g, t, n))],
        out_specs=[pl.BlockSpec((None, None, tk, tn), lambda g, k, n, t: (g, n, k, 0))],
        out_shape=[jax.ShapeDtypeStruct((G, N // tn, K, tn), BF16)],
        scratch_shapes=[pltpu.VMEM((tk, tn), F32)], args=(a, b))[0]


def mm_nt_acc(dy, w, base, *, tm, name, comm=None):
    G, T, N = dy.shape
    K = w.shape[1]

    def body(dy_ref, w_ref, base_ref, o_ref):
        g = pl.program_id(1)

        @pl.when(g == 0)
        def _():
            o_ref[...] = ALPHA * base_ref[...]

        o_ref[...] += _dot_nt(dy_ref[...], w_ref[...])

    res, got = carried_call(
        body, comm, name=name, grid=(T // tm, G),
        in_specs=[pl.BlockSpec((None, tm, N), lambda i, g: (g, i, 0)),
                  pl.BlockSpec((None, K, N), lambda i, g: (g, 0, 0)),
                  pl.BlockSpec((tm, K), lambda i, g: (i, 0))],
        out_specs=[pl.BlockSpec((tm, K), lambda i, g: (i, 0))],
        out_shape=[jax.ShapeDtypeStruct((T, K), F32)], scratch_shapes=[], args=(dy, w, base))
    return res[0], got


def _split3(x):
    x1 = x.astype(BF16)
    r1 = x - x1.astype(F32)
    x2 = r1.astype(BF16)
    x3 = (r1 - x2.astype(F32)).astype(BF16)
    return x1, x2, x3


def _tri_dot(tri, x):
    x1, x2, x3 = _split3(x)
    return _dot(tri, x1) + _dot(tri, x2) + _dot(tri, x3)


def _gates(pq, fz, lb):
    sg = _sigmoid(fz)
    f = lb + (1.0 - lb) * sg
    logf = jnp.log(jnp.maximum(f, GATE_EPS))
    sq = _sigmoid(pq)
    return pq * sq, 1.0 - f, logf, f, sg, sq


def _staggered(gens):
    live = []
    waiting = list(gens)
    for gen in waiting:
        next(gen)
    while live or waiting:
        if waiting:
            live.append(waiting.pop(0))
        nxt = []
        for gen in live:
            try:
                next(gen)
                nxt.append(gen)
            except StopIteration:
                pass
        live = nxt


def _block_diag_mask():
    ri = lax.broadcasted_iota(jnp.int32, (PAIR * HEAD, PAIR * HEAD), 0) // HEAD
    ci = lax.broadcasted_iota(jnp.int32, (PAIR * HEAD, PAIR * HEAD), 1) // HEAD
    return ri == ci


def _fill_off_diagonal(q_s, k_s, b_s, lhs, rhs):
    for i in range(1, CHUNK // SUB):
        lo = i * SUB
        ref = b_s[lo - 1:lo, :]
        qt = (q_s[lo:lo + SUB, :] * jnp.exp(b_s[lo:lo + SUB, :] - ref)).astype(BF16)
        kt = (k_s[0:lo, :] * jnp.exp(ref - b_s[0:lo, :])).astype(BF16)
        for h in range(PAIR):
            hl = slice(h * HEAD, (h + 1) * HEAD)
            lhs[h, lo:lo + SUB, (i - 1) * HEAD:i * HEAD] = qt[:, hl]
            rhs[h, 0:lo, (i - 1) * HEAD:i * HEAD] = kt[:, hl]


def hgrn_fwd(proj, lb, norm_g, *, rb, name, comm=None):
    _, T, D = proj.shape
    H = D // HEAD
    nb = T // rb
    nck = rb // CHUNK
    nsub = CHUNK // SUB
    W = PAIR * HEAD
    fam = CHUNK * SUB
    ntc = min(TRIP_CHUNKS, nck)

    def body(pq_ref, fz_ref, pv_ref, pg_ref, lb_ref, ng_ref, o_ref, og_ref, st_ref, S, q_a, k_a, v_a, b_a, lhs_a, rhs_a,
             p_a):
        @pl.when(pl.program_id(1) == 0)
        def _():
            S[...] = jnp.zeros_like(S)

        @pl.when((pl.program_id(0) == 0) & (pl.program_id(1) == 0))
        def _():
            lhs_a[...] = jnp.zeros_like(lhs_a)
            rhs_a[...] = jnp.zeros_like(rhs_a)

        ri = lax.broadcasted_iota(jnp.int32, (CHUNK, CHUNK), 0)
        ci = lax.broadcasted_iota(jnp.int32, (CHUNK, CHUNK), 1)
        tri = (ci <= ri).astype(BF16)
        ones = jnp.ones((HEAD, HEAD), BF16)
        bd = _block_diag_mask()

        def chunk_stages(j, c):
            q_s, k_s, v_s, b_s, lhs, rhs, p_s = (r.at[j] for r in (q_a, k_a, v_a, b_a, lhs_a, rhs_a, p_a))
            base = c * CHUNK
            rows = pl.ds(pl.multiple_of(base, CHUNK), CHUNK)
            q, k, logf, _, _, _ = _gates(pq_ref[rows, :], fz_ref[rows, :], lb_ref[...])
            v = pv_ref[rows, :]
            b = _tri_dot(tri, logf)
            yield
            q_s[...] = q
            k_s[...] = k
            v_s[...] = v
            b_s[...] = b
            bl = b_s[CHUNK - 1:CHUNK, :]
            upd = _dot_tn(v.astype(BF16), (k * jnp.exp(bl - b)).astype(BF16))
            Sv = S[...]
            for h in range(PAIR):
                st_ref[h, c] = Sv[h * HEAD:(h + 1) * HEAD, h * HEAD:(h + 1) * HEAD]
            o_int = _dot_nt((q * jnp.exp(b)).astype(BF16), Sv.astype(BF16))
            S[...] = Sv * jnp.exp(bl) + jnp.where(bd, upd, 0.0)
            _fill_off_diagonal(q_s, k_s, b_s, lhs, rhs)
            a = [_dot_nt(lhs[h], rhs[h]) for h in range(PAIR)]
            yield

            def diag_products(blocks):
                for i in blocks:
                    lo = i * SUB
                    for s in range(SUB):
                        m = lax.broadcasted_iota(jnp.int32, (SUB, W), 0) >= s
                        at = (i * SUB + s) * SUB
                        e = jnp.exp(b_s[lo:lo + SUB, :] - b_s[lo + s:lo + s + 1, :])
                        p = jnp.where(m, q_s[lo:lo + SUB, :] * (k_s[lo + s:lo + s + 1, :] * e), 0.0).astype(BF16)
                        for h in range(PAIR):
                            p_s[h * fam + at:h * fam + at + SUB, :] = p[:, h * HEAD:(h + 1) * HEAD]

            diag_products(range(0, nsub // 2))
            yield
            off = [_dot(a[h].astype(BF16), v_s[:, h * HEAD:(h + 1) * HEAD].astype(BF16)) for h in range(PAIR)]
            diag_products(range(nsub // 2, nsub))
            yield
            rs = _dot(p_s[...], ones)
            yield
            for i in range(nsub):
                lo = i * SUB
                blk = pl.ds(pl.multiple_of(base + lo, SUB), SUB)
                for h in range(PAIR):
                    hl = slice(h * HEAD, (h + 1) * HEAD)
                    acc = o_int[lo:lo + SUB, hl] + off[h][lo:lo + SUB, :]
                    for s in range(SUB):
                        at = h * fam + (i * SUB + s) * SUB
                        acc = acc + rs[at:at + SUB, :] * v_s[lo + s:lo + s + 1, hl]
                    o_ref[blk, hl] = acc
                    rinv = lax.rsqrt(jnp.mean(acc * acc, axis=-1, keepdims=True) + RMS_EPS)
                    pg = pg_ref[blk, hl]
                    og_ref[blk, hl] = (acc * rinv * ng_ref[:, hl] * (pg * _sigmoid(pg))).astype(BF16)
            yield

        def trip(g, carry):
            _staggered([chunk_stages(j, g * ntc + j) for j in range(ntc)])
            return carry

        lax.fori_loop(0, nck // ntc, trip, 0)

    def grp(gi):
        return pl.BlockSpec((None, rb, W), lambda h, r: (gi, r, h))

    vec = pl.BlockSpec((1, W), lambda h, r: (0, h))
    return carried_call(
        body, comm, name=name, grid=(H // PAIR, nb),
        in_specs=[grp(0), grp(1), grp(2), grp(3), vec, vec],
        out_specs=[pl.BlockSpec((rb, W), lambda h, r: (r, h)),
                   pl.BlockSpec((rb, W), lambda h, r: (r, h)),
                   pl.BlockSpec((PAIR, nck, HEAD, HEAD), lambda h, r: (h, r, 0, 0))],
        out_shape=[jax.ShapeDtypeStruct((T, D), F32), jax.ShapeDtypeStruct((T, D), BF16),
                   jax.ShapeDtypeStruct((H, T // CHUNK, HEAD, HEAD), F32)],
        scratch_shapes=[pltpu.VMEM((W, W), F32)] + [pltpu.VMEM((ntc, CHUNK, W), F32)] * 4
        + [pltpu.VMEM((ntc, PAIR, CHUNK, (nsub - 1) * HEAD), BF16)] * 2 + [pltpu.VMEM((ntc, PAIR * fam, HEAD), BF16)],
        args=(proj, proj, proj, proj, lb, norm_g))


def hgrn_bwd(proj, o, dog, states, lb, norm_g, *, rb, name, comm=None):
    _, T, D = proj.shape
    H = D // HEAD
    nb = T // rb
    nck = rb // CHUNK
    nsub = CHUNK // SUB
    W = PAIR * HEAD
    fam = CHUNK * SUB
    ntc = min(TRIP_CHUNKS, nck)

    def body(pq_ref, fz_ref, pv_ref, pg_ref, o_ref, dog_ref, st_ref, lb_ref, ng_ref, dp_ref, s_ref,
             dS, *per_chunk):
        S0_a, lhs_a, rhs_a = per_chunk[0], per_chunk[10], per_chunk[11]

        @pl.when(pl.program_id(1) == 0)
        def _():
            dS[...] = jnp.zeros_like(dS)
            s_ref[...] = jnp.zeros_like(s_ref)

        @pl.when((pl.program_id(0) == 0) & (pl.program_id(1) == 0))
        def _():
            lhs_a[...] = jnp.zeros_like(lhs_a)
            rhs_a[...] = jnp.zeros_like(rhs_a)
            S0_a[...] = jnp.zeros_like(S0_a)

        ri = lax.broadcasted_iota(jnp.int32, (CHUNK, CHUNK), 0)
        ci = lax.broadcasted_iota(jnp.int32, (CHUNK, CHUNK), 1)
        tri = (ci <= ri).astype(BF16)
        triu = (ci >= ri).astype(BF16)
        below = (ri // SUB) > (ci // SUB)
        ones = jnp.ones((HEAD, HEAD), BF16)
        bd = _block_diag_mask()
        last_row = lax.broadcasted_iota(jnp.int32, (CHUNK, W), 0) == CHUNK - 1

        def chunk_stages(j, c):
            (S0, q_s, k_s, v_s, b_s, do_s, dq_s, dk_s, dv_s, cr_s, lhs, rhs, ke_s, qe_s, p_s) = (r.at[j] for r in per_chunk)
            rows = pl.ds(pl.multiple_of(c * CHUNK, CHUNK), CHUNK)
            lb_ = lb_ref[...]
            pq = pq_ref[rows, :]
            q, k, logf, f, sg, sq = _gates(pq, fz_ref[rows, :], lb_)
            v = pv_ref[rows, :]
            b = _tri_dot(tri, logf)
            dpg = []
            for h in range(PAIR):
                hl = slice(h * HEAD, (h + 1) * HEAD)
                oh = o_ref[rows, hl]
                dog_ = dog_ref[rows, hl]
                pg = pg_ref[rows, hl]
                ng = ng_ref[:, hl]
                spg = _sigmoid(pg)
                rinv = lax.rsqrt(jnp.mean(oh * oh, axis=-1, keepdims=True) + RMS_EPS)
                on = oh * rinv
                dpg.append(dog_ * (on * ng) * (spg * (1.0 + pg * (1.0 - spg))))
                don = dog_ * (pg * spg)
                s_ref[0:1, hl] += jnp.sum(don * on, axis=0, keepdims=True)
                dxn = don * ng
                do_s[:, hl] = rinv * (dxn - on * jnp.mean(dxn * on, axis=-1, keepdims=True))
                S0[hl, hl] = st_ref[h, c]
            yield
            q_s[...] = q
            k_s[...] = k
            v_s[...] = v
            b_s[...] = b
            do = do_s[...]
            dob = do.astype(BF16)
            vb = v.astype(BF16)
            eb = jnp.exp(b)
            bl = b_s[CHUNK - 1:CHUNK, :]
            ebl = jnp.exp(bl)
            ekk = jnp.exp(bl - b)
            upd = _dot_tn(dob, (q * eb).astype(BF16))
            S0v = S0[...]
            dSv = dS[...]
            dSb = dSv.astype(BF16)
            dq_s[...] = _dot(dob, S0v.astype(BF16)) * eb
            dk_state = _dot(vb, dSb) * ekk
            dk_s[...] = dk_state
            dv_s[...] = _dot_nt((k * ekk).astype(BF16), dSb)
            extra = jnp.sum(k * dk_state, axis=0, keepdims=True) + ebl * jnp.sum(S0v * dSv, axis=0, keepdims=True)
            dS[...] = dSv * ebl + jnp.where(bd, upd, 0.0)

            _fill_off_diagonal(q_s, k_s, b_s, lhs, rhs)
            at = [_dot_nt(rhs[h], lhs[h]) for h in range(PAIR)]
            daf = [jnp.where(below, _dot_nt(dob[:, h * HEAD:(h + 1) * HEAD], vb[:, h * HEAD:(h + 1) * HEAD]), 0.0)
                   for h in range(PAIR)]
            yield

            def diag_products(blocks):
                for i in blocks:
                    lo = i * SUB
                    for s in range(SUB):
                        m = lax.broadcasted_iota(jnp.int32, (SUB, W), 0) >= s
                        at_ = (i * SUB + s) * SUB
                        qi = q_s[lo:lo + SUB, :]
                        e = jnp.where(m, jnp.exp(b_s[lo:lo + SUB, :] - b_s[lo + s:lo + s + 1, :]), 0.0)
                        ke = k_s[lo + s:lo + s + 1, :] * e
                        ke_s[at_:at_ + SUB, :] = ke
                        qe_s[at_:at_ + SUB, :] = qi * e
                        pa = (qi * ke).astype(BF16)
                        pd = jnp.where(m, do_s[lo:lo + SUB, :] * v_s[lo + s:lo + s + 1, :], 0.0).astype(BF16)
                        for h in range(PAIR):
                            hl = slice(h * HEAD, (h + 1) * HEAD)
                            p_s[(2 * h) * fam + at_:(2 * h) * fam + at_ + SUB, :] = pa[:, hl]
                            p_s[(2 * h + 1) * fam + at_:(2 * h + 1) * fam + at_ + SUB, :] = pd[:, hl]

            diag_products(range(0, nsub // 2))
            yield
            dqb, dkb = [], []
            for h in range(PAIR):
                hl = slice(h * HEAD, (h + 1) * HEAD)
                dv_s[:, hl] += _dot(at[h].astype(BF16), dob[:, hl])
                dqb.append(_dot(daf[h].astype(BF16), rhs[h]))
                dkb.append(_dot(daf[h].T.astype(BF16), lhs[h]))
            diag_products(range(nsub // 2, nsub))
            yield
            rs = _dot(p_s[...], ones)
            cr_s[...] = jnp.zeros_like(cr_s)
            for i in range(1, nsub):
                lo = i * SUB
                ref = b_s[lo - 1:lo, :]
                eq = jnp.exp(b_s[lo:lo + SUB, :] - ref)
                ek = jnp.exp(ref - b_s[0:lo, :])
                qtf = q_s[lo:lo + SUB, :] * eq
                ktf = k_s[0:lo, :] * ek
                cb = slice((i - 1) * HEAD, i * HEAD)
                for h in range(PAIR):
                    hl = slice(h * HEAD, (h + 1) * HEAD)
                    dqi = dqb[h][lo:lo + SUB, cb]
                    dki = dkb[h][0:lo, cb]
                    dq_s[lo:lo + SUB, hl] += dqi * eq[:, hl]
                    dk_s[0:lo, hl] += dki * ek[:, hl]
                    cr_s[lo:lo + SUB, hl] += (lhs[h, lo:lo + SUB, cb].astype(F32) - qtf[:, hl]) * dqi
                    cr_s[0:lo, hl] -= (rhs[h, 0:lo, cb].astype(F32) - ktf[:, hl]) * dki
            yield
            for i in range(nsub):
                lo = i * SUB
                for h in range(PAIR):
                    hl = slice(h * HEAD, (h + 1) * HEAD)
                    doi = do_s[lo:lo + SUB, hl]
                    dqa = dq_s[lo:lo + SUB, hl]
                    for s in range(SUB):
                        at = (i * SUB + s) * SUB
                        acol = rs[(2 * h) * fam + at:(2 * h) * fam + at + SUB, :]
                        dacol = rs[(2 * h + 1) * fam + at:(2 * h + 1) * fam + at + SUB, :]
                        dqa = dqa + dacol * ke_s[at:at + SUB, hl]
                        dk_s[lo + s:lo + s + 1, hl] += jnp.sum(dacol * qe_s[at:at + SUB, hl], axis=0, keepdims=True)
                        dv_s[lo + s:lo + s + 1, hl] += jnp.sum(acol * doi, axis=0, keepdims=True)
                    dq_s[lo:lo + SUB, hl] = dqa

            dq = dq_s[...]
            dk = dk_s[...]
            db = q * dq - k * dk + cr_s[...] + jnp.where(last_row, extra, 0.0)
            dlogf = _tri_dot(triu, db)
            df = jnp.where(f > GATE_EPS, dlogf / jnp.maximum(f, GATE_EPS), 0.0) - dk
            s_ref[1:2, :] += jnp.sum(df * (1.0 - sg), axis=0, keepdims=True)
            dp_ref[0, rows, :] = (dq * (sq * (1.0 + pq * (1.0 - sq)))).astype(BF16)
            dp_ref[1, rows, :] = (df * (1.0 - lb_) * sg * (1.0 - sg)).astype(BF16)
            dp_ref[2, rows, :] = dv_s[...].astype(BF16)
            for h in range(PAIR):
                dp_ref[3, rows, h * HEAD:(h + 1) * HEAD] = dpg[h].astype(BF16)
            yield

        def trip(g, carry):
            _staggered([chunk_stages(j, nck - 1 - (g * ntc + j)) for j in range(ntc)])
            return carry

        lax.fori_loop(0, nck // ntc, trip, 0)

    def grp(gi):
        return pl.BlockSpec((None, rb, W), lambda h, r: (gi, nb - 1 - r, h))

    rowsp = pl.BlockSpec((rb, W), lambda h, r: (nb - 1 - r, h))
    vec = pl.BlockSpec((1, W), lambda h, r: (0, h))
    return carried_call(
        body, comm, name=name, grid=(H // PAIR, nb),
        in_specs=[grp(0), grp(1), grp(2), grp(3), rowsp, rowsp,
                  pl.BlockSpec((PAIR, nck, HEAD, HEAD), lambda h, r: (h, nb - 1 - r, 0, 0)), vec, vec],
        out_specs=[pl.BlockSpec((4, rb, W), lambda h, r: (0, nb - 1 - r, h)),
                   pl.BlockSpec((8, W), lambda h, r: (0, h))],
        out_shape=[jax.ShapeDtypeStruct((4, T, D), BF16), jax.ShapeDtypeStruct((8, D), F32)],
        scratch_shapes=[pltpu.VMEM((W, W), F32), pltpu.VMEM((ntc, W, W), F32)] + [pltpu.VMEM((ntc, CHUNK, W), F32)] * 9
        + [pltpu.VMEM((ntc, PAIR, CHUNK, (nsub - 1) * HEAD), BF16)] * 2 + [pltpu.VMEM((ntc, fam, W), F32)] * 2
        + [pltpu.VMEM((ntc, 2 * PAIR * fam, HEAD), BF16)],
        args=(proj, proj, proj, proj, o, dog, states, lb, norm_g))


def lb_fwd(logits):
    def body(l_ref, o_ref):
        l = l_ref[...]
        mx = jnp.max(l, axis=0, keepdims=True)
        e = jnp.exp(l - mx)
        sm = e / jnp.sum(e, axis=0, keepdims=True)
        o_ref[0:1, :] = jnp.zeros_like(sm[0:1, :])
        o_ref[1:2, :] = sm[1:2, :]

    return pl.pallas_call(body, name="lb_fwd", out_shape=jax.ShapeDtypeStruct(logits.shape, F32))(logits)


def lb_bwd(logits, dlb):
    def body(l_ref, d_ref, o_ref):
        l = l_ref[...]
        mx = jnp.max(l, axis=0, keepdims=True)
        e = jnp.exp(l - mx)
        sm = e / jnp.sum(e, axis=0, keepdims=True)
        inner = d_ref[1:2, :] * sm[1:2, :]
        o_ref[0:1, :] = sm[0:1, :] * (0.0 - inner)
        o_ref[1:2, :] = sm[1:2, :] * (d_ref[1:2, :] - inner)

    return pl.pallas_call(body, name="lb_bwd", out_shape=jax.ShapeDtypeStruct(logits.shape, F32))(logits, dlb)


def _shifted_copies(sh, rows):
    for b in range(1, 8):
        sh[b, 0:rows, :] = sh[0, b:b + rows, :]


def conv_fwd(u, vec, *, tm, name, comm=None):
    _, T, D = u.shape
    hb = tm // HALO
    nlc = D // HEAD

    def body(a_ref, gt_ref, ap_ref, gp_ref, w_ref, c_ref, v_ref, sh):
        i = pl.program_id(0)
        sh[0, HALO:HALO + tm, :] = a_ref[...] * _sigmoid(gt_ref[...])
        prev = ap_ref[...] * _sigmoid(gp_ref[...])
        sh[0, 0:HALO, :] = jnp.where(i > 0, prev, 0.0)
        _shifted_copies(sh, tm + HALO - 8)

        def rowblock(r, carry):
            r0 = r * CONV_RB
            for cl in range(nlc):
                ls = slice(cl * HEAD, (cl + 1) * HEAD)
                acc = jnp.zeros((CONV_RB, HEAD), F32) + w_ref[32:33, ls]
                for j in range(CONV_W):
                    o = j + 2
                    at = pl.ds(pl.multiple_of(r0 + o - o % 8, 8), CONV_RB)
                    acc = acc + w_ref[j:j + 1, ls] * sh[o % 8, at, ls]
                c_ref[pl.ds(pl.multiple_of(r0, CONV_RB), CONV_RB), ls] = acc
            return carry

        lax.fori_loop(0, tm // CONV_RB, rowblock, 0)
        xhat, _ = _ln_stats(c_ref[...])
        y = xhat * w_ref[33:34, :] + w_ref[34:35, :]
        v_ref[...] = (y * _sigmoid(y)).astype(BF16)

    cur = lambda gi: pl.BlockSpec((None, tm, D), lambda i: (gi, i, 0))
    prv = lambda gi: pl.BlockSpec((None, HALO, D), lambda i: (gi, jnp.maximum(i * hb - 1, 0), 0))
    return carried_call(
        body, comm, name=name, grid=(T // tm,),
        in_specs=[cur(0), cur(1), prv(0), prv(1), pl.BlockSpec((VEC_ROWS, D), lambda i: (0, 0))],
        out_specs=[pl.BlockSpec((tm, D), lambda i: (i, 0)), pl.BlockSpec((tm, D), lambda i: (i, 0))],
        out_shape=[jax.ShapeDtypeStruct((T, D), F32), jax.ShapeDtypeStruct((T, D), BF16)],
        scratch_shapes=[pltpu.VMEM((8, tm + HALO, D), F32)],
        args=(u, u, u, u, vec))


def conv_bwd(dv2, c, u, vec, *, tm, name, comm=None):
    _, T, D = u.shape
    hb = tm // HALO
    nt = T // tm
    nh = T // HALO
    nlc = D // HEAD
    acc_rows = {j: j for j in range(CONV_W)}
    acc_rows.update({36: CONV_W, 37: CONV_W + 1})

    def body(dv_ref, c_ref, dvn_ref, cn_ref, a_ref, gt_ref, ap_ref, gp_ref, w_ref, du_ref, s_ref, gsh, dsh, part):
        i = pl.program_id(0)

        @pl.when(i == 0)
        def _():
            s_ref[...] = jnp.zeros_like(s_ref)
            part[...] = jnp.zeros_like(part)

        gam = w_ref[33:34, :]
        bet = w_ref[34:35, :]

        def dconv(dv, cc):
            xhat, rstd = _ln_stats(cc)
            y = xhat * gam + bet
            sy = _sigmoid(y)
            dy = dv * (sy * (1.0 + y * (1.0 - sy)))
            return _ln_bwd(dy, xhat, rstd, gam), dy, xhat

        dc, dy, xhat = dconv(dv_ref[...], c_ref[...])
        dcn, _, _ = dconv(dvn_ref[...], cn_ref[...])
        dsh[0, 0:tm, :] = dc
        dsh[0, tm:tm + HALO, :] = jnp.where(i < nt - 1, dcn, 0.0)
        gsh[0, HALO:HALO + tm, :] = a_ref[...] * _sigmoid(gt_ref[...])
        gsh[0, 0:HALO, :] = jnp.where(i > 0, ap_ref[...] * _sigmoid(gp_ref[...]), 0.0)
        s_ref[32:33, :] += jnp.sum(dc, axis=0, keepdims=True)
        s_ref[33:34, :] += jnp.sum(dy * xhat, axis=0, keepdims=True)
        s_ref[34:35, :] += jnp.sum(dy, axis=0, keepdims=True)
        _shifted_copies(dsh, tm + HALO - 8)
        _shifted_copies(gsh, tm + HALO - 8)

        def fold8(x):
            acc = x[0:8, :]
            for g in range(1, CONV_RB // 8):
                acc = acc + x[8 * g:8 * g + 8, :]
            return acc

        for cl in range(nlc):
            ls = slice(cl * HEAD, (cl + 1) * HEAD)

            def rowblock(r, sums, ls=ls):
                r0 = r * CONV_RB
                rows = pl.ds(pl.multiple_of(r0, CONV_RB), CONV_RB)
                dcb = dsh[0, rows, ls]
                dglu = jnp.zeros((CONV_RB, HEAD), F32)
                new = []
                for j in range(CONV_W):
                    od = 30 - j
                    og = j + 2
                    atd = pl.ds(pl.multiple_of(r0 + od - od % 8, 8), CONV_RB)
                    atg = pl.ds(pl.multiple_of(r0 + og - og % 8, 8), CONV_RB)
                    dglu = dglu + w_ref[j:j + 1, ls] * dsh[od % 8, atd, ls]
                    new.append(sums[j] + fold8(dcb * gsh[og % 8, atg, ls]))
                a = a_ref[rows, ls]
                sgt = _sigmoid(gt_ref[rows, ls])
                da = (dglu * sgt).astype(BF16)
                dg = (dglu * a * sgt * (1.0 - sgt)).astype(BF16)
                du_ref[0, rows, ls] = da
                du_ref[1, rows, ls] = dg
                new.append(sums[CONV_W] + fold8(da.astype(F32)))
                new.append(sums[CONV_W + 1] + fold8(dg.astype(F32)))
                return tuple(new)

            zero = jnp.zeros((8, HEAD), F32)
            sums = lax.fori_loop(0, tm // CONV_RB, rowblock, (zero,) * (CONV_W + 2))
            for k in range(CONV_W + 2):
                part[8 * k:8 * k + 8, ls] += sums[k]

        @pl.when(i == nt - 1)
        def _():
            for row, k in acc_rows.items():
                s_ref[row:row + 1, :] = jnp.sum(part[8 * k:8 * k + 8, :], axis=0, keepdims=True)

    row = lambda i: (i, 0)
    nxt = lambda i: (jnp.minimum((i + 1) * hb, nh - 1), 0)
    cur = lambda gi: pl.BlockSpec((None, tm, D), lambda i: (gi, i, 0))
    prv = lambda gi: pl.BlockSpec((None, HALO, D), lambda i: (gi, jnp.maximum(i * hb - 1, 0), 0))
    fix = lambda i: (0, 0)
    return carried_call(
        body, comm, name=name, grid=(nt,),
        in_specs=[pl.BlockSpec((tm, D), row), pl.BlockSpec((tm, D), row),
                  pl.BlockSpec((HALO, D), nxt), pl.BlockSpec((HALO, D), nxt),
                  cur(0), cur(1), prv(0), prv(1), pl.BlockSpec((VEC_ROWS, D), fix)],
        out_specs=[pl.BlockSpec((2, tm, D), lambda i: (0, i, 0)), pl.BlockSpec((VEC_ROWS, D), fix)],
        out_shape=[jax.ShapeDtypeStruct((2, T, D), BF16), jax.ShapeDtypeStruct((VEC_ROWS, D), F32)],
        scratch_shapes=[pltpu.VMEM((8, tm + HALO, D), F32), pltpu.VMEM((8, tm + HALO, D), F32),
                        pltpu.VMEM((8 * (CONV_W + 2), D), F32)],
        args=(dv2, c, dv2, c, u, u, u, u, vec))


def loss_grad(y, target, *, tm):
    T, D = y.shape
    nt = T // tm

    def body(y_ref, t_ref, l_ref, d_ref, acc):
        i = pl.program_id(0)

        @pl.when(i == 0)
        def _():
            acc[...] = jnp.zeros_like(acc)

        e = y_ref[...] - t_ref[...]
        d_ref[...] = e * (1.0 / D)
        acc[...] += jnp.sum(e * e, axis=0, keepdims=True)

        @pl.when(i == nt - 1)
        def _():
            l_ref[...] = 0.5 * jnp.sum(acc[...], axis=1, keepdims=True) * (1.0 / D)

    row = lambda i: (i, 0)
    return _call(
        body, name="loss_grad", grid=(nt,),
        in_specs=[pl.BlockSpec((tm, D), row), pl.BlockSpec((tm, D), row)],
        out_specs=[pl.BlockSpec((1, 1), lambda i: (0, 0)), pl.BlockSpec((tm, D), row)],
        out_shape=[jax.ShapeDtypeStruct((1, 1), F32), jax.ShapeDtypeStruct((T, D), F32)],
        scratch_shapes=[pltpu.VMEM((1, D), F32)], args=(y, target))


def _rows_block(R, C, budget=1 << 20):
    tr = R
    while tr * C * 4 > budget and tr % 32 == 0:
        tr //= 2
    return tr


def adamw(w, g, m, v, *, name):
    R, C = w.shape
    tr = _rows_block(R, C)

    def body(w_ref, g_ref, m_ref, v_ref, d_ref, mo_ref, vo_ref):
        g_ = g_ref[...]
        mn = ADAM_B1 * m_ref[...] + (1.0 - ADAM_B1) * g_
        vn = ADAM_B2 * v_ref[...] + (1.0 - ADAM_B2) * jnp.square(g_)
        m_hat = mn / (1.0 - ADAM_B1 ** ADAM_STEP)
        v_hat = vn / (1.0 - ADAM_B2 ** ADAM_STEP)
        d_ref[...] = -ADAM_LR * (m_hat / (jnp.sqrt(v_hat) + ADAM_EPS) + ADAM_WD * w_ref[...])
        mo_ref[...] = mn
        vo_ref[...] = vn

    spec = pl.BlockSpec((tr, C), lambda i: (i, 0))
    sd = jax.ShapeDtypeStruct((R, C), F32)
    return _call(body, name=name, grid=(R // tr,), in_specs=[spec] * 4, out_specs=[spec] * 3, out_shape=[sd] * 3,
                 args=(w, g, m, v))


def sum_slots(slots, *, name):
    _, R, C = slots.shape
    tr = _rows_block(R, C, budget=1 << 19)

    def body(s_ref, o_ref):
        acc = s_ref[0].astype(F32)
        for d in range(1, 8):
            acc = acc + s_ref[d].astype(F32)
        o_ref[...] = acc

    return _call(body, name=name, grid=(R // tr,), in_specs=[pl.BlockSpec((8, tr, C), lambda i: (0, i, 0))],
                 out_specs=[pl.BlockSpec((tr, C), lambda i: (i, 0))], out_shape=[jax.ShapeDtypeStruct((R, C), F32)],
                 args=(slots,))[0]


def _adam_nd(w, g, m, v, name):
    shp = w.shape
    c = shp[-1]
    f2 = lambda a: a.reshape(-1, c)
    d, mn, vn = adamw(f2(w), f2(g), f2(m), f2(v), name=name)
    return d.reshape(shp), mn.reshape(shp), vn.reshape(shp)


def _reduced(slots, name):
    out = []
    for s in slots:
        c = s.shape[-1]
        out.append(sum_slots(s.reshape(8, -1, c), name=name).reshape(s.shape[1:]))
    return out


def kernel(x, ln_mix_g, ln_mix_b, ln_ffn_g, ln_ffn_b, ffn_w1, ffn_w2, a_w_in, a_lb_logits, a_norm_g, a_w_out, b_w_pw1, b_b_pw1, b_w_dw, b_b_dw, b_ln_g, b_ln_b, b_w_pw2, b_b_pw2, loss_target, m_ln_mix_g, m_ln_mix_b, m_ln_ffn_g, m_ln_ffn_b, m_ffn_w1, m_ffn_w2, m_a_w_in, m_a_lb_logits, m_a_norm_g, m_a_w_out, m_b_w_pw1, m_b_b_pw1, m_b_w_dw, m_b_b_dw, m_b_ln_g, m_b_ln_b, m_b_w_pw2, m_b_b_pw2, v_ln_mix_g, v_ln_mix_b, v_ln_ffn_g, v_ln_ffn_b, v_ffn_w1, v_ffn_w2, v_a_w_in, v_a_lb_logits, v_a_norm_g, v_a_w_out, v_b_w_pw1, v_b_b_pw1, v_b_w_dw, v_b_b_dw, v_b_ln_g, v_b_ln_b, v_b_w_pw2, v_b_b_pw2):
    names = ["ln_mix_g", "ln_mix_b", "ln_ffn_g", "ln_ffn_b", "ffn_w1", "ffn_w2", "a_w_in", "a_lb_logits", "a_norm_g",
             "a_w_out", "b_w_pw1", "b_b_pw1", "b_w_dw", "b_b_dw", "b_ln_g", "b_ln_b", "b_w_pw2", "b_b_pw2"]
    w = dict(zip(names, [ln_mix_g, ln_mix_b, ln_ffn_g, ln_ffn_b, ffn_w1, ffn_w2, a_w_in, a_lb_logits, a_norm_g, a_w_out,
                         b_w_pw1, b_b_pw1, b_w_dw, b_b_dw, b_ln_g, b_ln_b, b_w_pw2, b_b_pw2]))
    m = dict(zip(names, [m_ln_mix_g, m_ln_mix_b, m_ln_ffn_g, m_ln_ffn_b, m_ffn_w1, m_ffn_w2, m_a_w_in, m_a_lb_logits,
                         m_a_norm_g, m_a_w_out, m_b_w_pw1, m_b_b_pw1, m_b_w_dw, m_b_b_dw, m_b_ln_g, m_b_ln_b, m_b_w_pw2,
                         m_b_b_pw2]))
    v = dict(zip(names, [v_ln_mix_g, v_ln_mix_b, v_ln_ffn_g, v_ln_ffn_b, v_ffn_w1, v_ffn_w2, v_a_w_in, v_a_lb_logits,
                         v_a_norm_g, v_a_w_out, v_b_w_pw1, v_b_b_pw1, v_b_w_dw, v_b_b_dw, v_b_ln_g, v_b_ln_b, v_b_w_pw2,
                         v_b_b_pw2]))
    T, D = x.shape[1], x.shape[2]
    DS = D // 4
    F = 4 * ffn_w1.shape[2]
    chip = 2 * lax.axis_index("x") + lax.axis_index("y")
    tm = min(T, 512)
    tmw = min(T, 1024)
    tmc = min(T, 256)
    rb = min(T, 512)
    tf = min(F // 4, 1024)
    xin, target = x[0], loss_target[0]

    def mix_shards(i):
        j = i // 2
        if i % 2 == 0:
            return [a_w_in[j].astype(BF16), a_w_out[j].astype(BF16)]
        vec = jnp.concatenate([b_w_dw[j], jnp.zeros((1, DS), F32), b_b_dw[j][None], b_ln_g[j][None], b_ln_b[j][None],
                               b_b_pw2[j][None], b_b_pw1[j].reshape(2, DS), jnp.zeros((2, DS), F32)], axis=0)
        return [b_w_pw1[j].astype(BF16), b_w_pw2[j].astype(BF16), vec]

    def ffn_shards(i):
        return [ffn_w1[i].astype(BF16), ffn_w2[i].astype(BF16)]

    def mix_weights(i, got):
        if i % 2 == 0:
            return {"w_in": got[0], "w_out": got[1].reshape(D, D)}
        pw1 = jnp.transpose(got[0].reshape(2, 2, D, D // 2), (0, 2, 1, 3)).reshape(2, D, D)
        vec = jnp.transpose(got[2], (1, 0, 2)).reshape(VEC_ROWS, D)
        return {"pw1": pw1, "pw2": got[1].reshape(D, D), "vec": vec,
                "b_pw1": got[2][:, 36:38, :].reshape(2, 1, D)}

    lb_all = lb_fwd(a_lb_logits)
    zeros_bias = jnp.zeros((1, D), F32)

    first = mix_shards(0)
    mixw = {"w_in": comm_call(GatherChips(first[:1]), name="gather_first")[0]}
    saved = []
    h = xin
    for i in range(DEPTH):
        j = i // 2
        s = {"x": h, "mixw": mixw}
        gf = GatherChips(ffn_shards(i) + (first[1:] if i == 0 else []))
        if i % 2 == 0:
            s["proj"] = mm_groups(h, mixw["w_in"], jnp.zeros((4, 1, D), F32), tm=tm, name="a_in_proj")
            (s["o"], s["og"], s["st"]), got = hgrn_fwd(s["proj"], lb_all[j:j + 1], a_norm_g[j:j + 1], rb=rb,
                                                       name="hgrn_fwd", comm=gf)
            if i == 0:
                mixw["w_out"] = got[2].reshape(D, D)
            s["r1"], s["x1"] = mm_res_ln(s["og"], mixw["w_out"], zeros_bias, h, ln_mix_g[i:i + 1], ln_mix_b[i:i + 1],
                                         tm=tm, name="a_out_ln")
        else:
            s["u"] = mm_groups(h, mixw["pw1"], mixw["b_pw1"], tm=tm, name="b_pw1")
            (s["c"], s["v2"]), got = conv_fwd(s["u"], mixw["vec"], tm=tmc, name="conv_fwd", comm=gf)
            s["r1"], s["x1"] = mm_res_ln(s["v2"], mixw["pw2"], mixw["vec"][35:36], h, ln_mix_g[i:i + 1],
                                         ln_mix_b[i:i + 1], tm=tm, name="b_pw2_ln")
        s["w1"], s["w2"] = got[0], got[1].reshape(F, D)
        gm = GatherChips(mix_shards(i + 1)) if i + 1 < DEPTH else None
        (s["z"], s["r2"], h), got = ffn_fwd(s["x1"], s["w1"], s["w2"], ln_ffn_g[i:i + 1], ln_ffn_b[i:i + 1],
                                            tm=tm, tf=tf, name="ffn_fwd", comm=gm)
        if gm is not None:
            mixw = mix_weights(i + 1, got)
        saved.append(s)

    loss_part, dh = loss_grad(h, target, tm=tm)
    loss = lax.psum(loss_part[0, 0], ("x", "y", "c"))

    gr = {k: [None] * DEPTH for k in ("ln_mix_g", "ln_mix_b", "ln_ffn_g", "ln_ffn_b", "ffn_w1", "ffn_w2")}
    for k in ("a_w_in", "a_w_out", "a_norm_g", "a_dlb", "b_w_pw1", "b_w_pw2", "b_vec", "b_b_pw2"):
        gr[k] = [None] * 2
    w_in_name = ("a_w_in", "b_w_pw1")
    w_out_name = ("a_w_out", "b_w_pw2")
    pending = None

    for i in reversed(range(DEPTH)):
        j = i // 2
        s = saved[i]
        mixw = s["mixw"]
        sm = ScatterPieces([pending[1]]) if pending is not None else None
        (dz, dx1, drb2, sums2), slots = ffn_bwd_dx(dh, s["r2"], ln_ffn_g[i:i + 1], s["z"], s["w1"], s["w2"],
                                                   tm=tm, tf=tf, name="ffn_bwd_dx", comm=sm)
        if pending is not None:
            gr[w_in_name[pending[0] % 2]][pending[0] // 2] = _reduced(slots, "sum_mix_grads")[0]
        gr["ln_ffn_g"][i], gr["ln_ffn_b"][i] = sums2[0], sums2[1]
        dw1 = mm_tn(s["x1"], dz[None], tm=tmw, tk=D, tn=F // 4, name="ffn_dw1")[0]
        dw2 = mm_tn(s["z"], drb2[None], tm=tmw, tk=F // 4, tn=D, relu2=True, name="ffn_dw2")[0, 0]
        wmix = mixw["w_out"] if i % 2 == 0 else mixw["pw2"]
        dr1, drb1, dmo, sums1 = ln_bwd_mm(dx1, s["r1"], ln_mix_g[i:i + 1], wmix, tm=tm, name="mix_ln_bwd")
        gr["ln_mix_g"][i], gr["ln_mix_b"][i] = sums1[0], sums1[1]
        if i % 2 == 0:
            dwo = mm_tn(s["og"], drb1[None], tm=tmw, tk=D, tn=D, name="a_dw_out")[0, 0].reshape(4, DS, D)
            sf = ScatterPieces([dw1, dw2.reshape(4, F // 4, D), dwo])
            (dproj, hs), slots = hgrn_bwd(s["proj"], s["o"], dmo, s["st"], lb_all[j:j + 1], a_norm_g[j:j + 1], rb=rb,
                                          name="hgrn_bwd", comm=sf)
            gr["a_norm_g"][j], gr["a_dlb"][j] = hs[0], hs[1]
            dwi = mm_tn(s["x"], dproj, tm=tmw, tk=D, tn=D, name="a_dw_in")[:, 0]
            dy_in, w_in_t, dx_name = dproj, mixw["w_in"], "a_dx"
        else:
            dwo = mm_tn(s["v2"], drb1[None], tm=tmw, tk=D, tn=D, name="b_dw_pw2")[0, 0].reshape(4, DS, D)
            sf = ScatterPieces([dw1, dw2.reshape(4, F // 4, D), dwo])
            (du, cs), slots = conv_bwd(dmo, s["c"], s["u"], mixw["vec"], tm=tmc, name="conv_bwd", comm=sf)
            gr["b_vec"][j], gr["b_b_pw2"][j] = cs, sums1[2]
            dwi = mm_tn(s["x"], du, tm=tmw, tk=D, tn=D // 2, name="b_dw_pw1").reshape(4, D, D // 2)
            dy_in, w_in_t, dx_name = du, mixw["pw1"], "b_dx"
        gr["ffn_w1"][i], gr["ffn_w2"][i], gr[w_out_name[i % 2]][j] = _reduced(slots, "sum_ffn_grads")
        dh, slots = mm_nt_acc(dy_in, w_in_t, dr1, tm=tm, name=dx_name, comm=ScatterPieces([dwi]) if i == 0 else None)
        if i == 0:
            gr[w_in_name[0]][0] = _reduced(slots, "sum_mix_grads")[0]
        pending = (i, dwi)
    grad_x = dh[None]

    small = {k: jnp.stack(gr[k]) for k in ("ln_mix_g", "ln_mix_b", "ln_ffn_g", "ln_ffn_b", "a_norm_g", "b_vec", "b_b_pw2")}
    small["a_lb_logits"] = lb_bwd(a_lb_logits, jnp.stack(gr["a_dlb"]))
    small_names = ["ln_mix_g", "ln_mix_b", "ln_ffn_g", "ln_ffn_b", "a_lb_logits", "a_norm_g", "b_b_pw2", "b_vec"]
    rows = [small[k].reshape(-1, D) for k in small_names]
    counts = [r.shape[0] for r in rows]
    rows = [jnp.pad(r, ((0, (-r.shape[0]) % 8), (0, 0))) for r in rows]
    summed = all_reduce_small(jnp.concatenate(rows, axis=0))
    sm = {}
    off = 0
    for k, n, r in zip(small_names, counts, rows):
        sm[k] = summed[off:off + n]
        off += r.shape[0]
    bvec = sm["b_vec"].reshape(2, VEC_ROWS, D)

    def shard_cols(a):
        return lax.dynamic_slice_in_dim(a, chip * DS, DS, axis=a.ndim - 1)

    grads = {k: jnp.stack(gr[k]) for k in ("ffn_w1", "ffn_w2", "a_w_in", "a_w_out", "b_w_pw1", "b_w_pw2")}
    for k in ("ln_mix_g", "ln_mix_b", "ln_ffn_g", "ln_ffn_b", "a_lb_logits", "a_norm_g"):
        grads[k] = sm[k]
    grads["b_b_pw1"] = lax.dynamic_slice_in_dim(bvec[:, 36:38, :].reshape(2, 2 * D), chip * (D // 2), D // 2, axis=1)
    grads["b_w_dw"] = shard_cols(bvec[:, 0:CONV_W, :])
    grads["b_b_dw"] = shard_cols(bvec[:, 32, :])
    grads["b_ln_g"] = shard_cols(bvec[:, 33, :])
    grads["b_ln_b"] = shard_cols(bvec[:, 34, :])
    grads["b_b_pw2"] = shard_cols(sm["b_b_pw2"])

    delta, new_m, new_v = {}, {}, {}
    for k in names:
        delta[k], new_m[k], new_v[k] = _adam_nd(w[k], grads[k], m[k], v[k], "adamw_" + k)
    return (loss, grad_x, *[grads[k] for k in names], *[delta[k] for k in names],
            *[new_m[k] for k in names], *[new_v[k] for k in names])
```

```python
import jax
import jax.numpy as jnp
from jax import lax
from jax.experimental import pallas as pl
from jax.experimental.pallas import tpu as pltpu

F32 = jnp.float32
BF16 = jnp.bfloat16
MESH = pl.DeviceIdType.MESH

DEPTH = 4
ALPHA = (2.0 * DEPTH) ** 0.25
LN_EPS = 1e-5
RMS_EPS = 1e-6
GATE_EPS = 1e-6
HEAD = 128
CHUNK = 128
SUB = 16
PAIR = 2
TRIP_CHUNKS = 2
CONV_W = 31
HALO = 32
VEC_ROWS = 40
CONV_RB = 32
ADAM_LR, ADAM_B1, ADAM_B2, ADAM_EPS, ADAM_WD, ADAM_STEP = 0.001, 0.9, 0.999, 1e-08, 0.01, 10
VMEM_LIMIT = 56 * 1024 * 1024
ANY = pl.BlockSpec(memory_space=pl.ANY)


def _dot(a, b):
    return jnp.dot(a, b, preferred_element_type=F32)


def _dot_nt(a, b):
    return lax.dot_general(a, b, (((1,), (1,)), ((), ())), preferred_element_type=F32)


def _dot_tn(a, b):
    return lax.dot_general(a, b, (((0,), (0,)), ((), ())), preferred_element_type=F32)


def _sigmoid(x):
    return 1.0 / (1.0 + jnp.exp(-x))


def _ln_stats(r):
    mu = jnp.mean(r, axis=-1, keepdims=True)
    xc = r - mu
    var = jnp.mean(xc * xc, axis=-1, keepdims=True)
    rstd = lax.rsqrt(var + LN_EPS)
    return xc * rstd, rstd


def _ln_bwd(dy, xhat, rstd, g):
    dyg = dy * g
    m1 = jnp.mean(dyg, axis=-1, keepdims=True)
    m2 = jnp.mean(dyg * xhat, axis=-1, keepdims=True)
    return rstd * (dyg - m1 - xhat * m2)


def _place():
    return lax.axis_index("x"), lax.axis_index("y"), lax.axis_index("c")


class GatherChips:
    def __init__(self, arrs):
        self.ins = list(arrs)
        n = len(arrs)
        self.out_shapes = [jax.ShapeDtypeStruct((4,) + a.shape, a.dtype) for a in arrs]
        self.sems = [pltpu.SemaphoreType.DMA((3 * n,)), pltpu.SemaphoreType.DMA((3 * n,)),
                     pltpu.SemaphoreType.DMA((n,))]

    def copies(self, ins, outs, send, recv, loc):
        x, y, c = _place()
        me = 2 * x + y
        local, remote = [], []
        for a in range(len(ins)):
            local.append(pltpu.make_async_copy(ins[a], outs[a].at[me], loc.at[a]))
            for j, (px, py) in enumerate([(1 - x, y), (x, 1 - y), (1 - x, 1 - y)]):
                remote.append(pltpu.make_async_remote_copy(
                    src_ref=ins[a], dst_ref=outs[a].at[me], send_sem=send.at[3 * a + j], recv_sem=recv.at[3 * a + j],
                    device_id=(px, py, c), device_id_type=MESH))
        return local + remote


class ScatterPieces:
    def __init__(self, arrs):
        self.ins = list(arrs)
        n = len(arrs)
        self.out_shapes = [jax.ShapeDtypeStruct((8,) + a.shape[1:], a.dtype) for a in arrs]
        self.sems = [pltpu.SemaphoreType.DMA((7 * n,)), pltpu.SemaphoreType.DMA((7 * n,)),
                     pltpu.SemaphoreType.DMA((n,))]

    def copies(self, ins, outs, send, recv, loc):
        x, y, c = _place()
        me = 4 * x + 2 * y + c
        local, remote = [], []
        for a in range(len(ins)):
            local.append(pltpu.make_async_copy(ins[a].at[2 * x + y], outs[a].at[me], loc.at[a]))
            k = 0
            for fx in (0, 1):
                for fy in (0, 1):
                    for fc in (0, 1):
                        if fx or fy or fc:
                            tx, ty = x ^ fx, y ^ fy
                            remote.append(pltpu.make_async_remote_copy(
                                src_ref=ins[a].at[2 * tx + ty], dst_ref=outs[a].at[me],
                                send_sem=send.at[7 * a + k], recv_sem=recv.at[7 * a + k],
                                device_id=(tx, ty, c ^ fc), device_id_type=MESH))
                            k += 1
        return local + remote


def carried_call(body, comm, *, name, grid, in_specs, out_specs, out_shape, scratch_shapes, args):
    sem = ("arbitrary",) * len(grid)
    params = pltpu.CompilerParams(dimension_semantics=sem, vmem_limit_bytes=VMEM_LIMIT)
    if comm is None:
        res = pl.pallas_call(body, name=name, grid=grid, in_specs=in_specs, out_specs=out_specs, out_shape=out_shape,
                             scratch_shapes=scratch_shapes, compiler_params=params)(*args)
        return res, []
    ni, no, nscr = len(in_specs), len(out_specs), len(scratch_shapes)
    ci, co = len(comm.ins), len(comm.out_shapes)

    def both(*refs):
        ins, refs = refs[:ni], refs[ni:]
        cins, refs = refs[:ci], refs[ci:]
        outs, refs = refs[:no], refs[no:]
        couts, refs = refs[:co], refs[co:]
        scr, sems = refs[:nscr], refs[nscr:]
        first = pl.program_id(0) == 0
        last = pl.program_id(0) == grid[0] - 1
        for d in range(1, len(grid)):
            first = first & (pl.program_id(d) == 0)
            last = last & (pl.program_id(d) == grid[d] - 1)

        @pl.when(first)
        def _():
            for cp in comm.copies(cins, couts, *sems):
                cp.start()

        body(*ins, *outs, *scr)

        @pl.when(last)
        def _():
            for cp in comm.copies(cins, couts, *sems):
                cp.wait()

    res = pl.pallas_call(
        both, name=name, grid=grid, in_specs=list(in_specs) + [ANY] * ci, out_specs=list(out_specs) + [ANY] * co,
        out_shape=list(out_shape) + comm.out_shapes, scratch_shapes=list(scratch_shapes) + comm.sems,
        compiler_params=params)(*args, *comm.ins)
    return res[:no], res[no:]


def comm_call(comm, *, name):
    ci, co = len(comm.ins), len(comm.out_shapes)

    def body(*refs):
        cps = comm.copies(refs[:ci], refs[ci:ci + co], *refs[ci + co:])
        for cp in cps:
            cp.start()
        for cp in cps:
            cp.wait()

    return pl.pallas_call(body, name=name, in_specs=[ANY] * ci, out_specs=[ANY] * co, out_shape=comm.out_shapes,
                          scratch_shapes=comm.sems, compiler_params=pltpu.CompilerParams(has_side_effects=True))(*comm.ins)


def all_reduce_small(v):
    R, C = v.shape

    def body(v_ref, o_ref, slots, send, recv):
        x, y, c = _place()
        me = 4 * x + 2 * y + c
        slots[me] = v_ref[...]
        cps = []
        k = 0
        for fx in (0, 1):
            for fy in (0, 1):
                for fc in (0, 1):
                    if fx or fy or fc:
                        cps.append(pltpu.make_async_remote_copy(
                            src_ref=v_ref, dst_ref=slots.at[me], send_sem=send.at[k], recv_sem=recv.at[k],
                            device_id=(x ^ fx, y ^ fy, c ^ fc), device_id_type=MESH))
                        k += 1
        for cp in cps:
            cp.start()
        for cp in cps:
            cp.wait()
        acc = slots[0]
        for d in range(1, 8):
            acc = acc + slots[d]
        o_ref[...] = acc

    vm = pl.BlockSpec(memory_space=pltpu.VMEM)
    return pl.pallas_call(
        body, name="all_reduce_small", in_specs=[vm], out_specs=vm,
        out_shape=jax.ShapeDtypeStruct((R, C), F32),
        scratch_shapes=[pltpu.VMEM((8, R, C), F32), pltpu.SemaphoreType.DMA((7,)), pltpu.SemaphoreType.DMA((7,))],
        compiler_params=pltpu.CompilerParams(has_side_effects=True, vmem_limit_bytes=VMEM_LIMIT),
    )(v)


def _call(body, *, name, grid, in_specs, out_specs, out_shape, scratch_shapes=(), args):
    res, _ = carried_call(body, None, name=name, grid=grid, in_specs=in_specs, out_specs=out_specs,
                          out_shape=out_shape, scratch_shapes=list(scratch_shapes), args=args)
    return res


def mm_groups(a, w, bias, *, tm, name):
    T, K = a.shape
    G, _, N = w.shape

    def body(a_ref, w_ref, b_ref, o_ref):
        o_ref[...] = _dot(a_ref[...], w_ref[...]) + b_ref[...]

    return _call(
        body, name=name, grid=(G, T // tm),
        in_specs=[pl.BlockSpec((tm, K), lambda g, i: (i, 0)),
                  pl.BlockSpec((None, K, N), lambda g, i: (g, 0, 0)),
                  pl.BlockSpec((None, 1, N), lambda g, i: (g, 0, 0))],
        out_specs=[pl.BlockSpec((None, tm, N), lambda g, i: (g, i, 0))],
        out_shape=[jax.ShapeDtypeStruct((G, T, N), F32)], args=(a, w, bias))[0]


def mm_res_ln(a, w, bias, res, g, b, *, tm, name):
    T, K = a.shape
    N = w.shape[1]

    def body(a_ref, w_ref, bias_ref, res_ref, g_ref, b_ref, r_ref, y_ref, yb_ref):
        r = ALPHA * res_ref[...] + _dot(a_ref[...], w_ref[...]) + bias_ref[...]
        r_ref[...] = r
        xhat, _ = _ln_stats(r)
        y = xhat * g_ref[...] + b_ref[...]
        y_ref[...] = y
        yb_ref[...] = y.astype(BF16)

    row = lambda i: (i, 0)
    fix = lambda i: (0, 0)
    return _call(
        body, name=name, grid=(T // tm,),
        in_specs=[pl.BlockSpec((tm, K), row), pl.BlockSpec((K, N), fix), pl.BlockSpec((1, N), fix),
                  pl.BlockSpec((tm, N), row), pl.BlockSpec((1, N), fix), pl.BlockSpec((1, N), fix)],
        out_specs=[pl.BlockSpec((tm, N), row), pl.BlockSpec((tm, N), row), pl.BlockSpec((tm, N), row)],
        out_shape=[jax.ShapeDtypeStruct((T, N), F32), jax.ShapeDtypeStruct((T, N), F32),
                   jax.ShapeDtypeStruct((T, N), BF16)],
        args=(a, w, bias, res, g, b))


def ffn_fwd(x, w1, w2, g, b, *, tm, tf, name, comm=None):
    T, D = x.shape
    NC, _, FC = w1.shape
    F = NC * FC
    per = FC // tf
    nf = F // tf

    def body(x_ref, w1_ref, w2_ref, g_ref, b_ref, z_ref, r_ref, y_ref, yb_ref, acc_ref, xb_ref):
        f = pl.program_id(1)

        @pl.when(f == 0)
        def _():
            acc_ref[...] = jnp.zeros_like(acc_ref)
            xb_ref[...] = x_ref[...].astype(BF16)

        z = _dot(xb_ref[...], w1_ref[...])
        z_ref[...] = z.astype(BF16)
        h = jnp.square(jnp.maximum(z, 0.0)).astype(BF16)
        acc_ref[...] += _dot(h, w2_ref[...])

        @pl.when(f == nf - 1)
        def _():
            r = ALPHA * x_ref[...] + acc_ref[...]
            r_ref[...] = r
            xhat, _ = _ln_stats(r)
            y = xhat * g_ref[...] + b_ref[...]
            y_ref[...] = y
            yb_ref[...] = y.astype(BF16)

    return carried_call(
        body, comm, name=name, grid=(T // tm, nf),
        in_specs=[pl.BlockSpec((tm, D), lambda i, f: (i, 0)),
                  pl.BlockSpec((None, D, tf), lambda i, f: (f // per, 0, f % per)),
                  pl.BlockSpec((tf, D), lambda i, f: (f, 0)),
                  pl.BlockSpec((1, D), lambda i, f: (0, 0)),
                  pl.BlockSpec((1, D), lambda i, f: (0, 0))],
        out_specs=[pl.BlockSpec((tm, tf), lambda i, f: (i, f)),
                   pl.BlockSpec((tm, D), lambda i, f: (i, 0)),
                   pl.BlockSpec((tm, D), lambda i, f: (i, 0)),
                   pl.BlockSpec((tm, D), lambda i, f: (i, 0))],
        out_shape=[jax.ShapeDtypeStruct((T, F), BF16), jax.ShapeDtypeStruct((T, D), F32),
                   jax.ShapeDtypeStruct((T, D), F32), jax.ShapeDtypeStruct((T, D), BF16)],
        scratch_shapes=[pltpu.VMEM((tm, D), F32), pltpu.VMEM((tm, D), BF16)],
        args=(x, w1, w2, g, b))


def ln_bwd_mm(dy, r, g, w, *, tm, name):
    T, N = dy.shape
    Ko = w.shape[0]

    def body(dy_ref, r_ref, g_ref, w_ref, dr_ref, drb_ref, o_ref, s_ref):
        @pl.when(pl.program_id(0) == 0)
        def _():
            s_ref[...] = jnp.zeros_like(s_ref)

        dy_ = dy_ref[...]
        xhat, rstd = _ln_stats(r_ref[...])
        dr = _ln_bwd(dy_, xhat, rstd, g_ref[...])
        dr_ref[...] = dr
        drb = dr.astype(BF16)
        drb_ref[...] = drb
        o_ref[...] = _dot_nt(drb, w_ref[...])
        s_ref[0:1, :] += jnp.sum(dy_ * xhat, axis=0, keepdims=True)
        s_ref[1:2, :] += jnp.sum(dy_, axis=0, keepdims=True)
        s_ref[2:3, :] += jnp.sum(dr, axis=0, keepdims=True)

    row = lambda i: (i, 0)
    fix = lambda i: (0, 0)
    return _call(
        body, name=name, grid=(T // tm,),
        in_specs=[pl.BlockSpec((tm, N), row), pl.BlockSpec((tm, N), row), pl.BlockSpec((1, N), fix),
                  pl.BlockSpec((Ko, N), fix)],
        out_specs=[pl.BlockSpec((tm, N), row), pl.BlockSpec((tm, N), row), pl.BlockSpec((tm, Ko), row),
                   pl.BlockSpec((8, N), fix)],
        out_shape=[jax.ShapeDtypeStruct((T, N), F32), jax.ShapeDtypeStruct((T, N), BF16),
                   jax.ShapeDtypeStruct((T, Ko), F32), jax.ShapeDtypeStruct((8, N), F32)],
        args=(dy, r, g, w))


def ffn_bwd_dx(dy, r, g, z, w1, w2, *, tm, tf, name, comm=None):
    T, D = dy.shape
    NC, _, FC = w1.shape
    F = NC * FC
    per = FC // tf
    nf = F // tf

    def body(dy_ref, r_ref, g_ref, z_ref, w1_ref, w2_ref, dz_ref, dx_ref, drb_ref, s_ref, dr_scr, acc_ref):
        i = pl.program_id(0)
        f = pl.program_id(1)

        @pl.when((i == 0) & (f == 0))
        def _():
            s_ref[...] = jnp.zeros_like(s_ref)

        @pl.when(f == 0)
        def _():
            dy_ = dy_ref[...]
            xhat, rstd = _ln_stats(r_ref[...])
            dr = _ln_bwd(dy_, xhat, rstd, g_ref[...])
            dr_scr[...] = dr
            drb_ref[...] = dr.astype(BF16)
            acc_ref[...] = jnp.zeros_like(acc_ref)
            s_ref[0:1, :] += jnp.sum(dy_ * xhat, axis=0, keepdims=True)
            s_ref[1:2, :] += jnp.sum(dy_, axis=0, keepdims=True)

        dh = _dot_nt(drb_ref[...], w2_ref[...])
        dz = (dh * (2.0 * jnp.maximum(z_ref[...].astype(F32), 0.0))).astype(BF16)
        dz_ref[...] = dz
        acc_ref[...] += _dot_nt(dz, w1_ref[...])

        @pl.when(f == nf - 1)
        def _():
            dx_ref[...] = ALPHA * dr_scr[...] + acc_ref[...]

    return carried_call(
        body, comm, name=name, grid=(T // tm, nf),
        in_specs=[pl.BlockSpec((tm, D), lambda i, f: (i, 0)),
                  pl.BlockSpec((tm, D), lambda i, f: (i, 0)),
                  pl.BlockSpec((1, D), lambda i, f: (0, 0)),
                  pl.BlockSpec((tm, tf), lambda i, f: (i, f)),
                  pl.BlockSpec((None, D, tf), lambda i, f: (f // per, 0, f % per)),
                  pl.BlockSpec((tf, D), lambda i, f: (f, 0))],
        out_specs=[pl.BlockSpec((tm, tf), lambda i, f: (i, f)),
                   pl.BlockSpec((tm, D), lambda i, f: (i, 0)),
                   pl.BlockSpec((tm, D), lambda i, f: (i, 0)),
                   pl.BlockSpec((8, D), lambda i, f: (0, 0))],
        out_shape=[jax.ShapeDtypeStruct((T, F), BF16), jax.ShapeDtypeStruct((T, D), F32),
                   jax.ShapeDtypeStruct((T, D), BF16), jax.ShapeDtypeStruct((8, D), F32)],
        scratch_shapes=[pltpu.VMEM((tm, D), F32), pltpu.VMEM((tm, D), F32)],
        args=(dy, r, g, z, w1, w2))


def mm_tn(a, b, *, tm, tk, tn, relu2=False, name):
    T, K = a.shape
    G, _, N = b.shape
    nt = T // tm

    def body(a_ref, b_ref, o_ref, acc_ref):
        t = pl.program_id(3)

        @pl.when(t == 0)
        def _():
            acc_ref[...] = jnp.zeros_like(acc_ref)

        av = a_ref[...]
        if relu2:
            av = jnp.square(jnp.maximum(av.astype(F32), 0.0))
        acc_ref[...] += _dot_tn(av.astype(BF16), b_ref[...])

        @pl.when(t == nt - 1)
        def _():
            o_ref[...] = acc_ref[...].astype(BF16)

    return _call(
        body, name=name, grid=(G, K // tk, N // tn, nt),
        in_specs=[pl.BlockSpec((tm, tk), lambda g, k, n, t: (t, k)),
                  pl.BlockSpec((None, tm, tn), lambda g, k, n, t: (g, t, n))],
        out_specs=[pl.BlockSpec((None, None, tk, tn), lambda g, k, n, t: (g, n, k, 0))],
        out_shape=[jax.ShapeDtypeStruct((G, N // tn, K, tn), BF16)],
        scratch_shapes=[pltpu.VMEM((tk, tn), F32)], args=(a, b))[0]


def mm_nt_acc(dy, w, base, *, tm, name, comm=None):
    G, T, N = dy.shape
    K = w.shape[1]

    def body(dy_ref, w_ref, base_ref, o_ref):
        g = pl.program_id(1)

        @pl.when(g == 0)
        def _():
            o_ref[...] = ALPHA * base_ref[...]

        o_ref[...] += _dot_nt(dy_ref[...], w_ref[...])

    res, got = carried_call(
        body, comm, name=name, grid=(T // tm, G),
        in_specs=[pl.BlockSpec((None, tm, N), lambda i, g: (g, i, 0)),
                  pl.BlockSpec((None, K, N), lambda i, g: (g, 0, 0)),
                  pl.BlockSpec((tm, K), lambda i, g: (i, 0))],
        out_specs=[pl.BlockSpec((tm, K), lambda i, g: (i, 0))],
        out_shape=[jax.ShapeDtypeStruct((T, K), F32)], scratch_shapes=[], args=(dy, w, base))
    return res[0], got


def _split3(x):
    x1 = x.astype(BF16)
    r1 = x - x1.astype(F32)
    x2 = r1.astype(BF16)
    x3 = (r1 - x2.astype(F32)).astype(BF16)
    return x1, x2, x3


def _tri_dot(tri, x):
    x1, x2, x3 = _split3(x)
    return _dot(tri, x1) + _dot(tri, x2) + _dot(tri, x3)


def _gates(pq, fz, lb):
    sg = _sigmoid(fz)
    f = lb + (1.0 - lb) * sg
    logf = jnp.log(jnp.maximum(f, GATE_EPS))
    sq = _sigmoid(pq)
    return pq * sq, 1.0 - f, logf, f, sg, sq


def _staggered(gens):
    live = []
    waiting = list(gens)
    for gen in waiting:
        next(gen)
    while live or waiting:
        if waiting:
            live.append(waiting.pop(0))
        nxt = []
        for gen in live:
            try:
                next(gen)
                nxt.append(gen)
            except StopIteration:
                pass
        live = nxt


def _butterfly(ys, combine):
    span = 4
    while len(ys) > 1:
        ys = [combine(u, v, span) for u, v in zip(ys[0::2], ys[1::2])]
        span //= 2
    return ys[0]


def _rows_of_sums(xs):
    lands = _butterfly([[j] * 8 for j in range(8)],
                       lambda u, v, span: [u[r] if (r // span) % 2 else v[r] for r in range(8)])
    src = [None] * 8
    for r in range(8):
        src[lands[r]] = xs[r]
    row = lax.broadcasted_iota(jnp.int32, xs[0].shape, 0)

    def combine(u, v, span):
        return jnp.where((row // span) % 2 == 1, u + pltpu.roll(u, span, 0), v + pltpu.roll(v, 8 - span, 0))

    return _butterfly(src, combine)


def _block_diag_mask():
    ri = lax.broadcasted_iota(jnp.int32, (PAIR * HEAD, PAIR * HEAD), 0) // HEAD
    ci = lax.broadcasted_iota(jnp.int32, (PAIR * HEAD, PAIR * HEAD), 1) // HEAD
    return ri == ci


def _fill_off_diagonal(q_s, k_s, b_s, lhs, rhs):
    for i in range(1, CHUNK // SUB):
        lo = i * SUB
        ref = b_s[lo - 1:lo, :]
        qt = (q_s[lo:lo + SUB, :] * jnp.exp(b_s[lo:lo + SUB, :] - ref)).astype(BF16)
        kt = (k_s[0:lo, :] * jnp.exp(ref - b_s[0:lo, :])).astype(BF16)
        for h in range(PAIR):
            hl = slice(h * HEAD, (h + 1) * HEAD)
            lhs[h, lo:lo + SUB, (i - 1) * HEAD:i * HEAD] = qt[:, hl]
            rhs[h, 0:lo, (i - 1) * HEAD:i * HEAD] = kt[:, hl]


def hgrn_fwd(proj, lb, norm_g, *, rb, name, comm=None):
    _, T, D = proj.shape
    H = D // HEAD
    nb = T // rb
    nck = rb // CHUNK
    nsub = CHUNK // SUB
    W = PAIR * HEAD
    fam = CHUNK * SUB
    ntc = min(2 * TRIP_CHUNKS, nck)

    def body(pq_ref, fz_ref, pv_ref, pg_ref, lb_ref, ng_ref, o_ref, og_ref, st_ref, S, q_a, k_a, v_a, b_a, lhs_a, rhs_a,
             p_a):
        @pl.when(pl.program_id(1) == 0)
        def _():
            S[...] = jnp.zeros_like(S)

        @pl.when((pl.program_id(0) == 0) & (pl.program_id(1) == 0))
        def _():
            lhs_a[...] = jnp.zeros_like(lhs_a)
            rhs_a[...] = jnp.zeros_like(rhs_a)

        ri = lax.broadcasted_iota(jnp.int32, (CHUNK, CHUNK), 0)
        ci = lax.broadcasted_iota(jnp.int32, (CHUNK, CHUNK), 1)
        tri = (ci <= ri).astype(BF16)
        ones = jnp.ones((HEAD, HEAD), BF16)
        bd = _block_diag_mask()

        def chunk_stages(j, c):
            q_s, k_s, v_s, b_s, lhs, rhs, p_s = (r.at[j] for r in (q_a, k_a, v_a, b_a, lhs_a, rhs_a, p_a))
            base = c * CHUNK
            rows = pl.ds(pl.multiple_of(base, CHUNK), CHUNK)
            q, k, logf, _, _, _ = _gates(pq_ref[rows, :], fz_ref[rows, :], lb_ref[...])
            v = pv_ref[rows, :]
            b = _tri_dot(tri, logf)
            yield
            q_s[...] = q
            k_s[...] = k
            v_s[...] = v
            b_s[...] = b
            bl = b_s[CHUNK - 1:CHUNK, :]
            upd = _dot_tn(v.astype(BF16), (k * jnp.exp(bl - b)).astype(BF16))
            Sv = S[...]
            for h in range(PAIR):
                st_ref[h, c] = Sv[h * HEAD:(h + 1) * HEAD, h * HEAD:(h + 1) * HEAD]
            o_int = _dot_nt((q * jnp.exp(b)).astype(BF16), Sv.astype(BF16))
            S[...] = Sv * jnp.exp(bl) + jnp.where(bd, upd, 0.0)
            _fill_off_diagonal(q_s, k_s, b_s, lhs, rhs)
            a = [_dot_nt(lhs[h], rhs[h]) for h in range(PAIR)]
            yield

            def diag_products(blocks):
                for i in blocks:
                    lo = i * SUB
                    for s in range(SUB):
                        m = lax.broadcasted_iota(jnp.int32, (SUB, W), 0) >= s
                        at = (i * SUB + s) * SUB
                        e = jnp.exp(b_s[lo:lo + SUB, :] - b_s[lo + s:lo + s + 1, :])
                        p = jnp.where(m, q_s[lo:lo + SUB, :] * (k_s[lo + s:lo + s + 1, :] * e), 0.0).astype(BF16)
                        for h in range(PAIR):
                            p_s[h * fam + at:h * fam + at + SUB, :] = p[:, h * HEAD:(h + 1) * HEAD]

            diag_products(range(0, nsub // 2))
            yield
            off = [_dot(a[h].astype(BF16), v_s[:, h * HEAD:(h + 1) * HEAD].astype(BF16)) for h in range(PAIR)]
            diag_products(range(nsub // 2, nsub))
            yield
            rs = _dot(p_s[...], ones)
            yield
            for i in range(nsub):
                lo = i * SUB
                blk = pl.ds(pl.multiple_of(base + lo, SUB), SUB)
                for h in range(PAIR):
                    hl = slice(h * HEAD, (h + 1) * HEAD)
                    acc = o_int[lo:lo + SUB, hl] + off[h][lo:lo + SUB, :]
                    for s in range(SUB):
                        at = h * fam + (i * SUB + s) * SUB
                        acc = acc + rs[at:at + SUB, :] * v_s[lo + s:lo + s + 1, hl]
                    o_ref[blk, hl] = acc
                    rinv = lax.rsqrt(jnp.mean(acc * acc, axis=-1, keepdims=True) + RMS_EPS)
                    pg = pg_ref[blk, hl]
                    og_ref[blk, hl] = (acc * rinv * ng_ref[:, hl] * (pg * _sigmoid(pg))).astype(BF16)
            yield

        def trip(g, carry):
            _staggered([chunk_stages(j, g * ntc + j) for j in range(ntc)])
            return carry

        lax.fori_loop(0, nck // ntc, trip, 0)

    def grp(gi):
        return pl.BlockSpec((None, rb, W), lambda h, r: (gi, r, h))

    vec = pl.BlockSpec((1, W), lambda h, r: (0, h))
    return carried_call(
        body, comm, name=name, grid=(H // PAIR, nb),
        in_specs=[grp(0), grp(1), grp(2), grp(3), vec, vec],
        out_specs=[pl.BlockSpec((rb, W), lambda h, r: (r, h)),
                   pl.BlockSpec((rb, W), lambda h, r: (r, h)),
                   pl.BlockSpec((PAIR, nck, HEAD, HEAD), lambda h, r: (h, r, 0, 0))],
        out_shape=[jax.ShapeDtypeStruct((T, D), F32), jax.ShapeDtypeStruct((T, D), BF16),
                   jax.ShapeDtypeStruct((H, T // CHUNK, HEAD, HEAD), F32)],
        scratch_shapes=[pltpu.VMEM((W, W), F32)] + [pltpu.VMEM((ntc, CHUNK, W), F32)] * 4
        + [pltpu.VMEM((ntc, PAIR, CHUNK, (nsub - 1) * HEAD), BF16)] * 2 + [pltpu.VMEM((ntc, PAIR * fam, HEAD), BF16)],
        args=(proj, proj, proj, proj, lb, norm_g))


def hgrn_bwd(proj, o, dog, states, lb, norm_g, *, rb, name, comm=None):
    _, T, D = proj.shape
    H = D // HEAD
    nb = T // rb
    nck = rb // CHUNK
    nsub = CHUNK // SUB
    W = PAIR * HEAD
    fam = CHUNK * SUB
    ntc = min(TRIP_CHUNKS, nck)

    def body(pq_ref, fz_ref, pv_ref, pg_ref, o_ref, dog_ref, st_ref, lb_ref, ng_ref, dp_ref, s_ref,
             dS, *per_chunk):
        S0_a, lhs_a, rhs_a = per_chunk[0], per_chunk[10], per_chunk[11]

        @pl.when(pl.program_id(1) == 0)
        def _():
            dS[...] = jnp.zeros_like(dS)
            s_ref[...] = jnp.zeros_like(s_ref)

        @pl.when((pl.program_id(0) == 0) & (pl.program_id(1) == 0))
        def _():
            lhs_a[...] = jnp.zeros_like(lhs_a)
            rhs_a[...] = jnp.zeros_like(rhs_a)
            S0_a[...] = jnp.zeros_like(S0_a)

        ri = lax.broadcasted_iota(jnp.int32, (CHUNK, CHUNK), 0)
        ci = lax.broadcasted_iota(jnp.int32, (CHUNK, CHUNK), 1)
        tri = (ci <= ri).astype(BF16)
        triu = (ci >= ri).astype(BF16)
        below = (ri // SUB) > (ci // SUB)
        ones = jnp.ones((HEAD, HEAD), BF16)
        bd = _block_diag_mask()
        last_row = lax.broadcasted_iota(jnp.int32, (CHUNK, W), 0) == CHUNK - 1

        def chunk_stages(j, c):
            (S0, q_s, k_s, v_s, b_s, do_s, dq_s, dk_s, dv_s, cr_s, lhs, rhs, ke_s, qe_s, p_s) = (r.at[j] for r in per_chunk)
            rows = pl.ds(pl.multiple_of(c * CHUNK, CHUNK), CHUNK)
            lb_ = lb_ref[...]
            pq = pq_ref[rows, :]
            q, k, logf, f, sg, sq = _gates(pq, fz_ref[rows, :], lb_)
            v = pv_ref[rows, :]
            b = _tri_dot(tri, logf)
            dpg = []
            for h in range(PAIR):
                hl = slice(h * HEAD, (h + 1) * HEAD)
                oh = o_ref[rows, hl]
                dog_ = dog_ref[rows, hl]
                pg = pg_ref[rows, hl]
                ng = ng_ref[:, hl]
                spg = _sigmoid(pg)
                rinv = lax.rsqrt(jnp.mean(oh * oh, axis=-1, keepdims=True) + RMS_EPS)
                on = oh * rinv
                dpg.append(dog_ * (on * ng) * (spg * (1.0 + pg * (1.0 - spg))))
                don = dog_ * (pg * spg)
                s_ref[0:1, hl] += jnp.sum(don * on, axis=0, keepdims=True)
                dxn = don * ng
                do_s[:, hl] = rinv * (dxn - on * jnp.mean(dxn * on, axis=-1, keepdims=True))
                S0[hl, hl] = st_ref[h, c]
            yield
            q_s[...] = q
            k_s[...] = k
            v_s[...] = v
            b_s[...] = b
            do = do_s[...]
            dob = do.astype(BF16)
            vb = v.astype(BF16)
            eb = jnp.exp(b)
            bl = b_s[CHUNK - 1:CHUNK, :]
            ebl = jnp.exp(bl)
            ekk = jnp.exp(bl - b)
            upd = _dot_tn(dob, (q * eb).astype(BF16))
            S0v = S0[...]
            dSv = dS[...]
            dSb = dSv.astype(BF16)
            dq_s[...] = _dot(dob, S0v.astype(BF16)) * eb
            dk_state = _dot(vb, dSb) * ekk
            dk_s[...] = dk_state
            dv_s[...] = _dot_nt((k * ekk).astype(BF16), dSb)
            extra = jnp.sum(k * dk_state, axis=0, keepdims=True) + ebl * jnp.sum(S0v * dSv, axis=0, keepdims=True)
            dS[...] = dSv * ebl + jnp.where(bd, upd, 0.0)

            _fill_off_diagonal(q_s, k_s, b_s, lhs, rhs)
            at = [_dot_nt(rhs[h], lhs[h]) for h in range(PAIR)]
            daf = [jnp.where(below, _dot_nt(dob[:, h * HEAD:(h + 1) * HEAD], vb[:, h * HEAD:(h + 1) * HEAD]), 0.0)
                   for h in range(PAIR)]
            yield

            def diag_products(blocks):
                for i in blocks:
                    lo = i * SUB
                    for s in range(SUB):
                        m = lax.broadcasted_iota(jnp.int32, (SUB, W), 0) >= s
                        at_ = (i * SUB + s) * SUB
                        qi = q_s[lo:lo + SUB, :]
                        e = jnp.where(m, jnp.exp(b_s[lo:lo + SUB, :] - b_s[lo + s:lo + s + 1, :]), 0.0)
                        ke = k_s[lo + s:lo + s + 1, :] * e
                        ke_s[at_:at_ + SUB, :] = ke
                        qe_s[at_:at_ + SUB, :] = qi * e
                        pa = (qi * ke).astype(BF16)
                        pd = jnp.where(m, do_s[lo:lo + SUB, :] * v_s[lo + s:lo + s + 1, :], 0.0).astype(BF16)
                        for h in range(PAIR):
                            hl = slice(h * HEAD, (h + 1) * HEAD)
                            p_s[(2 * h) * fam + at_:(2 * h) * fam + at_ + SUB, :] = pa[:, hl]
                            p_s[(2 * h + 1) * fam + at_:(2 * h + 1) * fam + at_ + SUB, :] = pd[:, hl]

            diag_products(range(0, nsub // 2))
            yield
            dqb, dkb = [], []
            for h in range(PAIR):
                hl = slice(h * HEAD, (h + 1) * HEAD)
                dv_s[:, hl] += _dot(at[h].astype(BF16), dob[:, hl])
                dqb.append(_dot(daf[h].astype(BF16), rhs[h]))
                dkb.append(_dot(daf[h].T.astype(BF16), lhs[h]))
            diag_products(range(nsub // 2, nsub))
            yield
            rs = _dot(p_s[...], ones)
            cr_s[...] = jnp.zeros_like(cr_s)
            for i in range(1, nsub):
                lo = i * SUB
                ref = b_s[lo - 1:lo, :]
                eq = jnp.exp(b_s[lo:lo + SUB, :] - ref)
                ek = jnp.exp(ref - b_s[0:lo, :])
                qtf = q_s[lo:lo + SUB, :] * eq
                ktf = k_s[0:lo, :] * ek
                cb = slice((i - 1) * HEAD, i * HEAD)
                for h in range(PAIR):
                    hl = slice(h * HEAD, (h + 1) * HEAD)
                    dqi = dqb[h][lo:lo + SUB, cb]
                    dki = dkb[h][0:lo, cb]
                    dq_s[lo:lo + SUB, hl] += dqi * eq[:, hl]
                    dk_s[0:lo, hl] += dki * ek[:, hl]
                    cr_s[lo:lo + SUB, hl] += (lhs[h, lo:lo + SUB, cb].astype(F32) - qtf[:, hl]) * dqi
                    cr_s[0:lo, hl] -= (rhs[h, 0:lo, cb].astype(F32) - ktf[:, hl]) * dki
            yield
            for i in range(nsub):
                lo = i * SUB
                for h in range(PAIR):
                    hl = slice(h * HEAD, (h + 1) * HEAD)
                    doi = do_s[lo:lo + SUB, hl]
                    dqa = dq_s[lo:lo + SUB, hl]
                    xk, xv = [], []
                    for s in range(SUB):
                        at = (i * SUB + s) * SUB
                        acol = rs[(2 * h) * fam + at:(2 * h) * fam + at + SUB, :]
                        dacol = rs[(2 * h + 1) * fam + at:(2 * h + 1) * fam + at + SUB, :]
                        dqa = dqa + dacol * ke_s[at:at + SUB, hl]
                        pk = dacol * qe_s[at:at + SUB, hl]
                        pv = acol * doi
                        xk.append(pk[0:8, :] + pk[8:SUB, :])
                        xv.append(pv[0:8, :] + pv[8:SUB, :])
                    dq_s[lo:lo + SUB, hl] = dqa
                    for g8 in range(SUB // 8):
                        r8 = slice(lo + 8 * g8, lo + 8 * g8 + 8)
                        dk_s[r8, hl] += _rows_of_sums(xk[8 * g8:8 * g8 + 8])
                        dv_s[r8, hl] += _rows_of_sums(xv[8 * g8:8 * g8 + 8])

            dq = dq_s[...]
            dk = dk_s[...]
            db = q * dq - k * dk + cr_s[...] + jnp.where(last_row, extra, 0.0)
            dlogf = _tri_dot(triu, db)
            df = jnp.where(f > GATE_EPS, dlogf / jnp.maximum(f, GATE_EPS), 0.0) - dk
            s_ref[1:2, :] += jnp.sum(df * (1.0 - sg), axis=0, keepdims=True)
            dp_ref[0, rows, :] = (dq * (sq * (1.0 + pq * (1.0 - sq)))).astype(BF16)
            dp_ref[1, rows, :] = (df * (1.0 - lb_) * sg * (1.0 - sg)).astype(BF16)
            dp_ref[2, rows, :] = dv_s[...].astype(BF16)
            for h in range(PAIR):
                dp_ref[3, rows, h * HEAD:(h + 1) * HEAD] = dpg[h].astype(BF16)
            yield

        def trip(g, carry):
            _staggered([chunk_stages(j, nck - 1 - (g * ntc + j)) for j in range(ntc)])
            return carry

        lax.fori_loop(0, nck // ntc, trip, 0)

    def grp(gi):
        return pl.BlockSpec((None, rb, W), lambda h, r: (gi, nb - 1 - r, h))

    rowsp = pl.BlockSpec((rb, W), lambda h, r: (nb - 1 - r, h))
    vec = pl.BlockSpec((1, W), lambda h, r: (0, h))
    return carried_call(
        body, comm, name=name, grid=(H // PAIR, nb),
        in_specs=[grp(0), grp(1), grp(2), grp(3), rowsp, rowsp,
                  pl.BlockSpec((PAIR, nck, HEAD, HEAD), lambda h, r: (h, nb - 1 - r, 0, 0)), vec, vec],
        out_specs=[pl.BlockSpec((4, rb, W), lambda h, r: (0, nb - 1 - r, h)),
                   pl.BlockSpec((8, W), lambda h, r: (0, h))],
        out_shape=[jax.ShapeDtypeStruct((4, T, D), BF16), jax.ShapeDtypeStruct((8, D), F32)],
        scratch_shapes=[pltpu.VMEM((W, W), F32), pltpu.VMEM((ntc, W, W), F32)] + [pltpu.VMEM((ntc, CHUNK, W), F32)] * 9
        + [pltpu.VMEM((ntc, PAIR, CHUNK, (nsub - 1) * HEAD), BF16)] * 2 + [pltpu.VMEM((ntc, fam, W), F32)] * 2
        + [pltpu.VMEM((ntc, 2 * PAIR * fam, HEAD), BF16)],
        args=(proj, proj, proj, proj, o, dog, states, lb, norm_g))


def lb_fwd(logits):
    def body(l_ref, o_ref):
        l = l_ref[...]
        mx = jnp.max(l, axis=0, keepdims=True)
        e = jnp.exp(l - mx)
        sm = e / jnp.sum(e, axis=0, keepdims=True)
        o_ref[0:1, :] = jnp.zeros_like(sm[0:1, :])
        o_ref[1:2, :] = sm[1:2, :]

    return pl.pallas_call(body, name="lb_fwd", out_shape=jax.ShapeDtypeStruct(logits.shape, F32))(logits)


def lb_bwd(logits, dlb):
    def body(l_ref, d_ref, o_ref):
        l = l_ref[...]
        mx = jnp.max(l, axis=0, keepdims=True)
        e = jnp.exp(l - mx)
        sm = e / jnp.sum(e, axis=0, keepdims=True)
        inner = d_ref[1:2, :] * sm[1:2, :]
        o_ref[0:1, :] = sm[0:1, :] * (0.0 - inner)
        o_ref[1:2, :] = sm[1:2, :] * (d_ref[1:2, :] - inner)

    return pl.pallas_call(body, name="lb_bwd", out_shape=jax.ShapeDtypeStruct(logits.shape, F32))(logits, dlb)


def _shifted_copies(sh, rows):
    for b in range(1, 8):
        sh[b, 0:rows, :] = sh[0, b:b + rows, :]


def conv_fwd(u, vec, *, tm, name, comm=None):
    _, T, D = u.shape
    hb = tm // HALO
    nlc = D // HEAD

    def body(a_ref, gt_ref, ap_ref, gp_ref, w_ref, c_ref, v_ref, sh):
        i = pl.program_id(0)
        sh[0, HALO:HALO + tm, :] = a_ref[...] * _sigmoid(gt_ref[...])
        prev = ap_ref[...] * _sigmoid(gp_ref[...])
        sh[0, 0:HALO, :] = jnp.where(i > 0, prev, 0.0)
        _shifted_copies(sh, tm + HALO - 8)

        def rowblock(r, carry):
            r0 = r * CONV_RB
            for cl in range(nlc):
                ls = slice(cl * HEAD, (cl + 1) * HEAD)
                acc = jnp.zeros((CONV_RB, HEAD), F32) + w_ref[32:33, ls]
                for j in range(CONV_W):
                    o = j + 2
                    at = pl.ds(pl.multiple_of(r0 + o - o % 8, 8), CONV_RB)
                    acc = acc + w_ref[j:j + 1, ls] * sh[o % 8, at, ls]
                c_ref[pl.ds(pl.multiple_of(r0, CONV_RB), CONV_RB), ls] = acc
            return carry

        lax.fori_loop(0, tm // CONV_RB, rowblock, 0)
        xhat, _ = _ln_stats(c_ref[...])
        y = xhat * w_ref[33:34, :] + w_ref[34:35, :]
        v_ref[...] = (y * _sigmoid(y)).astype(BF16)

    cur = lambda gi: pl.BlockSpec((None, tm, D), lambda i: (gi, i, 0))
    prv = lambda gi: pl.BlockSpec((None, HALO, D), lambda i: (gi, jnp.maximum(i * hb - 1, 0), 0))
    return carried_call(
        body, comm, name=name, grid=(T // tm,),
        in_specs=[cur(0), cur(1), prv(0), prv(1), pl.BlockSpec((VEC_ROWS, D), lambda i: (0, 0))],
        out_specs=[pl.BlockSpec((tm, D), lambda i: (i, 0)), pl.BlockSpec((tm, D), lambda i: (i, 0))],
        out_shape=[jax.ShapeDtypeStruct((T, D), F32), jax.ShapeDtypeStruct((T, D), BF16)],
        scratch_shapes=[pltpu.VMEM((8, tm + HALO, D), F32)],
        args=(u, u, u, u, vec))


def conv_bwd(dv2, c, u, vec, *, tm, name, comm=None):
    _, T, D = u.shape
    hb = tm // HALO
    nt = T // tm
    nh = T // HALO
    nlc = D // HEAD
    acc_rows = {j: j for j in range(CONV_W)}
    acc_rows.update({36: CONV_W, 37: CONV_W + 1})

    def body(dv_ref, c_ref, dvn_ref, cn_ref, a_ref, gt_ref, ap_ref, gp_ref, w_ref, du_ref, s_ref, gsh, dsh, part):
        i = pl.program_id(0)

        @pl.when(i == 0)
        def _():
            s_ref[...] = jnp.zeros_like(s_ref)
            part[...] = jnp.zeros_like(part)

        gam = w_ref[33:34, :]
        bet = w_ref[34:35, :]

        def dconv(dv, cc):
            xhat, rstd = _ln_stats(cc)
            y = xhat * gam + bet
            sy = _sigmoid(y)
            dy = dv * (sy * (1.0 + y * (1.0 - sy)))
            return _ln_bwd(dy, xhat, rstd, gam), dy, xhat

        dc, dy, xhat = dconv(dv_ref[...], c_ref[...])
        dcn, _, _ = dconv(dvn_ref[...], cn_ref[...])
        dsh[0, 0:tm, :] = dc
        dsh[0, tm:tm + HALO, :] = jnp.where(i < nt - 1, dcn, 0.0)
        gsh[0, HALO:HALO + tm, :] = a_ref[...] * _sigmoid(gt_ref[...])
        gsh[0, 0:HALO, :] = jnp.where(i > 0, ap_ref[...] * _sigmoid(gp_ref[...]), 0.0)
        s_ref[32:33, :] += jnp.sum(dc, axis=0, keepdims=True)
        s_ref[33:34, :] += jnp.sum(dy * xhat, axis=0, keepdims=True)
        s_ref[34:35, :] += jnp.sum(dy, axis=0, keepdims=True)
        _shifted_copies(dsh, tm + HALO - 8)
        _shifted_copies(gsh, tm + HALO - 8)

        def fold8(x):
            acc = x[0:8, :]
            for g in range(1, CONV_RB // 8):
                acc = acc + x[8 * g:8 * g + 8, :]
            return acc

        for cl in range(nlc):
            ls = slice(cl * HEAD, (cl + 1) * HEAD)

            def rowblock(r, sums, ls=ls):
                r0 = r * CONV_RB
                rows = pl.ds(pl.multiple_of(r0, CONV_RB), CONV_RB)
                dcb = dsh[0, rows, ls]
                dglu = jnp.zeros((CONV_RB, HEAD), F32)
                new = []
                for j in range(CONV_W):
                    od = 30 - j
                    og = j + 2
                    atd = pl.ds(pl.multiple_of(r0 + od - od % 8, 8), CONV_RB)
                    atg = pl.ds(pl.multiple_of(r0 + og - og % 8, 8), CONV_RB)
                    dglu = dglu + w_ref[j:j + 1, ls] * dsh[od % 8, atd, ls]
                    new.append(sums[j] + fold8(dcb * gsh[og % 8, atg, ls]))
                a = a_ref[rows, ls]
                sgt = _sigmoid(gt_ref[rows, ls])
                da = (dglu * sgt).astype(BF16)
                dg = (dglu * a * sgt * (1.0 - sgt)).astype(BF16)
                du_ref[0, rows, ls] = da
                du_ref[1, rows, ls] = dg
                new.append(sums[CONV_W] + fold8(da.astype(F32)))
                new.append(sums[CONV_W + 1] + fold8(dg.astype(F32)))
                return tuple(new)

            zero = jnp.zeros((8, HEAD), F32)
            sums = lax.fori_loop(0, tm // CONV_RB, rowblock, (zero,) * (CONV_W + 2))
            for k in range(CONV_W + 2):
                part[8 * k:8 * k + 8, ls] += sums[k]

        @pl.when(i == nt - 1)
        def _():
            for row, k in acc_rows.items():
                s_ref[row:row + 1, :] = jnp.sum(part[8 * k:8 * k + 8, :], axis=0, keepdims=True)

    row = lambda i: (i, 0)
    nxt = lambda i: (jnp.minimum((i + 1) * hb, nh - 1), 0)
    cur = lambda gi: pl.BlockSpec((None, tm, D), lambda i: (gi, i, 0))
    prv = lambda gi: pl.BlockSpec((None, HALO, D), lambda i: (gi, jnp.maximum(i * hb - 1, 0), 0))
    fix = lambda i: (0, 0)
    return carried_call(
        body, comm, name=name, grid=(nt,),
        in_specs=[pl.BlockSpec((tm, D), row), pl.BlockSpec((tm, D), row),
                  pl.BlockSpec((HALO, D), nxt), pl.BlockSpec((HALO, D), nxt),
                  cur(0), cur(1), prv(0), prv(1), pl.BlockSpec((VEC_ROWS, D), fix)],
        out_specs=[pl.BlockSpec((2, tm, D), lambda i: (0, i, 0)), pl.BlockSpec((VEC_ROWS, D), fix)],
        out_shape=[jax.ShapeDtypeStruct((2, T, D), BF16), jax.ShapeDtypeStruct((VEC_ROWS, D), F32)],
        scratch_shapes=[pltpu.VMEM((8, tm + HALO, D), F32), pltpu.VMEM((8, tm + HALO, D), F32),
                        pltpu.VMEM((8 * (CONV_W + 2), D), F32)],
        args=(dv2, c, dv2, c, u, u, u, u, vec))


def loss_grad(y, target, *, tm):
    T, D = y.shape
    nt = T // tm

    def body(y_ref, t_ref, l_ref, d_ref, acc):
        i = pl.program_id(0)

        @pl.when(i == 0)
        def _():
            acc[...] = jnp.zeros_like(acc)

        e = y_ref[...] - t_ref[...]
        d_ref[...] = e * (1.0 / D)
        acc[...] += jnp.sum(e * e, axis=0, keepdims=True)

        @pl.when(i == nt - 1)
        def _():
            l_ref[...] = 0.5 * jnp.sum(acc[...], axis=1, keepdims=True) * (1.0 / D)

    row = lambda i: (i, 0)
    return _call(
        body, name="loss_grad", grid=(nt,),
        in_specs=[pl.BlockSpec((tm, D), row), pl.BlockSpec((tm, D), row)],
        out_specs=[pl.BlockSpec((1, 1), lambda i: (0, 0)), pl.BlockSpec((tm, D), row)],
        out_shape=[jax.ShapeDtypeStruct((1, 1), F32), jax.ShapeDtypeStruct((T, D), F32)],
        scratch_shapes=[pltpu.VMEM((1, D), F32)], args=(y, target))


def _rows_block(R, C, budget=1 << 20):
    tr = R
    while tr * C * 4 > budget and tr % 32 == 0:
        tr //= 2
    return tr


def adamw(w, g, m, v, *, name):
    R, C = w.shape
    tr = _rows_block(R, C)

    def body(w_ref, g_ref, m_ref, v_ref, d_ref, mo_ref, vo_ref):
        g_ = g_ref[...]
        mn = ADAM_B1 * m_ref[...] + (1.0 - ADAM_B1) * g_
        vn = ADAM_B2 * v_ref[...] + (1.0 - ADAM_B2) * jnp.square(g_)
        m_hat = mn / (1.0 - ADAM_B1 ** ADAM_STEP)
        v_hat = vn / (1.0 - ADAM_B2 ** ADAM_STEP)
        d_ref[...] = -ADAM_LR * (m_hat / (jnp.sqrt(v_hat) + ADAM_EPS) + ADAM_WD * w_ref[...])
        mo_ref[...] = mn
        vo_ref[...] = vn

    spec = pl.BlockSpec((tr, C), lambda i: (i, 0))
    sd = jax.ShapeDtypeStruct((R, C), F32)
    return _call(body, name=name, grid=(R // tr,), in_specs=[spec] * 4, out_specs=[spec] * 3, out_shape=[sd] * 3,
                 args=(w, g, m, v))


def sum_slots(slots, *, name):
    _, R, C = slots.shape
    tr = _rows_block(R, C, budget=1 << 19)

    def body(s_ref, o_ref):
        acc = s_ref[0].astype(F32)
        for d in range(1, 8):
            acc = acc + s_ref[d].astype(F32)
        o_ref[...] = acc

    return _call(body, name=name, grid=(R // tr,), in_specs=[pl.BlockSpec((8, tr, C), lambda i: (0, i, 0))],
                 out_specs=[pl.BlockSpec((tr, C), lambda i: (i, 0))], out_shape=[jax.ShapeDtypeStruct((R, C), F32)],
                 args=(slots,))[0]


def _adam_nd(w, g, m, v, name):
    shp = w.shape
    c = shp[-1]
    f2 = lambda a: a.reshape(-1, c)
    d, mn, vn = adamw(f2(w), f2(g), f2(m), f2(v), name=name)
    return d.reshape(shp), mn.reshape(shp), vn.reshape(shp)


def _reduced(slots, name):
    out = []
    for s in slots:
        c = s.shape[-1]
        out.append(sum_slots(s.reshape(8, -1, c), name=name).reshape(s.shape[1:]))
    return out


def kernel(x, ln_mix_g, ln_mix_b, ln_ffn_g, ln_ffn_b, ffn_w1, ffn_w2, a_w_in, a_lb_logits, a_norm_g, a_w_out, b_w_pw1, b_b_pw1, b_w_dw, b_b_dw, b_ln_g, b_ln_b, b_w_pw2, b_b_pw2, loss_target, m_ln_mix_g, m_ln_mix_b, m_ln_ffn_g, m_ln_ffn_b, m_ffn_w1, m_ffn_w2, m_a_w_in, m_a_lb_logits, m_a_norm_g, m_a_w_out, m_b_w_pw1, m_b_b_pw1, m_b_w_dw, m_b_b_dw, m_b_ln_g, m_b_ln_b, m_b_w_pw2, m_b_b_pw2, v_ln_mix_g, v_ln_mix_b, v_ln_ffn_g, v_ln_ffn_b, v_ffn_w1, v_ffn_w2, v_a_w_in, v_a_lb_logits, v_a_norm_g, v_a_w_out, v_b_w_pw1, v_b_b_pw1, v_b_w_dw, v_b_b_dw, v_b_ln_g, v_b_ln_b, v_b_w_pw2, v_b_b_pw2):
    names = ["ln_mix_g", "ln_mix_b", "ln_ffn_g", "ln_ffn_b", "ffn_w1", "ffn_w2", "a_w_in", "a_lb_logits", "a_norm_g",
             "a_w_out", "b_w_pw1", "b_b_pw1", "b_w_dw", "b_b_dw", "b_ln_g", "b_ln_b", "b_w_pw2", "b_b_pw2"]
    w = dict(zip(names, [ln_mix_g, ln_mix_b, ln_ffn_g, ln_ffn_b, ffn_w1, ffn_w2, a_w_in, a_lb_logits, a_norm_g, a_w_out,
                         b_w_pw1, b_b_pw1, b_w_dw, b_b_dw, b_ln_g, b_ln_b, b_w_pw2, b_b_pw2]))
    m = dict(zip(names, [m_ln_mix_g, m_ln_mix_b, m_ln_ffn_g, m_ln_ffn_b, m_ffn_w1, m_ffn_w2, m_a_w_in, m_a_lb_logits,
                         m_a_norm_g, m_a_w_out, m_b_w_pw1, m_b_b_pw1, m_b_w_dw, m_b_b_dw, m_b_ln_g, m_b_ln_b, m_b_w_pw2,
                         m_b_b_pw2]))
    v = dict(zip(names, [v_ln_mix_g, v_ln_mix_b, v_ln_ffn_g, v_ln_ffn_b, v_ffn_w1, v_ffn_w2, v_a_w_in, v_a_lb_logits,
                         v_a_norm_g, v_a_w_out, v_b_w_pw1, v_b_b_pw1, v_b_w_dw, v_b_b_dw, v_b_ln_g, v_b_ln_b, v_b_w_pw2,
                         v_b_b_pw2]))
    T, D = x.shape[1], x.shape[2]
    DS = D // 4
    F = 4 * ffn_w1.shape[2]
    chip = 2 * lax.axis_index("x") + lax.axis_index("y")
    tm = min(T, 512)
    tmw = min(T, 1024)
    tmc = min(T, 256)
    rb = min(T, 512)
    tf = min(F // 4, 1024)
    xin, target = x[0], loss_target[0]

    def mix_shards(i):
        j = i // 2
        if i % 2 == 0:
            return [a_w_in[j].astype(BF16), a_w_out[j].astype(BF16)]
        vec = jnp.concatenate([b_w_dw[j], jnp.zeros((1, DS), F32), b_b_dw[j][None], b_ln_g[j][None], b_ln_b[j][None],
                               b_b_pw2[j][None], b_b_pw1[j].reshape(2, DS), jnp.zeros((2, DS), F32)], axis=0)
        return [b_w_pw1[j].astype(BF16), b_w_pw2[j].astype(BF16), vec]

    def ffn_shards(i):
        return [ffn_w1[i].astype(BF16), ffn_w2[i].astype(BF16)]

    def mix_weights(i, got):
        if i % 2 == 0:
            return {"w_in": got[0], "w_out": got[1].reshape(D, D)}
        pw1 = jnp.transpose(got[0].reshape(2, 2, D, D // 2), (0, 2, 1, 3)).reshape(2, D, D)
        vec = jnp.transpose(got[2], (1, 0, 2)).reshape(VEC_ROWS, D)
        return {"pw1": pw1, "pw2": got[1].reshape(D, D), "vec": vec,
                "b_pw1": got[2][:, 36:38, :].reshape(2, 1, D)}

    lb_all = lb_fwd(a_lb_logits)
    zeros_bias = jnp.zeros((1, D), F32)

    first = mix_shards(0)
    mixw = {"w_in": comm_call(GatherChips(first[:1]), name="gather_first")[0]}
    saved = []
    h, hb = xin, xin.astype(BF16)
    for i in range(DEPTH):
        j = i // 2
        s = {"xb": hb, "mixw": mixw}
        gf = GatherChips(ffn_shards(i) + (first[1:] if i == 0 else []))
        if i % 2 == 0:
            s["proj"] = mm_groups(hb, mixw["w_in"], jnp.zeros((4, 1, D), F32), tm=tm, name="a_in_proj")
            (s["o"], s["og"], s["st"]), got = hgrn_fwd(s["proj"], lb_all[j:j + 1], a_norm_g[j:j + 1], rb=rb,
                                                       name="hgrn_fwd", comm=gf)
            if i == 0:
                mixw["w_out"] = got[2].reshape(D, D)
            s["r1"], x1, s["x1b"] = mm_res_ln(s["og"], mixw["w_out"], zeros_bias, h, ln_mix_g[i:i + 1],
                                              ln_mix_b[i:i + 1], tm=tm, name="a_out_ln")
        else:
            s["u"] = mm_groups(hb, mixw["pw1"], mixw["b_pw1"], tm=tm, name="b_pw1")
            (s["c"], s["v2"]), got = conv_fwd(s["u"], mixw["vec"], tm=tmc, name="conv_fwd", comm=gf)
            s["r1"], x1, s["x1b"] = mm_res_ln(s["v2"], mixw["pw2"], mixw["vec"][35:36], h, ln_mix_g[i:i + 1],
                                              ln_mix_b[i:i + 1], tm=tm, name="b_pw2_ln")
        s["w1"], s["w2"] = got[0], got[1].reshape(F, D)
        gm = GatherChips(mix_shards(i + 1)) if i + 1 < DEPTH else None
        (s["z"], s["r2"], h, hb), got = ffn_fwd(x1, s["w1"], s["w2"], ln_ffn_g[i:i + 1], ln_ffn_b[i:i + 1],
                                                tm=tm, tf=tf, name="ffn_fwd", comm=gm)
        if gm is not None:
            mixw = mix_weights(i + 1, got)
        saved.append(s)

    loss_part, dh = loss_grad(h, target, tm=tm)
    loss = lax.psum(loss_part[0, 0], ("x", "y", "c"))

    gr = {k: [None] * DEPTH for k in ("ln_mix_g", "ln_mix_b", "ln_ffn_g", "ln_ffn_b", "ffn_w1", "ffn_w2")}
    for k in ("a_w_in", "a_w_out", "a_norm_g", "a_dlb", "b_w_pw1", "b_w_pw2", "b_vec", "b_b_pw2"):
        gr[k] = [None] * 2
    w_in_name = ("a_w_in", "b_w_pw1")
    w_out_name = ("a_w_out", "b_w_pw2")
    pending = None

    for i in reversed(range(DEPTH)):
        j = i // 2
        s = saved[i]
        mixw = s["mixw"]
        sm = ScatterPieces([pending[1]]) if pending is not None else None
        (dz, dx1, drb2, sums2), slots = ffn_bwd_dx(dh, s["r2"], ln_ffn_g[i:i + 1], s["z"], s["w1"], s["w2"],
                                                   tm=tm, tf=tf, name="ffn_bwd_dx", comm=sm)
        if pending is not None:
            gr[w_in_name[pending[0] % 2]][pending[0] // 2] = _reduced(slots, "sum_mix_grads")[0]
        gr["ln_ffn_g"][i], gr["ln_ffn_b"][i] = sums2[0], sums2[1]
        dw1 = mm_tn(s["x1b"], dz[None], tm=tmw, tk=D, tn=F // 4, name="ffn_dw1")[0]
        dw2 = mm_tn(s["z"], drb2[None], tm=tmw, tk=F // 4, tn=D, relu2=True, name="ffn_dw2")[0, 0]
        wmix = mixw["w_out"] if i % 2 == 0 else mixw["pw2"]
        dr1, drb1, dmo, sums1 = ln_bwd_mm(dx1, s["r1"], ln_mix_g[i:i + 1], wmix, tm=tm, name="mix_ln_bwd")
        gr["ln_mix_g"][i], gr["ln_mix_b"][i] = sums1[0], sums1[1]
        if i % 2 == 0:
            dwo = mm_tn(s["og"], drb1[None], tm=tmw, tk=D, tn=D, name="a_dw_out")[0, 0].reshape(4, DS, D)
            sf = ScatterPieces([dw1, dw2.reshape(4, F // 4, D), dwo])
            (dproj, hs), slots = hgrn_bwd(s["proj"], s["o"], dmo, s["st"], lb_all[j:j + 1], a_norm_g[j:j + 1], rb=rb,
                                          name="hgrn_bwd", comm=sf)
            gr["a_norm_g"][j], gr["a_dlb"][j] = hs[0], hs[1]
            dwi = mm_tn(s["xb"], dproj, tm=tmw, tk=D, tn=D, name="a_dw_in")[:, 0]
            dy_in, w_in_t, dx_name = dproj, mixw["w_in"], "a_dx"
        else:
            dwo = mm_tn(s["v2"], drb1[None], tm=tmw, tk=D, tn=D, name="b_dw_pw2")[0, 0].reshape(4, DS, D)
            sf = ScatterPieces([dw1, dw2.reshape(4, F // 4, D), dwo])
            (du, cs), slots = conv_bwd(dmo, s["c"], s["u"], mixw["vec"], tm=tmc, name="conv_bwd", comm=sf)
            gr["b_vec"][j], gr["b_b_pw2"][j] = cs, sums1[2]
            dwi = mm_tn(s["xb"], du, tm=tmw, tk=D, tn=D // 2, name="b_dw_pw1").reshape(4, D, D // 2)
            dy_in, w_in_t, dx_name = du, mixw["pw1"], "b_dx"
        gr["ffn_w1"][i], gr["ffn_w2"][i], gr[w_out_name[i % 2]][j] = _reduced(slots, "sum_ffn_grads")
        dh, slots = mm_nt_acc(dy_in, w_in_t, dr1, tm=tm, name=dx_name, comm=ScatterPieces([dwi]) if i == 0 else None)
        if i == 0:
            gr[w_in_name[0]][0] = _reduced(slots, "sum_mix_grads")[0]
        pending = (i, dwi)
    grad_x = dh[None]

    small = {k: jnp.stack(gr[k]) for k in ("ln_mix_g", "ln_mix_b", "ln_ffn_g", "ln_ffn_b", "a_norm_g", "b_vec", "b_b_pw2")}
    small["a_lb_logits"] = lb_bwd(a_lb_logits, jnp.stack(gr["a_dlb"]))
    small_names = ["ln_mix_g", "ln_mix_b", "ln_ffn_g", "ln_ffn_b", "a_lb_logits", "a_norm_g", "b_b_pw2", "b_vec"]
    rows = [small[k].reshape(-1, D) for k in small_names]
    counts = [r.shape[0] for r in rows]
    rows = [jnp.pad(r, ((0, (-r.shape[0]) % 8), (0, 0))) for r in rows]
    summed = all_reduce_small(jnp.concatenate(rows, axis=0))
    sm = {}
    off = 0
    for k, n, r in zip(small_names, counts, rows):
        sm[k] = summed[off:off + n]
        off += r.shape[0]
    bvec = sm["b_vec"].reshape(2, VEC_ROWS, D)

    def shard_cols(a):
        return lax.dynamic_slice_in_dim(a, chip * DS, DS, axis=a.ndim - 1)

    grads = {k: jnp.stack(gr[k]) for k in ("ffn_w1", "ffn_w2", "a_w_in", "a_w_out", "b_w_pw1", "b_w_pw2")}
    for k in ("ln_mix_g", "ln_mix_b", "ln_ffn_g", "ln_ffn_b", "a_lb_logits", "a_norm_g"):
        grads[k] = sm[k]
    grads["b_b_pw1"] = lax.dynamic_slice_in_dim(bvec[:, 36:38, :].reshape(2, 2 * D), chip * (D // 2), D // 2, axis=1)
    grads["b_w_dw"] = shard_cols(bvec[:, 0:CONV_W, :])
    grads["b_b_dw"] = shard_cols(bvec[:, 32, :])
    grads["b_ln_g"] = shard_cols(bvec[:, 33, :])
    grads["b_ln_b"] = shard_cols(bvec[:, 34, :])
    grads["b_b_pw2"] = shard_cols(sm["b_b_pw2"])

    delta, new_m, new_v = {}, {}, {}
    for k in names:
        delta[k], new_m[k], new_v[k] = _adam_nd(w[k], grads[k], m[k], v[k], "adamw_" + k)
    return (loss, grad_x, *[grads[k] for k in names], *[delta[k] for k in names],
            *[new_m[k] for k in names], *[new_v[k] for k in names])
```

```python
import jax
import jax.numpy as jnp
from jax import lax
from jax.experimental import pallas as pl
from jax.experimental.pallas import tpu as pltpu

F32 = jnp.float32
BF16 = jnp.bfloat16
MESH = pl.DeviceIdType.MESH

DEPTH = 4
ALPHA = (2.0 * DEPTH) ** 0.25
LN_EPS = 1e-5
RMS_EPS = 1e-6
GATE_EPS = 1e-6
HEAD = 128
CHUNK = 128
SUB = 16
PAIR = 2
TRIP_CHUNKS = 2
CONV_W = 31
HALO = 32
VEC_ROWS = 40
CONV_RB = 32
ADAM_LR, ADAM_B1, ADAM_B2, ADAM_EPS, ADAM_WD, ADAM_STEP = 0.001, 0.9, 0.999, 1e-08, 0.01, 10
VMEM_LIMIT = 56 * 1024 * 1024
ANY = pl.BlockSpec(memory_space=pl.ANY)


def _dot(a, b):
    return jnp.dot(a, b, preferred_element_type=F32)


def _dot_nt(a, b):
    return lax.dot_general(a, b, (((1,), (1,)), ((), ())), preferred_element_type=F32)


def _dot_tn(a, b):
    return lax.dot_general(a, b, (((0,), (0,)), ((), ())), preferred_element_type=F32)


def _sigmoid(x):
    return 1.0 / (1.0 + jnp.exp(-x))


def _ln_stats(r):
    mu = jnp.mean(r, axis=-1, keepdims=True)
    xc = r - mu
    var = jnp.mean(xc * xc, axis=-1, keepdims=True)
    rstd = lax.rsqrt(var + LN_EPS)
    return xc * rstd, rstd


def _ln_bwd(dy, xhat, rstd, g):
    dyg = dy * g
    m1 = jnp.mean(dyg, axis=-1, keepdims=True)
    m2 = jnp.mean(dyg * xhat, axis=-1, keepdims=True)
    return rstd * (dyg - m1 - xhat * m2)


def _place():
    return lax.axis_index("x"), lax.axis_index("y"), lax.axis_index("c")


class GatherChips:
    def __init__(self, arrs):
        self.ins = list(arrs)
        n = len(arrs)
        self.out_shapes = [jax.ShapeDtypeStruct((4,) + a.shape, a.dtype) for a in arrs]
        self.sems = [pltpu.SemaphoreType.DMA((3 * n,)), pltpu.SemaphoreType.DMA((3 * n,)),
                     pltpu.SemaphoreType.DMA((n,))]

    def copies(self, ins, outs, send, recv, loc):
        x, y, c = _place()
        me = 2 * x + y
        local, remote = [], []
        for a in range(len(ins)):
            local.append(pltpu.make_async_copy(ins[a], outs[a].at[me], loc.at[a]))
            for j, (px, py) in enumerate([(1 - x, y), (x, 1 - y), (1 - x, 1 - y)]):
                remote.append(pltpu.make_async_remote_copy(
                    src_ref=ins[a], dst_ref=outs[a].at[me], send_sem=send.at[3 * a + j], recv_sem=recv.at[3 * a + j],
                    device_id=(px, py, c), device_id_type=MESH))
        return local + remote


class ScatterPieces:
    def __init__(self, arrs):
        self.ins = list(arrs)
        n = len(arrs)
        self.out_shapes = [jax.ShapeDtypeStruct((8,) + a.shape[1:], a.dtype) for a in arrs]
        self.sems = [pltpu.SemaphoreType.DMA((7 * n,)), pltpu.SemaphoreType.DMA((7 * n,)),
                     pltpu.SemaphoreType.DMA((n,))]

    def copies(self, ins, outs, send, recv, loc):
        x, y, c = _place()
        me = 4 * x + 2 * y + c
        local, remote = [], []
        for a in range(len(ins)):
            local.append(pltpu.make_async_copy(ins[a].at[2 * x + y], outs[a].at[me], loc.at[a]))
            k = 0
            for fx in (0, 1):
                for fy in (0, 1):
                    for fc in (0, 1):
                        if fx or fy or fc:
                            tx, ty = x ^ fx, y ^ fy
                            remote.append(pltpu.make_async_remote_copy(
                                src_ref=ins[a].at[2 * tx + ty], dst_ref=outs[a].at[me],
                                send_sem=send.at[7 * a + k], recv_sem=recv.at[7 * a + k],
                                device_id=(tx, ty, c ^ fc), device_id_type=MESH))
                            k += 1
        return local + remote


def carried_call(body, comm, *, name, grid, in_specs, out_specs, out_shape, scratch_shapes, args):
    sem = ("arbitrary",) * len(grid)
    params = pltpu.CompilerParams(dimension_semantics=sem, vmem_limit_bytes=VMEM_LIMIT)
    if comm is None:
        res = pl.pallas_call(body, name=name, grid=grid, in_specs=in_specs, out_specs=out_specs, out_shape=out_shape,
                             scratch_shapes=scratch_shapes, compiler_params=params)(*args)
        return res, []
    ni, no, nscr = len(in_specs), len(out_specs), len(scratch_shapes)
    ci, co = len(comm.ins), len(comm.out_shapes)

    def both(*refs):
        ins, refs = refs[:ni], refs[ni:]
        cins, refs = refs[:ci], refs[ci:]
        outs, refs = refs[:no], refs[no:]
        couts, refs = refs[:co], refs[co:]
        scr, sems = refs[:nscr], refs[nscr:]
        first = pl.program_id(0) == 0
        last = pl.program_id(0) == grid[0] - 1
        for d in range(1, len(grid)):
            first = first & (pl.program_id(d) == 0)
            last = last & (pl.program_id(d) == grid[d] - 1)

        @pl.when(first)
        def _():
            for cp in comm.copies(cins, couts, *sems):
                cp.start()

        body(*ins, *outs, *scr)

        @pl.when(last)
        def _():
            for cp in comm.copies(cins, couts, *sems):
                cp.wait()

    res = pl.pallas_call(
        both, name=name, grid=grid, in_specs=list(in_specs) + [ANY] * ci, out_specs=list(out_specs) + [ANY] * co,
        out_shape=list(out_shape) + comm.out_shapes, scratch_shapes=list(scratch_shapes) + comm.sems,
        compiler_params=params)(*args, *comm.ins)
    return res[:no], res[no:]


def comm_call(comm, *, name):
    ci, co = len(comm.ins), len(comm.out_shapes)

    def body(*refs):
        cps = comm.copies(refs[:ci], refs[ci:ci + co], *refs[ci + co:])
        for cp in cps:
            cp.start()
        for cp in cps:
            cp.wait()

    return pl.pallas_call(body, name=name, in_specs=[ANY] * ci, out_specs=[ANY] * co, out_shape=comm.out_shapes,
                          scratch_shapes=comm.sems, compiler_params=pltpu.CompilerParams(has_side_effects=True))(*comm.ins)


def all_reduce_small(v):
    R, C = v.shape

    def body(v_ref, o_ref, slots, send, recv):
        x, y, c = _place()
        me = 4 * x + 2 * y + c
        slots[me] = v_ref[...]
        cps = []
        k = 0
        for fx in (0, 1):
            for fy in (0, 1):
                for fc in (0, 1):
                    if fx or fy or fc:
                        cps.append(pltpu.make_async_remote_copy(
                            src_ref=v_ref, dst_ref=slots.at[me], send_sem=send.at[k], recv_sem=recv.at[k],
                            device_id=(x ^ fx, y ^ fy, c ^ fc), device_id_type=MESH))
                        k += 1
        for cp in cps:
            cp.start()
        for cp in cps:
            cp.wait()
        acc = slots[0]
        for d in range(1, 8):
            acc = acc + slots[d]
        o_ref[...] = acc

    vm = pl.BlockSpec(memory_space=pltpu.VMEM)
    return pl.pallas_call(
        body, name="all_reduce_small", in_specs=[vm], out_specs=vm,
        out_shape=jax.ShapeDtypeStruct((R, C), F32),
        scratch_shapes=[pltpu.VMEM((8, R, C), F32), pltpu.SemaphoreType.DMA((7,)), pltpu.SemaphoreType.DMA((7,))],
        compiler_params=pltpu.CompilerParams(has_side_effects=True, vmem_limit_bytes=VMEM_LIMIT),
    )(v)


def _call(body, *, name, grid, in_specs, out_specs, out_shape, scratch_shapes=(), args):
    res, _ = carried_call(body, None, name=name, grid=grid, in_specs=in_specs, out_specs=out_specs,
                          out_shape=out_shape, scratch_shapes=list(scratch_shapes), args=args)
    return res


def mm_groups(a, w, bias, *, tm, name):
    T, K = a.shape
    G, _, N = w.shape

    def body(a_ref, w_ref, b_ref, o_ref):
        o_ref[...] = _dot(a_ref[...], w_ref[...]) + b_ref[...]

    return _call(
        body, name=name, grid=(G, T // tm),
        in_specs=[pl.BlockSpec((tm, K), lambda g, i: (i, 0)),
                  pl.BlockSpec((None, K, N), lambda g, i: (g, 0, 0)),
                  pl.BlockSpec((None, 1, N), lambda g, i: (g, 0, 0))],
        out_specs=[pl.BlockSpec((None, tm, N), lambda g, i: (g, i, 0))],
        out_shape=[jax.ShapeDtypeStruct((G, T, N), F32)], args=(a, w, bias))[0]


def mm_res_ln(a, w, bias, res, g, b, *, tm, name):
    T, K = a.shape
    N = w.shape[1]

    def body(a_ref, w_ref, bias_ref, res_ref, g_ref, b_ref, r_ref, y_ref, yb_ref):
        r = ALPHA * res_ref[...] + _dot(a_ref[...], w_ref[...]) + bias_ref[...]
        r_ref[...] = r
        xhat, _ = _ln_stats(r)
        y = xhat * g_ref[...] + b_ref[...]
        y_ref[...] = y
        yb_ref[...] = y.astype(BF16)

    row = lambda i: (i, 0)
    fix = lambda i: (0, 0)
    return _call(
        body, name=name, grid=(T // tm,),
        in_specs=[pl.BlockSpec((tm, K), row), pl.BlockSpec((K, N), fix), pl.BlockSpec((1, N), fix),
                  pl.BlockSpec((tm, N), row), pl.BlockSpec((1, N), fix), pl.BlockSpec((1, N), fix)],
        out_specs=[pl.BlockSpec((tm, N), row), pl.BlockSpec((tm, N), row), pl.BlockSpec((tm, N), row)],
        out_shape=[jax.ShapeDtypeStruct((T, N), F32), jax.ShapeDtypeStruct((T, N), F32),
                   jax.ShapeDtypeStruct((T, N), BF16)],
        args=(a, w, bias, res, g, b))


def ffn_fwd(x, w1, w2, g, b, *, tm, tf, name, comm=None):
    T, D = x.shape
    NC, _, FC = w1.shape
    F = NC * FC
    per = FC // tf
    nf = F // tf

    def body(x_ref, w1_ref, w2_ref, g_ref, b_ref, z_ref, r_ref, y_ref, yb_ref, acc_ref, xb_ref):
        f = pl.program_id(1)

        @pl.when(f == 0)
        def _():
            acc_ref[...] = jnp.zeros_like(acc_ref)
            xb_ref[...] = x_ref[...].astype(BF16)

        z = _dot(xb_ref[...], w1_ref[...])
        z_ref[...] = z.astype(BF16)
        h = jnp.square(jnp.maximum(z, 0.0)).astype(BF16)
        acc_ref[...] += _dot(h, w2_ref[...])

        @pl.when(f == nf - 1)
        def _():
            r = ALPHA * x_ref[...] + acc_ref[...]
            r_ref[...] = r
            xhat, _ = _ln_stats(r)
            y = xhat * g_ref[...] + b_ref[...]
            y_ref[...] = y
            yb_ref[...] = y.astype(BF16)

    return carried_call(
        body, comm, name=name, grid=(T // tm, nf),
        in_specs=[pl.BlockSpec((tm, D), lambda i, f: (i, 0)),
                  pl.BlockSpec((None, D, tf), lambda i, f: (f // per, 0, f % per)),
                  pl.BlockSpec((tf, D), lambda i, f: (f, 0)),
                  pl.BlockSpec((1, D), lambda i, f: (0, 0)),
                  pl.BlockSpec((1, D), lambda i, f: (0, 0))],
        out_specs=[pl.BlockSpec((tm, tf), lambda i, f: (i, f)),
                   pl.BlockSpec((tm, D), lambda i, f: (i, 0)),
                   pl.BlockSpec((tm, D), lambda i, f: (i, 0)),
                   pl.BlockSpec((tm, D), lambda i, f: (i, 0))],
        out_shape=[jax.ShapeDtypeStruct((T, F), BF16), jax.ShapeDtypeStruct((T, D), F32),
                   jax.ShapeDtypeStruct((T, D), F32), jax.ShapeDtypeStruct((T, D), BF16)],
        scratch_shapes=[pltpu.VMEM((tm, D), F32), pltpu.VMEM((tm, D), BF16)],
        args=(x, w1, w2, g, b))


def ln_bwd_mm(dy, r, g, w, *, tm, name):
    T, N = dy.shape
    Ko = w.shape[0]

    def body(dy_ref, r_ref, g_ref, w_ref, dr_ref, drb_ref, o_ref, s_ref):
        @pl.when(pl.program_id(0) == 0)
        def _():
            s_ref[...] = jnp.zeros_like(s_ref)

        dy_ = dy_ref[...]
        xhat, rstd = _ln_stats(r_ref[...])
        dr = _ln_bwd(dy_, xhat, rstd, g_ref[...])
        dr_ref[...] = dr
        drb = dr.astype(BF16)
        drb_ref[...] = drb
        o_ref[...] = _dot_nt(drb, w_ref[...])
        s_ref[0:1, :] += jnp.sum(dy_ * xhat, axis=0, keepdims=True)
        s_ref[1:2, :] += jnp.sum(dy_, axis=0, keepdims=True)
        s_ref[2:3, :] += jnp.sum(dr, axis=0, keepdims=True)

    row = lambda i: (i, 0)
    fix = lambda i: (0, 0)
    return _call(
        body, name=name, grid=(T // tm,),
        in_specs=[pl.BlockSpec((tm, N), row), pl.BlockSpec((tm, N), row), pl.BlockSpec((1, N), fix),
                  pl.BlockSpec((Ko, N), fix)],
        out_specs=[pl.BlockSpec((tm, N), row), pl.BlockSpec((tm, N), row), pl.BlockSpec((tm, Ko), row),
                   pl.BlockSpec((8, N), fix)],
        out_shape=[jax.ShapeDtypeStruct((T, N), F32), jax.ShapeDtypeStruct((T, N), BF16),
                   jax.ShapeDtypeStruct((T, Ko), F32), jax.ShapeDtypeStruct((8, N), F32)],
        args=(dy, r, g, w))


def ffn_bwd_dx(dy, r, g, z, w1, w2, *, tm, tf, name, comm=None):
    T, D = dy.shape
    NC, _, FC = w1.shape
    F = NC * FC
    per = FC // tf
    nf = F // tf

    def body(dy_ref, r_ref, g_ref, z_ref, w1_ref, w2_ref, dz_ref, dx_ref, drb_ref, s_ref, dr_scr, acc_ref):
        i = pl.program_id(0)
        f = pl.program_id(1)

        @pl.when((i == 0) & (f == 0))
        def _():
            s_ref[...] = jnp.zeros_like(s_ref)

        @pl.when(f == 0)
        def _():
            dy_ = dy_ref[...]
            xhat, rstd = _ln_stats(r_ref[...])
            dr = _ln_bwd(dy_, xhat, rstd, g_ref[...])
            dr_scr[...] = dr
            drb_ref[...] = dr.astype(BF16)
            acc_ref[...] = jnp.zeros_like(acc_ref)
            s_ref[0:1, :] += jnp.sum(dy_ * xhat, axis=0, keepdims=True)
            s_ref[1:2, :] += jnp.sum(dy_, axis=0, keepdims=True)

        dh = _dot_nt(drb_ref[...], w2_ref[...])
        dz = (dh * (2.0 * jnp.maximum(z_ref[...].astype(F32), 0.0))).astype(BF16)
        dz_ref[...] = dz
        acc_ref[...] += _dot_nt(dz, w1_ref[...])

        @pl.when(f == nf - 1)
        def _():
            dx_ref[...] = ALPHA * dr_scr[...] + acc_ref[...]

    return carried_call(
        body, comm, name=name, grid=(T // tm, nf),
        in_specs=[pl.BlockSpec((tm, D), lambda i, f: (i, 0)),
                  pl.BlockSpec((tm, D), lambda i, f: (i, 0)),
                  pl.BlockSpec((1, D), lambda i, f: (0, 0)),
                  pl.BlockSpec((tm, tf), lambda i, f: (i, f)),
                  pl.BlockSpec((None, D, tf), lambda i, f: (f // per, 0, f % per)),
                  pl.BlockSpec((tf, D), lambda i, f: (f, 0))],
        out_specs=[pl.BlockSpec((tm, tf), lambda i, f: (i, f)),
                   pl.BlockSpec((tm, D), lambda i, f: (i, 0)),
                   pl.BlockSpec((tm, D), lambda i, f: (i, 0)),
                   pl.BlockSpec((8, D), lambda i, f: (0, 0))],
        out_shape=[jax.ShapeDtypeStruct((T, F), BF16), jax.ShapeDtypeStruct((T, D), F32),
                   jax.ShapeDtypeStruct((T, D), BF16), jax.ShapeDtypeStruct((8, D), F32)],
        scratch_shapes=[pltpu.VMEM((tm, D), F32), pltpu.VMEM((tm, D), F32)],
        args=(dy, r, g, z, w1, w2))


def mm_tn(a, b, *, tm, tk, tn, relu2=False, name):
    T, K = a.shape
    G, _, N = b.shape
    nt = T // tm

    def body(a_ref, b_ref, o_ref, acc_ref):
        t = pl.program_id(3)

        @pl.when(t == 0)
        def _():
            acc_ref[...] = jnp.zeros_like(acc_ref)

        av = a_ref[...]
        if relu2:
            av = jnp.square(jnp.maximum(av.astype(F32), 0.0))
        acc_ref[...] += _dot_tn(av.astype(BF16), b_ref[...])

        @pl.when(t == nt - 1)
        def _():
            o_ref[...] = acc_ref[...].astype(BF16)

    return _call(
        body, name=name, grid=(G, K // tk, N // tn, nt),
        in_specs=[pl.BlockSpec((tm, tk), lambda g, k, n, t: (t, k)),
                  pl.BlockSpec((None, tm, tn), lambda g, k, n, t: (g, t, n))],
        out_specs=[pl.BlockSpec((None, None, tk, tn), lambda g, k, n, t: (g, n, k, 0))],
        out_shape=[jax.ShapeDtypeStruct((G, N // tn, K, tn), BF16)],
        scratch_shapes=[pltpu.VMEM((tk, tn), F32)], args=(a, b))[0]


def mm_nt_acc(dy, w, base, *, tm, name, comm=None):
    G, T, N = dy.shape
    K = w.shape[1]

    def body(dy_ref, w_ref, base_ref, o_ref):
        g = pl.program_id(1)

        @pl.when(g == 0)
        def _():
            o_ref[...] = ALPHA * base_ref[...]

        o_ref[...] += _dot_nt(dy_ref[...], w_ref[...])

    res, got = carried_call(
        body, comm, name=name, grid=(T // tm, G),
        in_specs=[pl.BlockSpec((None, tm, N), lambda i, g: (g, i, 0)),
                  pl.BlockSpec((None, K, N), lambda i, g: (g, 0, 0)),
                  pl.BlockSpec((tm, K), lambda i, g: (i, 0))],
        out_specs=[pl.BlockSpec((tm, K), lambda i, g: (i, 0))],
        out_shape=[jax.ShapeDtypeStruct((T, K), F32)], scratch_shapes=[], args=(dy, w, base))
    return res[0], got


def _split3(x):
    x1 = x.astype(BF16)
    r1 = x - x1.astype(F32)
    x2 = r1.astype(BF16)
    x3 = (r1 - x2.astype(F32)).astype(BF16)
    return x1, x2, x3


def _tri_dot(tri, x):
    x1, x2, x3 = _split3(x)
    return _dot(tri, x1) + _dot(tri, x2) + _dot(tri, x3)


def _gates(pq, fz, lb):
    sg = _sigmoid(fz)
    f = lb + (1.0 - lb) * sg
    logf = jnp.log(jnp.maximum(f, GATE_EPS))
    sq = _sigmoid(pq)
    return pq * sq, 1.0 - f, logf, f, sg, sq


def _staggered(gens):
    live = []
    waiting = list(gens)
    for gen in waiting:
        next(gen)
    while live or waiting:
        if waiting:
            live.append(waiting.pop(0))
        nxt = []
        for gen in live:
            try:
                next(gen)
                nxt.append(gen)
            except StopIteration:
                pass
        live = nxt


def _butterfly(ys, combine):
    span = 4
    while len(ys) > 1:
        ys = [combine(u, v, span) for u, v in zip(ys[0::2], ys[1::2])]
        span //= 2
    return ys[0]


def _rows_of_sums(xs):
    lands = _butterfly([[j] * 8 for j in range(8)],
                       lambda u, v, span: [u[r] if (r // span) % 2 else v[r] for r in range(8)])
    src = [None] * 8
    for r in range(8):
        src[lands[r]] = xs[r]
    row = lax.broadcasted_iota(jnp.int32, xs[0].shape, 0)

    def combine(u, v, span):
        return jnp.where((row // span) % 2 == 1, u + pltpu.roll(u, span, 0), v + pltpu.roll(v, 8 - span, 0))

    return _butterfly(src, combine)


def _block_diag_mask():
    ri = lax.broadcasted_iota(jnp.int32, (PAIR * HEAD, PAIR * HEAD), 0) // HEAD
    ci = lax.broadcasted_iota(jnp.int32, (PAIR * HEAD, PAIR * HEAD), 1) // HEAD
    return ri == ci


def _fill_off_diagonal(q_s, k_s, b_s, lhs, rhs):
    for i in range(1, CHUNK // SUB):
        lo = i * SUB
        ref = b_s[lo - 1:lo, :]
        qt = (q_s[lo:lo + SUB, :] * jnp.exp(b_s[lo:lo + SUB, :] - ref)).astype(BF16)
        kt = (k_s[0:lo, :] * jnp.exp(ref - b_s[0:lo, :])).astype(BF16)
        for h in range(PAIR):
            hl = slice(h * HEAD, (h + 1) * HEAD)
            lhs[h, lo:lo + SUB, (i - 1) * HEAD:i * HEAD] = qt[:, hl]
            rhs[h, 0:lo, (i - 1) * HEAD:i * HEAD] = kt[:, hl]


def hgrn_fwd(proj, lb, norm_g, *, rb, name, comm=None):
    _, T, D = proj.shape
    H = D // HEAD
    nb = T // rb
    nck = rb // CHUNK
    nsub = CHUNK // SUB
    W = PAIR * HEAD
    fam = CHUNK * SUB
    ntc = min(2 * TRIP_CHUNKS, nck)

    def body(pq_ref, fz_ref, pv_ref, pg_ref, lb_ref, ng_ref, o_ref, og_ref, st_ref, S, q_a, k_a, v_a, b_a, lhs_a, rhs_a,
             p_a):
        @pl.when(pl.program_id(1) == 0)
        def _():
            S[...] = jnp.zeros_like(S)

        @pl.when((pl.program_id(0) == 0) & (pl.program_id(1) == 0))
        def _():
            lhs_a[...] = jnp.zeros_like(lhs_a)
            rhs_a[...] = jnp.zeros_like(rhs_a)

        ri = lax.broadcasted_iota(jnp.int32, (CHUNK, CHUNK), 0)
        ci = lax.broadcasted_iota(jnp.int32, (CHUNK, CHUNK), 1)
        tri = (ci <= ri).astype(BF16)
        ones = jnp.ones((HEAD, HEAD), BF16)
        bd = _block_diag_mask()

        def chunk_stages(j, c):
            q_s, k_s, v_s, b_s, lhs, rhs, p_s = (r.at[j] for r in (q_a, k_a, v_a, b_a, lhs_a, rhs_a, p_a))
            base = c * CHUNK
            rows = pl.ds(pl.multiple_of(base, CHUNK), CHUNK)
            q, k, logf, _, _, _ = _gates(pq_ref[rows, :], fz_ref[rows, :], lb_ref[...])
            v = pv_ref[rows, :]
            b = _tri_dot(tri, logf)
            yield
            q_s[...] = q
            k_s[...] = k
            v_s[...] = v
            b_s[...] = b
            bl = b_s[CHUNK - 1:CHUNK, :]
            upd = _dot_tn(v.astype(BF16), (k * jnp.exp(bl - b)).astype(BF16))
            Sv = S[...]
            for h in range(PAIR):
                st_ref[h, c] = Sv[h * HEAD:(h + 1) * HEAD, h * HEAD:(h + 1) * HEAD]
            o_int = _dot_nt((q * jnp.exp(b)).astype(BF16), Sv.astype(BF16))
            S[...] = Sv * jnp.exp(bl) + jnp.where(bd, upd, 0.0)
            _fill_off_diagonal(q_s, k_s, b_s, lhs, rhs)
            a = [_dot_nt(lhs[h], rhs[h]) for h in range(PAIR)]
            yield

            def diag_products(blocks):
                for i in blocks:
                    lo = i * SUB
                    for s in range(SUB):
                        m = lax.broadcasted_iota(jnp.int32, (SUB, W), 0) >= s
                        at = (i * SUB + s) * SUB
                        e = jnp.exp(b_s[lo:lo + SUB, :] - b_s[lo + s:lo + s + 1, :])
                        p = jnp.where(m, q_s[lo:lo + SUB, :] * (k_s[lo + s:lo + s + 1, :] * e), 0.0).astype(BF16)
                        for h in range(PAIR):
                            p_s[h * fam + at:h * fam + at + SUB, :] = p[:, h * HEAD:(h + 1) * HEAD]

            diag_products(range(0, nsub // 2))
            yield
            off = [_dot(a[h].astype(BF16), v_s[:, h * HEAD:(h + 1) * HEAD].astype(BF16)) for h in range(PAIR)]
            diag_products(range(nsub // 2, nsub))
            yield
            rs = _dot(p_s[...], ones)
            yield
            for i in range(nsub):
                lo = i * SUB
                blk = pl.ds(pl.multiple_of(base + lo, SUB), SUB)
                for h in range(PAIR):
                    hl = slice(h * HEAD, (h + 1) * HEAD)
                    acc = o_int[lo:lo + SUB, hl] + off[h][lo:lo + SUB, :]
                    for s in range(SUB):
                        at = h * fam + (i * SUB + s) * SUB
                        acc = acc + rs[at:at + SUB, :] * v_s[lo + s:lo + s + 1, hl]
                    o_ref[blk, hl] = acc
                    rinv = lax.rsqrt(jnp.mean(acc * acc, axis=-1, keepdims=True) + RMS_EPS)
                    pg = pg_ref[blk, hl]
                    og_ref[blk, hl] = (acc * rinv * ng_ref[:, hl] * (pg * _sigmoid(pg))).astype(BF16)
            yield

        def trip(g, carry):
            _staggered([chunk_stages(j, g * ntc + j) for j in range(ntc)])
            return carry

        lax.fori_loop(0, nck // ntc, trip, 0)

    def grp(gi):
        return pl.BlockSpec((None, rb, W), lambda h, r: (gi, r, h))

    vec = pl.BlockSpec((1, W), lambda h, r: (0, h))
    return carried_call(
        body, comm, name=name, grid=(H // PAIR, nb),
        in_specs=[grp(0), grp(1), grp(2), grp(3), vec, vec],
        out_specs=[pl.BlockSpec((rb, W), lambda h, r: (r, h)),
                   pl.BlockSpec((rb, W), lambda h, r: (r, h)),
                   pl.BlockSpec((PAIR, nck, HEAD, HEAD), lambda h, r: (h, r, 0, 0))],
        out_shape=[jax.ShapeDtypeStruct((T, D), F32), jax.ShapeDtypeStruct((T, D), BF16),
                   jax.ShapeDtypeStruct((H, T // CHUNK, HEAD, HEAD), F32)],
        scratch_shapes=[pltpu.VMEM((W, W), F32)] + [pltpu.VMEM((ntc, CHUNK, W), F32)] * 4
        + [pltpu.VMEM((ntc, PAIR, CHUNK, (nsub - 1) * HEAD), BF16)] * 2 + [pltpu.VMEM((ntc, PAIR * fam, HEAD), BF16)],
        args=(proj, proj, proj, proj, lb, norm_g))


def hgrn_bwd(proj, o, dog, states, lb, norm_g, *, rb, name, comm=None):
    _, T, D = proj.shape
    H = D // HEAD
    nb = T // rb
    nck = rb // CHUNK
    nsub = CHUNK // SUB
    W = PAIR * HEAD
    fam = CHUNK * SUB
    ntc = min(TRIP_CHUNKS, nck)

    def body(pq_ref, fz_ref, pv_ref, pg_ref, o_ref, dog_ref, st_ref, lb_ref, ng_ref, dp_ref, s_ref,
             dS, *per_chunk):
        S0_a, lhs_a, rhs_a = per_chunk[0], per_chunk[10], per_chunk[11]

        @pl.when(pl.program_id(1) == 0)
        def _():
            dS[...] = jnp.zeros_like(dS)
            s_ref[...] = jnp.zeros_like(s_ref)

        @pl.when((pl.program_id(0) == 0) & (pl.program_id(1) == 0))
        def _():
            lhs_a[...] = jnp.zeros_like(lhs_a)
            rhs_a[...] = jnp.zeros_like(rhs_a)
            S0_a[...] = jnp.zeros_like(S0_a)

        ri = lax.broadcasted_iota(jnp.int32, (CHUNK, CHUNK), 0)
        ci = lax.broadcasted_iota(jnp.int32, (CHUNK, CHUNK), 1)
        tri = (ci <= ri).astype(BF16)
        triu = (ci >= ri).astype(BF16)
        below = (ri // SUB) > (ci // SUB)
        ones = jnp.ones((HEAD, HEAD), BF16)
        bd = _block_diag_mask()
        last_row = lax.broadcasted_iota(jnp.int32, (CHUNK, W), 0) == CHUNK - 1

        def chunk_stages(j, c):
            (S0, q_s, k_s, v_s, b_s, do_s, dq_s, dk_s, dv_s, cr_s, lhs, rhs, ke_s, qe_s, p_s) = (r.at[j] for r in per_chunk)
            rows = pl.ds(pl.multiple_of(c * CHUNK, CHUNK), CHUNK)
            lb_ = lb_ref[...]
            pq = pq_ref[rows, :]
            q, k, logf, f, sg, sq = _gates(pq, fz_ref[rows, :], lb_)
            v = pv_ref[rows, :]
            b = _tri_dot(tri, logf)
            dpg = []
            for h in range(PAIR):
                hl = slice(h * HEAD, (h + 1) * HEAD)
                oh = o_ref[rows, hl]
                dog_ = dog_ref[rows, hl]
                pg = pg_ref[rows, hl]
                ng = ng_ref[:, hl]
                spg = _sigmoid(pg)
                rinv = lax.rsqrt(jnp.mean(oh * oh, axis=-1, keepdims=True) + RMS_EPS)
                on = oh * rinv
                dpg.append(dog_ * (on * ng) * (spg * (1.0 + pg * (1.0 - spg))))
                don = dog_ * (pg * spg)
                s_ref[0:1, hl] += jnp.sum(don * on, axis=0, keepdims=True)
                dxn = don * ng
                do_s[:, hl] = rinv * (dxn - on * jnp.mean(dxn * on, axis=-1, keepdims=True))
                S0[hl, hl] = st_ref[h, c]
            yield
            q_s[...] = q
            k_s[...] = k
            v_s[...] = v
            b_s[...] = b
            do = do_s[...]
            dob = do.astype(BF16)
            vb = v.astype(BF16)
            eb = jnp.exp(b)
            bl = b_s[CHUNK - 1:CHUNK, :]
            ebl = jnp.exp(bl)
            ekk = jnp.exp(bl - b)
            upd = _dot_tn(dob, (q * eb).astype(BF16))
            S0v = S0[...]
            dSv = dS[...]
            dSb = dSv.astype(BF16)
            dq_s[...] = _dot(dob, S0v.astype(BF16)) * eb
            dk_state = _dot(vb, dSb) * ekk
            dk_s[...] = dk_state
            dv_s[...] = _dot_nt((k * ekk).astype(BF16), dSb)
            extra = jnp.sum(k * dk_state, axis=0, keepdims=True) + ebl * jnp.sum(S0v * dSv, axis=0, keepdims=True)
            dS[...] = dSv * ebl + jnp.where(bd, upd, 0.0)

            _fill_off_diagonal(q_s, k_s, b_s, lhs, rhs)
            at = [_dot_nt(rhs[h], lhs[h]) for h in range(PAIR)]
            daf = [jnp.where(below, _dot_nt(dob[:, h * HEAD:(h + 1) * HEAD], vb[:, h * HEAD:(h + 1) * HEAD]), 0.0)
                   for h in range(PAIR)]
            yield

            def diag_products(blocks):
                for i in blocks:
                    lo = i * SUB
                    for s in range(SUB):
                        m = lax.broadcasted_iota(jnp.int32, (SUB, W), 0) >= s
                        at_ = (i * SUB + s) * SUB
                        qi = q_s[lo:lo + SUB, :]
                        e = jnp.where(m, jnp.exp(b_s[lo:lo + SUB, :] - b_s[lo + s:lo + s + 1, :]), 0.0)
                        ke = k_s[lo + s:lo + s + 1, :] * e
                        ke_s[at_:at_ + SUB, :] = ke
                        qe_s[at_:at_ + SUB, :] = qi * e
                        pa = (qi * ke).astype(BF16)
                        pd = jnp.where(m, do_s[lo:lo + SUB, :] * v_s[lo + s:lo + s + 1, :], 0.0).astype(BF16)
                        for h in range(PAIR):
                            hl = slice(h * HEAD, (h + 1) * HEAD)
                            p_s[(2 * h) * fam + at_:(2 * h) * fam + at_ + SUB, :] = pa[:, hl]
                            p_s[(2 * h + 1) * fam + at_:(2 * h + 1) * fam + at_ + SUB, :] = pd[:, hl]

            diag_products(range(0, nsub // 2))
            yield
            dqb, dkb = [], []
            for h in range(PAIR):
                hl = slice(h * HEAD, (h + 1) * HEAD)
                dv_s[:, hl] += _dot(at[h].astype(BF16), dob[:, hl])
                dqb.append(_dot(daf[h].astype(BF16), rhs[h]))
                dkb.append(_dot(daf[h].T.astype(BF16), lhs[h]))
            diag_products(range(nsub // 2, nsub))
            yield
            rs = _dot(p_s[...], ones)
            cr_s[...] = jnp.zeros_like(cr_s)
            for i in range(1, nsub):
                lo = i * SUB
                ref = b_s[lo - 1:lo, :]
                eq = jnp.exp(b_s[lo:lo + SUB, :] - ref)
                ek = jnp.exp(ref - b_s[0:lo, :])
                qtf = q_s[lo:lo + SUB, :] * eq
                ktf = k_s[0:lo, :] * ek
                cb = slice((i - 1) * HEAD, i * HEAD)
                for h in range(PAIR):
                    hl = slice(h * HEAD, (h + 1) * HEAD)
                    dqi = dqb[h][lo:lo + SUB, cb]
                    dki = dkb[h][0:lo, cb]
                    dq_s[lo:lo + SUB, hl] += dqi * eq[:, hl]
                    dk_s[0:lo, hl] += dki * ek[:, hl]
                    cr_s[lo:lo + SUB, hl] += (lhs[h, lo:lo + SUB, cb].astype(F32) - qtf[:, hl]) * dqi
                    cr_s[0:lo, hl] -= (rhs[h, 0:lo, cb].astype(F32) - ktf[:, hl]) * dki
            yield
            for i in range(nsub):
                lo = i * SUB
                for h in range(PAIR):
                    hl = slice(h * HEAD, (h + 1) * HEAD)
                    doi = do_s[lo:lo + SUB, hl]
                    dqa = dq_s[lo:lo + SUB, hl]
                    xk, xv = [], []
                    for s in range(SUB):
                        at = (i * SUB + s) * SUB
                        acol = rs[(2 * h) * fam + at:(2 * h) * fam + at + SUB, :]
                        dacol = rs[(2 * h + 1) * fam + at:(2 * h + 1) * fam + at + SUB, :]
                        dqa = dqa + dacol * ke_s[at:at + SUB, hl]
                        pk = dacol * qe_s[at:at + SUB, hl]
                        pv = acol * doi
                        xk.append(pk[0:8, :] + pk[8:SUB, :])
                        xv.append(pv[0:8, :] + pv[8:SUB, :])
                    dq_s[lo:lo + SUB, hl] = dqa
                    for g8 in range(SUB // 8):
                        r8 = slice(lo + 8 * g8, lo + 8 * g8 + 8)
                        dk_s[r8, hl] += _rows_of_sums(xk[8 * g8:8 * g8 + 8])
                        dv_s[r8, hl] += _rows_of_sums(xv[8 * g8:8 * g8 + 8])

            dq = dq_s[...]
            dk = dk_s[...]
            db = q * dq - k * dk + cr_s[...] + jnp.where(last_row, extra, 0.0)
            dlogf = _tri_dot(triu, db)
            df = jnp.where(f > GATE_EPS, dlogf / jnp.maximum(f, GATE_EPS), 0.0) - dk
            s_ref[1:2, :] += jnp.sum(df * (1.0 - sg), axis=0, keepdims=True)
            dp_ref[0, rows, :] = (dq * (sq * (1.0 + pq * (1.0 - sq)))).astype(BF16)
            dp_ref[1, rows, :] = (df * (1.0 - lb_) * sg * (1.0 - sg)).astype(BF16)
            dp_ref[2, rows, :] = dv_s[...].astype(BF16)
            for h in range(PAIR):
                dp_ref[3, rows, h * HEAD:(h + 1) * HEAD] = dpg[h].astype(BF16)
            yield

        def trip(g, carry):
            _staggered([chunk_stages(j, nck - 1 - (g * ntc + j)) for j in range(ntc)])
            return carry

        lax.fori_loop(0, nck // ntc, trip, 0)

    def grp(gi):
        return pl.BlockSpec((None, rb, W), lambda h, r: (gi, nb - 1 - r, h))

    rowsp = pl.BlockSpec((rb, W), lambda h, r: (nb - 1 - r, h))
    vec = pl.BlockSpec((1, W), lambda h, r: (0, h))
    return carried_call(
        body, comm, name=name, grid=(H // PAIR, nb),
        in_specs=[grp(0), grp(1), grp(2), grp(3), rowsp, rowsp,
                  pl.BlockSpec((PAIR, nck, HEAD, HEAD), lambda h, r: (h, nb - 1 - r, 0, 0)), vec, vec],
        out_specs=[pl.BlockSpec((4, rb, W), lambda h, r: (0, nb - 1 - r, h)),
                   pl.BlockSpec((8, W), lambda h, r: (0, h))],
        out_shape=[jax.ShapeDtypeStruct((4, T, D), BF16), jax.ShapeDtypeStruct((8, D), F32)],
        scratch_shapes=[pltpu.VMEM((W, W), F32), pltpu.VMEM((ntc, W, W), F32)] + [pltpu.VMEM((ntc, CHUNK, W), F32)] * 9
        + [pltpu.VMEM((ntc, PAIR, CHUNK, (nsub - 1) * HEAD), BF16)] * 2 + [pltpu.VMEM((ntc, fam, W), F32)] * 2
        + [pltpu.VMEM((ntc, 2 * PAIR * fam, HEAD), BF16)],
        args=(proj, proj, proj, proj, o, dog, states, lb, norm_g))


def lb_fwd(logits):
    def body(l_ref, o_ref):
        l = l_ref[...]
        mx = jnp.max(l, axis=0, keepdims=True)
        e = jnp.exp(l - mx)
        sm = e / jnp.sum(e, axis=0, keepdims=True)
        o_ref[0:1, :] = jnp.zeros_like(sm[0:1, :])
        o_ref[1:2, :] = sm[1:2, :]

    return pl.pallas_call(body, name="lb_fwd", out_shape=jax.ShapeDtypeStruct(logits.shape, F32))(logits)


def lb_bwd(logits, dlb):
    def body(l_ref, d_ref, o_ref):
        l = l_ref[...]
        mx = jnp.max(l, axis=0, keepdims=True)
        e = jnp.exp(l - mx)
        sm = e / jnp.sum(e, axis=0, keepdims=True)
        inner = d_ref[1:2, :] * sm[1:2, :]
        o_ref[0:1, :] = sm[0:1, :] * (0.0 - inner)
        o_ref[1:2, :] = sm[1:2, :] * (d_ref[1:2, :] - inner)

    return pl.pallas_call(body, name="lb_bwd", out_shape=jax.ShapeDtypeStruct(logits.shape, F32))(logits, dlb)


def _shifted_copies(sh, rows):
    for b in range(1, 8):
        sh[b, 0:rows, :] = sh[0, b:b + rows, :]


def conv_fwd(u, vec, *, tm, name, comm=None):
    _, T, D = u.shape
    hb = tm // HALO
    nlc = D // HEAD

    def body(a_ref, gt_ref, ap_ref, gp_ref, w_ref, c_ref, v_ref, sh):
        i = pl.program_id(0)
        sh[0, HALO:HALO + tm, :] = a_ref[...] * _sigmoid(gt_ref[...])
        prev = ap_ref[...] * _sigmoid(gp_ref[...])
        sh[0, 0:HALO, :] = jnp.where(i > 0, prev, 0.0)
        _shifted_copies(sh, tm + HALO - 8)

        def rowblock(r, carry):
            r0 = r * CONV_RB
            for cl in range(nlc):
                ls = slice(cl * HEAD, (cl + 1) * HEAD)
                acc = jnp.zeros((CONV_RB, HEAD), F32) + w_ref[32:33, ls]
                for j in range(CONV_W):
                    o = j + 2
                    at = pl.ds(pl.multiple_of(r0 + o - o % 8, 8), CONV_RB)
                    acc = acc + w_ref[j:j + 1, ls] * sh[o % 8, at, ls]
                c_ref[pl.ds(pl.multiple_of(r0, CONV_RB), CONV_RB), ls] = acc
            return carry

        lax.fori_loop(0, tm // CONV_RB, rowblock, 0)
        xhat, _ = _ln_stats(c_ref[...])
        y = xhat * w_ref[33:34, :] + w_ref[34:35, :]
        v_ref[...] = (y * _sigmoid(y)).astype(BF16)

    cur = lambda gi: pl.BlockSpec((None, tm, D), lambda i: (gi, i, 0))
    prv = lambda gi: pl.BlockSpec((None, HALO, D), lambda i: (gi, jnp.maximum(i * hb - 1, 0), 0))
    return carried_call(
        body, comm, name=name, grid=(T // tm,),
        in_specs=[cur(0), cur(1), prv(0), prv(1), pl.BlockSpec((VEC_ROWS, D), lambda i: (0, 0))],
        out_specs=[pl.BlockSpec((tm, D), lambda i: (i, 0)), pl.BlockSpec((tm, D), lambda i: (i, 0))],
        out_shape=[jax.ShapeDtypeStruct((T, D), F32), jax.ShapeDtypeStruct((T, D), BF16)],
        scratch_shapes=[pltpu.VMEM((8, tm + HALO, D), F32)],
        args=(u, u, u, u, vec))


def conv_bwd(dv2, c, u, vec, *, tm, name, comm=None):
    _, T, D = u.shape
    hb = tm // HALO
    nt = T // tm
    nh = T // HALO
    nlc = D // HEAD
    acc_rows = {j: j for j in range(CONV_W)}
    acc_rows.update({36: CONV_W, 37: CONV_W + 1})

    def body(dv_ref, c_ref, dvn_ref, cn_ref, a_ref, gt_ref, ap_ref, gp_ref, w_ref, du_ref, s_ref, gsh, dsh, part):
        i = pl.program_id(0)

        @pl.when(i == 0)
        def _():
            s_ref[...] = jnp.zeros_like(s_ref)
            part[...] = jnp.zeros_like(part)

        gam = w_ref[33:34, :]
        bet = w_ref[34:35, :]

        def dconv(dv, cc):
            xhat, rstd = _ln_stats(cc)
            y = xhat * gam + bet
            sy = _sigmoid(y)
            dy = dv * (sy * (1.0 + y * (1.0 - sy)))
            return _ln_bwd(dy, xhat, rstd, gam), dy, xhat

        dc, dy, xhat = dconv(dv_ref[...], c_ref[...])
        dcn, _, _ = dconv(dvn_ref[...], cn_ref[...])
        dsh[0, 0:tm, :] = dc
        dsh[0, tm:tm + HALO, :] = jnp.where(i < nt - 1, dcn, 0.0)
        gsh[0, HALO:HALO + tm, :] = a_ref[...] * _sigmoid(gt_ref[...])
        gsh[0, 0:HALO, :] = jnp.where(i > 0, ap_ref[...] * _sigmoid(gp_ref[...]), 0.0)
        s_ref[32:33, :] += jnp.sum(dc, axis=0, keepdims=True)
        s_ref[33:34, :] += jnp.sum(dy * xhat, axis=0, keepdims=True)
        s_ref[34:35, :] += jnp.sum(dy, axis=0, keepdims=True)
        _shifted_copies(dsh, tm + HALO - 8)
        _shifted_copies(gsh, tm + HALO - 8)

        def fold8(x):
            acc = x[0:8, :]
            for g in range(1, CONV_RB // 8):
                acc = acc + x[8 * g:8 * g + 8, :]
            return acc

        for cl in range(nlc):
            ls = slice(cl * HEAD, (cl + 1) * HEAD)

            def rowblock(r, sums, ls=ls):
                r0 = r * CONV_RB
                rows = pl.ds(pl.multiple_of(r0, CONV_RB), CONV_RB)
                dcb = dsh[0, rows, ls]
                dglu = jnp.zeros((CONV_RB, HEAD), F32)
                new = []
                for j in range(CONV_W):
                    od = 30 - j
                    og = j + 2
                    atd = pl.ds(pl.multiple_of(r0 + od - od % 8, 8), CONV_RB)
                    atg = pl.ds(pl.multiple_of(r0 + og - og % 8, 8), CONV_RB)
                    dglu = dglu + w_ref[j:j + 1, ls] * dsh[od % 8, atd, ls]
                    new.append(sums[j] + fold8(dcb * gsh[og % 8, atg, ls]))
                a = a_ref[rows, ls]
                sgt = _sigmoid(gt_ref[rows, ls])
                da = (dglu * sgt).astype(BF16)
                dg = (dglu * a * sgt * (1.0 - sgt)).astype(BF16)
                du_ref[0, rows, ls] = da
                du_ref[1, rows, ls] = dg
                new.append(sums[CONV_W] + fold8(da.astype(F32)))
                new.append(sums[CONV_W + 1] + fold8(dg.astype(F32)))
                return tuple(new)

            zero = jnp.zeros((8, HEAD), F32)
            sums = lax.fori_loop(0, tm // CONV_RB, rowblock, (zero,) * (CONV_W + 2))
            for k in range(CONV_W + 2):
                part[8 * k:8 * k + 8, ls] += sums[k]

        @pl.when(i == nt - 1)
        def _():
            for row, k in acc_rows.items():
                s_ref[row:row + 1, :] = jnp.sum(part[8 * k:8 * k + 8, :], axis=0, keepdims=True)

    row = lambda i: (i, 0)
    nxt = lambda i: (jnp.minimum((i + 1) * hb, nh - 1), 0)
    cur = lambda gi: pl.BlockSpec((None, tm, D), lambda i: (gi, i, 0))
    prv = lambda gi: pl.BlockSpec((None, HALO, D), lambda i: (gi, jnp.maximum(i * hb - 1, 0), 0))
    fix = lambda i: (0, 0)
    return carried_call(
        body, comm, name=name, grid=(nt,),
        in_specs=[pl.BlockSpec((tm, D), row), pl.BlockSpec((tm, D), row),
                  pl.BlockSpec((HALO, D), nxt), pl.BlockSpec((HALO, D), nxt),
                  cur(0), cur(1), prv(0), prv(1), pl.BlockSpec((VEC_ROWS, D), fix)],
        out_specs=[pl.BlockSpec((2, tm, D), lambda i: (0, i, 0)), pl.BlockSpec((VEC_ROWS, D), fix)],
        out_shape=[jax.ShapeDtypeStruct((2, T, D), BF16), jax.ShapeDtypeStruct((VEC_ROWS, D), F32)],
        scratch_shapes=[pltpu.VMEM((8, tm + HALO, D), F32), pltpu.VMEM((8, tm + HALO, D), F32),
                        pltpu.VMEM((8 * (CONV_W + 2), D), F32)],
        args=(dv2, c, dv2, c, u, u, u, u, vec))


def loss_grad(y, target, *, tm):
    T, D = y.shape
    nt = T // tm

    def body(y_ref, t_ref, l_ref, d_ref, acc):
        i = pl.program_id(0)

        @pl.when(i == 0)
        def _():
            acc[...] = jnp.zeros_like(acc)

        e = y_ref[...] - t_ref[...]
        d_ref[...] = e * (1.0 / D)
        acc[...] += jnp.sum(e * e, axis=0, keepdims=True)

        @pl.when(i == nt - 1)
        def _():
            l_ref[...] = 0.5 * jnp.sum(acc[...], axis=1, keepdims=True) * (1.0 / D)

    row = lambda i: (i, 0)
    return _call(
        body, name="loss_grad", grid=(nt,),
        in_specs=[pl.BlockSpec((tm, D), row), pl.BlockSpec((tm, D), row)],
        out_specs=[pl.BlockSpec((1, 1), lambda i: (0, 0)), pl.BlockSpec((tm, D), row)],
        out_shape=[jax.ShapeDtypeStruct((1, 1), F32), jax.ShapeDtypeStruct((T, D), F32)],
        scratch_shapes=[pltpu.VMEM((1, D), F32)], args=(y, target))


def _rows_block(R, C, budget=1 << 20):
    tr = R
    while tr * C * 4 > budget and tr % 32 == 0:
        tr //= 2
    return tr


def adamw(w, g, m, v, *, name):
    R, C = w.shape
    tr = _rows_block(R, C)

    def body(w_ref, g_ref, m_ref, v_ref, d_ref, mo_ref, vo_ref):
        g_ = g_ref[...]
        mn = ADAM_B1 * m_ref[...] + (1.0 - ADAM_B1) * g_
        vn = ADAM_B2 * v_ref[...] + (1.0 - ADAM_B2) * jnp.square(g_)
        m_hat = mn / (1.0 - ADAM_B1 ** ADAM_STEP)
        v_hat = vn / (1.0 - ADAM_B2 ** ADAM_STEP)
        d_ref[...] = -ADAM_LR * (m_hat / (jnp.sqrt(v_hat) + ADAM_EPS) + ADAM_WD * w_ref[...])
        mo_ref[...] = mn
        vo_ref[...] = vn

    spec = pl.BlockSpec((tr, C), lambda i: (i, 0))
    sd = jax.ShapeDtypeStruct((R, C), F32)
    return _call(body, name=name, grid=(R // tr,), in_specs=[spec] * 4, out_specs=[spec] * 3, out_shape=[sd] * 3,
                 args=(w, g, m, v))


def sum_slots(slots, *, name):
    _, R, C = slots.shape
    tr = _rows_block(R, C, budget=1 << 19)

    def body(s_ref, o_ref):
        acc = s_ref[0].astype(F32)
        for d in range(1, 8):
            acc = acc + s_ref[d].astype(F32)
        o_ref[...] = acc

    return _call(body, name=name, grid=(R // tr,), in_specs=[pl.BlockSpec((8, tr, C), lambda i: (0, i, 0))],
                 out_specs=[pl.BlockSpec((tr, C), lambda i: (i, 0))], out_shape=[jax.ShapeDtypeStruct((R, C), F32)],
                 args=(slots,))[0]


def _adam_nd(w, g, m, v, name):
    shp = w.shape
    c = shp[-1]
    f2 = lambda a: a.reshape(-1, c)
    d, mn, vn = adamw(f2(w), f2(g), f2(m), f2(v), name=name)
    return d.reshape(shp), mn.reshape(shp), vn.reshape(shp)


def _reduced(slots, name):
    out = []
    for s in slots:
        c = s.shape[-1]
        out.append(sum_slots(s.reshape(8, -1, c), name=name).reshape(s.shape[1:]))
    return out


def kernel(x, ln_mix_g, ln_mix_b, ln_ffn_g, ln_ffn_b, ffn_w1, ffn_w2, a_w_in, a_lb_logits, a_norm_g, a_w_out, b_w_pw1, b_b_pw1, b_w_dw, b_b_dw, b_ln_g, b_ln_b, b_w_pw2, b_b_pw2, loss_target, m_ln_mix_g, m_ln_mix_b, m_ln_ffn_g, m_ln_ffn_b, m_ffn_w1, m_ffn_w2, m_a_w_in, m_a_lb_logits, m_a_norm_g, m_a_w_out, m_b_w_pw1, m_b_b_pw1, m_b_w_dw, m_b_b_dw, m_b_ln_g, m_b_ln_b, m_b_w_pw2, m_b_b_pw2, v_ln_mix_g, v_ln_mix_b, v_ln_ffn_g, v_ln_ffn_b, v_ffn_w1, v_ffn_w2, v_a_w_in, v_a_lb_logits, v_a_norm_g, v_a_w_out, v_b_w_pw1, v_b_b_pw1, v_b_w_dw, v_b_b_dw, v_b_ln_g, v_b_ln_b, v_b_w_pw2, v_b_b_pw2):
    names = ["ln_mix_g", "ln_mix_b", "ln_ffn_g", "ln_ffn_b", "ffn_w1", "ffn_w2", "a_w_in", "a_lb_logits", "a_norm_g",
             "a_w_out", "b_w_pw1", "b_b_pw1", "b_w_dw", "b_b_dw", "b_ln_g", "b_ln_b", "b_w_pw2", "b_b_pw2"]
    w = dict(zip(names, [ln_mix_g, ln_mix_b, ln_ffn_g, ln_ffn_b, ffn_w1, ffn_w2, a_w_in, a_lb_logits, a_norm_g, a_w_out,
                         b_w_pw1, b_b_pw1, b_w_dw, b_b_dw, b_ln_g, b_ln_b, b_w_pw2, b_b_pw2]))
    m = dict(zip(names, [m_ln_mix_g, m_ln_mix_b, m_ln_ffn_g, m_ln_ffn_b, m_ffn_w1, m_ffn_w2, m_a_w_in, m_a_lb_logits,
                         m_a_norm_g, m_a_w_out, m_b_w_pw1, m_b_b_pw1, m_b_w_dw, m_b_b_dw, m_b_ln_g, m_b_ln_b, m_b_w_pw2,
                         m_b_b_pw2]))
    v = dict(zip(names, [v_ln_mix_g, v_ln_mix_b, v_ln_ffn_g, v_ln_ffn_b, v_ffn_w1, v_ffn_w2, v_a_w_in, v_a_lb_logits,
                         v_a_norm_g, v_a_w_out, v_b_w_pw1, v_b_b_pw1, v_b_w_dw, v_b_b_dw, v_b_ln_g, v_b_ln_b, v_b_w_pw2,
                         v_b_b_pw2]))
    T, D = x.shape[1], x.shape[2]
    DS = D // 4
    F = 4 * ffn_w1.shape[2]
    chip = 2 * lax.axis_index("x") + lax.axis_index("y")
    tm = min(T, 512)
    tmw = min(T, 1024)
    tmc = min(T, 256)
    rb = min(T, 1024)
    tf = min(F // 4, 1024)
    xin, target = x[0], loss_target[0]

    def mix_shards(i):
        j = i // 2
        if i % 2 == 0:
            return [a_w_in[j].astype(BF16), a_w_out[j].astype(BF16)]
        vec = jnp.concatenate([b_w_dw[j], jnp.zeros((1, DS), F32), b_b_dw[j][None], b_ln_g[j][None], b_ln_b[j][None],
                               b_b_pw2[j][None], b_b_pw1[j].reshape(2, DS), jnp.zeros((2, DS), F32)], axis=0)
        return [b_w_pw1[j].astype(BF16), b_w_pw2[j].astype(BF16), vec]

    def ffn_shards(i):
        return [ffn_w1[i].astype(BF16), ffn_w2[i].astype(BF16)]

    def mix_weights(i, got):
        if i % 2 == 0:
            return {"w_in": got[0], "w_out": got[1].reshape(D, D)}
        pw1 = jnp.transpose(got[0].reshape(2, 2, D, D // 2), (0, 2, 1, 3)).reshape(2, D, D)
        vec = jnp.transpose(got[2], (1, 0, 2)).reshape(VEC_ROWS, D)
        return {"pw1": pw1, "pw2": got[1].reshape(D, D), "vec": vec,
                "b_pw1": got[2][:, 36:38, :].reshape(2, 1, D)}

    lb_all = lb_fwd(a_lb_logits)
    zeros_bias = jnp.zeros((1, D), F32)

    first = mix_shards(0)
    mixw = {"w_in": comm_call(GatherChips(first[:1]), name="gather_first")[0]}
    saved = []
    h, hb = xin, xin.astype(BF16)
    for i in range(DEPTH):
        j = i // 2
        s = {"xb": hb, "mixw": mixw}
        gf = GatherChips(ffn_shards(i) + (first[1:] if i == 0 else []))
        if i % 2 == 0:
            s["proj"] = mm_groups(hb, mixw["w_in"], jnp.zeros((4, 1, D), F32), tm=tmw, name="a_in_proj")
            (s["o"], s["og"], s["st"]), got = hgrn_fwd(s["proj"], lb_all[j:j + 1], a_norm_g[j:j + 1], rb=rb,
                                                       name="hgrn_fwd", comm=gf)
            if i == 0:
                mixw["w_out"] = got[2].reshape(D, D)
            s["r1"], x1, s["x1b"] = mm_res_ln(s["og"], mixw["w_out"], zeros_bias, h, ln_mix_g[i:i + 1],
                                              ln_mix_b[i:i + 1], tm=tm, name="a_out_ln")
        else:
            s["u"] = mm_groups(hb, mixw["pw1"], mixw["b_pw1"], tm=tmw, name="b_pw1")
            (s["c"], s["v2"]), got = conv_fwd(s["u"], mixw["vec"], tm=min(T, 2 * tmc), name="conv_fwd", comm=gf)
            s["r1"], x1, s["x1b"] = mm_res_ln(s["v2"], mixw["pw2"], mixw["vec"][35:36], h, ln_mix_g[i:i + 1],
                                              ln_mix_b[i:i + 1], tm=tm, name="b_pw2_ln")
        s["w1"], s["w2"] = got[0], got[1].reshape(F, D)
        gm = GatherChips(mix_shards(i + 1)) if i + 1 < DEPTH else None
        (s["z"], s["r2"], h, hb), got = ffn_fwd(x1, s["w1"], s["w2"], ln_ffn_g[i:i + 1], ln_ffn_b[i:i + 1],
                                                tm=tm, tf=tf, name="ffn_fwd", comm=gm)
        if gm is not None:
            mixw = mix_weights(i + 1, got)
        saved.append(s)

    loss_part, dh = loss_grad(h, target, tm=tm)
    loss = lax.psum(loss_part[0, 0], ("x", "y", "c"))

    gr = {k: [None] * DEPTH for k in ("ln_mix_g", "ln_mix_b", "ln_ffn_g", "ln_ffn_b", "ffn_w1", "ffn_w2")}
    for k in ("a_w_in", "a_w_out", "a_norm_g", "a_dlb", "b_w_pw1", "b_w_pw2", "b_vec", "b_b_pw2"):
        gr[k] = [None] * 2
    w_in_name = ("a_w_in", "b_w_pw1")
    w_out_name = ("a_w_out", "b_w_pw2")
    pending = None

    for i in reversed(range(DEPTH)):
        j = i // 2
        s = saved[i]
        mixw = s["mixw"]
        sm = ScatterPieces([pending[1]]) if pending is not None else None
        (dz, dx1, drb2, sums2), slots = ffn_bwd_dx(dh, s["r2"], ln_ffn_g[i:i + 1], s["z"], s["w1"], s["w2"],
                                                   tm=tm, tf=tf, name="ffn_bwd_dx", comm=sm)
        if pending is not None:
            gr[w_in_name[pending[0] % 2]][pending[0] // 2] = _reduced(slots, "sum_mix_grads")[0]
        gr["ln_ffn_g"][i], gr["ln_ffn_b"][i] = sums2[0], sums2[1]
        dw1 = mm_tn(s["x1b"], dz[None], tm=tmw, tk=D, tn=F // 4, name="ffn_dw1")[0]
        dw2 = mm_tn(s["z"], drb2[None], tm=tmw, tk=F // 4, tn=D, relu2=True, name="ffn_dw2")[0, 0]
        wmix = mixw["w_out"] if i % 2 == 0 else mixw["pw2"]
        dr1, drb1, dmo, sums1 = ln_bwd_mm(dx1, s["r1"], ln_mix_g[i:i + 1], wmix, tm=tm, name="mix_ln_bwd")
        gr["ln_mix_g"][i], gr["ln_mix_b"][i] = sums1[0], sums1[1]
        if i % 2 == 0:
            dwo = mm_tn(s["og"], drb1[None], tm=tmw, tk=D, tn=D, name="a_dw_out")[0, 0].reshape(4, DS, D)
            sf = ScatterPieces([dw1, dw2.reshape(4, F // 4, D), dwo])
            (dproj, hs), slots = hgrn_bwd(s["proj"], s["o"], dmo, s["st"], lb_all[j:j + 1], a_norm_g[j:j + 1], rb=rb,
                                          name="hgrn_bwd", comm=sf)
            gr["a_norm_g"][j], gr["a_dlb"][j] = hs[0], hs[1]
            dwi = mm_tn(s["xb"], dproj, tm=tmw, tk=D, tn=D, name="a_dw_in")[:, 0]
            dy_in, w_in_t, dx_name = dproj, mixw["w_in"], "a_dx"
        else:
            dwo = mm_tn(s["v2"], drb1[None], tm=tmw, tk=D, tn=D, name="b_dw_pw2")[0, 0].reshape(4, DS, D)
            sf = ScatterPieces([dw1, dw2.reshape(4, F // 4, D), dwo])
            (du, cs), slots = conv_bwd(dmo, s["c"], s["u"], mixw["vec"], tm=tmc, name="conv_bwd", comm=sf)
            gr["b_vec"][j], gr["b_b_pw2"][j] = cs, sums1[2]
            dwi = mm_tn(s["xb"], du, tm=tmw, tk=D, tn=D // 2, name="b_dw_pw1").reshape(4, D, D // 2)
            dy_in, w_in_t, dx_name = du, mixw["pw1"], "b_dx"
        gr["ffn_w1"][i], gr["ffn_w2"][i], gr[w_out_name[i % 2]][j] = _reduced(slots, "sum_ffn_grads")
        dh, slots = mm_nt_acc(dy_in, w_in_t, dr1, tm=tmw, name=dx_name, comm=ScatterPieces([dwi]) if i == 0 else None)
        if i == 0:
            gr[w_in_name[0]][0] = _reduced(slots, "sum_mix_grads")[0]
        pending = (i, dwi)
    grad_x = dh[None]

    small = {k: jnp.stack(gr[k]) for k in ("ln_mix_g", "ln_mix_b", "ln_ffn_g", "ln_ffn_b", "a_norm_g", "b_vec", "b_b_pw2")}
    small["a_lb_logits"] = lb_bwd(a_lb_logits, jnp.stack(gr["a_dlb"]))
    small_names = ["ln_mix_g", "ln_mix_b", "ln_ffn_g", "ln_ffn_b", "a_lb_logits", "a_norm_g", "b_b_pw2", "b_vec"]
    rows = [small[k].reshape(-1, D) for k in small_names]
    counts = [r.shape[0] for r in rows]
    rows = [jnp.pad(r, ((0, (-r.shape[0]) % 8), (0, 0))) for r in rows]
    summed = all_reduce_small(jnp.concatenate(rows, axis=0))
    sm = {}
    off = 0
    for k, n, r in zip(small_names, counts, rows):
        sm[k] = summed[off:off + n]
        off += r.shape[0]
    bvec = sm["b_vec"].reshape(2, VEC_ROWS, D)

    def shard_cols(a):
        return lax.dynamic_slice_in_dim(a, chip * DS, DS, axis=a.ndim - 1)

    grads = {k: jnp.stack(gr[k]) for k in ("ffn_w1", "ffn_w2", "a_w_in", "a_w_out", "b_w_pw1", "b_w_pw2")}
    for k in ("ln_mix_g", "ln_mix_b", "ln_ffn_g", "ln_ffn_b", "a_lb_logits", "a_norm_g"):
        grads[k] = sm[k]
    grads["b_b_pw1"] = lax.dynamic_slice_in_dim(bvec[:, 36:38, :].reshape(2, 2 * D), chip * (D // 2), D // 2, axis=1)
    grads["b_w_dw"] = shard_cols(bvec[:, 0:CONV_W, :])
    grads["b_b_dw"] = shard_cols(bvec[:, 32, :])
    grads["b_ln_g"] = shard_cols(bvec[:, 33, :])
    grads["b_ln_b"] = shard_cols(bvec[:, 34, :])
    grads["b_b_pw2"] = shard_cols(sm["b_b_pw2"])

    delta, new_m, new_v = {}, {}, {}
    for k in names:
        delta[k], new_m[k], new_v[k] = _adam_nd(w[k], grads[k], m[k], v[k], "adamw_" + k)
    return (loss, grad_x, *[grads[k] for k in names], *[delta[k] for k in names],
            *[new_m[k] for k in names], *[new_v[k] for k in names])
```

```python
import jax
import jax.numpy as jnp
from jax import lax
from jax.experimental import pallas as pl
from jax.experimental.pallas import tpu as pltpu

F32 = jnp.float32
BF16 = jnp.bfloat16
MESH = pl.DeviceIdType.MESH

DEPTH = 4
ALPHA = (2.0 * DEPTH) ** 0.25
LN_EPS = 1e-5
RMS_EPS = 1e-6
GATE_EPS = 1e-6
HEAD = 128
CHUNK = 128
SUB = 16
PAIR = 2
TRIP_CHUNKS = 2
CONV_W = 31
HALO = 32
VEC_ROWS = 40
CONV_RB = 32
ADAM_LR, ADAM_B1, ADAM_B2, ADAM_EPS, ADAM_WD, ADAM_STEP = 0.001, 0.9, 0.999, 1e-08, 0.01, 10
VMEM_LIMIT = 56 * 1024 * 1024
ANY = pl.BlockSpec(memory_space=pl.ANY)


def _dot(a, b):
    return jnp.dot(a, b, preferred_element_type=F32)


def _dot_nt(a, b):
    return lax.dot_general(a, b, (((1,), (1,)), ((), ())), preferred_element_type=F32)


def _dot_tn(a, b):
    return lax.dot_general(a, b, (((0,), (0,)), ((), ())), preferred_element_type=F32)


def _sigmoid(x):
    return 1.0 / (1.0 + jnp.exp(-x))


def _ln_stats(r):
    mu = jnp.mean(r, axis=-1, keepdims=True)
    xc = r - mu
    var = jnp.mean(xc * xc, axis=-1, keepdims=True)
    rstd = lax.rsqrt(var + LN_EPS)
    return xc * rstd, rstd


def _ln_bwd(dy, xhat, rstd, g):
    dyg = dy * g
    m1 = jnp.mean(dyg, axis=-1, keepdims=True)
    m2 = jnp.mean(dyg * xhat, axis=-1, keepdims=True)
    return rstd * (dyg - m1 - xhat * m2)


def _place():
    return lax.axis_index("x"), lax.axis_index("y"), lax.axis_index("c")


class GatherChips:
    def __init__(self, arrs):
        self.ins = list(arrs)
        n = len(arrs)
        self.out_shapes = [jax.ShapeDtypeStruct((4,) + a.shape, a.dtype) for a in arrs]
        self.sems = [pltpu.SemaphoreType.DMA((3 * n,)), pltpu.SemaphoreType.DMA((3 * n,)),
                     pltpu.SemaphoreType.DMA((n,))]

    def copies(self, ins, outs, send, recv, loc):
        x, y, c = _place()
        me = 2 * x + y
        local, remote = [], []
        for a in range(len(ins)):
            local.append(pltpu.make_async_copy(ins[a], outs[a].at[me], loc.at[a]))
            for j, (px, py) in enumerate([(1 - x, y), (x, 1 - y), (1 - x, 1 - y)]):
                remote.append(pltpu.make_async_remote_copy(
                    src_ref=ins[a], dst_ref=outs[a].at[me], send_sem=send.at[3 * a + j], recv_sem=recv.at[3 * a + j],
                    device_id=(px, py, c), device_id_type=MESH))
        return local + remote


class ScatterPieces:
    def __init__(self, arrs):
        self.ins = list(arrs)
        n = len(arrs)
        self.out_shapes = [jax.ShapeDtypeStruct((8,) + a.shape[1:], a.dtype) for a in arrs]
        self.sems = [pltpu.SemaphoreType.DMA((7 * n,)), pltpu.SemaphoreType.DMA((7 * n,)),
                     pltpu.SemaphoreType.DMA((n,))]

    def copies(self, ins, outs, send, recv, loc):
        x, y, c = _place()
        me = 4 * x + 2 * y + c
        local, remote = [], []
        for a in range(len(ins)):
            local.append(pltpu.make_async_copy(ins[a].at[2 * x + y], outs[a].at[me], loc.at[a]))
            k = 0
            for fx in (0, 1):
                for fy in (0, 1):
                    for fc in (0, 1):
                        if fx or fy or fc:
                            tx, ty = x ^ fx, y ^ fy
                            remote.append(pltpu.make_async_remote_copy(
                                src_ref=ins[a].at[2 * tx + ty], dst_ref=outs[a].at[me],
                                send_sem=send.at[7 * a + k], recv_sem=recv.at[7 * a + k],
                                device_id=(tx, ty, c ^ fc), device_id_type=MESH))
                            k += 1
        return local + remote


def carried_call(body, comm, *, name, grid, in_specs, out_specs, out_shape, scratch_shapes, args):
    sem = ("arbitrary",) * len(grid)
    params = pltpu.CompilerParams(dimension_semantics=sem, vmem_limit_bytes=VMEM_LIMIT)
    if comm is None:
        res = pl.pallas_call(body, name=name, grid=grid, in_specs=in_specs, out_specs=out_specs, out_shape=out_shape,
                             scratch_shapes=scratch_shapes, compiler_params=params)(*args)
        return res, []
    ni, no, nscr = len(in_specs), len(out_specs), len(scratch_shapes)
    ci, co = len(comm.ins), len(comm.out_shapes)

    def both(*refs):
        ins, refs = refs[:ni], refs[ni:]
        cins, refs = refs[:ci], refs[ci:]
        outs, refs = refs[:no], refs[no:]
        couts, refs = refs[:co], refs[co:]
        scr, sems = refs[:nscr], refs[nscr:]
        first = pl.program_id(0) == 0
        last = pl.program_id(0) == grid[0] - 1
        for d in range(1, len(grid)):
            first = first & (pl.program_id(d) == 0)
            last = last & (pl.program_id(d) == grid[d] - 1)

        @pl.when(first)
        def _():
            for cp in comm.copies(cins, couts, *sems):
                cp.start()

        body(*ins, *outs, *scr)

        @pl.when(last)
        def _():
            for cp in comm.copies(cins, couts, *sems):
                cp.wait()

    res = pl.pallas_call(
        both, name=name, grid=grid, in_specs=list(in_specs) + [ANY] * ci, out_specs=list(out_specs) + [ANY] * co,
        out_shape=list(out_shape) + comm.out_shapes, scratch_shapes=list(scratch_shapes) + comm.sems,
        compiler_params=params)(*args, *comm.ins)
    return res[:no], res[no:]


def gather_chips_via_sibling(shard):
    half = shard.shape[0] // 2

    def body(in_ref, out_ref, send1, recv1, send2, recv2, loc):
        x, y, c = _place()
        me = 2 * x + y
        mine = pl.ds(c * half, half)
        local = pltpu.make_async_copy(in_ref, out_ref.at[me], loc)
        local.start()
        chips = [(1 - x, y), (x, 1 - y), (1 - x, 1 - y)]
        first = [pltpu.make_async_remote_copy(
            src_ref=in_ref.at[mine], dst_ref=out_ref.at[me, mine], send_sem=send1.at[j], recv_sem=recv1.at[j],
            device_id=(px, py, c), device_id_type=MESH) for j, (px, py) in enumerate(chips)]
        for cp in first:
            cp.start()
        passed = []
        for j, (px, py) in enumerate(chips):
            first[j].wait_recv()
            got = out_ref.at[2 * px + py, mine]
            cp = pltpu.make_async_remote_copy(src_ref=got, dst_ref=got, send_sem=send2.at[j], recv_sem=recv2.at[j],
                                              device_id=(x, y, 1 - c), device_id_type=MESH)
            cp.start()
            passed.append(cp)
        for cp in first:
            cp.wait_send()
        for cp in passed:
            cp.wait()
        local.wait()

    return pl.pallas_call(
        body, name="gather_first", in_specs=[ANY], out_specs=ANY,
        out_shape=jax.ShapeDtypeStruct((4,) + shard.shape, shard.dtype),
        scratch_shapes=[pltpu.SemaphoreType.DMA((3,))] * 4 + [pltpu.SemaphoreType.DMA],
        compiler_params=pltpu.CompilerParams(has_side_effects=True))(shard)


def all_reduce_small(v):
    R, C = v.shape

    def body(v_ref, o_ref, slots, send, recv):
        x, y, c = _place()
        me = 4 * x + 2 * y + c
        slots[me] = v_ref[...]
        cps = []
        k = 0
        for fx in (0, 1):
            for fy in (0, 1):
                for fc in (0, 1):
                    if fx or fy or fc:
                        cps.append(pltpu.make_async_remote_copy(
                            src_ref=v_ref, dst_ref=slots.at[me], send_sem=send.at[k], recv_sem=recv.at[k],
                            device_id=(x ^ fx, y ^ fy, c ^ fc), device_id_type=MESH))
                        k += 1
        for cp in cps:
            cp.start()
        for cp in cps:
            cp.wait()
        acc = slots[0]
        for d in range(1, 8):
            acc = acc + slots[d]
        o_ref[...] = acc

    vm = pl.BlockSpec(memory_space=pltpu.VMEM)
    return pl.pallas_call(
        body, name="all_reduce_small", in_specs=[vm], out_specs=vm,
        out_shape=jax.ShapeDtypeStruct((R, C), F32),
        scratch_shapes=[pltpu.VMEM((8, R, C), F32), pltpu.SemaphoreType.DMA((7,)), pltpu.SemaphoreType.DMA((7,))],
        compiler_params=pltpu.CompilerParams(has_side_effects=True, vmem_limit_bytes=VMEM_LIMIT),
    )(v)


def _call(body, *, name, grid, in_specs, out_specs, out_shape, scratch_shapes=(), args):
    res, _ = carried_call(body, None, name=name, grid=grid, in_specs=in_specs, out_specs=out_specs,
                          out_shape=out_shape, scratch_shapes=list(scratch_shapes), args=args)
    return res


def mm_groups(a, w, bias, *, tm, name):
    T, K = a.shape
    G, _, N = w.shape

    def body(a_ref, w_ref, b_ref, o_ref):
        o_ref[...] = _dot(a_ref[...], w_ref[...]) + b_ref[...]

    return _call(
        body, name=name, grid=(G, T // tm),
        in_specs=[pl.BlockSpec((tm, K), lambda g, i: (i, 0)),
                  pl.BlockSpec((None, K, N), lambda g, i: (g, 0, 0)),
                  pl.BlockSpec((None, 1, N), lambda g, i: (g, 0, 0))],
        out_specs=[pl.BlockSpec((None, tm, N), lambda g, i: (g, i, 0))],
        out_shape=[jax.ShapeDtypeStruct((G, T, N), F32)], args=(a, w, bias))[0]


def mm_res_ln(a, w, bias, res, g, b, *, tm, name):
    T, K = a.shape
    N = w.shape[1]

    def body(a_ref, w_ref, bias_ref, res_ref, g_ref, b_ref, r_ref, y_ref, yb_ref):
        r = ALPHA * res_ref[...] + _dot(a_ref[...], w_ref[...]) + bias_ref[...]
        r_ref[...] = r
        xhat, _ = _ln_stats(r)
        y = xhat * g_ref[...] + b_ref[...]
        y_ref[...] = y
        yb_ref[...] = y.astype(BF16)

    row = lambda i: (i, 0)
    fix = lambda i: (0, 0)
    return _call(
        body, name=name, grid=(T // tm,),
        in_specs=[pl.BlockSpec((tm, K), row), pl.BlockSpec((K, N), fix), pl.BlockSpec((1, N), fix),
                  pl.BlockSpec((tm, N), row), pl.BlockSpec((1, N), fix), pl.BlockSpec((1, N), fix)],
        out_specs=[pl.BlockSpec((tm, N), row), pl.BlockSpec((tm, N), row), pl.BlockSpec((tm, N), row)],
        out_shape=[jax.ShapeDtypeStruct((T, N), F32), jax.ShapeDtypeStruct((T, N), F32),
                   jax.ShapeDtypeStruct((T, N), BF16)],
        args=(a, w, bias, res, g, b))


def ffn_fwd(x, w1, w2, g, b, *, tm, tf, name, comm=None):
    T, D = x.shape
    NC, _, FC = w1.shape
    F = NC * FC
    per = FC // tf
    nf = F // tf

    def body(x_ref, w1_ref, w2_ref, g_ref, b_ref, z_ref, r_ref, y_ref, yb_ref, acc_ref, xb_ref):
        f = pl.program_id(1)

        @pl.when(f == 0)
        def _():
            acc_ref[...] = jnp.zeros_like(acc_ref)
            xb_ref[...] = x_ref[...].astype(BF16)

        z = _dot(xb_ref[...], w1_ref[...])
        z_ref[...] = z.astype(BF16)
        h = jnp.square(jnp.maximum(z, 0.0)).astype(BF16)
        acc_ref[...] += _dot(h, w2_ref[...])

        @pl.when(f == nf - 1)
        def _():
            r = ALPHA * x_ref[...] + acc_ref[...]
            r_ref[...] = r
            xhat, _ = _ln_stats(r)
            y = xhat * g_ref[...] + b_ref[...]
            y_ref[...] = y
            yb_ref[...] = y.astype(BF16)

    return carried_call(
        body, comm, name=name, grid=(T // tm, nf),
        in_specs=[pl.BlockSpec((tm, D), lambda i, f: (i, 0)),
                  pl.BlockSpec((None, D, tf), lambda i, f: (f // per, 0, f % per)),
                  pl.BlockSpec((tf, D), lambda i, f: (f, 0)),
                  pl.BlockSpec((1, D), lambda i, f: (0, 0)),
                  pl.BlockSpec((1, D), lambda i, f: (0, 0))],
        out_specs=[pl.BlockSpec((tm, tf), lambda i, f: (i, f)),
                   pl.BlockSpec((tm, D), lambda i, f: (i, 0)),
                   pl.BlockSpec((tm, D), lambda i, f: (i, 0)),
                   pl.BlockSpec((tm, D), lambda i, f: (i, 0))],
        out_shape=[jax.ShapeDtypeStruct((T, F), BF16), jax.ShapeDtypeStruct((T, D), F32),
                   jax.ShapeDtypeStruct((T, D), F32), jax.ShapeDtypeStruct((T, D), BF16)],
        scratch_shapes=[pltpu.VMEM((tm, D), F32), pltpu.VMEM((tm, D), BF16)],
        args=(x, w1, w2, g, b))


def ln_bwd_mm(dy, r, g, w, *, tm, name):
    T, N = dy.shape
    Ko = w.shape[0]

    def body(dy_ref, r_ref, g_ref, w_ref, dr_ref, drb_ref, o_ref, s_ref):
        @pl.when(pl.program_id(0) == 0)
        def _():
            s_ref[...] = jnp.zeros_like(s_ref)

        dy_ = dy_ref[...]
        xhat, rstd = _ln_stats(r_ref[...])
        dr = _ln_bwd(dy_, xhat, rstd, g_ref[...])
        dr_ref[...] = dr
        drb = dr.astype(BF16)
        drb_ref[...] = drb
        o_ref[...] = _dot_nt(drb, w_ref[...])
        s_ref[0:1, :] += jnp.sum(dy_ * xhat, axis=0, keepdims=True)
        s_ref[1:2, :] += jnp.sum(dy_, axis=0, keepdims=True)
        s_ref[2:3, :] += jnp.sum(dr, axis=0, keepdims=True)

    row = lambda i: (i, 0)
    fix = lambda i: (0, 0)
    return _call(
        body, name=name, grid=(T // tm,),
        in_specs=[pl.BlockSpec((tm, N), row), pl.BlockSpec((tm, N), row), pl.BlockSpec((1, N), fix),
                  pl.BlockSpec((Ko, N), fix)],
        out_specs=[pl.BlockSpec((tm, N), row), pl.BlockSpec((tm, N), row), pl.BlockSpec((tm, Ko), row),
                   pl.BlockSpec((8, N), fix)],
        out_shape=[jax.ShapeDtypeStruct((T, N), F32), jax.ShapeDtypeStruct((T, N), BF16),
                   jax.ShapeDtypeStruct((T, Ko), F32), jax.ShapeDtypeStruct((8, N), F32)],
        args=(dy, r, g, w))


def ffn_bwd_dx(dy, r, g, z, w1, w2, *, tm, tf, name, comm=None):
    T, D = dy.shape
    NC, _, FC = w1.shape
    F = NC * FC
    per = FC // tf
    nf = F // tf

    def body(dy_ref, r_ref, g_ref, z_ref, w1_ref, w2_ref, dz_ref, dx_ref, drb_ref, s_ref, dr_scr, acc_ref):
        i = pl.program_id(0)
        f = pl.program_id(1)

        @pl.when((i == 0) & (f == 0))
        def _():
            s_ref[...] = jnp.zeros_like(s_ref)

        @pl.when(f == 0)
        def _():
            dy_ = dy_ref[...]
            xhat, rstd = _ln_stats(r_ref[...])
            dr = _ln_bwd(dy_, xhat, rstd, g_ref[...])
            dr_scr[...] = dr
            drb_ref[...] = dr.astype(BF16)
            acc_ref[...] = jnp.zeros_like(acc_ref)
            s_ref[0:1, :] += jnp.sum(dy_ * xhat, axis=0, keepdims=True)
            s_ref[1:2, :] += jnp.sum(dy_, axis=0, keepdims=True)

        dh = _dot_nt(drb_ref[...], w2_ref[...])
        dz = (dh * (2.0 * jnp.maximum(z_ref[...].astype(F32), 0.0))).astype(BF16)
        dz_ref[...] = dz
        acc_ref[...] += _dot_nt(dz, w1_ref[...])

        @pl.when(f == nf - 1)
        def _():
            dx_ref[...] = ALPHA * dr_scr[...] + acc_ref[...]

    return carried_call(
        body, comm, name=name, grid=(T // tm, nf),
        in_specs=[pl.BlockSpec((tm, D), lambda i, f: (i, 0)),
                  pl.BlockSpec((tm, D), lambda i, f: (i, 0)),
                  pl.BlockSpec((1, D), lambda i, f: (0, 0)),
                  pl.BlockSpec((tm, tf), lambda i, f: (i, f)),
                  pl.BlockSpec((None, D, tf), lambda i, f: (f // per, 0, f % per)),
                  pl.BlockSpec((tf, D), lambda i, f: (f, 0))],
        out_specs=[pl.BlockSpec((tm, tf), lambda i, f: (i, f)),
                   pl.BlockSpec((tm, D), lambda i, f: (i, 0)),
                   pl.BlockSpec((tm, D), lambda i, f: (i, 0)),
                   pl.BlockSpec((8, D), lambda i, f: (0, 0))],
        out_shape=[jax.ShapeDtypeStruct((T, F), BF16), jax.ShapeDtypeStruct((T, D), F32),
                   jax.ShapeDtypeStruct((T, D), BF16), jax.ShapeDtypeStruct((8, D), F32)],
        scratch_shapes=[pltpu.VMEM((tm, D), F32), pltpu.VMEM((tm, D), F32)],
        args=(dy, r, g, z, w1, w2))


def mm_tn(a, b, *, tm, tk, tn, relu2=False, name):
    T, K = a.shape
    G, _, N = b.shape
    nt = T // tm

    def body(a_ref, b_ref, o_ref, acc_ref):
        t = pl.program_id(3)

        @pl.when(t == 0)
        def _():
            acc_ref[...] = jnp.zeros_like(acc_ref)

        av = a_ref[...]
        if relu2:
            av = jnp.square(jnp.maximum(av.astype(F32), 0.0))
        acc_ref[...] += _dot_tn(av.astype(BF16), b_ref[...])

        @pl.when(t == nt - 1)
        def _():
            o_ref[...] = acc_ref[...].astype(BF16)

    return _call(
        body, name=name, grid=(G, K // tk, N // tn, nt),
        in_specs=[pl.BlockSpec((tm, tk), lambda g, k, n, t: (t, k)),
                  pl.BlockSpec((None, tm, tn), lambda g, k, n, t: (g, t, n))],
        out_specs=[pl.BlockSpec((None, None, tk, tn), lambda g, k, n, t: (g, n, k, 0))],
        out_shape=[jax.ShapeDtypeStruct((G, N // tn, K, tn), BF16)],
        scratch_shapes=[pltpu.VMEM((tk, tn), F32)], args=(a, b))[0]


def mm_nt_acc(dy, w, base, *, tm, name, comm=None):
    G, T, N = dy.shape
    K = w.shape[1]

    def body(dy_ref, w_ref, base_ref, o_ref):
        g = pl.program_id(1)

        @pl.when(g == 0)
        def _():
            o_ref[...] = ALPHA * base_ref[...]

        o_ref[...] += _dot_nt(dy_ref[...], w_ref[...])

    res, got = carried_call(
        body, comm, name=name, grid=(T // tm, G),
        in_specs=[pl.BlockSpec((None, tm, N), lambda i, g: (g, i, 0)),
                  pl.BlockSpec((None, K, N), lambda i, g: (g, 0, 0)),
                  pl.BlockSpec((tm, K), lambda i, g: (i, 0))],
        out_specs=[pl.BlockSpec((tm, K), lambda i, g: (i, 0))],
        out_shape=[jax.ShapeDtypeStruct((T, K), F32)], scratch_shapes=[], args=(dy, w, base))
    return res[0], got


def _split3(x):
    x1 = x.astype(BF16)
    r1 = x - x1.astype(F32)
    x2 = r1.astype(BF16)
    x3 = (r1 - x2.astype(F32)).astype(BF16)
    return x1, x2, x3


def _tri_dot(tri, x):
    x1, x2, x3 = _split3(x)
    return _dot(tri, x1) + _dot(tri, x2) + _dot(tri, x3)


def _gates(pq, fz, lb):
    sg = _sigmoid(fz)
    f = lb + (1.0 - lb) * sg
    logf = jnp.log(jnp.maximum(f, GATE_EPS))
    sq = _sigmoid(pq)
    return pq * sq, 1.0 - f, logf, f, sg, sq


def _staggered(gens):
    live = []
    waiting = list(gens)
    for gen in waiting:
        next(gen)
    while live or waiting:
        if waiting:
            live.append(waiting.pop(0))
        nxt = []
        for gen in live:
            try:
                next(gen)
                nxt.append(gen)
            except StopIteration:
                pass
        live = nxt


def _butterfly(ys, combine):
    span = 4
    while len(ys) > 1:
        ys = [combine(u, v, span) for u, v in zip(ys[0::2], ys[1::2])]
        span //= 2
    return ys[0]


def _rows_of_sums(xs):
    lands = _butterfly([[j] * 8 for j in range(8)],
                       lambda u, v, span: [u[r] if (r // span) % 2 else v[r] for r in range(8)])
    src = [None] * 8
    for r in range(8):
        src[lands[r]] = xs[r]
    row = lax.broadcasted_iota(jnp.int32, xs[0].shape, 0)

    def combine(u, v, span):
        return jnp.where((row // span) % 2 == 1, u + pltpu.roll(u, span, 0), v + pltpu.roll(v, 8 - span, 0))

    return _butterfly(src, combine)


def _block_diag_mask():
    ri = lax.broadcasted_iota(jnp.int32, (PAIR * HEAD, PAIR * HEAD), 0) // HEAD
    ci = lax.broadcasted_iota(jnp.int32, (PAIR * HEAD, PAIR * HEAD), 1) // HEAD
    return ri == ci


def _fill_off_diagonal(q_s, k_s, b_s, lhs, rhs):
    for i in range(1, CHUNK // SUB):
        lo = i * SUB
        ref = b_s[lo - 1:lo, :]
        qt = (q_s[lo:lo + SUB, :] * jnp.exp(b_s[lo:lo + SUB, :] - ref)).astype(BF16)
        kt = (k_s[0:lo, :] * jnp.exp(ref - b_s[0:lo, :])).astype(BF16)
        for h in range(PAIR):
            hl = slice(h * HEAD, (h + 1) * HEAD)
            lhs[h, lo:lo + SUB, (i - 1) * HEAD:i * HEAD] = qt[:, hl]
            rhs[h, 0:lo, (i - 1) * HEAD:i * HEAD] = kt[:, hl]


def hgrn_fwd(proj, lb, norm_g, *, rb, name, comm=None):
    _, T, D = proj.shape
    H = D // HEAD
    nb = T // rb
    nck = rb // CHUNK
    nsub = CHUNK // SUB
    W = PAIR * HEAD
    fam = CHUNK * SUB
    ntc = min(2 * TRIP_CHUNKS, nck)

    def body(pq_ref, fz_ref, pv_ref, pg_ref, lb_ref, ng_ref, o_ref, og_ref, st_ref, S, q_a, k_a, v_a, b_a, lhs_a, rhs_a,
             p_a):
        @pl.when(pl.program_id(1) == 0)
        def _():
            S[...] = jnp.zeros_like(S)

        @pl.when((pl.program_id(0) == 0) & (pl.program_id(1) == 0))
        def _():
            lhs_a[...] = jnp.zeros_like(lhs_a)
            rhs_a[...] = jnp.zeros_like(rhs_a)

        ri = lax.broadcasted_iota(jnp.int32, (CHUNK, CHUNK), 0)
        ci = lax.broadcasted_iota(jnp.int32, (CHUNK, CHUNK), 1)
        tri = (ci <= ri).astype(BF16)
        ones = jnp.ones((HEAD, HEAD), BF16)
        bd = _block_diag_mask()

        def chunk_stages(j, c):
            q_s, k_s, v_s, b_s, lhs, rhs, p_s = (r.at[j] for r in (q_a, k_a, v_a, b_a, lhs_a, rhs_a, p_a))
            base = c * CHUNK
            rows = pl.ds(pl.multiple_of(base, CHUNK), CHUNK)
            q, k, logf, _, _, _ = _gates(pq_ref[rows, :], fz_ref[rows, :], lb_ref[...])
            v = pv_ref[rows, :]
            b = _tri_dot(tri, logf)
            yield
            q_s[...] = q
            k_s[...] = k
            v_s[...] = v
            b_s[...] = b
            bl = b_s[CHUNK - 1:CHUNK, :]
            upd = _dot_tn(v.astype(BF16), (k * jnp.exp(bl - b)).astype(BF16))
            Sv = S[...]
            for h in range(PAIR):
                st_ref[h, c] = Sv[h * HEAD:(h + 1) * HEAD, h * HEAD:(h + 1) * HEAD]
            o_int = _dot_nt((q * jnp.exp(b)).astype(BF16), Sv.astype(BF16))
            S[...] = Sv * jnp.exp(bl) + jnp.where(bd, upd, 0.0)
            _fill_off_diagonal(q_s, k_s, b_s, lhs, rhs)
            a = [_dot_nt(lhs[h], rhs[h]) for h in range(PAIR)]
            yield

            def diag_products(blocks):
                for i in blocks:
                    lo = i * SUB
                    for s in range(SUB):
                        m = lax.broadcasted_iota(jnp.int32, (SUB, W), 0) >= s
                        at = (i * SUB + s) * SUB
                        e = jnp.exp(b_s[lo:lo + SUB, :] - b_s[lo + s:lo + s + 1, :])
                        p = jnp.where(m, q_s[lo:lo + SUB, :] * (k_s[lo + s:lo + s + 1, :] * e), 0.0).astype(BF16)
                        for h in range(PAIR):
                            p_s[h * fam + at:h * fam + at + SUB, :] = p[:, h * HEAD:(h + 1) * HEAD]

            diag_products(range(0, nsub // 2))
            yield
            off = [_dot(a[h].astype(BF16), v_s[:, h * HEAD:(h + 1) * HEAD].astype(BF16)) for h in range(PAIR)]
            diag_products(range(nsub // 2, nsub))
            yield
            rs = _dot(p_s[...], ones)
            yield
            for i in range(nsub):
                lo = i * SUB
                blk = pl.ds(pl.multiple_of(base + lo, SUB), SUB)
                for h in range(PAIR):
                    hl = slice(h * HEAD, (h + 1) * HEAD)
                    acc = o_int[lo:lo + SUB, hl] + off[h][lo:lo + SUB, :]
                    for s in range(SUB):
                        at = h * fam + (i * SUB + s) * SUB
                        acc = acc + rs[at:at + SUB, :] * v_s[lo + s:lo + s + 1, hl]
                    o_ref[blk, hl] = acc
                    rinv = lax.rsqrt(jnp.mean(acc * acc, axis=-1, keepdims=True) + RMS_EPS)
                    pg = pg_ref[blk, hl]
                    og_ref[blk, hl] = (acc * rinv * ng_ref[:, hl] * (pg * _sigmoid(pg))).astype(BF16)
            yield

        def trip(g, carry):
            _staggered([chunk_stages(j, g * ntc + j) for j in range(ntc)])
            return carry

        lax.fori_loop(0, nck // ntc, trip, 0)

    def grp(gi):
        return pl.BlockSpec((None, rb, W), lambda h, r: (gi, r, h))

    vec = pl.BlockSpec((1, W), lambda h, r: (0, h))
    return carried_call(
        body, comm, name=name, grid=(H // PAIR, nb),
        in_specs=[grp(0), grp(1), grp(2), grp(3), vec, vec],
        out_specs=[pl.BlockSpec((rb, W), lambda h, r: (r, h)),
                   pl.BlockSpec((rb, W), lambda h, r: (r, h)),
                   pl.BlockSpec((PAIR, nck, HEAD, HEAD), lambda h, r: (h, r, 0, 0))],
        out_shape=[jax.ShapeDtypeStruct((T, D), F32), jax.ShapeDtypeStruct((T, D), BF16),
                   jax.ShapeDtypeStruct((H, T // CHUNK, HEAD, HEAD), F32)],
        scratch_shapes=[pltpu.VMEM((W, W), F32)] + [pltpu.VMEM((ntc, CHUNK, W), F32)] * 4
        + [pltpu.VMEM((ntc, PAIR, CHUNK, (nsub - 1) * HEAD), BF16)] * 2 + [pltpu.VMEM((ntc, PAIR * fam, HEAD), BF16)],
        args=(proj, proj, proj, proj, lb, norm_g))


def hgrn_bwd(proj, o, dog, states, lb, norm_g, *, rb, name, comm=None):
    _, T, D = proj.shape
    H = D // HEAD
    nb = T // rb
    nck = rb // CHUNK
    nsub = CHUNK // SUB
    W = PAIR * HEAD
    fam = CHUNK * SUB
    ntc = min(TRIP_CHUNKS, nck)

    def body(pq_ref, fz_ref, pv_ref, pg_ref, o_ref, dog_ref, st_ref, lb_ref, ng_ref, dp_ref, s_ref,
             dS, *per_chunk):
        S0_a, lhs_a, rhs_a = per_chunk[0], per_chunk[10], per_chunk[11]

        @pl.when(pl.program_id(1) == 0)
        def _():
            dS[...] = jnp.zeros_like(dS)
            s_ref[...] = jnp.zeros_like(s_ref)

        @pl.when((pl.program_id(0) == 0) & (pl.program_id(1) == 0))
        def _():
            lhs_a[...] = jnp.zeros_like(lhs_a)
            rhs_a[...] = jnp.zeros_like(rhs_a)
            S0_a[...] = jnp.zeros_like(S0_a)

        ri = lax.broadcasted_iota(jnp.int32, (CHUNK, CHUNK), 0)
        ci = lax.broadcasted_iota(jnp.int32, (CHUNK, CHUNK), 1)
        tri = (ci <= ri).astype(BF16)
        triu = (ci >= ri).astype(BF16)
        below = (ri // SUB) > (ci // SUB)
        ones = jnp.ones((HEAD, HEAD), BF16)
        bd = _block_diag_mask()
        last_row = lax.broadcasted_iota(jnp.int32, (CHUNK, W), 0) == CHUNK - 1

        def chunk_stages(j, c):
            (S0, q_s, k_s, v_s, b_s, do_s, dq_s, dk_s, dv_s, cr_s, lhs, rhs, ke_s, qe_s, p_s) = (r.at[j] for r in per_chunk)
            rows = pl.ds(pl.multiple_of(c * CHUNK, CHUNK), CHUNK)
            lb_ = lb_ref[...]
            pq = pq_ref[rows, :]
            q, k, logf, f, sg, sq = _gates(pq, fz_ref[rows, :], lb_)
            v = pv_ref[rows, :]
            b = _tri_dot(tri, logf)
            dpg = []
            for h in range(PAIR):
                hl = slice(h * HEAD, (h + 1) * HEAD)
                oh = o_ref[rows, hl]
                dog_ = dog_ref[rows, hl]
                pg = pg_ref[rows, hl]
                ng = ng_ref[:, hl]
                spg = _sigmoid(pg)
                rinv = lax.rsqrt(jnp.mean(oh * oh, axis=-1, keepdims=True) + RMS_EPS)
                on = oh * rinv
                dpg.append(dog_ * (on * ng) * (spg * (1.0 + pg * (1.0 - spg))))
                don = dog_ * (pg * spg)
                s_ref[0:1, hl] += jnp.sum(don * on, axis=0, keepdims=True)
                dxn = don * ng
                do_s[:, hl] = rinv * (dxn - on * jnp.mean(dxn * on, axis=-1, keepdims=True))
                S0[hl, hl] = st_ref[h, c]
            yield
            q_s[...] = q
            k_s[...] = k
            v_s[...] = v
            b_s[...] = b
            do = do_s[...]
            dob = do.astype(BF16)
            vb = v.astype(BF16)
            eb = jnp.exp(b)
            bl = b_s[CHUNK - 1:CHUNK, :]
            ebl = jnp.exp(bl)
            ekk = jnp.exp(bl - b)
            upd = _dot_tn(dob, (q * eb).astype(BF16))
            S0v = S0[...]
            dSv = dS[...]
            dSb = dSv.astype(BF16)
            dq_s[...] = _dot(dob, S0v.astype(BF16)) * eb
            dk_state = _dot(vb, dSb) * ekk
            dk_s[...] = dk_state
            dv_s[...] = _dot_nt((k * ekk).astype(BF16), dSb)
            extra = jnp.sum(k * dk_state, axis=0, keepdims=True) + ebl * jnp.sum(S0v * dSv, axis=0, keepdims=True)
            dS[...] = dSv * ebl + jnp.where(bd, upd, 0.0)

            _fill_off_diagonal(q_s, k_s, b_s, lhs, rhs)
            at = [_dot_nt(rhs[h], lhs[h]) for h in range(PAIR)]
            daf = [jnp.where(below, _dot_nt(dob[:, h * HEAD:(h + 1) * HEAD], vb[:, h * HEAD:(h + 1) * HEAD]), 0.0)
                   for h in range(PAIR)]
            yield

            def diag_products(blocks):
                for i in blocks:
                    lo = i * SUB
                    for s in range(SUB):
                        m = lax.broadcasted_iota(jnp.int32, (SUB, W), 0) >= s
                        at_ = (i * SUB + s) * SUB
                        qi = q_s[lo:lo + SUB, :]
                        e = jnp.where(m, jnp.exp(b_s[lo:lo + SUB, :] - b_s[lo + s:lo + s + 1, :]), 0.0)
                        ke = k_s[lo + s:lo + s + 1, :] * e
                        ke_s[at_:at_ + SUB, :] = ke
                        qe_s[at_:at_ + SUB, :] = qi * e
                        pa = (qi * ke).astype(BF16)
                        pd = jnp.where(m, do_s[lo:lo + SUB, :] * v_s[lo + s:lo + s + 1, :], 0.0).astype(BF16)
                        for h in range(PAIR):
                            hl = slice(h * HEAD, (h + 1) * HEAD)
                            p_s[(2 * h) * fam + at_:(2 * h) * fam + at_ + SUB, :] = pa[:, hl]
                            p_s[(2 * h + 1) * fam + at_:(2 * h + 1) * fam + at_ + SUB, :] = pd[:, hl]

            diag_products(range(0, nsub // 2))
            yield
            dqb, dkb = [], []
            for h in range(PAIR):
                hl = slice(h * HEAD, (h + 1) * HEAD)
                dv_s[:, hl] += _dot(at[h].astype(BF16), dob[:, hl])
                dqb.append(_dot(daf[h].astype(BF16), rhs[h]))
                dkb.append(_dot(daf[h].T.astype(BF16), lhs[h]))
            diag_products(range(nsub // 2, nsub))
            yield
            rs = _dot(p_s[...], ones)
            cr_s[...] = jnp.zeros_like(cr_s)
            for i in range(1, nsub):
                lo = i * SUB
                ref = b_s[lo - 1:lo, :]
                eq = jnp.exp(b_s[lo:lo + SUB, :] - ref)
                ek = jnp.exp(ref - b_s[0:lo, :])
                qtf = q_s[lo:lo + SUB, :] * eq
                ktf = k_s[0:lo, :] * ek
                cb = slice((i - 1) * HEAD, i * HEAD)
                for h in range(PAIR):
                    hl = slice(h * HEAD, (h + 1) * HEAD)
                    dqi = dqb[h][lo:lo + SUB, cb]
                    dki = dkb[h][0:lo, cb]
                    dq_s[lo:lo + SUB, hl] += dqi * eq[:, hl]
                    dk_s[0:lo, hl] += dki * ek[:, hl]
                    cr_s[lo:lo + SUB, hl] += (lhs[h, lo:lo + SUB, cb].astype(F32) - qtf[:, hl]) * dqi
                    cr_s[0:lo, hl] -= (rhs[h, 0:lo, cb].astype(F32) - ktf[:, hl]) * dki
            yield
            for i in range(nsub):
                lo = i * SUB
                for h in range(PAIR):
                    hl = slice(h * HEAD, (h + 1) * HEAD)
                    doi = do_s[lo:lo + SUB, hl]
                    dqa = dq_s[lo:lo + SUB, hl]
                    xk, xv = [], []
                    for s in range(SUB):
                        at = (i * SUB + s) * SUB
                        acol = rs[(2 * h) * fam + at:(2 * h) * fam + at + SUB, :]
                        dacol = rs[(2 * h + 1) * fam + at:(2 * h + 1) * fam + at + SUB, :]
                        dqa = dqa + dacol * ke_s[at:at + SUB, hl]
                        pk = dacol * qe_s[at:at + SUB, hl]
                        pv = acol * doi
                        xk.append(pk[0:8, :] + pk[8:SUB, :])
                        xv.append(pv[0:8, :] + pv[8:SUB, :])
                    dq_s[lo:lo + SUB, hl] = dqa
                    for g8 in range(SUB // 8):
                        r8 = slice(lo + 8 * g8, lo + 8 * g8 + 8)
                        dk_s[r8, hl] += _rows_of_sums(xk[8 * g8:8 * g8 + 8])
                        dv_s[r8, hl] += _rows_of_sums(xv[8 * g8:8 * g8 + 8])

            dq = dq_s[...]
            dk = dk_s[...]
            db = q * dq - k * dk + cr_s[...] + jnp.where(last_row, extra, 0.0)
            dlogf = _tri_dot(triu, db)
            df = jnp.where(f > GATE_EPS, dlogf / jnp.maximum(f, GATE_EPS), 0.0) - dk
            s_ref[1:2, :] += jnp.sum(df * (1.0 - sg), axis=0, keepdims=True)
            dp_ref[0, rows, :] = (dq * (sq * (1.0 + pq * (1.0 - sq)))).astype(BF16)
            dp_ref[1, rows, :] = (df * (1.0 - lb_) * sg * (1.0 - sg)).astype(BF16)
            dp_ref[2, rows, :] = dv_s[...].astype(BF16)
            for h in range(PAIR):
                dp_ref[3, rows, h * HEAD:(h + 1) * HEAD] = dpg[h].astype(BF16)
            yield

        def trip(g, carry):
            _staggered([chunk_stages(j, nck - 1 - (g * ntc + j)) for j in range(ntc)])
            return carry

        lax.fori_loop(0, nck // ntc, trip, 0)

    def grp(gi):
        return pl.BlockSpec((None, rb, W), lambda h, r: (gi, nb - 1 - r, h))

    rowsp = pl.BlockSpec((rb, W), lambda h, r: (nb - 1 - r, h))
    vec = pl.BlockSpec((1, W), lambda h, r: (0, h))
    return carried_call(
        body, comm, name=name, grid=(H // PAIR, nb),
        in_specs=[grp(0), grp(1), grp(2), grp(3), rowsp, rowsp,
                  pl.BlockSpec((PAIR, nck, HEAD, HEAD), lambda h, r: (h, nb - 1 - r, 0, 0)), vec, vec],
        out_specs=[pl.BlockSpec((4, rb, W), lambda h, r: (0, nb - 1 - r, h)),
                   pl.BlockSpec((8, W), lambda h, r: (0, h))],
        out_shape=[jax.ShapeDtypeStruct((4, T, D), BF16), jax.ShapeDtypeStruct((8, D), F32)],
        scratch_shapes=[pltpu.VMEM((W, W), F32), pltpu.VMEM((ntc, W, W), F32)] + [pltpu.VMEM((ntc, CHUNK, W), F32)] * 9
        + [pltpu.VMEM((ntc, PAIR, CHUNK, (nsub - 1) * HEAD), BF16)] * 2 + [pltpu.VMEM((ntc, fam, W), F32)] * 2
        + [pltpu.VMEM((ntc, 2 * PAIR * fam, HEAD), BF16)],
        args=(proj, proj, proj, proj, o, dog, states, lb, norm_g))


def lb_fwd(logits):
    def body(l_ref, o_ref):
        l = l_ref[...]
        mx = jnp.max(l, axis=0, keepdims=True)
        e = jnp.exp(l - mx)
        sm = e / jnp.sum(e, axis=0, keepdims=True)
        o_ref[0:1, :] = jnp.zeros_like(sm[0:1, :])
        o_ref[1:2, :] = sm[1:2, :]

    return pl.pallas_call(body, name="lb_fwd", out_shape=jax.ShapeDtypeStruct(logits.shape, F32))(logits)


def lb_bwd(logits, dlb):
    def body(l_ref, d_ref, o_ref):
        l = l_ref[...]
        mx = jnp.max(l, axis=0, keepdims=True)
        e = jnp.exp(l - mx)
        sm = e / jnp.sum(e, axis=0, keepdims=True)
        inner = d_ref[1:2, :] * sm[1:2, :]
        o_ref[0:1, :] = sm[0:1, :] * (0.0 - inner)
        o_ref[1:2, :] = sm[1:2, :] * (d_ref[1:2, :] - inner)

    return pl.pallas_call(body, name="lb_bwd", out_shape=jax.ShapeDtypeStruct(logits.shape, F32))(logits, dlb)


def _shifted_copies(sh, rows):
    for b in range(1, 8):
        sh[b, 0:rows, :] = sh[0, b:b + rows, :]


def conv_fwd(u, vec, *, tm, name, comm=None):
    _, T, D = u.shape
    hb = tm // HALO
    nlc = D // HEAD

    def body(a_ref, gt_ref, ap_ref, gp_ref, w_ref, c_ref, v_ref, sh):
        i = pl.program_id(0)
        sh[0, HALO:HALO + tm, :] = a_ref[...] * _sigmoid(gt_ref[...])
        prev = ap_ref[...] * _sigmoid(gp_ref[...])
        sh[0, 0:HALO, :] = jnp.where(i > 0, prev, 0.0)
        _shifted_copies(sh, tm + HALO - 8)

        def rowblock(r, carry):
            r0 = r * CONV_RB
            for cl in range(nlc):
                ls = slice(cl * HEAD, (cl + 1) * HEAD)
                acc = jnp.zeros((CONV_RB, HEAD), F32) + w_ref[32:33, ls]
                for j in range(CONV_W):
                    o = j + 2
                    at = pl.ds(pl.multiple_of(r0 + o - o % 8, 8), CONV_RB)
                    acc = acc + w_ref[j:j + 1, ls] * sh[o % 8, at, ls]
                c_ref[pl.ds(pl.multiple_of(r0, CONV_RB), CONV_RB), ls] = acc
            return carry

        lax.fori_loop(0, tm // CONV_RB, rowblock, 0)
        xhat, _ = _ln_stats(c_ref[...])
        y = xhat * w_ref[33:34, :] + w_ref[34:35, :]
        v_ref[...] = (y * _sigmoid(y)).astype(BF16)

    cur = lambda gi: pl.BlockSpec((None, tm, D), lambda i: (gi, i, 0))
    prv = lambda gi: pl.BlockSpec((None, HALO, D), lambda i: (gi, jnp.maximum(i * hb - 1, 0), 0))
    return carried_call(
        body, comm, name=name, grid=(T // tm,),
        in_specs=[cur(0), cur(1), prv(0), prv(1), pl.BlockSpec((VEC_ROWS, D), lambda i: (0, 0))],
        out_specs=[pl.BlockSpec((tm, D), lambda i: (i, 0)), pl.BlockSpec((tm, D), lambda i: (i, 0))],
        out_shape=[jax.ShapeDtypeStruct((T, D), F32), jax.ShapeDtypeStruct((T, D), BF16)],
        scratch_shapes=[pltpu.VMEM((8, tm + HALO, D), F32)],
        args=(u, u, u, u, vec))


def conv_bwd(dv2, c, u, vec, *, tm, name, comm=None):
    _, T, D = u.shape
    hb = tm // HALO
    nt = T // tm
    nh = T // HALO
    nlc = D // HEAD
    acc_rows = {j: j for j in range(CONV_W)}
    acc_rows.update({36: CONV_W, 37: CONV_W + 1})

    def body(dv_ref, c_ref, dvn_ref, cn_ref, a_ref, gt_ref, ap_ref, gp_ref, w_ref, du_ref, s_ref, gsh, dsh, part):
        i = pl.program_id(0)

        @pl.when(i == 0)
        def _():
            s_ref[...] = jnp.zeros_like(s_ref)
            part[...] = jnp.zeros_like(part)

        gam = w_ref[33:34, :]
        bet = w_ref[34:35, :]

        def dconv(dv, cc):
            xhat, rstd = _ln_stats(cc)
            y = xhat * gam + bet
            sy = _sigmoid(y)
            dy = dv * (sy * (1.0 + y * (1.0 - sy)))
            return _ln_bwd(dy, xhat, rstd, gam), dy, xhat

        dc, dy, xhat = dconv(dv_ref[...], c_ref[...])
        dcn, _, _ = dconv(dvn_ref[...], cn_ref[...])
        dsh[0, 0:tm, :] = dc
        dsh[0, tm:tm + HALO, :] = jnp.where(i < nt - 1, dcn, 0.0)
        gsh[0, HALO:HALO + tm, :] = a_ref[...] * _sigmoid(gt_ref[...])
        gsh[0, 0:HALO, :] = jnp.where(i > 0, ap_ref[...] * _sigmoid(gp_ref[...]), 0.0)
        s_ref[32:33, :] += jnp.sum(dc, axis=0, keepdims=True)
        s_ref[33:34, :] += jnp.sum(dy * xhat, axis=0, keepdims=True)
        s_ref[34:35, :] += jnp.sum(dy, axis=0, keepdims=True)
        _shifted_copies(dsh, tm + HALO - 8)
        _shifted_copies(gsh, tm + HALO - 8)

        def fold8(x):
            acc = x[0:8, :]
            for g in range(1, CONV_RB // 8):
                acc = acc + x[8 * g:8 * g + 8, :]
            return acc

        for cl in range(nlc):
            ls = slice(cl * HEAD, (cl + 1) * HEAD)

            def rowblock(r, sums, ls=ls):
                r0 = r * CONV_RB
                rows = pl.ds(pl.multiple_of(r0, CONV_RB), CONV_RB)
                dcb = dsh[0, rows, ls]
                dglu = jnp.zeros((CONV_RB, HEAD), F32)
                new = []
                for j in range(CONV_W):
                    od = 30 - j
                    og = j + 2
                    atd = pl.ds(pl.multiple_of(r0 + od - od % 8, 8), CONV_RB)
                    atg = pl.ds(pl.multiple_of(r0 + og - og % 8, 8), CONV_RB)
                    dglu = dglu + w_ref[j:j + 1, ls] * dsh[od % 8, atd, ls]
                    new.append(sums[j] + fold8(dcb * gsh[og % 8, atg, ls]))
                a = a_ref[rows, ls]
                sgt = _sigmoid(gt_ref[rows, ls])
                da = (dglu * sgt).astype(BF16)
                dg = (dglu * a * sgt * (1.0 - sgt)).astype(BF16)
                du_ref[0, rows, ls] = da
                du_ref[1, rows, ls] = dg
                new.append(sums[CONV_W] + fold8(da.astype(F32)))
                new.append(sums[CONV_W + 1] + fold8(dg.astype(F32)))
                return tuple(new)

            zero = jnp.zeros((8, HEAD), F32)
            sums = lax.fori_loop(0, tm // CONV_RB, rowblock, (zero,) * (CONV_W + 2))
            for k in range(CONV_W + 2):
                part[8 * k:8 * k + 8, ls] += sums[k]

        @pl.when(i == nt - 1)
        def _():
            for row, k in acc_rows.items():
                s_ref[row:row + 1, :] = jnp.sum(part[8 * k:8 * k + 8, :], axis=0, keepdims=True)

    row = lambda i: (i, 0)
    nxt = lambda i: (jnp.minimum((i + 1) * hb, nh - 1), 0)
    cur = lambda gi: pl.BlockSpec((None, tm, D), lambda i: (gi, i, 0))
    prv = lambda gi: pl.BlockSpec((None, HALO, D), lambda i: (gi, jnp.maximum(i * hb - 1, 0), 0))
    fix = lambda i: (0, 0)
    return carried_call(
        body, comm, name=name, grid=(nt,),
        in_specs=[pl.BlockSpec((tm, D), row), pl.BlockSpec((tm, D), row),
                  pl.BlockSpec((HALO, D), nxt), pl.BlockSpec((HALO, D), nxt),
                  cur(0), cur(1), prv(0), prv(1), pl.BlockSpec((VEC_ROWS, D), fix)],
        out_specs=[pl.BlockSpec((2, tm, D), lambda i: (0, i, 0)), pl.BlockSpec((VEC_ROWS, D), fix)],
        out_shape=[jax.ShapeDtypeStruct((2, T, D), BF16), jax.ShapeDtypeStruct((VEC_ROWS, D), F32)],
        scratch_shapes=[pltpu.VMEM((8, tm + HALO, D), F32), pltpu.VMEM((8, tm + HALO, D), F32),
                        pltpu.VMEM((8 * (CONV_W + 2), D), F32)],
        args=(dv2, c, dv2, c, u, u, u, u, vec))


def loss_grad(y, target, *, tm):
    T, D = y.shape
    nt = T // tm

    def body(y_ref, t_ref, l_ref, d_ref, acc):
        i = pl.program_id(0)

        @pl.when(i == 0)
        def _():
            acc[...] = jnp.zeros_like(acc)

        e = y_ref[...] - t_ref[...]
        d_ref[...] = e * (1.0 / D)
        acc[...] += jnp.sum(e * e, axis=0, keepdims=True)

        @pl.when(i == nt - 1)
        def _():
            l_ref[...] = 0.5 * jnp.sum(acc[...], axis=1, keepdims=True) * (1.0 / D)

    row = lambda i: (i, 0)
    return _call(
        body, name="loss_grad", grid=(nt,),
        in_specs=[pl.BlockSpec((tm, D), row), pl.BlockSpec((tm, D), row)],
        out_specs=[pl.BlockSpec((1, 1), lambda i: (0, 0)), pl.BlockSpec((tm, D), row)],
        out_shape=[jax.ShapeDtypeStruct((1, 1), F32), jax.ShapeDtypeStruct((T, D), F32)],
        scratch_shapes=[pltpu.VMEM((1, D), F32)], args=(y, target))


def _rows_block(R, C, budget=1 << 20):
    tr = R
    while tr * C * 4 > budget and tr % 32 == 0:
        tr //= 2
    return tr


def adamw(w, g, m, v, *, name):
    R, C = w.shape
    tr = _rows_block(R, C)

    def body(w_ref, g_ref, m_ref, v_ref, d_ref, mo_ref, vo_ref):
        g_ = g_ref[...]
        mn = ADAM_B1 * m_ref[...] + (1.0 - ADAM_B1) * g_
        vn = ADAM_B2 * v_ref[...] + (1.0 - ADAM_B2) * jnp.square(g_)
        m_hat = mn / (1.0 - ADAM_B1 ** ADAM_STEP)
        v_hat = vn / (1.0 - ADAM_B2 ** ADAM_STEP)
        d_ref[...] = -ADAM_LR * (m_hat / (jnp.sqrt(v_hat) + ADAM_EPS) + ADAM_WD * w_ref[...])
        mo_ref[...] = mn
        vo_ref[...] = vn

    spec = pl.BlockSpec((tr, C), lambda i: (i, 0))
    sd = jax.ShapeDtypeStruct((R, C), F32)
    return _call(body, name=name, grid=(R // tr,), in_specs=[spec] * 4, out_specs=[spec] * 3, out_shape=[sd] * 3,
                 args=(w, g, m, v))


def sum_slots(slots, *, name):
    _, R, C = slots.shape
    tr = _rows_block(R, C, budget=1 << 19)

    def body(s_ref, o_ref):
        acc = s_ref[0].astype(F32)
        for d in range(1, 8):
            acc = acc + s_ref[d].astype(F32)
        o_ref[...] = acc

    return _call(body, name=name, grid=(R // tr,), in_specs=[pl.BlockSpec((8, tr, C), lambda i: (0, i, 0))],
                 out_specs=[pl.BlockSpec((tr, C), lambda i: (i, 0))], out_shape=[jax.ShapeDtypeStruct((R, C), F32)],
                 args=(slots,))[0]


def _adam_nd(w, g, m, v, name):
    shp = w.shape
    c = shp[-1]
    f2 = lambda a: a.reshape(-1, c)
    d, mn, vn = adamw(f2(w), f2(g), f2(m), f2(v), name=name)
    return d.reshape(shp), mn.reshape(shp), vn.reshape(shp)


def _reduced(slots, name):
    out = []
    for s in slots:
        c = s.shape[-1]
        out.append(sum_slots(s.reshape(8, -1, c), name=name).reshape(s.shape[1:]))
    return out


def kernel(x, ln_mix_g, ln_mix_b, ln_ffn_g, ln_ffn_b, ffn_w1, ffn_w2, a_w_in, a_lb_logits, a_norm_g, a_w_out, b_w_pw1, b_b_pw1, b_w_dw, b_b_dw, b_ln_g, b_ln_b, b_w_pw2, b_b_pw2, loss_target, m_ln_mix_g, m_ln_mix_b, m_ln_ffn_g, m_ln_ffn_b, m_ffn_w1, m_ffn_w2, m_a_w_in, m_a_lb_logits, m_a_norm_g, m_a_w_out, m_b_w_pw1, m_b_b_pw1, m_b_w_dw, m_b_b_dw, m_b_ln_g, m_b_ln_b, m_b_w_pw2, m_b_b_pw2, v_ln_mix_g, v_ln_mix_b, v_ln_ffn_g, v_ln_ffn_b, v_ffn_w1, v_ffn_w2, v_a_w_in, v_a_lb_logits, v_a_norm_g, v_a_w_out, v_b_w_pw1, v_b_b_pw1, v_b_w_dw, v_b_b_dw, v_b_ln_g, v_b_ln_b, v_b_w_pw2, v_b_b_pw2):
    names = ["ln_mix_g", "ln_mix_b", "ln_ffn_g", "ln_ffn_b", "ffn_w1", "ffn_w2", "a_w_in", "a_lb_logits", "a_norm_g",
             "a_w_out", "b_w_pw1", "b_b_pw1", "b_w_dw", "b_b_dw", "b_ln_g", "b_ln_b", "b_w_pw2", "b_b_pw2"]
    w = dict(zip(names, [ln_mix_g, ln_mix_b, ln_ffn_g, ln_ffn_b, ffn_w1, ffn_w2, a_w_in, a_lb_logits, a_norm_g, a_w_out,
                         b_w_pw1, b_b_pw1, b_w_dw, b_b_dw, b_ln_g, b_ln_b, b_w_pw2, b_b_pw2]))
    m = dict(zip(names, [m_ln_mix_g, m_ln_mix_b, m_ln_ffn_g, m_ln_ffn_b, m_ffn_w1, m_ffn_w2, m_a_w_in, m_a_lb_logits,
                         m_a_norm_g, m_a_w_out, m_b_w_pw1, m_b_b_pw1, m_b_w_dw, m_b_b_dw, m_b_ln_g, m_b_ln_b, m_b_w_pw2,
                         m_b_b_pw2]))
    v = dict(zip(names, [v_ln_mix_g, v_ln_mix_b, v_ln_ffn_g, v_ln_ffn_b, v_ffn_w1, v_ffn_w2, v_a_w_in, v_a_lb_logits,
                         v_a_norm_g, v_a_w_out, v_b_w_pw1, v_b_b_pw1, v_b_w_dw, v_b_b_dw, v_b_ln_g, v_b_ln_b, v_b_w_pw2,
                         v_b_b_pw2]))
    T, D = x.shape[1], x.shape[2]
    DS = D // 4
    F = 4 * ffn_w1.shape[2]
    chip = 2 * lax.axis_index("x") + lax.axis_index("y")
    tm = min(T, 512)
    tmw = min(T, 1024)
    tmc = min(T, 256)
    rb = min(T, 1024)
    tf = min(F // 4, 1024)
    xin, target = x[0], loss_target[0]

    def mix_shards(i):
        j = i // 2
        if i % 2 == 0:
            return [a_w_in[j].astype(BF16), a_w_out[j].astype(BF16)]
        vec = jnp.concatenate([b_w_dw[j], jnp.zeros((1, DS), F32), b_b_dw[j][None], b_ln_g[j][None], b_ln_b[j][None],
                               b_b_pw2[j][None], b_b_pw1[j].reshape(2, DS), jnp.zeros((2, DS), F32)], axis=0)
        return [b_w_pw1[j].astype(BF16), b_w_pw2[j].astype(BF16), vec]

    def ffn_shards(i):
        return [ffn_w1[i].astype(BF16), ffn_w2[i].astype(BF16)]

    def mix_weights(i, got):
        if i % 2 == 0:
            return {"w_in": got[0], "w_out": got[1].reshape(D, D)}
        pw1 = jnp.transpose(got[0].reshape(2, 2, D, D // 2), (0, 2, 1, 3)).reshape(2, D, D)
        vec = jnp.transpose(got[2], (1, 0, 2)).reshape(VEC_ROWS, D)
        return {"pw1": pw1, "pw2": got[1].reshape(D, D), "vec": vec,
                "b_pw1": got[2][:, 36:38, :].reshape(2, 1, D)}

    lb_all = lb_fwd(a_lb_logits)
    zeros_bias = jnp.zeros((1, D), F32)

    first = mix_shards(0)
    mixw = {"w_in": gather_chips_via_sibling(first[0])}
    saved = []
    h, hb = xin, xin.astype(BF16)
    for i in range(DEPTH):
        j = i // 2
        s = {"xb": hb, "mixw": mixw}
        gf = GatherChips(ffn_shards(i) + (first[1:] if i == 0 else []))
        if i % 2 == 0:
            s["proj"] = mm_groups(hb, mixw["w_in"], jnp.zeros((4, 1, D), F32), tm=tmw, name="a_in_proj")
            (s["o"], s["og"], s["st"]), got = hgrn_fwd(s["proj"], lb_all[j:j + 1], a_norm_g[j:j + 1], rb=rb,
                                                       name="hgrn_fwd", comm=gf)
            if i == 0:
                mixw["w_out"] = got[2].reshape(D, D)
            s["r1"], x1, s["x1b"] = mm_res_ln(s["og"], mixw["w_out"], zeros_bias, h, ln_mix_g[i:i + 1],
                                              ln_mix_b[i:i + 1], tm=tmw, name="a_out_ln")
        else:
            s["u"] = mm_groups(hb, mixw["pw1"], mixw["b_pw1"], tm=tmw, name="b_pw1")
            (s["c"], s["v2"]), got = conv_fwd(s["u"], mixw["vec"], tm=min(T, 2 * tmc), name="conv_fwd", comm=gf)
            s["r1"], x1, s["x1b"] = mm_res_ln(s["v2"], mixw["pw2"], mixw["vec"][35:36], h, ln_mix_g[i:i + 1],
                                              ln_mix_b[i:i + 1], tm=tmw, name="b_pw2_ln")
        s["w1"], s["w2"] = got[0], got[1].reshape(F, D)
        gm = GatherChips(mix_shards(i + 1)) if i + 1 < DEPTH else None
        (s["z"], s["r2"], h, hb), got = ffn_fwd(x1, s["w1"], s["w2"], ln_ffn_g[i:i + 1], ln_ffn_b[i:i + 1],
                                                tm=tm, tf=tf, name="ffn_fwd", comm=gm)
        if gm is not None:
            mixw = mix_weights(i + 1, got)
        saved.append(s)

    loss_part, dh = loss_grad(h, target, tm=tmw)
    loss = lax.psum(loss_part[0, 0], ("x", "y", "c"))

    gr = {k: [None] * DEPTH for k in ("ln_mix_g", "ln_mix_b", "ln_ffn_g", "ln_ffn_b", "ffn_w1", "ffn_w2")}
    for k in ("a_w_in", "a_w_out", "a_norm_g", "a_dlb", "b_w_pw1", "b_w_pw2", "b_vec", "b_b_pw2"):
        gr[k] = [None] * 2
    w_in_name = ("a_w_in", "b_w_pw1")
    w_out_name = ("a_w_out", "b_w_pw2")
    pending = None

    for i in reversed(range(DEPTH)):
        j = i // 2
        s = saved[i]
        mixw = s["mixw"]
        sm = ScatterPieces([pending[1]]) if pending is not None else None
        (dz, dx1, drb2, sums2), slots = ffn_bwd_dx(dh, s["r2"], ln_ffn_g[i:i + 1], s["z"], s["w1"], s["w2"],
                                                   tm=tm, tf=tf, name="ffn_bwd_dx", comm=sm)
        if pending is not None:
            gr[w_in_name[pending[0] % 2]][pending[0] // 2] = _reduced(slots, "sum_mix_grads")[0]
        gr["ln_ffn_g"][i], gr["ln_ffn_b"][i] = sums2[0], sums2[1]
        dw1 = mm_tn(s["x1b"], dz[None], tm=tmw, tk=D, tn=F // 4, name="ffn_dw1")[0]
        dw2 = mm_tn(s["z"], drb2[None], tm=tmw, tk=F // 4, tn=D, relu2=True, name="ffn_dw2")[0, 0]
        wmix = mixw["w_out"] if i % 2 == 0 else mixw["pw2"]
        dr1, drb1, dmo, sums1 = ln_bwd_mm(dx1, s["r1"], ln_mix_g[i:i + 1], wmix, tm=tmw, name="mix_ln_bwd")
        gr["ln_mix_g"][i], gr["ln_mix_b"][i] = sums1[0], sums1[1]
        if i % 2 == 0:
            dwo = mm_tn(s["og"], drb1[None], tm=tmw, tk=D, tn=D, name="a_dw_out")[0, 0].reshape(4, DS, D)
            sf = ScatterPieces([dw1, dw2.reshape(4, F // 4, D), dwo])
            (dproj, hs), slots = hgrn_bwd(s["proj"], s["o"], dmo, s["st"], lb_all[j:j + 1], a_norm_g[j:j + 1], rb=rb,
                                          name="hgrn_bwd", comm=sf)
            gr["a_norm_g"][j], gr["a_dlb"][j] = hs[0], hs[1]
            dwi = mm_tn(s["xb"], dproj, tm=tmw, tk=D, tn=D, name="a_dw_in")[:, 0]
            dy_in, w_in_t, dx_name = dproj, mixw["w_in"], "a_dx"
        else:
            dwo = mm_tn(s["v2"], drb1[None], tm=tmw, tk=D, tn=D, name="b_dw_pw2")[0, 0].reshape(4, DS, D)
            sf = ScatterPieces([dw1, dw2.reshape(4, F // 4, D), dwo])
            (du, cs), slots = conv_bwd(dmo, s["c"], s["u"], mixw["vec"], tm=tmc, name="conv_bwd", comm=sf)
            gr["b_vec"][j], gr["b_b_pw2"][j] = cs, sums1[2]
            dwi = mm_tn(s["xb"], du, tm=tmw, tk=D, tn=D // 2, name="b_dw_pw1").reshape(4, D, D // 2)
            dy_in, w_in_t, dx_name = du, mixw["pw1"], "b_dx"
        gr["ffn_w1"][i], gr["ffn_w2"][i], gr[w_out_name[i % 2]][j] = _reduced(slots, "sum_ffn_grads")
        dh, slots = mm_nt_acc(dy_in, w_in_t, dr1, tm=tmw, name=dx_name, comm=ScatterPieces([dwi]) if i == 0 else None)
        if i == 0:
            gr[w_in_name[0]][0] = _reduced(slots, "sum_mix_grads")[0]
        pending = (i, dwi)
    grad_x = dh[None]

    small = {k: jnp.stack(gr[k]) for k in ("ln_mix_g", "ln_mix_b", "ln_ffn_g", "ln_ffn_b", "a_norm_g", "b_vec", "b_b_pw2")}
    small["a_lb_logits"] = lb_bwd(a_lb_logits, jnp.stack(gr["a_dlb"]))
    small_names = ["ln_mix_g", "ln_mix_b", "ln_ffn_g", "ln_ffn_b", "a_lb_logits", "a_norm_g", "b_b_pw2", "b_vec"]
    rows = [small[k].reshape(-1, D) for k in small_names]
    counts = [r.shape[0] for r in rows]
    rows = [jnp.pad(r, ((0, (-r.shape[0]) % 8), (0, 0))) for r in rows]
    summed = all_reduce_small(jnp.concatenate(rows, axis=0))
    sm = {}
    off = 0
    for k, n, r in zip(small_names, counts, rows):
        sm[k] = summed[off:off + n]
        off += r.shape[0]
    bvec = sm["b_vec"].reshape(2, VEC_ROWS, D)

    def shard_cols(a):
        return lax.dynamic_slice_in_dim(a, chip * DS, DS, axis=a.ndim - 1)

    grads = {k: jnp.stack(gr[k]) for k in ("ffn_w1", "ffn_w2", "a_w_in", "a_w_out", "b_w_pw1", "b_w_pw2")}
    for k in ("ln_mix_g", "ln_mix_b", "ln_ffn_g", "ln_ffn_b", "a_lb_logits", "a_norm_g"):
        grads[k] = sm[k]
    grads["b_b_pw1"] = lax.dynamic_slice_in_dim(bvec[:, 36:38, :].reshape(2, 2 * D), chip * (D // 2), D // 2, axis=1)
    grads["b_w_dw"] = shard_cols(bvec[:, 0:CONV_W, :])
    grads["b_b_dw"] = shard_cols(bvec[:, 32, :])
    grads["b_ln_g"] = shard_cols(bvec[:, 33, :])
    grads["b_ln_b"] = shard_cols(bvec[:, 34, :])
    grads["b_b_pw2"] = shard_cols(sm["b_b_pw2"])

    delta, new_m, new_v = {}, {}, {}
    for k in names:
        delta[k], new_m[k], new_v[k] = _adam_nd(w[k], grads[k], m[k], v[k], "adamw_" + k)
    return (loss, grad_x, *[grads[k] for k in names], *[delta[k] for k in names],
            *[new_m[k] for k in names], *[new_v[k] for k in names])
```

```python
import jax
import jax.numpy as jnp
from jax import lax
from jax.experimental import pallas as pl
from jax.experimental.pallas import tpu as pltpu

F32 = jnp.float32
BF16 = jnp.bfloat16
MESH = pl.DeviceIdType.MESH

DEPTH = 4
ALPHA = (2.0 * DEPTH) ** 0.25
LN_EPS = 1e-5
RMS_EPS = 1e-6
GATE_EPS = 1e-6
HEAD = 128
CHUNK = 128
SUB = 16
PAIR = 2
TRIP_CHUNKS = 2
CONV_W = 31
HALO = 32
VEC_ROWS = 40
CONV_RB = 32
ADAM_LR, ADAM_B1, ADAM_B2, ADAM_EPS, ADAM_WD, ADAM_STEP = 0.001, 0.9, 0.999, 1e-08, 0.01, 10
VMEM_LIMIT = 56 * 1024 * 1024
ANY = pl.BlockSpec(memory_space=pl.ANY)


def _dot(a, b):
    return jnp.dot(a, b, preferred_element_type=F32)


def _dot_nt(a, b):
    return lax.dot_general(a, b, (((1,), (1,)), ((), ())), preferred_element_type=F32)


def _dot_tn(a, b):
    return lax.dot_general(a, b, (((0,), (0,)), ((), ())), preferred_element_type=F32)


def _sigmoid(x):
    return 1.0 / (1.0 + jnp.exp(-x))


def _ln_stats(r):
    mu = jnp.mean(r, axis=-1, keepdims=True)
    xc = r - mu
    var = jnp.mean(xc * xc, axis=-1, keepdims=True)
    rstd = lax.rsqrt(var + LN_EPS)
    return xc * rstd, rstd


def _ln_bwd(dy, xhat, rstd, g):
    dyg = dy * g
    m1 = jnp.mean(dyg, axis=-1, keepdims=True)
    m2 = jnp.mean(dyg * xhat, axis=-1, keepdims=True)
    return rstd * (dyg - m1 - xhat * m2)


def _place():
    return lax.axis_index("x"), lax.axis_index("y"), lax.axis_index("c")


class GatherChips:
    def __init__(self, arrs):
        self.ins = list(arrs)
        n = len(arrs)
        self.out_shapes = [jax.ShapeDtypeStruct((4,) + a.shape, a.dtype) for a in arrs]
        self.sems = [pltpu.SemaphoreType.DMA((3 * n,)), pltpu.SemaphoreType.DMA((3 * n,)),
                     pltpu.SemaphoreType.DMA((n,))]

    def copies(self, ins, outs, send, recv, loc):
        x, y, c = _place()
        me = 2 * x + y
        local, remote = [], []
        for a in range(len(ins)):
            local.append(pltpu.make_async_copy(ins[a], outs[a].at[me], loc.at[a]))
            for j, (px, py) in enumerate([(1 - x, y), (x, 1 - y), (1 - x, 1 - y)]):
                remote.append(pltpu.make_async_remote_copy(
                    src_ref=ins[a], dst_ref=outs[a].at[me], send_sem=send.at[3 * a + j], recv_sem=recv.at[3 * a + j],
                    device_id=(px, py, c), device_id_type=MESH))
        return local + remote


class ScatterPieces:
    def __init__(self, arrs):
        self.ins = list(arrs)
        n = len(arrs)
        self.out_shapes = [jax.ShapeDtypeStruct((8,) + a.shape[1:], a.dtype) for a in arrs]
        self.sems = [pltpu.SemaphoreType.DMA((7 * n,)), pltpu.SemaphoreType.DMA((7 * n,)),
                     pltpu.SemaphoreType.DMA((n,))]

    def copies(self, ins, outs, send, recv, loc):
        x, y, c = _place()
        me = 4 * x + 2 * y + c
        local, remote = [], []
        for a in range(len(ins)):
            local.append(pltpu.make_async_copy(ins[a].at[2 * x + y], outs[a].at[me], loc.at[a]))
            k = 0
            for fx in (0, 1):
                for fy in (0, 1):
                    for fc in (0, 1):
                        if fx or fy or fc:
                            tx, ty = x ^ fx, y ^ fy
                            remote.append(pltpu.make_async_remote_copy(
                                src_ref=ins[a].at[2 * tx + ty], dst_ref=outs[a].at[me],
                                send_sem=send.at[7 * a + k], recv_sem=recv.at[7 * a + k],
                                device_id=(tx, ty, c ^ fc), device_id_type=MESH))
                            k += 1
        return local + remote


class ScatterPiecesViaSibling:
    def __init__(self, arrs):
        self.ins = list(arrs)
        n = len(arrs)
        self.out_shapes = [jax.ShapeDtypeStruct((8,) + a.shape[1:], a.dtype) for a in arrs]
        self.sems = [pltpu.SemaphoreType.DMA((3 * n,)), pltpu.SemaphoreType.DMA((3 * n,)),
                     pltpu.SemaphoreType.DMA((n,)), pltpu.SemaphoreType.DMA((4 * n,)), pltpu.SemaphoreType.DMA((4 * n,))]

    def copies(self, ins, outs, send, recv, loc, send2, recv2):
        x, y, c = _place()
        me = 4 * x + 2 * y + c
        local, remote = [], []
        for a in range(len(ins)):
            local.append(pltpu.make_async_copy(ins[a].at[2 * x + y], outs[a].at[me], loc.at[a]))
            for j, (px, py) in enumerate([(1 - x, y), (x, 1 - y), (1 - x, 1 - y)]):
                remote.append(pltpu.make_async_remote_copy(
                    src_ref=ins[a].at[2 * px + py], dst_ref=outs[a].at[me], send_sem=send.at[3 * a + j],
                    recv_sem=recv.at[3 * a + j], device_id=(px, py, c), device_id_type=MESH))
        return local + remote

    def forwards(self, ins, outs, send, recv, loc, send2, recv2):
        x, y, c = _place()
        passed = []
        for a in range(len(ins)):
            for k, (qx, qy) in enumerate([(x, y), (1 - x, y), (x, 1 - y), (1 - x, 1 - y)]):
                slot = outs[a].at[4 * qx + 2 * qy + c]
                passed.append(pltpu.make_async_remote_copy(
                    src_ref=slot, dst_ref=slot, send_sem=send2.at[4 * a + k], recv_sem=recv2.at[4 * a + k],
                    device_id=(x, y, 1 - c), device_id_type=MESH))
        return passed


def carried_call(body, comm, *, name, grid, in_specs, out_specs, out_shape, scratch_shapes, args):
    sem = ("arbitrary",) * len(grid)
    params = pltpu.CompilerParams(dimension_semantics=sem, vmem_limit_bytes=VMEM_LIMIT)
    if comm is None:
        res = pl.pallas_call(body, name=name, grid=grid, in_specs=in_specs, out_specs=out_specs, out_shape=out_shape,
                             scratch_shapes=scratch_shapes, compiler_params=params)(*args)
        return res, []
    ni, no, nscr = len(in_specs), len(out_specs), len(scratch_shapes)
    ci, co = len(comm.ins), len(comm.out_shapes)

    def both(*refs):
        ins, refs = refs[:ni], refs[ni:]
        cins, refs = refs[:ci], refs[ci:]
        outs, refs = refs[:no], refs[no:]
        couts, refs = refs[:co], refs[co:]
        scr, sems = refs[:nscr], refs[nscr:]
        first = pl.program_id(0) == 0
        last = pl.program_id(0) == grid[0] - 1
        for d in range(1, len(grid)):
            first = first & (pl.program_id(d) == 0)
            last = last & (pl.program_id(d) == grid[d] - 1)

        @pl.when(first)
        def _():
            for cp in comm.copies(cins, couts, *sems):
                cp.start()

        body(*ins, *outs, *scr)

        @pl.when(last)
        def _():
            for cp in comm.copies(cins, couts, *sems):
                cp.wait()
            if hasattr(comm, "forwards"):
                passed = comm.forwards(cins, couts, *sems)
                for cp in passed:
                    cp.start()
                for cp in passed:
                    cp.wait()

    res = pl.pallas_call(
        both, name=name, grid=grid, in_specs=list(in_specs) + [ANY] * ci, out_specs=list(out_specs) + [ANY] * co,
        out_shape=list(out_shape) + comm.out_shapes, scratch_shapes=list(scratch_shapes) + comm.sems,
        compiler_params=params)(*args, *comm.ins)
    return res[:no], res[no:]


def gather_chips_via_sibling(shard):
    half = shard.shape[0] // 2

    def body(in_ref, out_ref, send1, recv1, send2, recv2, loc):
        x, y, c = _place()
        me = 2 * x + y
        mine = pl.ds(c * half, half)
        local = pltpu.make_async_copy(in_ref, out_ref.at[me], loc)
        local.start()
        chips = [(1 - x, y), (x, 1 - y), (1 - x, 1 - y)]
        first = [pltpu.make_async_remote_copy(
            src_ref=in_ref.at[mine], dst_ref=out_ref.at[me, mine], send_sem=send1.at[j], recv_sem=recv1.at[j],
            device_id=(px, py, c), device_id_type=MESH) for j, (px, py) in enumerate(chips)]
        for cp in first:
            cp.start()
        passed = []
        for j, (px, py) in enumerate(chips):
            first[j].wait_recv()
            got = out_ref.at[2 * px + py, mine]
            cp = pltpu.make_async_remote_copy(src_ref=got, dst_ref=got, send_sem=send2.at[j], recv_sem=recv2.at[j],
                                              device_id=(x, y, 1 - c), device_id_type=MESH)
            cp.start()
            passed.append(cp)
        for cp in first:
            cp.wait_send()
        for cp in passed:
            cp.wait()
        local.wait()

    return pl.pallas_call(
        body, name="gather_first", in_specs=[ANY], out_specs=ANY,
        out_shape=jax.ShapeDtypeStruct((4,) + shard.shape, shard.dtype),
        scratch_shapes=[pltpu.SemaphoreType.DMA((3,))] * 4 + [pltpu.SemaphoreType.DMA],
        compiler_params=pltpu.CompilerParams(has_side_effects=True))(shard)


def all_reduce_small(v):
    R, C = v.shape

    def body(v_ref, o_ref, slots, send, recv):
        x, y, c = _place()
        me = 4 * x + 2 * y + c
        slots[me] = v_ref[...]
        cps = []
        k = 0
        for fx in (0, 1):
            for fy in (0, 1):
                for fc in (0, 1):
                    if fx or fy or fc:
                        cps.append(pltpu.make_async_remote_copy(
                            src_ref=v_ref, dst_ref=slots.at[me], send_sem=send.at[k], recv_sem=recv.at[k],
                            device_id=(x ^ fx, y ^ fy, c ^ fc), device_id_type=MESH))
                        k += 1
        for cp in cps:
            cp.start()
        for cp in cps:
            cp.wait()
        acc = slots[0]
        for d in range(1, 8):
            acc = acc + slots[d]
        o_ref[...] = acc

    vm = pl.BlockSpec(memory_space=pltpu.VMEM)
    return pl.pallas_call(
        body, name="all_reduce_small", in_specs=[vm], out_specs=vm,
        out_shape=jax.ShapeDtypeStruct((R, C), F32),
        scratch_shapes=[pltpu.VMEM((8, R, C), F32), pltpu.SemaphoreType.DMA((7,)), pltpu.SemaphoreType.DMA((7,))],
        compiler_params=pltpu.CompilerParams(has_side_effects=True, vmem_limit_bytes=VMEM_LIMIT),
    )(v)


def _call(body, *, name, grid, in_specs, out_specs, out_shape, scratch_shapes=(), args):
    res, _ = carried_call(body, None, name=name, grid=grid, in_specs=in_specs, out_specs=out_specs,
                          out_shape=out_shape, scratch_shapes=list(scratch_shapes), args=args)
    return res


def mm_groups(a, w, bias, *, tm, name):
    T, K = a.shape
    G, _, N = w.shape

    def body(a_ref, w_ref, b_ref, o_ref):
        o_ref[...] = _dot(a_ref[...], w_ref[...]) + b_ref[...]

    return _call(
        body, name=name, grid=(G, T // tm),
        in_specs=[pl.BlockSpec((tm, K), lambda g, i: (i, 0)),
                  pl.BlockSpec((None, K, N), lambda g, i: (g, 0, 0)),
                  pl.BlockSpec((None, 1, N), lambda g, i: (g, 0, 0))],
        out_specs=[pl.BlockSpec((None, tm, N), lambda g, i: (g, i, 0))],
        out_shape=[jax.ShapeDtypeStruct((G, T, N), F32)], args=(a, w, bias))[0]


def mm_res_ln(a, w, bias, res, g, b, *, tm, name):
    T, K = a.shape
    N = w.shape[1]

    def body(a_ref, w_ref, bias_ref, res_ref, g_ref, b_ref, r_ref, y_ref, yb_ref):
        r = ALPHA * res_ref[...] + _dot(a_ref[...], w_ref[...]) + bias_ref[...]
        r_ref[...] = r
        xhat, _ = _ln_stats(r)
        y = xhat * g_ref[...] + b_ref[...]
        y_ref[...] = y
        yb_ref[...] = y.astype(BF16)

    row = lambda i: (i, 0)
    fix = lambda i: (0, 0)
    return _call(
        body, name=name, grid=(T // tm,),
        in_specs=[pl.BlockSpec((tm, K), row), pl.BlockSpec((K, N), fix), pl.BlockSpec((1, N), fix),
                  pl.BlockSpec((tm, N), row), pl.BlockSpec((1, N), fix), pl.BlockSpec((1, N), fix)],
        out_specs=[pl.BlockSpec((tm, N), row), pl.BlockSpec((tm, N), row), pl.BlockSpec((tm, N), row)],
        out_shape=[jax.ShapeDtypeStruct((T, N), F32), jax.ShapeDtypeStruct((T, N), F32),
                   jax.ShapeDtypeStruct((T, N), BF16)],
        args=(a, w, bias, res, g, b))


def ffn_fwd(x, w1, w2, g, b, *, tm, tf, name, comm=None):
    T, D = x.shape
    NC, _, FC = w1.shape
    F = NC * FC
    per = FC // tf
    nf = F // tf

    def body(x_ref, w1_ref, w2_ref, g_ref, b_ref, z_ref, r_ref, y_ref, yb_ref, acc_ref, xb_ref):
        f = pl.program_id(1)

        @pl.when(f == 0)
        def _():
            acc_ref[...] = jnp.zeros_like(acc_ref)
            xb_ref[...] = x_ref[...].astype(BF16)

        z = _dot(xb_ref[...], w1_ref[...])
        z_ref[...] = z.astype(BF16)
        h = jnp.square(jnp.maximum(z, 0.0)).astype(BF16)
        acc_ref[...] += _dot(h, w2_ref[...])

        @pl.when(f == nf - 1)
        def _():
            r = ALPHA * x_ref[...] + acc_ref[...]
            r_ref[...] = r
            xhat, _ = _ln_stats(r)
            y = xhat * g_ref[...] + b_ref[...]
            y_ref[...] = y
            yb_ref[...] = y.astype(BF16)

    return carried_call(
        body, comm, name=name, grid=(T // tm, nf),
        in_specs=[pl.BlockSpec((tm, D), lambda i, f: (i, 0)),
                  pl.BlockSpec((None, D, tf), lambda i, f: (f // per, 0, f % per)),
                  pl.BlockSpec((tf, D), lambda i, f: (f, 0)),
                  pl.BlockSpec((1, D), lambda i, f: (0, 0)),
                  pl.BlockSpec((1, D), lambda i, f: (0, 0))],
        out_specs=[pl.BlockSpec((tm, tf), lambda i, f: (i, f)),
                   pl.BlockSpec((tm, D), lambda i, f: (i, 0)),
                   pl.BlockSpec((tm, D), lambda i, f: (i, 0)),
                   pl.BlockSpec((tm, D), lambda i, f: (i, 0))],
        out_shape=[jax.ShapeDtypeStruct((T, F), BF16), jax.ShapeDtypeStruct((T, D), F32),
                   jax.ShapeDtypeStruct((T, D), F32), jax.ShapeDtypeStruct((T, D), BF16)],
        scratch_shapes=[pltpu.VMEM((tm, D), F32), pltpu.VMEM((tm, D), BF16)],
        args=(x, w1, w2, g, b))


def ln_bwd_mm(dy, r, g, w, *, tm, name):
    T, N = dy.shape
    Ko = w.shape[0]

    def body(dy_ref, r_ref, g_ref, w_ref, dr_ref, drb_ref, o_ref, s_ref):
        @pl.when(pl.program_id(0) == 0)
        def _():
            s_ref[...] = jnp.zeros_like(s_ref)

        dy_ = dy_ref[...]
        xhat, rstd = _ln_stats(r_ref[...])
        dr = _ln_bwd(dy_, xhat, rstd, g_ref[...])
        dr_ref[...] = dr
        drb = dr.astype(BF16)
        drb_ref[...] = drb
        o_ref[...] = _dot_nt(drb, w_ref[...])
        s_ref[0:1, :] += jnp.sum(dy_ * xhat, axis=0, keepdims=True)
        s_ref[1:2, :] += jnp.sum(dy_, axis=0, keepdims=True)
        s_ref[2:3, :] += jnp.sum(dr, axis=0, keepdims=True)

    row = lambda i: (i, 0)
    fix = lambda i: (0, 0)
    return _call(
        body, name=name, grid=(T // tm,),
        in_specs=[pl.BlockSpec((tm, N), row), pl.BlockSpec((tm, N), row), pl.BlockSpec((1, N), fix),
                  pl.BlockSpec((Ko, N), fix)],
        out_specs=[pl.BlockSpec((tm, N), row), pl.BlockSpec((tm, N), row), pl.BlockSpec((tm, Ko), row),
                   pl.BlockSpec((8, N), fix)],
        out_shape=[jax.ShapeDtypeStruct((T, N), F32), jax.ShapeDtypeStruct((T, N), BF16),
                   jax.ShapeDtypeStruct((T, Ko), F32), jax.ShapeDtypeStruct((8, N), F32)],
        args=(dy, r, g, w))


def ffn_bwd_dx(dy, r, g, z, w1, w2, *, tm, tf, name, comm=None):
    T, D = dy.shape
    NC, _, FC = w1.shape
    F = NC * FC
    per = FC // tf
    nf = F // tf

    def body(dy_ref, r_ref, g_ref, z_ref, w1_ref, w2_ref, dz_ref, dx_ref, drb_ref, s_ref, dr_scr, acc_ref):
        i = pl.program_id(0)
        f = pl.program_id(1)

        @pl.when((i == 0) & (f == 0))
        def _():
            s_ref[...] = jnp.zeros_like(s_ref)

        @pl.when(f == 0)
        def _():
            dy_ = dy_ref[...]
            xhat, rstd = _ln_stats(r_ref[...])
            dr = _ln_bwd(dy_, xhat, rstd, g_ref[...])
            dr_scr[...] = dr
            drb_ref[...] = dr.astype(BF16)
            acc_ref[...] = jnp.zeros_like(acc_ref)
            s_ref[0:1, :] += jnp.sum(dy_ * xhat, axis=0, keepdims=True)
            s_ref[1:2, :] += jnp.sum(dy_, axis=0, keepdims=True)

        dh = _dot_nt(drb_ref[...], w2_ref[...])
        dz = (dh * (2.0 * jnp.maximum(z_ref[...].astype(F32), 0.0))).astype(BF16)
        dz_ref[...] = dz
        acc_ref[...] += _dot_nt(dz, w1_ref[...])

        @pl.when(f == nf - 1)
        def _():
            dx_ref[...] = ALPHA * dr_scr[...] + acc_ref[...]

    return carried_call(
        body, comm, name=name, grid=(T // tm, nf),
        in_specs=[pl.BlockSpec((tm, D), lambda i, f: (i, 0)),
                  pl.BlockSpec((tm, D), lambda i, f: (i, 0)),
                  pl.BlockSpec((1, D), lambda i, f: (0, 0)),
                  pl.BlockSpec((tm, tf), lambda i, f: (i, f)),
                  pl.BlockSpec((None, D, tf), lambda i, f: (f // per, 0, f % per)),
                  pl.BlockSpec((tf, D), lambda i, f: (f, 0))],
        out_specs=[pl.BlockSpec((tm, tf), lambda i, f: (i, f)),
                   pl.BlockSpec((tm, D), lambda i, f: (i, 0)),
                   pl.BlockSpec((tm, D), lambda i, f: (i, 0)),
                   pl.BlockSpec((8, D), lambda i, f: (0, 0))],
        out_shape=[jax.ShapeDtypeStruct((T, F), BF16), jax.ShapeDtypeStruct((T, D), F32),
                   jax.ShapeDtypeStruct((T, D), BF16), jax.ShapeDtypeStruct((8, D), F32)],
        scratch_shapes=[pltpu.VMEM((tm, D), F32), pltpu.VMEM((tm, D), F32)],
        args=(dy, r, g, z, w1, w2))


def mm_tn(a, b, *, tm, tk, tn, relu2=False, name):
    T, K = a.shape
    G, _, N = b.shape
    nt = T // tm

    def body(a_ref, b_ref, o_ref, acc_ref):
        t = pl.program_id(3)

        @pl.when(t == 0)
        def _():
            acc_ref[...] = jnp.zeros_like(acc_ref)

        av = a_ref[...]
        if relu2:
            av = jnp.square(jnp.maximum(av.astype(F32), 0.0))
        acc_ref[...] += _dot_tn(av.astype(BF16), b_ref[...])

        @pl.when(t == nt - 1)
        def _():
            o_ref[...] = acc_ref[...].astype(BF16)

    return _call(
        body, name=name, grid=(G, K // tk, N // tn, nt),
        in_specs=[pl.BlockSpec((tm, tk), lambda g, k, n, t: (t, k)),
                  pl.BlockSpec((None, tm, tn), lambda g, k, n, t: (g, t, n))],
        out_specs=[pl.BlockSpec((None, None, tk, tn), lambda g, k, n, t: (g, n, k, 0))],
        out_shape=[jax.ShapeDtypeStruct((G, N // tn, K, tn), BF16)],
        scratch_shapes=[pltpu.VMEM((tk, tn), F32)], args=(a, b))[0]


def mm_nt_acc(dy, w, base, *, tm, name, comm=None):
    G, T, N = dy.shape
    K = w.shape[1]

    def body(dy_ref, w_ref, base_ref, o_ref):
        g = pl.program_id(1)

        @pl.when(g == 0)
        def _():
            o_ref[...] = ALPHA * base_ref[...]

        o_ref[...] += _dot_nt(dy_ref[...], w_ref[...])

    res, got = carried_call(
        body, comm, name=name, grid=(T // tm, G),
        in_specs=[pl.BlockSpec((None, tm, N), lambda i, g: (g, i, 0)),
                  pl.BlockSpec((None, K, N), lambda i, g: (g, 0, 0)),
                  pl.BlockSpec((tm, K), lambda i, g: (i, 0))],
        out_specs=[pl.BlockSpec((tm, K), lambda i, g: (i, 0))],
        out_shape=[jax.ShapeDtypeStruct((T, K), F32)], scratch_shapes=[], args=(dy, w, base))
    return res[0], got


def _split3(x):
    x1 = x.astype(BF16)
    r1 = x - x1.astype(F32)
    x2 = r1.astype(BF16)
    x3 = (r1 - x2.astype(F32)).astype(BF16)
    return x1, x2, x3


def _tri_dot(tri, x):
    x1, x2, x3 = _split3(x)
    return _dot(tri, x1) + _dot(tri, x2) + _dot(tri, x3)


def _gates(pq, fz, lb):
    sg = _sigmoid(fz)
    f = lb + (1.0 - lb) * sg
    logf = jnp.log(jnp.maximum(f, GATE_EPS))
    sq = _sigmoid(pq)
    return pq * sq, 1.0 - f, logf, f, sg, sq


def _staggered(gens):
    live = []
    waiting = list(gens)
    for gen in waiting:
        next(gen)
    while live or waiting:
        if waiting:
            live.append(waiting.pop(0))
        nxt = []
        for gen in live:
            try:
                next(gen)
                nxt.append(gen)
            except StopIteration:
                pass
        live = nxt


def _butterfly(ys, combine):
    span = 4
    while len(ys) > 1:
        ys = [combine(u, v, span) for u, v in zip(ys[0::2], ys[1::2])]
        span //= 2
    return ys[0]


def _rows_of_sums(xs):
    lands = _butterfly([[j] * 8 for j in range(8)],
                       lambda u, v, span: [u[r] if (r // span) % 2 else v[r] for r in range(8)])
    src = [None] * 8
    for r in range(8):
        src[lands[r]] = xs[r]
    row = lax.broadcasted_iota(jnp.int32, xs[0].shape, 0)

    def combine(u, v, span):
        return jnp.where((row // span) % 2 == 1, u + pltpu.roll(u, span, 0), v + pltpu.roll(v, 8 - span, 0))

    return _butterfly(src, combine)


def _block_diag_mask():
    ri = lax.broadcasted_iota(jnp.int32, (PAIR * HEAD, PAIR * HEAD), 0) // HEAD
    ci = lax.broadcasted_iota(jnp.int32, (PAIR * HEAD, PAIR * HEAD), 1) // HEAD
    return ri == ci


def _fill_off_diagonal(q_s, k_s, b_s, lhs, rhs):
    for i in range(1, CHUNK // SUB):
        lo = i * SUB
        ref = b_s[lo - 1:lo, :]
        qt = (q_s[lo:lo + SUB, :] * jnp.exp(b_s[lo:lo + SUB, :] - ref)).astype(BF16)
        kt = (k_s[0:lo, :] * jnp.exp(ref - b_s[0:lo, :])).astype(BF16)
        for h in range(PAIR):
            hl = slice(h * HEAD, (h + 1) * HEAD)
            lhs[h, lo:lo + SUB, (i - 1) * HEAD:i * HEAD] = qt[:, hl]
            rhs[h, 0:lo, (i - 1) * HEAD:i * HEAD] = kt[:, hl]


def hgrn_fwd(proj, lb, norm_g, *, rb, name, comm=None):
    _, T, D = proj.shape
    H = D // HEAD
    nb = T // rb
    nck = rb // CHUNK
    nsub = CHUNK // SUB
    W = PAIR * HEAD
    fam = CHUNK * SUB
    ntc = min(2 * TRIP_CHUNKS, nck)

    def body(pq_ref, fz_ref, pv_ref, pg_ref, lb_ref, ng_ref, o_ref, og_ref, st_ref, S, q_a, k_a, v_a, b_a, lhs_a, rhs_a,
             p_a):
        @pl.when(pl.program_id(1) == 0)
        def _():
            S[...] = jnp.zeros_like(S)

        @pl.when((pl.program_id(0) == 0) & (pl.program_id(1) == 0))
        def _():
            lhs_a[...] = jnp.zeros_like(lhs_a)
            rhs_a[...] = jnp.zeros_like(rhs_a)

        ri = lax.broadcasted_iota(jnp.int32, (CHUNK, CHUNK), 0)
        ci = lax.broadcasted_iota(jnp.int32, (CHUNK, CHUNK), 1)
        tri = (ci <= ri).astype(BF16)
        ones = jnp.ones((HEAD, HEAD), BF16)
        bd = _block_diag_mask()

        def chunk_stages(j, c):
            q_s, k_s, v_s, b_s, lhs, rhs, p_s = (r.at[j] for r in (q_a, k_a, v_a, b_a, lhs_a, rhs_a, p_a))
            base = c * CHUNK
            rows = pl.ds(pl.multiple_of(base, CHUNK), CHUNK)
            q, k, logf, _, _, _ = _gates(pq_ref[rows, :], fz_ref[rows, :], lb_ref[...])
            v = pv_ref[rows, :]
            b = _tri_dot(tri, logf)
            yield
            q_s[...] = q
            k_s[...] = k
            v_s[...] = v
            b_s[...] = b
            bl = b_s[CHUNK - 1:CHUNK, :]
            upd = _dot_tn(v.astype(BF16), (k * jnp.exp(bl - b)).astype(BF16))
            Sv = S[...]
            for h in range(PAIR):
                st_ref[h, c] = Sv[h * HEAD:(h + 1) * HEAD, h * HEAD:(h + 1) * HEAD]
            o_int = _dot_nt((q * jnp.exp(b)).astype(BF16), Sv.astype(BF16))
            S[...] = Sv * jnp.exp(bl) + jnp.where(bd, upd, 0.0)
            _fill_off_diagonal(q_s, k_s, b_s, lhs, rhs)
            a = [_dot_nt(lhs[h], rhs[h]) for h in range(PAIR)]
            yield

            def diag_products(blocks):
                for i in blocks:
                    lo = i * SUB
                    for s in range(SUB):
                        m = lax.broadcasted_iota(jnp.int32, (SUB, W), 0) >= s
                        at = (i * SUB + s) * SUB
                        e = jnp.exp(b_s[lo:lo + SUB, :] - b_s[lo + s:lo + s + 1, :])
                        p = jnp.where(m, q_s[lo:lo + SUB, :] * (k_s[lo + s:lo + s + 1, :] * e), 0.0).astype(BF16)
                        for h in range(PAIR):
                            p_s[h * fam + at:h * fam + at + SUB, :] = p[:, h * HEAD:(h + 1) * HEAD]

            diag_products(range(0, nsub // 2))
            yield
            off = [_dot(a[h].astype(BF16), v_s[:, h * HEAD:(h + 1) * HEAD].astype(BF16)) for h in range(PAIR)]
            diag_products(range(nsub // 2, nsub))
            yield
            rs = _dot(p_s[...], ones)
            yield
            for i in range(nsub):
                lo = i * SUB
                blk = pl.ds(pl.multiple_of(base + lo, SUB), SUB)
                for h in range(PAIR):
                    hl = slice(h * HEAD, (h + 1) * HEAD)
                    acc = o_int[lo:lo + SUB, hl] + off[h][lo:lo + SUB, :]
                    for s in range(SUB):
                        at = h * fam + (i * SUB + s) * SUB
                        acc = acc + rs[at:at + SUB, :] * v_s[lo + s:lo + s + 1, hl]
                    o_ref[blk, hl] = acc
                    rinv = lax.rsqrt(jnp.mean(acc * acc, axis=-1, keepdims=True) + RMS_EPS)
                    pg = pg_ref[blk, hl]
                    og_ref[blk, hl] = (acc * rinv * ng_ref[:, hl] * (pg * _sigmoid(pg))).astype(BF16)
            yield

        def trip(g, carry):
            _staggered([chunk_stages(j, g * ntc + j) for j in range(ntc)])
            return carry

        lax.fori_loop(0, nck // ntc, trip, 0)

    def grp(gi):
        return pl.BlockSpec((None, rb, W), lambda h, r: (gi, r, h))

    vec = pl.BlockSpec((1, W), lambda h, r: (0, h))
    return carried_call(
        body, comm, name=name, grid=(H // PAIR, nb),
        in_specs=[grp(0), grp(1), grp(2), grp(3), vec, vec],
        out_specs=[pl.BlockSpec((rb, W), lambda h, r: (r, h)),
                   pl.BlockSpec((rb, W), lambda h, r: (r, h)),
                   pl.BlockSpec((PAIR, nck, HEAD, HEAD), lambda h, r: (h, r, 0, 0))],
        out_shape=[jax.ShapeDtypeStruct((T, D), F32), jax.ShapeDtypeStruct((T, D), BF16),
                   jax.ShapeDtypeStruct((H, T // CHUNK, HEAD, HEAD), F32)],
        scratch_shapes=[pltpu.VMEM((W, W), F32)] + [pltpu.VMEM((ntc, CHUNK, W), F32)] * 4
        + [pltpu.VMEM((ntc, PAIR, CHUNK, (nsub - 1) * HEAD), BF16)] * 2 + [pltpu.VMEM((ntc, PAIR * fam, HEAD), BF16)],
        args=(proj, proj, proj, proj, lb, norm_g))


def hgrn_bwd(proj, o, dog, states, lb, norm_g, *, rb, name, comm=None):
    _, T, D = proj.shape
    H = D // HEAD
    nb = T // rb
    nck = rb // CHUNK
    nsub = CHUNK // SUB
    W = PAIR * HEAD
    fam = CHUNK * SUB
    ntc = min(TRIP_CHUNKS, nck)

    def body(pq_ref, fz_ref, pv_ref, pg_ref, o_ref, dog_ref, st_ref, lb_ref, ng_ref, dp_ref, s_ref,
             dS, *per_chunk):
        S0_a, lhs_a, rhs_a = per_chunk[0], per_chunk[10], per_chunk[11]

        @pl.when(pl.program_id(1) == 0)
        def _():
            dS[...] = jnp.zeros_like(dS)
            s_ref[...] = jnp.zeros_like(s_ref)

        @pl.when((pl.program_id(0) == 0) & (pl.program_id(1) == 0))
        def _():
            lhs_a[...] = jnp.zeros_like(lhs_a)
            rhs_a[...] = jnp.zeros_like(rhs_a)
            S0_a[...] = jnp.zeros_like(S0_a)

        ri = lax.broadcasted_iota(jnp.int32, (CHUNK, CHUNK), 0)
        ci = lax.broadcasted_iota(jnp.int32, (CHUNK, CHUNK), 1)
        tri = (ci <= ri).astype(BF16)
        triu = (ci >= ri).astype(BF16)
        below = (ri // SUB) > (ci // SUB)
        ones = jnp.ones((HEAD, HEAD), BF16)
        bd = _block_diag_mask()
        last_row = lax.broadcasted_iota(jnp.int32, (CHUNK, W), 0) == CHUNK - 1

        def chunk_stages(j, c):
            (S0, q_s, k_s, v_s, b_s, do_s, dq_s, dk_s, dv_s, cr_s, lhs, rhs, ke_s, qe_s, p_s) = (r.at[j] for r in per_chunk)
            rows = pl.ds(pl.multiple_of(c * CHUNK, CHUNK), CHUNK)
            lb_ = lb_ref[...]
            pq = pq_ref[rows, :]
            q, k, logf, f, sg, sq = _gates(pq, fz_ref[rows, :], lb_)
            v = pv_ref[rows, :]
            b = _tri_dot(tri, logf)
            dpg = []
            for h in range(PAIR):
                hl = slice(h * HEAD, (h + 1) * HEAD)
                oh = o_ref[rows, hl]
                dog_ = dog_ref[rows, hl]
                pg = pg_ref[rows, hl]
                ng = ng_ref[:, hl]
                spg = _sigmoid(pg)
                rinv = lax.rsqrt(jnp.mean(oh * oh, axis=-1, keepdims=True) + RMS_EPS)
                on = oh * rinv
                dpg.append(dog_ * (on * ng) * (spg * (1.0 + pg * (1.0 - spg))))
                don = dog_ * (pg * spg)
                s_ref[0:1, hl] += jnp.sum(don * on, axis=0, keepdims=True)
                dxn = don * ng
                do_s[:, hl] = rinv * (dxn - on * jnp.mean(dxn * on, axis=-1, keepdims=True))
                S0[hl, hl] = st_ref[h, c]
            yield
            q_s[...] = q
            k_s[...] = k
            v_s[...] = v
            b_s[...] = b
            do = do_s[...]
            dob = do.astype(BF16)
            vb = v.astype(BF16)
            eb = jnp.exp(b)
            bl = b_s[CHUNK - 1:CHUNK, :]
            ebl = jnp.exp(bl)
            ekk = jnp.exp(bl - b)
            upd = _dot_tn(dob, (q * eb).astype(BF16))
            S0v = S0[...]
            dSv = dS[...]
            dSb = dSv.astype(BF16)
            dq_s[...] = _dot(dob, S0v.astype(BF16)) * eb
            dk_state = _dot(vb, dSb) * ekk
            dk_s[...] = dk_state
            dv_s[...] = _dot_nt((k * ekk).astype(BF16), dSb)
            extra = jnp.sum(k * dk_state, axis=0, keepdims=True) + ebl * jnp.sum(S0v * dSv, axis=0, keepdims=True)
            dS[...] = dSv * ebl + jnp.where(bd, upd, 0.0)

            _fill_off_diagonal(q_s, k_s, b_s, lhs, rhs)
            at = [_dot_nt(rhs[h], lhs[h]) for h in range(PAIR)]
            daf = [jnp.where(below, _dot_nt(dob[:, h * HEAD:(h + 1) * HEAD], vb[:, h * HEAD:(h + 1) * HEAD]), 0.0)
                   for h in range(PAIR)]
            yield

            def diag_products(blocks):
                for i in blocks:
                    lo = i * SUB
                    for s in range(SUB):
                        m = lax.broadcasted_iota(jnp.int32, (SUB, W), 0) >= s
                        at_ = (i * SUB + s) * SUB
                        qi = q_s[lo:lo + SUB, :]
                        e = jnp.where(m, jnp.exp(b_s[lo:lo + SUB, :] - b_s[lo + s:lo + s + 1, :]), 0.0)
                        ke = k_s[lo + s:lo + s + 1, :] * e
                        ke_s[at_:at_ + SUB, :] = ke
                        qe_s[at_:at_ + SUB, :] = qi * e
                        pa = (qi * ke).astype(BF16)
                        pd = jnp.where(m, do_s[lo:lo + SUB, :] * v_s[lo + s:lo + s + 1, :], 0.0).astype(BF16)
                        for h in range(PAIR):
                            hl = slice(h * HEAD, (h + 1) * HEAD)
                            p_s[(2 * h) * fam + at_:(2 * h) * fam + at_ + SUB, :] = pa[:, hl]
                            p_s[(2 * h + 1) * fam + at_:(2 * h + 1) * fam + at_ + SUB, :] = pd[:, hl]

            diag_products(range(0, nsub // 2))
            yield
            dqb, dkb = [], []
            for h in range(PAIR):
                hl = slice(h * HEAD, (h + 1) * HEAD)
                dv_s[:, hl] += _dot(at[h].astype(BF16), dob[:, hl])
                dqb.append(_dot(daf[h].astype(BF16), rhs[h]))
                dkb.append(_dot(daf[h].T.astype(BF16), lhs[h]))
            diag_products(range(nsub // 2, nsub))
            yield
            rs = _dot(p_s[...], ones)
            cr_s[...] = jnp.zeros_like(cr_s)
            for i in range(1, nsub):
                lo = i * SUB
                ref = b_s[lo - 1:lo, :]
                eq = jnp.exp(b_s[lo:lo + SUB, :] - ref)
                ek = jnp.exp(ref - b_s[0:lo, :])
                qtf = q_s[lo:lo + SUB, :] * eq
                ktf = k_s[0:lo, :] * ek
                cb = slice((i - 1) * HEAD, i * HEAD)
                for h in range(PAIR):
                    hl = slice(h * HEAD, (h + 1) * HEAD)
                    dqi = dqb[h][lo:lo + SUB, cb]
                    dki = dkb[h][0:lo, cb]
                    dq_s[lo:lo + SUB, hl] += dqi * eq[:, hl]
                    dk_s[0:lo, hl] += dki * ek[:, hl]
                    cr_s[lo:lo + SUB, hl] += (lhs[h, lo:lo + SUB, cb].astype(F32) - qtf[:, hl]) * dqi
                    cr_s[0:lo, hl] -= (rhs[h, 0:lo, cb].astype(F32) - ktf[:, hl]) * dki
            yield
            for i in range(nsub):
                lo = i * SUB
                for h in range(PAIR):
                    hl = slice(h * HEAD, (h + 1) * HEAD)
                    doi = do_s[lo:lo + SUB, hl]
                    dqa = dq_s[lo:lo + SUB, hl]
                    xk, xv = [], []
                    for s in range(SUB):
                        at = (i * SUB + s) * SUB
                        acol = rs[(2 * h) * fam + at:(2 * h) * fam + at + SUB, :]
                        dacol = rs[(2 * h + 1) * fam + at:(2 * h + 1) * fam + at + SUB, :]
                        dqa = dqa + dacol * ke_s[at:at + SUB, hl]
                        pk = dacol * qe_s[at:at + SUB, hl]
                        pv = acol * doi
                        xk.append(pk[0:8, :] + pk[8:SUB, :])
                        xv.append(pv[0:8, :] + pv[8:SUB, :])
                    dq_s[lo:lo + SUB, hl] = dqa
                    for g8 in range(SUB // 8):
                        r8 = slice(lo + 8 * g8, lo + 8 * g8 + 8)
                        dk_s[r8, hl] += _rows_of_sums(xk[8 * g8:8 * g8 + 8])
                        dv_s[r8, hl] += _rows_of_sums(xv[8 * g8:8 * g8 + 8])

            dq = dq_s[...]
            dk = dk_s[...]
            db = q * dq - k * dk + cr_s[...] + jnp.where(last_row, extra, 0.0)
            dlogf = _tri_dot(triu, db)
            df = jnp.where(f > GATE_EPS, dlogf / jnp.maximum(f, GATE_EPS), 0.0) - dk
            s_ref[1:2, :] += jnp.sum(df * (1.0 - sg), axis=0, keepdims=True)
            dp_ref[0, rows, :] = (dq * (sq * (1.0 + pq * (1.0 - sq)))).astype(BF16)
            dp_ref[1, rows, :] = (df * (1.0 - lb_) * sg * (1.0 - sg)).astype(BF16)
            dp_ref[2, rows, :] = dv_s[...].astype(BF16)
            for h in range(PAIR):
                dp_ref[3, rows, h * HEAD:(h + 1) * HEAD] = dpg[h].astype(BF16)
            yield

        def trip(g, carry):
            _staggered([chunk_stages(j, nck - 1 - (g * ntc + j)) for j in range(ntc)])
            return carry

        lax.fori_loop(0, nck // ntc, trip, 0)

    def grp(gi):
        return pl.BlockSpec((None, rb, W), lambda h, r: (gi, nb - 1 - r, h))

    rowsp = pl.BlockSpec((rb, W), lambda h, r: (nb - 1 - r, h))
    vec = pl.BlockSpec((1, W), lambda h, r: (0, h))
    return carried_call(
        body, comm, name=name, grid=(H // PAIR, nb),
        in_specs=[grp(0), grp(1), grp(2), grp(3), rowsp, rowsp,
                  pl.BlockSpec((PAIR, nck, HEAD, HEAD), lambda h, r: (h, nb - 1 - r, 0, 0)), vec, vec],
        out_specs=[pl.BlockSpec((4, rb, W), lambda h, r: (0, nb - 1 - r, h)),
                   pl.BlockSpec((8, W), lambda h, r: (0, h))],
        out_shape=[jax.ShapeDtypeStruct((4, T, D), BF16), jax.ShapeDtypeStruct((8, D), F32)],
        scratch_shapes=[pltpu.VMEM((W, W), F32), pltpu.VMEM((ntc, W, W), F32)] + [pltpu.VMEM((ntc, CHUNK, W), F32)] * 9
        + [pltpu.VMEM((ntc, PAIR, CHUNK, (nsub - 1) * HEAD), BF16)] * 2 + [pltpu.VMEM((ntc, fam, W), F32)] * 2
        + [pltpu.VMEM((ntc, 2 * PAIR * fam, HEAD), BF16)],
        args=(proj, proj, proj, proj, o, dog, states, lb, norm_g))


def lb_fwd(logits):
    def body(l_ref, o_ref):
        l = l_ref[...]
        mx = jnp.max(l, axis=0, keepdims=True)
        e = jnp.exp(l - mx)
        sm = e / jnp.sum(e, axis=0, keepdims=True)
        o_ref[0:1, :] = jnp.zeros_like(sm[0:1, :])
        o_ref[1:2, :] = sm[1:2, :]

    return pl.pallas_call(body, name="lb_fwd", out_shape=jax.ShapeDtypeStruct(logits.shape, F32))(logits)


def lb_bwd(logits, dlb):
    def body(l_ref, d_ref, o_ref):
        l = l_ref[...]
        mx = jnp.max(l, axis=0, keepdims=True)
        e = jnp.exp(l - mx)
        sm = e / jnp.sum(e, axis=0, keepdims=True)
        inner = d_ref[1:2, :] * sm[1:2, :]
        o_ref[0:1, :] = sm[0:1, :] * (0.0 - inner)
        o_ref[1:2, :] = sm[1:2, :] * (d_ref[1:2, :] - inner)

    return pl.pallas_call(body, name="lb_bwd", out_shape=jax.ShapeDtypeStruct(logits.shape, F32))(logits, dlb)


def _shifted_copies(sh, rows):
    for b in range(1, 8):
        sh[b, 0:rows, :] = sh[0, b:b + rows, :]


def conv_fwd(u, vec, *, tm, name, comm=None):
    _, T, D = u.shape
    hb = tm // HALO
    nlc = D // HEAD

    def body(a_ref, gt_ref, ap_ref, gp_ref, w_ref, c_ref, v_ref, sh):
        i = pl.program_id(0)
        sh[0, HALO:HALO + tm, :] = a_ref[...] * _sigmoid(gt_ref[...])
        prev = ap_ref[...] * _sigmoid(gp_ref[...])
        sh[0, 0:HALO, :] = jnp.where(i > 0, prev, 0.0)
        _shifted_copies(sh, tm + HALO - 8)

        def rowblock(r, carry):
            r0 = r * CONV_RB
            for cl in range(nlc):
                ls = slice(cl * HEAD, (cl + 1) * HEAD)
                acc = jnp.zeros((CONV_RB, HEAD), F32) + w_ref[32:33, ls]
                for j in range(CONV_W):
                    o = j + 2
                    at = pl.ds(pl.multiple_of(r0 + o - o % 8, 8), CONV_RB)
                    acc = acc + w_ref[j:j + 1, ls] * sh[o % 8, at, ls]
                c_ref[pl.ds(pl.multiple_of(r0, CONV_RB), CONV_RB), ls] = acc
            return carry

        lax.fori_loop(0, tm // CONV_RB, rowblock, 0)
        xhat, _ = _ln_stats(c_ref[...])
        y = xhat * w_ref[33:34, :] + w_ref[34:35, :]
        v_ref[...] = (y * _sigmoid(y)).astype(BF16)

    cur = lambda gi: pl.BlockSpec((None, tm, D), lambda i: (gi, i, 0))
    prv = lambda gi: pl.BlockSpec((None, HALO, D), lambda i: (gi, jnp.maximum(i * hb - 1, 0), 0))
    return carried_call(
        body, comm, name=name, grid=(T // tm,),
        in_specs=[cur(0), cur(1), prv(0), prv(1), pl.BlockSpec((VEC_ROWS, D), lambda i: (0, 0))],
        out_specs=[pl.BlockSpec((tm, D), lambda i: (i, 0)), pl.BlockSpec((tm, D), lambda i: (i, 0))],
        out_shape=[jax.ShapeDtypeStruct((T, D), F32), jax.ShapeDtypeStruct((T, D), BF16)],
        scratch_shapes=[pltpu.VMEM((8, tm + HALO, D), F32)],
        args=(u, u, u, u, vec))


def conv_bwd(dv2, c, u, vec, *, tm, name, comm=None):
    _, T, D = u.shape
    hb = tm // HALO
    nt = T // tm
    nh = T // HALO
    nlc = D // HEAD
    acc_rows = {j: j for j in range(CONV_W)}
    acc_rows.update({36: CONV_W, 37: CONV_W + 1})

    def body(dv_ref, c_ref, dvn_ref, cn_ref, a_ref, gt_ref, ap_ref, gp_ref, w_ref, du_ref, s_ref, gsh, dsh, part):
        i = pl.program_id(0)

        @pl.when(i == 0)
        def _():
            s_ref[...] = jnp.zeros_like(s_ref)
            part[...] = jnp.zeros_like(part)

        gam = w_ref[33:34, :]
        bet = w_ref[34:35, :]

        def dconv(dv, cc):
            xhat, rstd = _ln_stats(cc)
            y = xhat * gam + bet
            sy = _sigmoid(y)
            dy = dv * (sy * (1.0 + y * (1.0 - sy)))
            return _ln_bwd(dy, xhat, rstd, gam), dy, xhat

        dc, dy, xhat = dconv(dv_ref[...], c_ref[...])
        dcn, _, _ = dconv(dvn_ref[...], cn_ref[...])
        dsh[0, 0:tm, :] = dc
        dsh[0, tm:tm + HALO, :] = jnp.where(i < nt - 1, dcn, 0.0)
        gsh[0, HALO:HALO + tm, :] = a_ref[...] * _sigmoid(gt_ref[...])
        gsh[0, 0:HALO, :] = jnp.where(i > 0, ap_ref[...] * _sigmoid(gp_ref[...]), 0.0)
        s_ref[32:33, :] += jnp.sum(dc, axis=0, keepdims=True)
        s_ref[33:34, :] += jnp.sum(dy * xhat, axis=0, keepdims=True)
        s_ref[34:35, :] += jnp.sum(dy, axis=0, keepdims=True)
        _shifted_copies(dsh, tm + HALO - 8)
        _shifted_copies(gsh, tm + HALO - 8)

        def fold8(x):
            acc = x[0:8, :]
            for g in range(1, CONV_RB // 8):
                acc = acc + x[8 * g:8 * g + 8, :]
            return acc

        for cl in range(nlc):
            ls = slice(cl * HEAD, (cl + 1) * HEAD)

            def rowblock(r, sums, ls=ls):
                r0 = r * CONV_RB
                rows = pl.ds(pl.multiple_of(r0, CONV_RB), CONV_RB)
                dcb = dsh[0, rows, ls]
                dglu = jnp.zeros((CONV_RB, HEAD), F32)
                new = []
                for j in range(CONV_W):
                    od = 30 - j
                    og = j + 2
                    atd = pl.ds(pl.multiple_of(r0 + od - od % 8, 8), CONV_RB)
                    atg = pl.ds(pl.multiple_of(r0 + og - og % 8, 8), CONV_RB)
                    dglu = dglu + w_ref[j:j + 1, ls] * dsh[od % 8, atd, ls]
                    new.append(sums[j] + fold8(dcb * gsh[og % 8, atg, ls]))
                a = a_ref[rows, ls]
                sgt = _sigmoid(gt_ref[rows, ls])
                da = (dglu * sgt).astype(BF16)
                dg = (dglu * a * sgt * (1.0 - sgt)).astype(BF16)
                du_ref[0, rows, ls] = da
                du_ref[1, rows, ls] = dg
                new.append(sums[CONV_W] + fold8(da.astype(F32)))
                new.append(sums[CONV_W + 1] + fold8(dg.astype(F32)))
                return tuple(new)

            zero = jnp.zeros((8, HEAD), F32)
            sums = lax.fori_loop(0, tm // CONV_RB, rowblock, (zero,) * (CONV_W + 2))
            for k in range(CONV_W + 2):
                part[8 * k:8 * k + 8, ls] += sums[k]

        @pl.when(i == nt - 1)
        def _():
            for row, k in acc_rows.items():
                s_ref[row:row + 1, :] = jnp.sum(part[8 * k:8 * k + 8, :], axis=0, keepdims=True)

    row = lambda i: (i, 0)
    nxt = lambda i: (jnp.minimum((i + 1) * hb, nh - 1), 0)
    cur = lambda gi: pl.BlockSpec((None, tm, D), lambda i: (gi, i, 0))
    prv = lambda gi: pl.BlockSpec((None, HALO, D), lambda i: (gi, jnp.maximum(i * hb - 1, 0), 0))
    fix = lambda i: (0, 0)
    return carried_call(
        body, comm, name=name, grid=(nt,),
        in_specs=[pl.BlockSpec((tm, D), row), pl.BlockSpec((tm, D), row),
                  pl.BlockSpec((HALO, D), nxt), pl.BlockSpec((HALO, D), nxt),
                  cur(0), cur(1), prv(0), prv(1), pl.BlockSpec((VEC_ROWS, D), fix)],
        out_specs=[pl.BlockSpec((2, tm, D), lambda i: (0, i, 0)), pl.BlockSpec((VEC_ROWS, D), fix)],
        out_shape=[jax.ShapeDtypeStruct((2, T, D), BF16), jax.ShapeDtypeStruct((VEC_ROWS, D), F32)],
        scratch_shapes=[pltpu.VMEM((8, tm + HALO, D), F32), pltpu.VMEM((8, tm + HALO, D), F32),
                        pltpu.VMEM((8 * (CONV_W + 2), D), F32)],
        args=(dv2, c, dv2, c, u, u, u, u, vec))


def loss_grad(y, target, *, tm):
    T, D = y.shape
    nt = T // tm

    def body(y_ref, t_ref, l_ref, d_ref, acc):
        i = pl.program_id(0)

        @pl.when(i == 0)
        def _():
            acc[...] = jnp.zeros_like(acc)

        e = y_ref[...] - t_ref[...]
        d_ref[...] = e * (1.0 / D)
        acc[...] += jnp.sum(e * e, axis=0, keepdims=True)

        @pl.when(i == nt - 1)
        def _():
            l_ref[...] = 0.5 * jnp.sum(acc[...], axis=1, keepdims=True) * (1.0 / D)

    row = lambda i: (i, 0)
    return _call(
        body, name="loss_grad", grid=(nt,),
        in_specs=[pl.BlockSpec((tm, D), row), pl.BlockSpec((tm, D), row)],
        out_specs=[pl.BlockSpec((1, 1), lambda i: (0, 0)), pl.BlockSpec((tm, D), row)],
        out_shape=[jax.ShapeDtypeStruct((1, 1), F32), jax.ShapeDtypeStruct((T, D), F32)],
        scratch_shapes=[pltpu.VMEM((1, D), F32)], args=(y, target))


def _rows_block(R, C, budget=1 << 20):
    tr = R
    while tr * C * 4 > budget and tr % 32 == 0:
        tr //= 2
    return tr


def adamw(w, g, m, v, *, name):
    R, C = w.shape
    tr = _rows_block(R, C)

    def body(w_ref, g_ref, m_ref, v_ref, d_ref, mo_ref, vo_ref):
        g_ = g_ref[...]
        mn = ADAM_B1 * m_ref[...] + (1.0 - ADAM_B1) * g_
        vn = ADAM_B2 * v_ref[...] + (1.0 - ADAM_B2) * jnp.square(g_)
        m_hat = mn / (1.0 - ADAM_B1 ** ADAM_STEP)
        v_hat = vn / (1.0 - ADAM_B2 ** ADAM_STEP)
        d_ref[...] = -ADAM_LR * (m_hat / (jnp.sqrt(v_hat) + ADAM_EPS) + ADAM_WD * w_ref[...])
        mo_ref[...] = mn
        vo_ref[...] = vn

    spec = pl.BlockSpec((tr, C), lambda i: (i, 0))
    sd = jax.ShapeDtypeStruct((R, C), F32)
    return _call(body, name=name, grid=(R // tr,), in_specs=[spec] * 4, out_specs=[spec] * 3, out_shape=[sd] * 3,
                 args=(w, g, m, v))


def sum_slots(slots, *, name):
    _, R, C = slots.shape
    tr = _rows_block(R, C, budget=1 << 19)

    def body(s_ref, o_ref):
        acc = s_ref[0].astype(F32)
        for d in range(1, 8):
            acc = acc + s_ref[d].astype(F32)
        o_ref[...] = acc

    return _call(body, name=name, grid=(R // tr,), in_specs=[pl.BlockSpec((8, tr, C), lambda i: (0, i, 0))],
                 out_specs=[pl.BlockSpec((tr, C), lambda i: (i, 0))], out_shape=[jax.ShapeDtypeStruct((R, C), F32)],
                 args=(slots,))[0]


def _adam_nd(w, g, m, v, name):
    shp = w.shape
    c = shp[-1]
    f2 = lambda a: a.reshape(-1, c)
    d, mn, vn = adamw(f2(w), f2(g), f2(m), f2(v), name=name)
    return d.reshape(shp), mn.reshape(shp), vn.reshape(shp)


def _reduced(slots, name):
    out = []
    for s in slots:
        c = s.shape[-1]
        out.append(sum_slots(s.reshape(8, -1, c), name=name).reshape(s.shape[1:]))
    return out


def kernel(x, ln_mix_g, ln_mix_b, ln_ffn_g, ln_ffn_b, ffn_w1, ffn_w2, a_w_in, a_lb_logits, a_norm_g, a_w_out, b_w_pw1, b_b_pw1, b_w_dw, b_b_dw, b_ln_g, b_ln_b, b_w_pw2, b_b_pw2, loss_target, m_ln_mix_g, m_ln_mix_b, m_ln_ffn_g, m_ln_ffn_b, m_ffn_w1, m_ffn_w2, m_a_w_in, m_a_lb_logits, m_a_norm_g, m_a_w_out, m_b_w_pw1, m_b_b_pw1, m_b_w_dw, m_b_b_dw, m_b_ln_g, m_b_ln_b, m_b_w_pw2, m_b_b_pw2, v_ln_mix_g, v_ln_mix_b, v_ln_ffn_g, v_ln_ffn_b, v_ffn_w1, v_ffn_w2, v_a_w_in, v_a_lb_logits, v_a_norm_g, v_a_w_out, v_b_w_pw1, v_b_b_pw1, v_b_w_dw, v_b_b_dw, v_b_ln_g, v_b_ln_b, v_b_w_pw2, v_b_b_pw2):
    names = ["ln_mix_g", "ln_mix_b", "ln_ffn_g", "ln_ffn_b", "ffn_w1", "ffn_w2", "a_w_in", "a_lb_logits", "a_norm_g",
             "a_w_out", "b_w_pw1", "b_b_pw1", "b_w_dw", "b_b_dw", "b_ln_g", "b_ln_b", "b_w_pw2", "b_b_pw2"]
    w = dict(zip(names, [ln_mix_g, ln_mix_b, ln_ffn_g, ln_ffn_b, ffn_w1, ffn_w2, a_w_in, a_lb_logits, a_norm_g, a_w_out,
                         b_w_pw1, b_b_pw1, b_w_dw, b_b_dw, b_ln_g, b_ln_b, b_w_pw2, b_b_pw2]))
    m = dict(zip(names, [m_ln_mix_g, m_ln_mix_b, m_ln_ffn_g, m_ln_ffn_b, m_ffn_w1, m_ffn_w2, m_a_w_in, m_a_lb_logits,
                         m_a_norm_g, m_a_w_out, m_b_w_pw1, m_b_b_pw1, m_b_w_dw, m_b_b_dw, m_b_ln_g, m_b_ln_b, m_b_w_pw2,
                         m_b_b_pw2]))
    v = dict(zip(names, [v_ln_mix_g, v_ln_mix_b, v_ln_ffn_g, v_ln_ffn_b, v_ffn_w1, v_ffn_w2, v_a_w_in, v_a_lb_logits,
                         v_a_norm_g, v_a_w_out, v_b_w_pw1, v_b_b_pw1, v_b_w_dw, v_b_b_dw, v_b_ln_g, v_b_ln_b, v_b_w_pw2,
                         v_b_b_pw2]))
    T, D = x.shape[1], x.shape[2]
    DS = D // 4
    F = 4 * ffn_w1.shape[2]
    chip = 2 * lax.axis_index("x") + lax.axis_index("y")
    tm = min(T, 512)
    tmw = min(T, 1024)
    tmc = min(T, 256)
    rb = min(T, 1024)
    tf = min(F // 4, 1024)
    xin, target = x[0], loss_target[0]

    def mix_shards(i):
        j = i // 2
        if i % 2 == 0:
            return [a_w_in[j].astype(BF16), a_w_out[j].astype(BF16)]
        vec = jnp.concatenate([b_w_dw[j], jnp.zeros((1, DS), F32), b_b_dw[j][None], b_ln_g[j][None], b_ln_b[j][None],
                               b_b_pw2[j][None], b_b_pw1[j].reshape(2, DS), jnp.zeros((2, DS), F32)], axis=0)
        return [b_w_pw1[j].astype(BF16), b_w_pw2[j].astype(BF16), vec]

    def ffn_shards(i):
        return [ffn_w1[i].astype(BF16), ffn_w2[i].astype(BF16)]

    def mix_weights(i, got):
        if i % 2 == 0:
            return {"w_in": got[0], "w_out": got[1].reshape(D, D)}
        pw1 = jnp.transpose(got[0].reshape(2, 2, D, D // 2), (0, 2, 1, 3)).reshape(2, D, D)
        vec = jnp.transpose(got[2], (1, 0, 2)).reshape(VEC_ROWS, D)
        return {"pw1": pw1, "pw2": got[1].reshape(D, D), "vec": vec,
                "b_pw1": got[2][:, 36:38, :].reshape(2, 1, D)}

    lb_all = lb_fwd(a_lb_logits)
    zeros_bias = jnp.zeros((1, D), F32)

    first = mix_shards(0)
    mixw = {"w_in": gather_chips_via_sibling(first[0])}
    saved = []
    h, hb = xin, xin.astype(BF16)
    for i in range(DEPTH):
        j = i // 2
        s = {"xb": hb, "mixw": mixw}
        gf = GatherChips(ffn_shards(i) + (first[1:] if i == 0 else []))
        if i % 2 == 0:
            s["proj"] = mm_groups(hb, mixw["w_in"], jnp.zeros((4, 1, D), F32), tm=tmw, name="a_in_proj")
            (s["o"], s["og"], s["st"]), got = hgrn_fwd(s["proj"], lb_all[j:j + 1], a_norm_g[j:j + 1], rb=rb,
                                                       name="hgrn_fwd", comm=gf)
            if i == 0:
                mixw["w_out"] = got[2].reshape(D, D)
            s["r1"], x1, s["x1b"] = mm_res_ln(s["og"], mixw["w_out"], zeros_bias, h, ln_mix_g[i:i + 1],
                                              ln_mix_b[i:i + 1], tm=tmw, name="a_out_ln")
        else:
            s["u"] = mm_groups(hb, mixw["pw1"], mixw["b_pw1"], tm=tmw, name="b_pw1")
            (s["c"], s["v2"]), got = conv_fwd(s["u"], mixw["vec"], tm=min(T, 2 * tmc), name="conv_fwd", comm=gf)
            s["r1"], x1, s["x1b"] = mm_res_ln(s["v2"], mixw["pw2"], mixw["vec"][35:36], h, ln_mix_g[i:i + 1],
                                              ln_mix_b[i:i + 1], tm=tmw, name="b_pw2_ln")
        s["w1"], s["w2"] = got[0], got[1].reshape(F, D)
        gm = GatherChips(mix_shards(i + 1)) if i + 1 < DEPTH else None
        (s["z"], s["r2"], h, hb), got = ffn_fwd(x1, s["w1"], s["w2"], ln_ffn_g[i:i + 1], ln_ffn_b[i:i + 1],
                                                tm=tm, tf=tf, name="ffn_fwd", comm=gm)
        if gm is not None:
            mixw = mix_weights(i + 1, got)
        saved.append(s)

    loss_part, dh = loss_grad(h, target, tm=tmw)
    loss = lax.psum(loss_part[0, 0], ("x", "y", "c"))

    gr = {k: [None] * DEPTH for k in ("ln_mix_g", "ln_mix_b", "ln_ffn_g", "ln_ffn_b", "ffn_w1", "ffn_w2")}
    for k in ("a_w_in", "a_w_out", "a_norm_g", "a_dlb", "b_w_pw1", "b_w_pw2", "b_vec", "b_b_pw2"):
        gr[k] = [None] * 2
    w_in_name = ("a_w_in", "b_w_pw1")
    w_out_name = ("a_w_out", "b_w_pw2")
    pending = None

    for i in reversed(range(DEPTH)):
        j = i // 2
        s = saved[i]
        mixw = s["mixw"]
        sm = ScatterPieces([pending[1]]) if pending is not None else None
        (dz, dx1, drb2, sums2), slots = ffn_bwd_dx(dh, s["r2"], ln_ffn_g[i:i + 1], s["z"], s["w1"], s["w2"],
                                                   tm=tm, tf=tf, name="ffn_bwd_dx", comm=sm)
        if pending is not None:
            gr[w_in_name[pending[0] % 2]][pending[0] // 2] = _reduced(slots, "sum_mix_grads")[0]
        gr["ln_ffn_g"][i], gr["ln_ffn_b"][i] = sums2[0], sums2[1]
        dw1 = mm_tn(s["x1b"], dz[None], tm=tmw, tk=D, tn=F // 4, name="ffn_dw1")[0]
        dw2 = mm_tn(s["z"], drb2[None], tm=tmw, tk=F // 4, tn=D, relu2=True, name="ffn_dw2")[0, 0]
        wmix = mixw["w_out"] if i % 2 == 0 else mixw["pw2"]
        dr1, drb1, dmo, sums1 = ln_bwd_mm(dx1, s["r1"], ln_mix_g[i:i + 1], wmix, tm=tmw, name="mix_ln_bwd")
        gr["ln_mix_g"][i], gr["ln_mix_b"][i] = sums1[0], sums1[1]
        if i % 2 == 0:
            dwo = mm_tn(s["og"], drb1[None], tm=tmw, tk=D, tn=D, name="a_dw_out")[0, 0].reshape(4, DS, D)
            sf = ScatterPieces([dw1, dw2.reshape(4, F // 4, D), dwo])
            (dproj, hs), slots = hgrn_bwd(s["proj"], s["o"], dmo, s["st"], lb_all[j:j + 1], a_norm_g[j:j + 1], rb=rb,
                                          name="hgrn_bwd", comm=sf)
            gr["a_norm_g"][j], gr["a_dlb"][j] = hs[0], hs[1]
            dwi = mm_tn(s["xb"], dproj, tm=tmw, tk=D, tn=D, name="a_dw_in")[:, 0]
            dy_in, w_in_t, dx_name = dproj, mixw["w_in"], "a_dx"
        else:
            dwo = mm_tn(s["v2"], drb1[None], tm=tmw, tk=D, tn=D, name="b_dw_pw2")[0, 0].reshape(4, DS, D)
            sf = ScatterPieces([dw1, dw2.reshape(4, F // 4, D), dwo])
            (du, cs), slots = conv_bwd(dmo, s["c"], s["u"], mixw["vec"], tm=tmc, name="conv_bwd", comm=sf)
            gr["b_vec"][j], gr["b_b_pw2"][j] = cs, sums1[2]
            dwi = mm_tn(s["xb"], du, tm=tmw, tk=D, tn=D // 2, name="b_dw_pw1").reshape(4, D, D // 2)
            dy_in, w_in_t, dx_name = du, mixw["pw1"], "b_dx"
        gr["ffn_w1"][i], gr["ffn_w2"][i], gr[w_out_name[i % 2]][j] = _reduced(slots, "sum_ffn_grads")
        dh, slots = mm_nt_acc(dy_in, w_in_t, dr1, tm=tmw, name=dx_name, comm=ScatterPiecesViaSibling([dwi]) if i == 0 else None)
        if i == 0:
            gr[w_in_name[0]][0] = _reduced(slots, "sum_mix_grads")[0]
        pending = (i, dwi)
    grad_x = dh[None]

    small = {k: jnp.stack(gr[k]) for k in ("ln_mix_g", "ln_mix_b", "ln_ffn_g", "ln_ffn_b", "a_norm_g", "b_vec", "b_b_pw2")}
    small["a_lb_logits"] = lb_bwd(a_lb_logits, jnp.stack(gr["a_dlb"]))
    small_names = ["ln_mix_g", "ln_mix_b", "ln_ffn_g", "ln_ffn_b", "a_lb_logits", "a_norm_g", "b_b_pw2", "b_vec"]
    rows = [small[k].reshape(-1, D) for k in small_names]
    counts = [r.shape[0] for r in rows]
    rows = [jnp.pad(r, ((0, (-r.shape[0]) % 8), (0, 0))) for r in rows]
    summed = all_reduce_small(jnp.concatenate(rows, axis=0))
    sm = {}
    off = 0
    for k, n, r in zip(small_names, counts, rows):
        sm[k] = summed[off:off + n]
        off += r.shape[0]
    bvec = sm["b_vec"].reshape(2, VEC_ROWS, D)

    def shard_cols(a):
        return lax.dynamic_slice_in_dim(a, chip * DS, DS, axis=a.ndim - 1)

    grads = {k: jnp.stack(gr[k]) for k in ("ffn_w1", "ffn_w2", "a_w_in", "a_w_out", "b_w_pw1", "b_w_pw2")}
    for k in ("ln_mix_g", "ln_mix_b", "ln_ffn_g", "ln_ffn_b", "a_lb_logits", "a_norm_g"):
        grads[k] = sm[k]
    grads["b_b_pw1"] = lax.dynamic_slice_in_dim(bvec[:, 36:38, :].reshape(2, 2 * D), chip * (D // 2), D // 2, axis=1)
    grads["b_w_dw"] = shard_cols(bvec[:, 0:CONV_W, :])
    grads["b_b_dw"] = shard_cols(bvec[:, 32, :])
    grads["b_ln_g"] = shard_cols(bvec[:, 33, :])
    grads["b_ln_b"] = shard_cols(bvec[:, 34, :])
    grads["b_b_pw2"] = shard_cols(sm["b_b_pw2"])

    delta, new_m, new_v = {}, {}, {}
    for k in names:
        delta[k], new_m[k], new_v[k] = _adam_nd(w[k], grads[k], m[k], v[k], "adamw_" + k)
    return (loss, grad_x, *[grads[k] for k in names], *[delta[k] for k in names],
            *[new_m[k] for k in names], *[new_v[k] for k in names])
```

```python
import jax
import jax.numpy as jnp
from jax import lax
from jax.experimental import pallas as pl
from jax.experimental.pallas import tpu as pltpu

F32 = jnp.float32
BF16 = jnp.bfloat16
MESH = pl.DeviceIdType.MESH

DEPTH = 4
ALPHA = (2.0 * DEPTH) ** 0.25
LN_EPS = 1e-5
RMS_EPS = 1e-6
GATE_EPS = 1e-6
HEAD = 128
CHUNK = 128
SUB = 16
PAIR = 2
TRIP_CHUNKS = 2
CONV_W = 31
HALO = 32
VEC_ROWS = 40
CONV_RB = 32
ADAM_LR, ADAM_B1, ADAM_B2, ADAM_EPS, ADAM_WD, ADAM_STEP = 0.001, 0.9, 0.999, 1e-08, 0.01, 10
VMEM_LIMIT = 56 * 1024 * 1024
ANY = pl.BlockSpec(memory_space=pl.ANY)


def _dot(a, b):
    return jnp.dot(a, b, preferred_element_type=F32)


def _dot_nt(a, b):
    return lax.dot_general(a, b, (((1,), (1,)), ((), ())), preferred_element_type=F32)


def _dot_tn(a, b):
    return lax.dot_general(a, b, (((0,), (0,)), ((), ())), preferred_element_type=F32)


def _sigmoid(x):
    return 1.0 / (1.0 + jnp.exp(-x))


def _ln_stats(r):
    mu = jnp.mean(r, axis=-1, keepdims=True)
    xc = r - mu
    var = jnp.mean(xc * xc, axis=-1, keepdims=True)
    rstd = lax.rsqrt(var + LN_EPS)
    return xc * rstd, rstd


def _ln_bwd(dy, xhat, rstd, g):
    dyg = dy * g
    m1 = jnp.mean(dyg, axis=-1, keepdims=True)
    m2 = jnp.mean(dyg * xhat, axis=-1, keepdims=True)
    return rstd * (dyg - m1 - xhat * m2)


def _place():
    return lax.axis_index("x"), lax.axis_index("y"), lax.axis_index("c")


class GatherChips:
    def __init__(self, arrs):
        self.ins = list(arrs)
        n = len(arrs)
        self.out_shapes = [jax.ShapeDtypeStruct((4,) + a.shape, a.dtype) for a in arrs]
        self.sems = [pltpu.SemaphoreType.DMA((3 * n,)), pltpu.SemaphoreType.DMA((3 * n,)),
                     pltpu.SemaphoreType.DMA((n,))]

    def copies(self, ins, outs, send, recv, loc):
        x, y, c = _place()
        me = 2 * x + y
        local, remote = [], []
        for a in range(len(ins)):
            local.append(pltpu.make_async_copy(ins[a], outs[a].at[me], loc.at[a]))
            for j, (px, py) in enumerate([(1 - x, y), (x, 1 - y), (1 - x, 1 - y)]):
                remote.append(pltpu.make_async_remote_copy(
                    src_ref=ins[a], dst_ref=outs[a].at[me], send_sem=send.at[3 * a + j], recv_sem=recv.at[3 * a + j],
                    device_id=(px, py, c), device_id_type=MESH))
        return local + remote


class ScatterPieces:
    def __init__(self, arrs):
        self.ins = list(arrs)
        n = len(arrs)
        self.out_shapes = [jax.ShapeDtypeStruct((8,) + a.shape[1:], a.dtype) for a in arrs]
        self.sems = [pltpu.SemaphoreType.DMA((7 * n,)), pltpu.SemaphoreType.DMA((7 * n,)),
                     pltpu.SemaphoreType.DMA((n,))]

    def copies(self, ins, outs, send, recv, loc):
        x, y, c = _place()
        me = 4 * x + 2 * y + c
        local, remote = [], []
        for a in range(len(ins)):
            local.append(pltpu.make_async_copy(ins[a].at[2 * x + y], outs[a].at[me], loc.at[a]))
            k = 0
            for fx in (0, 1):
                for fy in (0, 1):
                    for fc in (0, 1):
                        if fx or fy or fc:
                            tx, ty = x ^ fx, y ^ fy
                            remote.append(pltpu.make_async_remote_copy(
                                src_ref=ins[a].at[2 * tx + ty], dst_ref=outs[a].at[me],
                                send_sem=send.at[7 * a + k], recv_sem=recv.at[7 * a + k],
                                device_id=(tx, ty, c ^ fc), device_id_type=MESH))
                            k += 1
        return local + remote


class ScatterPiecesViaSibling:
    def __init__(self, arrs):
        self.ins = list(arrs)
        n = len(arrs)
        self.out_shapes = [jax.ShapeDtypeStruct((8,) + a.shape[1:], a.dtype) for a in arrs]
        self.sems = [pltpu.SemaphoreType.DMA((3 * n,)), pltpu.SemaphoreType.DMA((3 * n,)),
                     pltpu.SemaphoreType.DMA((n,)), pltpu.SemaphoreType.DMA((4 * n,)), pltpu.SemaphoreType.DMA((4 * n,))]

    def copies(self, ins, outs, send, recv, loc, send2, recv2):
        x, y, c = _place()
        me = 4 * x + 2 * y + c
        local, remote = [], []
        for a in range(len(ins)):
            local.append(pltpu.make_async_copy(ins[a].at[2 * x + y], outs[a].at[me], loc.at[a]))
            for j, (px, py) in enumerate([(1 - x, y), (x, 1 - y), (1 - x, 1 - y)]):
                remote.append(pltpu.make_async_remote_copy(
                    src_ref=ins[a].at[2 * px + py], dst_ref=outs[a].at[me], send_sem=send.at[3 * a + j],
                    recv_sem=recv.at[3 * a + j], device_id=(px, py, c), device_id_type=MESH))
        return local + remote

    def forwards(self, ins, outs, send, recv, loc, send2, recv2):
        x, y, c = _place()
        passed = []
        for a in range(len(ins)):
            for k, (qx, qy) in enumerate([(x, y), (1 - x, y), (x, 1 - y), (1 - x, 1 - y)]):
                slot = outs[a].at[4 * qx + 2 * qy + c]
                passed.append(pltpu.make_async_remote_copy(
                    src_ref=slot, dst_ref=slot, send_sem=send2.at[4 * a + k], recv_sem=recv2.at[4 * a + k],
                    device_id=(x, y, 1 - c), device_id_type=MESH))
        return passed


def carried_call(body, comm, *, name, grid, in_specs, out_specs, out_shape, scratch_shapes, args):
    sem = ("arbitrary",) * len(grid)
    params = pltpu.CompilerParams(dimension_semantics=sem, vmem_limit_bytes=VMEM_LIMIT)
    if comm is None:
        res = pl.pallas_call(body, name=name, grid=grid, in_specs=in_specs, out_specs=out_specs, out_shape=out_shape,
                             scratch_shapes=scratch_shapes, compiler_params=params)(*args)
        return res, []
    ni, no, nscr = len(in_specs), len(out_specs), len(scratch_shapes)
    ci, co = len(comm.ins), len(comm.out_shapes)

    def both(*refs):
        ins, refs = refs[:ni], refs[ni:]
        cins, refs = refs[:ci], refs[ci:]
        outs, refs = refs[:no], refs[no:]
        couts, refs = refs[:co], refs[co:]
        scr, sems = refs[:nscr], refs[nscr:]
        first = pl.program_id(0) == 0
        last = pl.program_id(0) == grid[0] - 1
        for d in range(1, len(grid)):
            first = first & (pl.program_id(d) == 0)
            last = last & (pl.program_id(d) == grid[d] - 1)

        @pl.when(first)
        def _():
            for cp in comm.copies(cins, couts, *sems):
                cp.start()

        body(*ins, *outs, *scr)

        @pl.when(last)
        def _():
            for cp in comm.copies(cins, couts, *sems):
                cp.wait()
            if hasattr(comm, "forwards"):
                passed = comm.forwards(cins, couts, *sems)
                for cp in passed:
                    cp.start()
                for cp in passed:
                    cp.wait()

    res = pl.pallas_call(
        both, name=name, grid=grid, in_specs=list(in_specs) + [ANY] * ci, out_specs=list(out_specs) + [ANY] * co,
        out_shape=list(out_shape) + comm.out_shapes, scratch_shapes=list(scratch_shapes) + comm.sems,
        compiler_params=params)(*args, *comm.ins)
    return res[:no], res[no:]


def gather_chips_via_sibling(shard):
    half = shard.shape[0] // 2

    def body(in_ref, out_ref, send1, recv1, send2, recv2, loc):
        x, y, c = _place()
        me = 2 * x + y
        mine = pl.ds(c * half, half)
        local = pltpu.make_async_copy(in_ref, out_ref.at[me], loc)
        local.start()
        chips = [(1 - x, y), (x, 1 - y), (1 - x, 1 - y)]
        first = [pltpu.make_async_remote_copy(
            src_ref=in_ref.at[mine], dst_ref=out_ref.at[me, mine], send_sem=send1.at[j], recv_sem=recv1.at[j],
            device_id=(px, py, c), device_id_type=MESH) for j, (px, py) in enumerate(chips)]
        for cp in first:
            cp.start()
        passed = []
        for j, (px, py) in enumerate(chips):
            first[j].wait_recv()
            got = out_ref.at[2 * px + py, mine]
            cp = pltpu.make_async_remote_copy(src_ref=got, dst_ref=got, send_sem=send2.at[j], recv_sem=recv2.at[j],
                                              device_id=(x, y, 1 - c), device_id_type=MESH)
            cp.start()
            passed.append(cp)
        for cp in first:
            cp.wait_send()
        for cp in passed:
            cp.wait()
        local.wait()

    return pl.pallas_call(
        body, name="gather_first", in_specs=[ANY], out_specs=ANY,
        out_shape=jax.ShapeDtypeStruct((4,) + shard.shape, shard.dtype),
        scratch_shapes=[pltpu.SemaphoreType.DMA((3,))] * 4 + [pltpu.SemaphoreType.DMA],
        compiler_params=pltpu.CompilerParams(has_side_effects=True))(shard)


def all_reduce_small(v):
    R, C = v.shape

    def body(v_ref, o_ref, slots, send, recv, send2, recv2):
        x, y, c = _place()
        me = 4 * x + 2 * y + c
        slots[me] = v_ref[...]
        chips = [(1 - x, y), (x, 1 - y), (1 - x, 1 - y)]
        first = [pltpu.make_async_remote_copy(
            src_ref=v_ref, dst_ref=slots.at[me], send_sem=send.at[j], recv_sem=recv.at[j],
            device_id=(px, py, c), device_id_type=MESH) for j, (px, py) in enumerate(chips)]
        for cp in first:
            cp.start()
        for cp in first:
            cp.wait()
        passed = []
        for k, (qx, qy) in enumerate([(x, y)] + chips):
            slot = slots.at[4 * qx + 2 * qy + c]
            passed.append(pltpu.make_async_remote_copy(
                src_ref=slot, dst_ref=slot, send_sem=send2.at[k], recv_sem=recv2.at[k],
                device_id=(x, y, 1 - c), device_id_type=MESH))
        for cp in passed:
            cp.start()
        for cp in passed:
            cp.wait()
        acc = slots[0]
        for d in range(1, 8):
            acc = acc + slots[d]
        o_ref[...] = acc

    vm = pl.BlockSpec(memory_space=pltpu.VMEM)
    return pl.pallas_call(
        body, name="all_reduce_small", in_specs=[vm], out_specs=vm,
        out_shape=jax.ShapeDtypeStruct((R, C), F32),
        scratch_shapes=[pltpu.VMEM((8, R, C), F32), pltpu.SemaphoreType.DMA((3,)), pltpu.SemaphoreType.DMA((3,)),
                        pltpu.SemaphoreType.DMA((4,)), pltpu.SemaphoreType.DMA((4,))],
        compiler_params=pltpu.CompilerParams(has_side_effects=True, vmem_limit_bytes=VMEM_LIMIT),
    )(v)


def _call(body, *, name, grid, in_specs, out_specs, out_shape, scratch_shapes=(), args):
    res, _ = carried_call(body, None, name=name, grid=grid, in_specs=in_specs, out_specs=out_specs,
                          out_shape=out_shape, scratch_shapes=list(scratch_shapes), args=args)
    return res


def mm_groups(a, w, bias, *, tm, name):
    T, K = a.shape
    G, _, N = w.shape

    def body(a_ref, w_ref, b_ref, o_ref):
        o_ref[...] = _dot(a_ref[...], w_ref[...]) + b_ref[...]

    return _call(
        body, name=name, grid=(G, T // tm),
        in_specs=[pl.BlockSpec((tm, K), lambda g, i: (i, 0)),
                  pl.BlockSpec((None, K, N), lambda g, i: (g, 0, 0)),
                  pl.BlockSpec((None, 1, N), lambda g, i: (g, 0, 0))],
        out_specs=[pl.BlockSpec((None, tm, N), lambda g, i: (g, i, 0))],
        out_shape=[jax.ShapeDtypeStruct((G, T, N), F32)], args=(a, w, bias))[0]


def mm_res_ln(a, w, bias, res, g, b, *, tm, name):
    T, K = a.shape
    N = w.shape[1]

    def body(a_ref, w_ref, bias_ref, res_ref, g_ref, b_ref, r_ref, y_ref, yb_ref):
        r = ALPHA * res_ref[...] + _dot(a_ref[...], w_ref[...]) + bias_ref[...]
        r_ref[...] = r
        xhat, _ = _ln_stats(r)
        y = xhat * g_ref[...] + b_ref[...]
        y_ref[...] = y
        yb_ref[...] = y.astype(BF16)

    row = lambda i: (i, 0)
    fix = lambda i: (0, 0)
    return _call(
        body, name=name, grid=(T // tm,),
        in_specs=[pl.BlockSpec((tm, K), row), pl.BlockSpec((K, N), fix), pl.BlockSpec((1, N), fix),
                  pl.BlockSpec((tm, N), row), pl.BlockSpec((1, N), fix), pl.BlockSpec((1, N), fix)],
        out_specs=[pl.BlockSpec((tm, N), row), pl.BlockSpec((tm, N), row), pl.BlockSpec((tm, N), row)],
        out_shape=[jax.ShapeDtypeStruct((T, N), F32), jax.ShapeDtypeStruct((T, N), F32),
                   jax.ShapeDtypeStruct((T, N), BF16)],
        args=(a, w, bias, res, g, b))


def ffn_fwd(x, w1, w2, g, b, *, tm, tf, name, comm=None):
    T, D = x.shape
    NC, _, FC = w1.shape
    F = NC * FC
    per = FC // tf
    nf = F // tf

    def body(x_ref, w1_ref, w2_ref, g_ref, b_ref, z_ref, r_ref, y_ref, yb_ref, acc_ref, xb_ref):
        f = pl.program_id(1)

        @pl.when(f == 0)
        def _():
            acc_ref[...] = jnp.zeros_like(acc_ref)
            xb_ref[...] = x_ref[...].astype(BF16)

        z = _dot(xb_ref[...], w1_ref[...])
        z_ref[...] = z.astype(BF16)
        h = jnp.square(jnp.maximum(z, 0.0)).astype(BF16)
        acc_ref[...] += _dot(h, w2_ref[...])

        @pl.when(f == nf - 1)
        def _():
            r = ALPHA * x_ref[...] + acc_ref[...]
            r_ref[...] = r
            xhat, _ = _ln_stats(r)
            y = xhat * g_ref[...] + b_ref[...]
            y_ref[...] = y
            yb_ref[...] = y.astype(BF16)

    return carried_call(
        body, comm, name=name, grid=(T // tm, nf),
        in_specs=[pl.BlockSpec((tm, D), lambda i, f: (i, 0)),
                  pl.BlockSpec((None, D, tf), lambda i, f: (f // per, 0, f % per)),
                  pl.BlockSpec((tf, D), lambda i, f: (f, 0)),
                  pl.BlockSpec((1, D), lambda i, f: (0, 0)),
                  pl.BlockSpec((1, D), lambda i, f: (0, 0))],
        out_specs=[pl.BlockSpec((tm, tf), lambda i, f: (i, f)),
                   pl.BlockSpec((tm, D), lambda i, f: (i, 0)),
                   pl.BlockSpec((tm, D), lambda i, f: (i, 0)),
                   pl.BlockSpec((tm, D), lambda i, f: (i, 0))],
        out_shape=[jax.ShapeDtypeStruct((T, F), BF16), jax.ShapeDtypeStruct((T, D), F32),
                   jax.ShapeDtypeStruct((T, D), F32), jax.ShapeDtypeStruct((T, D), BF16)],
        scratch_shapes=[pltpu.VMEM((tm, D), F32), pltpu.VMEM((tm, D), BF16)],
        args=(x, w1, w2, g, b))


def ln_bwd_mm(dy, r, g, w, *, tm, name):
    T, N = dy.shape
    Ko = w.shape[0]

    def body(dy_ref, r_ref, g_ref, w_ref, dr_ref, drb_ref, o_ref, s_ref):
        @pl.when(pl.program_id(0) == 0)
        def _():
            s_ref[...] = jnp.zeros_like(s_ref)

        dy_ = dy_ref[...]
        xhat, rstd = _ln_stats(r_ref[...])
        dr = _ln_bwd(dy_, xhat, rstd, g_ref[...])
        dr_ref[...] = dr
        drb = dr.astype(BF16)
        drb_ref[...] = drb
        o_ref[...] = _dot_nt(drb, w_ref[...])
        s_ref[0:1, :] += jnp.sum(dy_ * xhat, axis=0, keepdims=True)
        s_ref[1:2, :] += jnp.sum(dy_, axis=0, keepdims=True)
        s_ref[2:3, :] += jnp.sum(dr, axis=0, keepdims=True)

    row = lambda i: (i, 0)
    fix = lambda i: (0, 0)
    return _call(
        body, name=name, grid=(T // tm,),
        in_specs=[pl.BlockSpec((tm, N), row), pl.BlockSpec((tm, N), row), pl.BlockSpec((1, N), fix),
                  pl.BlockSpec((Ko, N), fix)],
        out_specs=[pl.BlockSpec((tm, N), row), pl.BlockSpec((tm, N), row), pl.BlockSpec((tm, Ko), row),
                   pl.BlockSpec((8, N), fix)],
        out_shape=[jax.ShapeDtypeStruct((T, N), F32), jax.ShapeDtypeStruct((T, N), BF16),
                   jax.ShapeDtypeStruct((T, Ko), F32), jax.ShapeDtypeStruct((8, N), F32)],
        args=(dy, r, g, w))


def ffn_bwd_dx(dy, r, g, z, w1, w2, *, tm, tf, name, comm=None):
    T, D = dy.shape
    NC, _, FC = w1.shape
    F = NC * FC
    per = FC // tf
    nf = F // tf

    def body(dy_ref, r_ref, g_ref, z_ref, w1_ref, w2_ref, dz_ref, dx_ref, drb_ref, s_ref, dr_scr, acc_ref):
        i = pl.program_id(0)
        f = pl.program_id(1)

        @pl.when((i == 0) & (f == 0))
        def _():
            s_ref[...] = jnp.zeros_like(s_ref)

        @pl.when(f == 0)
        def _():
            dy_ = dy_ref[...]
            xhat, rstd = _ln_stats(r_ref[...])
            dr = _ln_bwd(dy_, xhat, rstd, g_ref[...])
            dr_scr[...] = dr
            drb_ref[...] = dr.astype(BF16)
            acc_ref[...] = jnp.zeros_like(acc_ref)
            s_ref[0:1, :] += jnp.sum(dy_ * xhat, axis=0, keepdims=True)
            s_ref[1:2, :] += jnp.sum(dy_, axis=0, keepdims=True)

        dh = _dot_nt(drb_ref[...], w2_ref[...])
        dz = (dh * (2.0 * jnp.maximum(z_ref[...].astype(F32), 0.0))).astype(BF16)
        dz_ref[...] = dz
        acc_ref[...] += _dot_nt(dz, w1_ref[...])

        @pl.when(f == nf - 1)
        def _():
            dx_ref[...] = ALPHA * dr_scr[...] + acc_ref[...]

    return carried_call(
        body, comm, name=name, grid=(T // tm, nf),
        in_specs=[pl.BlockSpec((tm, D), lambda i, f: (i, 0)),
                  pl.BlockSpec((tm, D), lambda i, f: (i, 0)),
                  pl.BlockSpec((1, D), lambda i, f: (0, 0)),
                  pl.BlockSpec((tm, tf), lambda i, f: (i, f)),
                  pl.BlockSpec((None, D, tf), lambda i, f: (f // per, 0, f % per)),
                  pl.BlockSpec((tf, D), lambda i, f: (f, 0))],
        out_specs=[pl.BlockSpec((tm, tf), lambda i, f: (i, f)),
                   pl.BlockSpec((tm, D), lambda i, f: (i, 0)),
                   pl.BlockSpec((tm, D), lambda i, f: (i, 0)),
                   pl.BlockSpec((8, D), lambda i, f: (0, 0))],
        out_shape=[jax.ShapeDtypeStruct((T, F), BF16), jax.ShapeDtypeStruct((T, D), F32),
                   jax.ShapeDtypeStruct((T, D), BF16), jax.ShapeDtypeStruct((8, D), F32)],
        scratch_shapes=[pltpu.VMEM((tm, D), F32), pltpu.VMEM((tm, D), F32)],
        args=(dy, r, g, z, w1, w2))


def mm_tn(a, b, *, tm, tk, tn, relu2=False, name):
    T, K = a.shape
    G, _, N = b.shape
    nt = T // tm

    def body(a_ref, b_ref, o_ref, acc_ref):
        t = pl.program_id(3)

        @pl.when(t == 0)
        def _():
            acc_ref[...] = jnp.zeros_like(acc_ref)

        av = a_ref[...]
        if relu2:
            av = jnp.square(jnp.maximum(av.astype(F32), 0.0))
        acc_ref[...] += _dot_tn(av.astype(BF16), b_ref[...])

        @pl.when(t == nt - 1)
        def _():
            o_ref[...] = acc_ref[...].astype(BF16)

    return _call(
        body, name=name, grid=(G, K // tk, N // tn, nt),
        in_specs=[pl.BlockSpec((tm, tk), lambda g, k, n, t: (t, k)),
                  pl.BlockSpec((None, tm, tn), lambda g, k, n, t: (g, t, n))],
        out_specs=[pl.BlockSpec((None, None, tk, tn), lambda g, k, n, t: (g, n, k, 0))],
        out_shape=[jax.ShapeDtypeStruct((G, N // tn, K, tn), BF16)],
        scratch_shapes=[pltpu.VMEM((tk, tn), F32)], args=(a, b))[0]


def mm_nt_acc(dy, w, base, *, tm, name, comm=None):
    G, T, N = dy.shape
    K = w.shape[1]

    def body(dy_ref, w_ref, base_ref, o_ref):
        g = pl.program_id(1)

        @pl.when(g == 0)
        def _():
            o_ref[...] = ALPHA * base_ref[...]

        o_ref[...] += _dot_nt(dy_ref[...], w_ref[...])

    res, got = carried_call(
        body, comm, name=name, grid=(T // tm, G),
        in_specs=[pl.BlockSpec((None, tm, N), lambda i, g: (g, i, 0)),
                  pl.BlockSpec((None, K, N), lambda i, g: (g, 0, 0)),
                  pl.BlockSpec((tm, K), lambda i, g: (i, 0))],
        out_specs=[pl.BlockSpec((tm, K), lambda i, g: (i, 0))],
        out_shape=[jax.ShapeDtypeStruct((T, K), F32)], scratch_shapes=[], args=(dy, w, base))
    return res[0], got


def _split3(x):
    x1 = x.astype(BF16)
    r1 = x - x1.astype(F32)
    x2 = r1.astype(BF16)
    x3 = (r1 - x2.astype(F32)).astype(BF16)
    return x1, x2, x3


def _tri_dot(tri, x):
    x1, x2, x3 = _split3(x)
    return _dot(tri, x1) + _dot(tri, x2) + _dot(tri, x3)


def _gates(pq, fz, lb):
    sg = _sigmoid(fz)
    f = lb + (1.0 - lb) * sg
    logf = jnp.log(jnp.maximum(f, GATE_EPS))
    sq = _sigmoid(pq)
    return pq * sq, 1.0 - f, logf, f, sg, sq


def _staggered(gens):
    live = []
    waiting = list(gens)
    for gen in waiting:
        next(gen)
    while live or waiting:
        if waiting:
            live.append(waiting.pop(0))
        nxt = []
        for gen in live:
            try:
                next(gen)
                nxt.append(gen)
            except StopIteration:
                pass
        live = nxt


def _butterfly(ys, combine):
    span = 4
    while len(ys) > 1:
        ys = [combine(u, v, span) for u, v in zip(ys[0::2], ys[1::2])]
        span //= 2
    return ys[0]


def _rows_of_sums(xs):
    lands = _butterfly([[j] * 8 for j in range(8)],
                       lambda u, v, span: [u[r] if (r // span) % 2 else v[r] for r in range(8)])
    src = [None] * 8
    for r in range(8):
        src[lands[r]] = xs[r]
    row = lax.broadcasted_iota(jnp.int32, xs[0].shape, 0)

    def combine(u, v, span):
        return jnp.where((row // span) % 2 == 1, u + pltpu.roll(u, span, 0), v + pltpu.roll(v, 8 - span, 0))

    return _butterfly(src, combine)


def _block_diag_mask():
    ri = lax.broadcasted_iota(jnp.int32, (PAIR * HEAD, PAIR * HEAD), 0) // HEAD
    ci = lax.broadcasted_iota(jnp.int32, (PAIR * HEAD, PAIR * HEAD), 1) // HEAD
    return ri == ci


def _fill_off_diagonal(q_s, k_s, b_s, lhs, rhs):
    for i in range(1, CHUNK // SUB):
        lo = i * SUB
        ref = b_s[lo - 1:lo, :]
        qt = (q_s[lo:lo + SUB, :] * jnp.exp(b_s[lo:lo + SUB, :] - ref)).astype(BF16)
        kt = (k_s[0:lo, :] * jnp.exp(ref - b_s[0:lo, :])).astype(BF16)
        for h in range(PAIR):
            hl = slice(h * HEAD, (h + 1) * HEAD)
            lhs[h, lo:lo + SUB, (i - 1) * HEAD:i * HEAD] = qt[:, hl]
            rhs[h, 0:lo, (i - 1) * HEAD:i * HEAD] = kt[:, hl]


def hgrn_fwd(proj, lb, norm_g, *, rb, name, comm=None):
    _, T, D = proj.shape
    H = D // HEAD
    nb = T // rb
    nck = rb // CHUNK
    nsub = CHUNK // SUB
    W = PAIR * HEAD
    fam = CHUNK * SUB
    ntc = min(2 * TRIP_CHUNKS, nck)

    def body(pq_ref, fz_ref, pv_ref, pg_ref, lb_ref, ng_ref, o_ref, og_ref, st_ref, S, q_a, k_a, v_a, b_a, lhs_a, rhs_a,
             p_a):
        @pl.when(pl.program_id(1) == 0)
        def _():
            S[...] = jnp.zeros_like(S)

        @pl.when((pl.program_id(0) == 0) & (pl.program_id(1) == 0))
        def _():
            lhs_a[...] = jnp.zeros_like(lhs_a)
            rhs_a[...] = jnp.zeros_like(rhs_a)

        ri = lax.broadcasted_iota(jnp.int32, (CHUNK, CHUNK), 0)
        ci = lax.broadcasted_iota(jnp.int32, (CHUNK, CHUNK), 1)
        tri = (ci <= ri).astype(BF16)
        ones = jnp.ones((HEAD, HEAD), BF16)
        bd = _block_diag_mask()

        def chunk_stages(j, c):
            q_s, k_s, v_s, b_s, lhs, rhs, p_s = (r.at[j] for r in (q_a, k_a, v_a, b_a, lhs_a, rhs_a, p_a))
            base = c * CHUNK
            rows = pl.ds(pl.multiple_of(base, CHUNK), CHUNK)
            q, k, logf, _, _, _ = _gates(pq_ref[rows, :], fz_ref[rows, :], lb_ref[...])
            v = pv_ref[rows, :]
            b = _tri_dot(tri, logf)
            yield
            q_s[...] = q
            k_s[...] = k
            v_s[...] = v
            b_s[...] = b
            bl = b_s[CHUNK - 1:CHUNK, :]
            upd = _dot_tn(v.astype(BF16), (k * jnp.exp(bl - b)).astype(BF16))
            Sv = S[...]
            for h in range(PAIR):
                st_ref[h, c] = Sv[h * HEAD:(h + 1) * HEAD, h * HEAD:(h + 1) * HEAD]
            o_int = _dot_nt((q * jnp.exp(b)).astype(BF16), Sv.astype(BF16))
            S[...] = Sv * jnp.exp(bl) + jnp.where(bd, upd, 0.0)
            _fill_off_diagonal(q_s, k_s, b_s, lhs, rhs)
            a = [_dot_nt(lhs[h], rhs[h]) for h in range(PAIR)]
            yield

            def diag_products(blocks):
                for i in blocks:
                    lo = i * SUB
                    for s in range(SUB):
                        m = lax.broadcasted_iota(jnp.int32, (SUB, W), 0) >= s
                        at = (i * SUB + s) * SUB
                        e = jnp.exp(b_s[lo:lo + SUB, :] - b_s[lo + s:lo + s + 1, :])
                        p = jnp.where(m, q_s[lo:lo + SUB, :] * (k_s[lo + s:lo + s + 1, :] * e), 0.0).astype(BF16)
                        for h in range(PAIR):
                            p_s[h * fam + at:h * fam + at + SUB, :] = p[:, h * HEAD:(h + 1) * HEAD]

            diag_products(range(0, nsub // 2))
            yield
            off = [_dot(a[h].astype(BF16), v_s[:, h * HEAD:(h + 1) * HEAD].astype(BF16)) for h in range(PAIR)]
            diag_products(range(nsub // 2, nsub))
            yield
            rs = _dot(p_s[...], ones)
            yield
            for i in range(nsub):
                lo = i * SUB
                blk = pl.ds(pl.multiple_of(base + lo, SUB), SUB)
                for h in range(PAIR):
                    hl = slice(h * HEAD, (h + 1) * HEAD)
                    acc = o_int[lo:lo + SUB, hl] + off[h][lo:lo + SUB, :]
                    for s in range(SUB):
                        at = h * fam + (i * SUB + s) * SUB
                        acc = acc + rs[at:at + SUB, :] * v_s[lo + s:lo + s + 1, hl]
                    o_ref[blk, hl] = acc
                    rinv = lax.rsqrt(jnp.mean(acc * acc, axis=-1, keepdims=True) + RMS_EPS)
                    pg = pg_ref[blk, hl]
                    og_ref[blk, hl] = (acc * rinv * ng_ref[:, hl] * (pg * _sigmoid(pg))).astype(BF16)
            yield

        def trip(g, carry):
            _staggered([chunk_stages(j, g * ntc + j) for j in range(ntc)])
            return carry

        lax.fori_loop(0, nck // ntc, trip, 0)

    def grp(gi):
        return pl.BlockSpec((None, rb, W), lambda h, r: (gi, r, h))

    vec = pl.BlockSpec((1, W), lambda h, r: (0, h))
    return carried_call(
        body, comm, name=name, grid=(H // PAIR, nb),
        in_specs=[grp(0), grp(1), grp(2), grp(3), vec, vec],
        out_specs=[pl.BlockSpec((rb, W), lambda h, r: (r, h)),
                   pl.BlockSpec((rb, W), lambda h, r: (r, h)),
                   pl.BlockSpec((PAIR, nck, HEAD, HEAD), lambda h, r: (h, r, 0, 0))],
        out_shape=[jax.ShapeDtypeStruct((T, D), F32), jax.ShapeDtypeStruct((T, D), BF16),
                   jax.ShapeDtypeStruct((H, T // CHUNK, HEAD, HEAD), F32)],
        scratch_shapes=[pltpu.VMEM((W, W), F32)] + [pltpu.VMEM((ntc, CHUNK, W), F32)] * 4
        + [pltpu.VMEM((ntc, PAIR, CHUNK, (nsub - 1) * HEAD), BF16)] * 2 + [pltpu.VMEM((ntc, PAIR * fam, HEAD), BF16)],
        args=(proj, proj, proj, proj, lb, norm_g))


def hgrn_bwd(proj, o, dog, states, lb, norm_g, *, rb, name, comm=None):
    _, T, D = proj.shape
    H = D // HEAD
    nb = T // rb
    nck = rb // CHUNK
    nsub = CHUNK // SUB
    W = PAIR * HEAD
    fam = CHUNK * SUB
    ntc = min(TRIP_CHUNKS, nck)

    def body(pq_ref, fz_ref, pv_ref, pg_ref, o_ref, dog_ref, st_ref, lb_ref, ng_ref, dp_ref, s_ref,
             dS, *per_chunk):
        S0_a, lhs_a, rhs_a = per_chunk[0], per_chunk[10], per_chunk[11]

        @pl.when(pl.program_id(1) == 0)
        def _():
            dS[...] = jnp.zeros_like(dS)
            s_ref[...] = jnp.zeros_like(s_ref)

        @pl.when((pl.program_id(0) == 0) & (pl.program_id(1) == 0))
        def _():
            lhs_a[...] = jnp.zeros_like(lhs_a)
            rhs_a[...] = jnp.zeros_like(rhs_a)
            S0_a[...] = jnp.zeros_like(S0_a)

        ri = lax.broadcasted_iota(jnp.int32, (CHUNK, CHUNK), 0)
        ci = lax.broadcasted_iota(jnp.int32, (CHUNK, CHUNK), 1)
        tri = (ci <= ri).astype(BF16)
        triu = (ci >= ri).astype(BF16)
        below = (ri // SUB) > (ci // SUB)
        ones = jnp.ones((HEAD, HEAD), BF16)
        bd = _block_diag_mask()
        last_row = lax.broadcasted_iota(jnp.int32, (CHUNK, W), 0) == CHUNK - 1

        def chunk_stages(j, c):
            (S0, q_s, k_s, v_s, b_s, do_s, dq_s, dk_s, dv_s, cr_s, lhs, rhs, ke_s, qe_s, p_s) = (r.at[j] for r in per_chunk)
            rows = pl.ds(pl.multiple_of(c * CHUNK, CHUNK), CHUNK)
            lb_ = lb_ref[...]
            pq = pq_ref[rows, :]
            q, k, logf, f, sg, sq = _gates(pq, fz_ref[rows, :], lb_)
            v = pv_ref[rows, :]
            b = _tri_dot(tri, logf)
            dpg = []
            for h in range(PAIR):
                hl = slice(h * HEAD, (h + 1) * HEAD)
                oh = o_ref[rows, hl]
                dog_ = dog_ref[rows, hl]
                pg = pg_ref[rows, hl]
                ng = ng_ref[:, hl]
                spg = _sigmoid(pg)
                rinv = lax.rsqrt(jnp.mean(oh * oh, axis=-1, keepdims=True) + RMS_EPS)
                on = oh * rinv
                dpg.append(dog_ * (on * ng) * (spg * (1.0 + pg * (1.0 - spg))))
                don = dog_ * (pg * spg)
                s_ref[0:1, hl] += jnp.sum(don * on, axis=0, keepdims=True)
                dxn = don * ng
                do_s[:, hl] = rinv * (dxn - on * jnp.mean(dxn * on, axis=-1, keepdims=True))
                S0[hl, hl] = st_ref[h, c]
            yield
            q_s[...] = q
            k_s[...] = k
            v_s[...] = v
            b_s[...] = b
            do = do_s[...]
            dob = do.astype(BF16)
            vb = v.astype(BF16)
            eb = jnp.exp(b)
            bl = b_s[CHUNK - 1:CHUNK, :]
            ebl = jnp.exp(bl)
            ekk = jnp.exp(bl - b)
            upd = _dot_tn(dob, (q * eb).astype(BF16))
            S0v = S0[...]
            dSv = dS[...]
            dSb = dSv.astype(BF16)
            dq_s[...] = _dot(dob, S0v.astype(BF16)) * eb
            dk_state = _dot(vb, dSb) * ekk
            dk_s[...] = dk_state
            dv_s[...] = _dot_nt((k * ekk).astype(BF16), dSb)
            extra = jnp.sum(k * dk_state, axis=0, keepdims=True) + ebl * jnp.sum(S0v * dSv, axis=0, keepdims=True)
            dS[...] = dSv * ebl + jnp.where(bd, upd, 0.0)

            _fill_off_diagonal(q_s, k_s, b_s, lhs, rhs)
            at = [_dot_nt(rhs[h], lhs[h]) for h in range(PAIR)]
            daf = [jnp.where(below, _dot_nt(dob[:, h * HEAD:(h + 1) * HEAD], vb[:, h * HEAD:(h + 1) * HEAD]), 0.0)
                   for h in range(PAIR)]
            yield

            def diag_products(blocks):
                for i in blocks:
                    lo = i * SUB
                    for s in range(SUB):
                        m = lax.broadcasted_iota(jnp.int32, (SUB, W), 0) >= s
                        at_ = (i * SUB + s) * SUB
                        qi = q_s[lo:lo + SUB, :]
                        e = jnp.where(m, jnp.exp(b_s[lo:lo + SUB, :] - b_s[lo + s:lo + s + 1, :]), 0.0)
                        ke = k_s[lo + s:lo + s + 1, :] * e
                        ke_s[at_:at_ + SUB, :] = ke
                        qe_s[at_:at_ + SUB, :] = qi * e
                        pa = (qi * ke).astype(BF16)
                        pd = jnp.where(m, do_s[lo:lo + SUB, :] * v_s[lo + s:lo + s + 1, :], 0.0).astype(BF16)
                        for h in range(PAIR):
                            hl = slice(h * HEAD, (h + 1) * HEAD)
                            p_s[(2 * h) * fam + at_:(2 * h) * fam + at_ + SUB, :] = pa[:, hl]
                            p_s[(2 * h + 1) * fam + at_:(2 * h + 1) * fam + at_ + SUB, :] = pd[:, hl]

            diag_products(range(0, nsub // 2))
            yield
            dqb, dkb = [], []
            for h in range(PAIR):
                hl = slice(h * HEAD, (h + 1) * HEAD)
                dv_s[:, hl] += _dot(at[h].astype(BF16), dob[:, hl])
                dqb.append(_dot(daf[h].astype(BF16), rhs[h]))
                dkb.append(_dot(daf[h].T.astype(BF16), lhs[h]))
            diag_products(range(nsub // 2, nsub))
            yield
            rs = _dot(p_s[...], ones)
            cr_s[...] = jnp.zeros_like(cr_s)
            for i in range(1, nsub):
                lo = i * SUB
                ref = b_s[lo - 1:lo, :]
                eq = jnp.exp(b_s[lo:lo + SUB, :] - ref)
                ek = jnp.exp(ref - b_s[0:lo, :])
                qtf = q_s[lo:lo + SUB, :] * eq
                ktf = k_s[0:lo, :] * ek
                cb = slice((i - 1) * HEAD, i * HEAD)
                for h in range(PAIR):
                    hl = slice(h * HEAD, (h + 1) * HEAD)
                    dqi = dqb[h][lo:lo + SUB, cb]
                    dki = dkb[h][0:lo, cb]
                    dq_s[lo:lo + SUB, hl] += dqi * eq[:, hl]
                    dk_s[0:lo, hl] += dki * ek[:, hl]
                    cr_s[lo:lo + SUB, hl] += (lhs[h, lo:lo + SUB, cb].astype(F32) - qtf[:, hl]) * dqi
                    cr_s[0:lo, hl] -= (rhs[h, 0:lo, cb].astype(F32) - ktf[:, hl]) * dki
            yield
            for i in range(nsub):
                lo = i * SUB
                for h in range(PAIR):
                    hl = slice(h * HEAD, (h + 1) * HEAD)
                    doi = do_s[lo:lo + SUB, hl]
                    dqa = dq_s[lo:lo + SUB, hl]
                    xk, xv = [], []
                    for s in range(SUB):
                        at = (i * SUB + s) * SUB
                        acol = rs[(2 * h) * fam + at:(2 * h) * fam + at + SUB, :]
                        dacol = rs[(2 * h + 1) * fam + at:(2 * h + 1) * fam + at + SUB, :]
                        dqa = dqa + dacol * ke_s[at:at + SUB, hl]
                        pk = dacol * qe_s[at:at + SUB, hl]
                        pv = acol * doi
                        xk.append(pk[0:8, :] + pk[8:SUB, :])
                        xv.append(pv[0:8, :] + pv[8:SUB, :])
                    dq_s[lo:lo + SUB, hl] = dqa
                    for g8 in range(SUB // 8):
                        r8 = slice(lo + 8 * g8, lo + 8 * g8 + 8)
                        dk_s[r8, hl] += _rows_of_sums(xk[8 * g8:8 * g8 + 8])
                        dv_s[r8, hl] += _rows_of_sums(xv[8 * g8:8 * g8 + 8])

            dq = dq_s[...]
            dk = dk_s[...]
            db = q * dq - k * dk + cr_s[...] + jnp.where(last_row, extra, 0.0)
            dlogf = _tri_dot(triu, db)
            df = jnp.where(f > GATE_EPS, dlogf / jnp.maximum(f, GATE_EPS), 0.0) - dk
            s_ref[1:2, :] += jnp.sum(df * (1.0 - sg), axis=0, keepdims=True)
            dp_ref[0, rows, :] = (dq * (sq * (1.0 + pq * (1.0 - sq)))).astype(BF16)
            dp_ref[1, rows, :] = (df * (1.0 - lb_) * sg * (1.0 - sg)).astype(BF16)
            dp_ref[2, rows, :] = dv_s[...].astype(BF16)
            for h in range(PAIR):
                dp_ref[3, rows, h * HEAD:(h + 1) * HEAD] = dpg[h].astype(BF16)
            yield

        def trip(g, carry):
            _staggered([chunk_stages(j, nck - 1 - (g * ntc + j)) for j in range(ntc)])
            return carry

        lax.fori_loop(0, nck // ntc, trip, 0)

    def grp(gi):
        return pl.BlockSpec((None, rb, W), lambda h, r: (gi, nb - 1 - r, h))

    rowsp = pl.BlockSpec((rb, W), lambda h, r: (nb - 1 - r, h))
    vec = pl.BlockSpec((1, W), lambda h, r: (0, h))
    return carried_call(
        body, comm, name=name, grid=(H // PAIR, nb),
        in_specs=[grp(0), grp(1), grp(2), grp(3), rowsp, rowsp,
                  pl.BlockSpec((PAIR, nck, HEAD, HEAD), lambda h, r: (h, nb - 1 - r, 0, 0)), vec, vec],
        out_specs=[pl.BlockSpec((4, rb, W), lambda h, r: (0, nb - 1 - r, h)),
                   pl.BlockSpec((8, W), lambda h, r: (0, h))],
        out_shape=[jax.ShapeDtypeStruct((4, T, D), BF16), jax.ShapeDtypeStruct((8, D), F32)],
        scratch_shapes=[pltpu.VMEM((W, W), F32), pltpu.VMEM((ntc, W, W), F32)] + [pltpu.VMEM((ntc, CHUNK, W), F32)] * 9
        + [pltpu.VMEM((ntc, PAIR, CHUNK, (nsub - 1) * HEAD), BF16)] * 2 + [pltpu.VMEM((ntc, fam, W), F32)] * 2
        + [pltpu.VMEM((ntc, 2 * PAIR * fam, HEAD), BF16)],
        args=(proj, proj, proj, proj, o, dog, states, lb, norm_g))


def lb_fwd(logits):
    def body(l_ref, o_ref):
        l = l_ref[...]
        mx = jnp.max(l, axis=0, keepdims=True)
        e = jnp.exp(l - mx)
        sm = e / jnp.sum(e, axis=0, keepdims=True)
        o_ref[0:1, :] = jnp.zeros_like(sm[0:1, :])
        o_ref[1:2, :] = sm[1:2, :]

    return pl.pallas_call(body, name="lb_fwd", out_shape=jax.ShapeDtypeStruct(logits.shape, F32))(logits)


def lb_bwd(logits, dlb):
    def body(l_ref, d_ref, o_ref):
        l = l_ref[...]
        mx = jnp.max(l, axis=0, keepdims=True)
        e = jnp.exp(l - mx)
        sm = e / jnp.sum(e, axis=0, keepdims=True)
        inner = d_ref[1:2, :] * sm[1:2, :]
        o_ref[0:1, :] = sm[0:1, :] * (0.0 - inner)
        o_ref[1:2, :] = sm[1:2, :] * (d_ref[1:2, :] - inner)

    return pl.pallas_call(body, name="lb_bwd", out_shape=jax.ShapeDtypeStruct(logits.shape, F32))(logits, dlb)


def _shifted_copies(sh, rows):
    for b in range(1, 8):
        sh[b, 0:rows, :] = sh[0, b:b + rows, :]


def conv_fwd(u, vec, *, tm, name, comm=None):
    _, T, D = u.shape
    hb = tm // HALO
    nlc = D // HEAD

    def body(a_ref, gt_ref, ap_ref, gp_ref, w_ref, c_ref, v_ref, sh):
        i = pl.program_id(0)
        sh[0, HALO:HALO + tm, :] = a_ref[...] * _sigmoid(gt_ref[...])
        prev = ap_ref[...] * _sigmoid(gp_ref[...])
        sh[0, 0:HALO, :] = jnp.where(i > 0, prev, 0.0)
        _shifted_copies(sh, tm + HALO - 8)

        def rowblock(r, carry):
            r0 = r * CONV_RB
            for cl in range(nlc):
                ls = slice(cl * HEAD, (cl + 1) * HEAD)
                acc = jnp.zeros((CONV_RB, HEAD), F32) + w_ref[32:33, ls]
                for j in range(CONV_W):
                    o = j + 2
                    at = pl.ds(pl.multiple_of(r0 + o - o % 8, 8), CONV_RB)
                    acc = acc + w_ref[j:j + 1, ls] * sh[o % 8, at, ls]
                c_ref[pl.ds(pl.multiple_of(r0, CONV_RB), CONV_RB), ls] = acc
            return carry

        lax.fori_loop(0, tm // CONV_RB, rowblock, 0)
        xhat, _ = _ln_stats(c_ref[...])
        y = xhat * w_ref[33:34, :] + w_ref[34:35, :]
        v_ref[...] = (y * _sigmoid(y)).astype(BF16)

    cur = lambda gi: pl.BlockSpec((None, tm, D), lambda i: (gi, i, 0))
    prv = lambda gi: pl.BlockSpec((None, HALO, D), lambda i: (gi, jnp.maximum(i * hb - 1, 0), 0))
    return carried_call(
        body, comm, name=name, grid=(T // tm,),
        in_specs=[cur(0), cur(1), prv(0), prv(1), pl.BlockSpec((VEC_ROWS, D), lambda i: (0, 0))],
        out_specs=[pl.BlockSpec((tm, D), lambda i: (i, 0)), pl.BlockSpec((tm, D), lambda i: (i, 0))],
        out_shape=[jax.ShapeDtypeStruct((T, D), F32), jax.ShapeDtypeStruct((T, D), BF16)],
        scratch_shapes=[pltpu.VMEM((8, tm + HALO, D), F32)],
        args=(u, u, u, u, vec))


def conv_bwd(dv2, c, u, vec, *, tm, name, comm=None):
    _, T, D = u.shape
    hb = tm // HALO
    nt = T // tm
    nh = T // HALO
    nlc = D // HEAD
    acc_rows = {j: j for j in range(CONV_W)}
    acc_rows.update({36: CONV_W, 37: CONV_W + 1})

    def body(dv_ref, c_ref, dvn_ref, cn_ref, a_ref, gt_ref, ap_ref, gp_ref, w_ref, du_ref, s_ref, gsh, dsh, part):
        i = pl.program_id(0)

        @pl.when(i == 0)
        def _():
            s_ref[...] = jnp.zeros_like(s_ref)
            part[...] = jnp.zeros_like(part)

        gam = w_ref[33:34, :]
        bet = w_ref[34:35, :]

        def dconv(dv, cc):
            xhat, rstd = _ln_stats(cc)
            y = xhat * gam + bet
            sy = _sigmoid(y)
            dy = dv * (sy * (1.0 + y * (1.0 - sy)))
            return _ln_bwd(dy, xhat, rstd, gam), dy, xhat

        dc, dy, xhat = dconv(dv_ref[...], c_ref[...])
        dcn, _, _ = dconv(dvn_ref[...], cn_ref[...])
        dsh[0, 0:tm, :] = dc
        dsh[0, tm:tm + HALO, :] = jnp.where(i < nt - 1, dcn, 0.0)
        gsh[0, HALO:HALO + tm, :] = a_ref[...] * _sigmoid(gt_ref[...])
        gsh[0, 0:HALO, :] = jnp.where(i > 0, ap_ref[...] * _sigmoid(gp_ref[...]), 0.0)
        s_ref[32:33, :] += jnp.sum(dc, axis=0, keepdims=True)
        s_ref[33:34, :] += jnp.sum(dy * xhat, axis=0, keepdims=True)
        s_ref[34:35, :] += jnp.sum(dy, axis=0, keepdims=True)
        _shifted_copies(dsh, tm + HALO - 8)
        _shifted_copies(gsh, tm + HALO - 8)

        def fold8(x):
            acc = x[0:8, :]
            for g in range(1, CONV_RB // 8):
                acc = acc + x[8 * g:8 * g + 8, :]
            return acc

        for cl in range(nlc):
            ls = slice(cl * HEAD, (cl + 1) * HEAD)

            def rowblock(r, sums, ls=ls):
                r0 = r * CONV_RB
                rows = pl.ds(pl.multiple_of(r0, CONV_RB), CONV_RB)
                dcb = dsh[0, rows, ls]
                dglu = jnp.zeros((CONV_RB, HEAD), F32)
                new = []
                for j in range(CONV_W):
                    od = 30 - j
                    og = j + 2
                    atd = pl.ds(pl.multiple_of(r0 + od - od % 8, 8), CONV_RB)
                    atg = pl.ds(pl.multiple_of(r0 + og - og % 8, 8), CONV_RB)
                    dglu = dglu + w_ref[j:j + 1, ls] * dsh[od % 8, atd, ls]
                    new.append(sums[j] + fold8(dcb * gsh[og % 8, atg, ls]))
                a = a_ref[rows, ls]
                sgt = _sigmoid(gt_ref[rows, ls])
                da = (dglu * sgt).astype(BF16)
                dg = (dglu * a * sgt * (1.0 - sgt)).astype(BF16)
                du_ref[0, rows, ls] = da
                du_ref[1, rows, ls] = dg
                new.append(sums[CONV_W] + fold8(da.astype(F32)))
                new.append(sums[CONV_W + 1] + fold8(dg.astype(F32)))
                return tuple(new)

            zero = jnp.zeros((8, HEAD), F32)
            sums = lax.fori_loop(0, tm // CONV_RB, rowblock, (zero,) * (CONV_W + 2))
            for k in range(CONV_W + 2):
                part[8 * k:8 * k + 8, ls] += sums[k]

        @pl.when(i == nt - 1)
        def _():
            for row, k in acc_rows.items():
                s_ref[row:row + 1, :] = jnp.sum(part[8 * k:8 * k + 8, :], axis=0, keepdims=True)

    row = lambda i: (i, 0)
    nxt = lambda i: (jnp.minimum((i + 1) * hb, nh - 1), 0)
    cur = lambda gi: pl.BlockSpec((None, tm, D), lambda i: (gi, i, 0))
    prv = lambda gi: pl.BlockSpec((None, HALO, D), lambda i: (gi, jnp.maximum(i * hb - 1, 0), 0))
    fix = lambda i: (0, 0)
    return carried_call(
        body, comm, name=name, grid=(nt,),
        in_specs=[pl.BlockSpec((tm, D), row), pl.BlockSpec((tm, D), row),
                  pl.BlockSpec((HALO, D), nxt), pl.BlockSpec((HALO, D), nxt),
                  cur(0), cur(1), prv(0), prv(1), pl.BlockSpec((VEC_ROWS, D), fix)],
        out_specs=[pl.BlockSpec((2, tm, D), lambda i: (0, i, 0)), pl.BlockSpec((VEC_ROWS, D), fix)],
        out_shape=[jax.ShapeDtypeStruct((2, T, D), BF16), jax.ShapeDtypeStruct((VEC_ROWS, D), F32)],
        scratch_shapes=[pltpu.VMEM((8, tm + HALO, D), F32), pltpu.VMEM((8, tm + HALO, D), F32),
                        pltpu.VMEM((8 * (CONV_W + 2), D), F32)],
        args=(dv2, c, dv2, c, u, u, u, u, vec))


def loss_grad(y, target, *, tm):
    T, D = y.shape
    nt = T // tm

    def body(y_ref, t_ref, l_ref, d_ref, acc):
        i = pl.program_id(0)

        @pl.when(i == 0)
        def _():
            acc[...] = jnp.zeros_like(acc)

        e = y_ref[...] - t_ref[...]
        d_ref[...] = e * (1.0 / D)
        acc[...] += jnp.sum(e * e, axis=0, keepdims=True)

        @pl.when(i == nt - 1)
        def _():
            l_ref[...] = 0.5 * jnp.sum(acc[...], axis=1, keepdims=True) * (1.0 / D)

    row = lambda i: (i, 0)
    return _call(
        body, name="loss_grad", grid=(nt,),
        in_specs=[pl.BlockSpec((tm, D), row), pl.BlockSpec((tm, D), row)],
        out_specs=[pl.BlockSpec((1, 1), lambda i: (0, 0)), pl.BlockSpec((tm, D), row)],
        out_shape=[jax.ShapeDtypeStruct((1, 1), F32), jax.ShapeDtypeStruct((T, D), F32)],
        scratch_shapes=[pltpu.VMEM((1, D), F32)], args=(y, target))


def _rows_block(R, C, budget=1 << 20):
    tr = R
    while tr * C * 4 > budget and tr % 32 == 0:
        tr //= 2
    return tr


def adamw(w, g, m, v, *, name):
    R, C = w.shape
    tr = _rows_block(R, C)

    def body(w_ref, g_ref, m_ref, v_ref, d_ref, mo_ref, vo_ref):
        g_ = g_ref[...]
        mn = ADAM_B1 * m_ref[...] + (1.0 - ADAM_B1) * g_
        vn = ADAM_B2 * v_ref[...] + (1.0 - ADAM_B2) * jnp.square(g_)
        m_hat = mn / (1.0 - ADAM_B1 ** ADAM_STEP)
        v_hat = vn / (1.0 - ADAM_B2 ** ADAM_STEP)
        d_ref[...] = -ADAM_LR * (m_hat / (jnp.sqrt(v_hat) + ADAM_EPS) + ADAM_WD * w_ref[...])
        mo_ref[...] = mn
        vo_ref[...] = vn

    spec = pl.BlockSpec((tr, C), lambda i: (i, 0))
    sd = jax.ShapeDtypeStruct((R, C), F32)
    return _call(body, name=name, grid=(R // tr,), in_specs=[spec] * 4, out_specs=[spec] * 3, out_shape=[sd] * 3,
                 args=(w, g, m, v))


def sum_slots(slots, *, name):
    _, R, C = slots.shape
    tr = _rows_block(R, C, budget=1 << 19)

    def body(s_ref, o_ref):
        acc = s_ref[0].astype(F32)
        for d in range(1, 8):
            acc = acc + s_ref[d].astype(F32)
        o_ref[...] = acc

    return _call(body, name=name, grid=(R // tr,), in_specs=[pl.BlockSpec((8, tr, C), lambda i: (0, i, 0))],
                 out_specs=[pl.BlockSpec((tr, C), lambda i: (i, 0))], out_shape=[jax.ShapeDtypeStruct((R, C), F32)],
                 args=(slots,))[0]


def _adam_nd(w, g, m, v, name):
    shp = w.shape
    c = shp[-1]
    f2 = lambda a: a.reshape(-1, c)
    d, mn, vn = adamw(f2(w), f2(g), f2(m), f2(v), name=name)
    return d.reshape(shp), mn.reshape(shp), vn.reshape(shp)


def _reduced(slots, name):
    out = []
    for s in slots:
        c = s.shape[-1]
        out.append(sum_slots(s.reshape(8, -1, c), name=name).reshape(s.shape[1:]))
    return out


def kernel(x, ln_mix_g, ln_mix_b, ln_ffn_g, ln_ffn_b, ffn_w1, ffn_w2, a_w_in, a_lb_logits, a_norm_g, a_w_out, b_w_pw1, b_b_pw1, b_w_dw, b_b_dw, b_ln_g, b_ln_b, b_w_pw2, b_b_pw2, loss_target, m_ln_mix_g, m_ln_mix_b, m_ln_ffn_g, m_ln_ffn_b, m_ffn_w1, m_ffn_w2, m_a_w_in, m_a_lb_logits, m_a_norm_g, m_a_w_out, m_b_w_pw1, m_b_b_pw1, m_b_w_dw, m_b_b_dw, m_b_ln_g, m_b_ln_b, m_b_w_pw2, m_b_b_pw2, v_ln_mix_g, v_ln_mix_b, v_ln_ffn_g, v_ln_ffn_b, v_ffn_w1, v_ffn_w2, v_a_w_in, v_a_lb_logits, v_a_norm_g, v_a_w_out, v_b_w_pw1, v_b_b_pw1, v_b_w_dw, v_b_b_dw, v_b_ln_g, v_b_ln_b, v_b_w_pw2, v_b_b_pw2):
    names = ["ln_mix_g", "ln_mix_b", "ln_ffn_g", "ln_ffn_b", "ffn_w1", "ffn_w2", "a_w_in", "a_lb_logits", "a_norm_g",
             "a_w_out", "b_w_pw1", "b_b_pw1", "b_w_dw", "b_b_dw", "b_ln_g", "b_ln_b", "b_w_pw2", "b_b_pw2"]
    w = dict(zip(names, [ln_mix_g, ln_mix_b, ln_ffn_g, ln_ffn_b, ffn_w1, ffn_w2, a_w_in, a_lb_logits, a_norm_g, a_w_out,
                         b_w_pw1, b_b_pw1, b_w_dw, b_b_dw, b_ln_g, b_ln_b, b_w_pw2, b_b_pw2]))
    m = dict(zip(names, [m_ln_mix_g, m_ln_mix_b, m_ln_ffn_g, m_ln_ffn_b, m_ffn_w1, m_ffn_w2, m_a_w_in, m_a_lb_logits,
                         m_a_norm_g, m_a_w_out, m_b_w_pw1, m_b_b_pw1, m_b_w_dw, m_b_b_dw, m_b_ln_g, m_b_ln_b, m_b_w_pw2,
                         m_b_b_pw2]))
    v = dict(zip(names, [v_ln_mix_g, v_ln_mix_b, v_ln_ffn_g, v_ln_ffn_b, v_ffn_w1, v_ffn_w2, v_a_w_in, v_a_lb_logits,
                         v_a_norm_g, v_a_w_out, v_b_w_pw1, v_b_b_pw1, v_b_w_dw, v_b_b_dw, v_b_ln_g, v_b_ln_b, v_b_w_pw2,
                         v_b_b_pw2]))
    T, D = x.shape[1], x.shape[2]
    DS = D // 4
    F = 4 * ffn_w1.shape[2]
    chip = 2 * lax.axis_index("x") + lax.axis_index("y")
    tm = min(T, 512)
    tmw = min(T, 1024)
    tmc = min(T, 256)
    rb = min(T, 1024)
    tf = min(F // 4, 1024)
    xin, target = x[0], loss_target[0]

    def mix_shards(i):
        j = i // 2
        if i % 2 == 0:
            return [a_w_in[j].astype(BF16), a_w_out[j].astype(BF16)]
        vec = jnp.concatenate([b_w_dw[j], jnp.zeros((1, DS), F32), b_b_dw[j][None], b_ln_g[j][None], b_ln_b[j][None],
                               b_b_pw2[j][None], b_b_pw1[j].reshape(2, DS), jnp.zeros((2, DS), F32)], axis=0)
        return [b_w_pw1[j].astype(BF16), b_w_pw2[j].astype(BF16), vec]

    def ffn_shards(i):
        return [ffn_w1[i].astype(BF16), ffn_w2[i].astype(BF16)]

    def mix_weights(i, got):
        if i % 2 == 0:
            return {"w_in": got[0], "w_out": got[1].reshape(D, D)}
        pw1 = jnp.transpose(got[0].reshape(2, 2, D, D // 2), (0, 2, 1, 3)).reshape(2, D, D)
        vec = jnp.transpose(got[2], (1, 0, 2)).reshape(VEC_ROWS, D)
        return {"pw1": pw1, "pw2": got[1].reshape(D, D), "vec": vec,
                "b_pw1": got[2][:, 36:38, :].reshape(2, 1, D)}

    lb_all = lb_fwd(a_lb_logits)
    zeros_bias = jnp.zeros((1, D), F32)

    first = mix_shards(0)
    mixw = {"w_in": gather_chips_via_sibling(first[0])}
    saved = []
    h, hb = xin, xin.astype(BF16)
    for i in range(DEPTH):
        j = i // 2
        s = {"xb": hb, "mixw": mixw}
        gf = GatherChips(ffn_shards(i) + (first[1:] if i == 0 else []))
        if i % 2 == 0:
            s["proj"] = mm_groups(hb, mixw["w_in"], jnp.zeros((4, 1, D), F32), tm=tmw, name="a_in_proj")
            (s["o"], s["og"], s["st"]), got = hgrn_fwd(s["proj"], lb_all[j:j + 1], a_norm_g[j:j + 1], rb=rb,
                                                       name="hgrn_fwd", comm=gf)
            if i == 0:
                mixw["w_out"] = got[2].reshape(D, D)
            s["r1"], x1, s["x1b"] = mm_res_ln(s["og"], mixw["w_out"], zeros_bias, h, ln_mix_g[i:i + 1],
                                              ln_mix_b[i:i + 1], tm=tmw, name="a_out_ln")
        else:
            s["u"] = mm_groups(hb, mixw["pw1"], mixw["b_pw1"], tm=tmw, name="b_pw1")
            (s["c"], s["v2"]), got = conv_fwd(s["u"], mixw["vec"], tm=min(T, 2 * tmc), name="conv_fwd", comm=gf)
            s["r1"], x1, s["x1b"] = mm_res_ln(s["v2"], mixw["pw2"], mixw["vec"][35:36], h, ln_mix_g[i:i + 1],
                                              ln_mix_b[i:i + 1], tm=tmw, name="b_pw2_ln")
        s["w1"], s["w2"] = got[0], got[1].reshape(F, D)
        gm = GatherChips(mix_shards(i + 1)) if i + 1 < DEPTH else None
        (s["z"], s["r2"], h, hb), got = ffn_fwd(x1, s["w1"], s["w2"], ln_ffn_g[i:i + 1], ln_ffn_b[i:i + 1],
                                                tm=tm, tf=tf, name="ffn_fwd", comm=gm)
        if gm is not None:
            mixw = mix_weights(i + 1, got)
        saved.append(s)

    loss_part, dh = loss_grad(h, target, tm=tmw)
    loss = lax.psum(loss_part[0, 0], ("x", "y", "c"))

    gr = {k: [None] * DEPTH for k in ("ln_mix_g", "ln_mix_b", "ln_ffn_g", "ln_ffn_b", "ffn_w1", "ffn_w2")}
    for k in ("a_w_in", "a_w_out", "a_norm_g", "a_dlb", "b_w_pw1", "b_w_pw2", "b_vec", "b_b_pw2"):
        gr[k] = [None] * 2
    w_in_name = ("a_w_in", "b_w_pw1")
    w_out_name = ("a_w_out", "b_w_pw2")
    pending = None

    for i in reversed(range(DEPTH)):
        j = i // 2
        s = saved[i]
        mixw = s["mixw"]
        sm = ScatterPieces([pending[1]]) if pending is not None else None
        (dz, dx1, drb2, sums2), slots = ffn_bwd_dx(dh, s["r2"], ln_ffn_g[i:i + 1], s["z"], s["w1"], s["w2"],
                                                   tm=tm, tf=tf, name="ffn_bwd_dx", comm=sm)
        if pending is not None:
            gr[w_in_name[pending[0] % 2]][pending[0] // 2] = _reduced(slots, "sum_mix_grads")[0]
        gr["ln_ffn_g"][i], gr["ln_ffn_b"][i] = sums2[0], sums2[1]
        dw1 = mm_tn(s["x1b"], dz[None], tm=tmw, tk=D, tn=F // 4, name="ffn_dw1")[0]
        dw2 = mm_tn(s["z"], drb2[None], tm=tmw, tk=F // 4, tn=D, relu2=True, name="ffn_dw2")[0, 0]
        wmix = mixw["w_out"] if i % 2 == 0 else mixw["pw2"]
        dr1, drb1, dmo, sums1 = ln_bwd_mm(dx1, s["r1"], ln_mix_g[i:i + 1], wmix, tm=tmw, name="mix_ln_bwd")
        gr["ln_mix_g"][i], gr["ln_mix_b"][i] = sums1[0], sums1[1]
        if i % 2 == 0:
            dwo = mm_tn(s["og"], drb1[None], tm=tmw, tk=D, tn=D, name="a_dw_out")[0, 0].reshape(4, DS, D)
            sf = ScatterPieces([dw1, dw2.reshape(4, F // 4, D), dwo])
            (dproj, hs), slots = hgrn_bwd(s["proj"], s["o"], dmo, s["st"], lb_all[j:j + 1], a_norm_g[j:j + 1], rb=rb,
                                          name="hgrn_bwd", comm=sf)
            gr["a_norm_g"][j], gr["a_dlb"][j] = hs[0], hs[1]
            dwi = mm_tn(s["xb"], dproj, tm=tmw, tk=D, tn=D, name="a_dw_in")[:, 0]
            dy_in, w_in_t, dx_name = dproj, mixw["w_in"], "a_dx"
        else:
            dwo = mm_tn(s["v2"], drb1[None], tm=tmw, tk=D, tn=D, name="b_dw_pw2")[0, 0].reshape(4, DS, D)
            sf = ScatterPieces([dw1, dw2.reshape(4, F // 4, D), dwo])
            (du, cs), slots = conv_bwd(dmo, s["c"], s["u"], mixw["vec"], tm=tmc, name="conv_bwd", comm=sf)
            gr["b_vec"][j], gr["b_b_pw2"][j] = cs, sums1[2]
            dwi = mm_tn(s["xb"], du, tm=tmw, tk=D, tn=D // 2, name="b_dw_pw1").reshape(4, D, D // 2)
            dy_in, w_in_t, dx_name = du, mixw["pw1"], "b_dx"
        gr["ffn_w1"][i], gr["ffn_w2"][i], gr[w_out_name[i % 2]][j] = _reduced(slots, "sum_ffn_grads")
        dh, slots = mm_nt_acc(dy_in, w_in_t, dr1, tm=tmw, name=dx_name, comm=ScatterPiecesViaSibling([dwi]) if i == 0 else None)
        if i == 0:
            gr[w_in_name[0]][0] = _reduced(slots, "sum_mix_grads")[0]
        pending = (i, dwi)
    grad_x = dh[None]

    small = {k: jnp.stack(gr[k]) for k in ("ln_mix_g", "ln_mix_b", "ln_ffn_g", "ln_ffn_b", "a_norm_g", "b_vec", "b_b_pw2")}
    small["a_lb_logits"] = lb_bwd(a_lb_logits, jnp.stack(gr["a_dlb"]))
    small_names = ["ln_mix_g", "ln_mix_b", "ln_ffn_g", "ln_ffn_b", "a_lb_logits", "a_norm_g", "b_b_pw2", "b_vec"]
    rows = [small[k].reshape(-1, D) for k in small_names]
    counts = [r.shape[0] for r in rows]
    rows = [jnp.pad(r, ((0, (-r.shape[0]) % 8), (0, 0))) for r in rows]
    summed = all_reduce_small(jnp.concatenate(rows, axis=0))
    sm = {}
    off = 0
    for k, n, r in zip(small_names, counts, rows):
        sm[k] = summed[off:off + n]
        off += r.shape[0]
    bvec = sm["b_vec"].reshape(2, VEC_ROWS, D)

    def shard_cols(a):
        return lax.dynamic_slice_in_dim(a, chip * DS, DS, axis=a.ndim - 1)

    grads = {k: jnp.stack(gr[k]) for k in ("ffn_w1", "ffn_w2", "a_w_in", "a_w_out", "b_w_pw1", "b_w_pw2")}
    for k in ("ln_mix_g", "ln_mix_b", "ln_ffn_g", "ln_ffn_b", "a_lb_logits", "a_norm_g"):
        grads[k] = sm[k]
    grads["b_b_pw1"] = lax.dynamic_slice_in_dim(bvec[:, 36:38, :].reshape(2, 2 * D), chip * (D // 2), D // 2, axis=1)
    grads["b_w_dw"] = shard_cols(bvec[:, 0:CONV_W, :])
    grads["b_b_dw"] = shard_cols(bvec[:, 32, :])
    grads["b_ln_g"] = shard_cols(bvec[:, 33, :])
    grads["b_ln_b"] = shard_cols(bvec[:, 34, :])
    grads["b_b_pw2"] = shard_cols(sm["b_b_pw2"])

    delta, new_m, new_v = {}, {}, {}
    for k in names:
        delta[k], new_m[k], new_v[k] = _adam_nd(w[k], grads[k], m[k], v[k], "adamw_" + k)
    return (loss, grad_x, *[grads[k] for k in names], *[delta[k] for k in names],
            *[new_m[k] for k in names], *[new_v[k] for k in names])
```

```python
import jax
import jax.numpy as jnp
from jax import lax
from jax.experimental import pallas as pl
from jax.experimental.pallas import tpu as pltpu

F32 = jnp.float32
BF16 = jnp.bfloat16
MESH = pl.DeviceIdType.MESH

DEPTH = 4
ALPHA = (2.0 * DEPTH) ** 0.25
LN_EPS = 1e-5
RMS_EPS = 1e-6
GATE_EPS = 1e-6
HEAD = 128
CHUNK = 128
SUB = 16
PAIR = 2
TRIP_CHUNKS = 2
LN_ROWS = 32
CONV_W = 31
HALO = 32
VEC_ROWS = 40
CONV_RB = 32
ADAM_LR, ADAM_B1, ADAM_B2, ADAM_EPS, ADAM_WD, ADAM_STEP = 0.001, 0.9, 0.999, 1e-08, 0.01, 10
VMEM_LIMIT = 56 * 1024 * 1024
ANY = pl.BlockSpec(memory_space=pl.ANY)


def _dot(a, b):
    return jnp.dot(a, b, preferred_element_type=F32)


def _dot_nt(a, b):
    return lax.dot_general(a, b, (((1,), (1,)), ((), ())), preferred_element_type=F32)


def _dot_tn(a, b):
    return lax.dot_general(a, b, (((0,), (0,)), ((), ())), preferred_element_type=F32)


def _sigmoid(x):
    return 1.0 / (1.0 + jnp.exp(-x))


def _ln_stats(r):
    mu = jnp.mean(r, axis=-1, keepdims=True)
    xc = r - mu
    var = jnp.mean(xc * xc, axis=-1, keepdims=True)
    rstd = lax.rsqrt(var + LN_EPS)
    return xc * rstd, rstd


def _ln_bwd_rows(dy_ref, r_ref, g_ref, dr_ref, drb_ref, rows):
    n = dy_ref.shape[-1]
    g = g_ref[...]

    def block(k, sums):
        at = pl.ds(pl.multiple_of(k * LN_ROWS, LN_ROWS), LN_ROWS)
        dy = dy_ref[at, :]
        xhat, rstd = _ln_stats(r_ref[at, :])
        dr = _ln_bwd(dy, xhat, rstd, g)
        dr_ref[at, :] = dr
        drb_ref[at, :] = dr.astype(BF16)
        return (sums[0] + jnp.sum(dy * xhat, axis=0, keepdims=True), sums[1] + jnp.sum(dy, axis=0, keepdims=True),
                sums[2] + jnp.sum(dr, axis=0, keepdims=True))

    zero = jnp.zeros((1, n), F32)
    return lax.fori_loop(0, rows // LN_ROWS, block, (zero, zero, zero))


def _ln_bwd(dy, xhat, rstd, g):
    dyg = dy * g
    m1 = jnp.mean(dyg, axis=-1, keepdims=True)
    m2 = jnp.mean(dyg * xhat, axis=-1, keepdims=True)
    return rstd * (dyg - m1 - xhat * m2)


def _place():
    return lax.axis_index("x"), lax.axis_index("y"), lax.axis_index("c")


class GatherChips:
    def __init__(self, arrs):
        self.ins = list(arrs)
        n = len(arrs)
        self.out_shapes = [jax.ShapeDtypeStruct((4,) + a.shape, a.dtype) for a in arrs]
        self.sems = [pltpu.SemaphoreType.DMA((3 * n,)), pltpu.SemaphoreType.DMA((3 * n,)),
                     pltpu.SemaphoreType.DMA((n,))]

    def copies(self, ins, outs, send, recv, loc):
        x, y, c = _place()
        me = 2 * x + y
        local, remote = [], []
        for a in range(len(ins)):
            local.append(pltpu.make_async_copy(ins[a], outs[a].at[me], loc.at[a]))
            for j, (px, py) in enumerate([(1 - x, y), (x, 1 - y), (1 - x, 1 - y)]):
                remote.append(pltpu.make_async_remote_copy(
                    src_ref=ins[a], dst_ref=outs[a].at[me], send_sem=send.at[3 * a + j], recv_sem=recv.at[3 * a + j],
                    device_id=(px, py, c), device_id_type=MESH))
        return local + remote


class ScatterPieces:
    def __init__(self, arrs):
        self.ins = list(arrs)
        n = len(arrs)
        self.out_shapes = [jax.ShapeDtypeStruct((8,) + a.shape[1:], a.dtype) for a in arrs]
        self.sems = [pltpu.SemaphoreType.DMA((7 * n,)), pltpu.SemaphoreType.DMA((7 * n,)),
                     pltpu.SemaphoreType.DMA((n,))]

    def copies(self, ins, outs, send, recv, loc):
        x, y, c = _place()
        me = 4 * x + 2 * y + c
        local, remote = [], []
        for a in range(len(ins)):
            local.append(pltpu.make_async_copy(ins[a].at[2 * x + y], outs[a].at[me], loc.at[a]))
            k = 0
            for fx in (0, 1):
                for fy in (0, 1):
                    for fc in (0, 1):
                        if fx or fy or fc:
                            tx, ty = x ^ fx, y ^ fy
                            remote.append(pltpu.make_async_remote_copy(
                                src_ref=ins[a].at[2 * tx + ty], dst_ref=outs[a].at[me],
                                send_sem=send.at[7 * a + k], recv_sem=recv.at[7 * a + k],
                                device_id=(tx, ty, c ^ fc), device_id_type=MESH))
                            k += 1
        return local + remote


class ScatterPiecesViaSibling:
    def __init__(self, arrs):
        self.ins = list(arrs)
        n = len(arrs)
        self.out_shapes = [jax.ShapeDtypeStruct((8,) + a.shape[1:], a.dtype) for a in arrs]
        self.sems = [pltpu.SemaphoreType.DMA((3 * n,)), pltpu.SemaphoreType.DMA((3 * n,)),
                     pltpu.SemaphoreType.DMA((n,)), pltpu.SemaphoreType.DMA((4 * n,)), pltpu.SemaphoreType.DMA((4 * n,))]

    def copies(self, ins, outs, send, recv, loc, send2, recv2):
        x, y, c = _place()
        me = 4 * x + 2 * y + c
        local, remote = [], []
        for a in range(len(ins)):
            local.append(pltpu.make_async_copy(ins[a].at[2 * x + y], outs[a].at[me], loc.at[a]))
            for j, (px, py) in enumerate([(1 - x, y), (x, 1 - y), (1 - x, 1 - y)]):
                remote.append(pltpu.make_async_remote_copy(
                    src_ref=ins[a].at[2 * px + py], dst_ref=outs[a].at[me], send_sem=send.at[3 * a + j],
                    recv_sem=recv.at[3 * a + j], device_id=(px, py, c), device_id_type=MESH))
        return local + remote

    def forwards(self, ins, outs, send, recv, loc, send2, recv2):
        x, y, c = _place()
        passed = []
        for a in range(len(ins)):
            for k, (qx, qy) in enumerate([(x, y), (1 - x, y), (x, 1 - y), (1 - x, 1 - y)]):
                slot = outs[a].at[4 * qx + 2 * qy + c]
                passed.append(pltpu.make_async_remote_copy(
                    src_ref=slot, dst_ref=slot, send_sem=send2.at[4 * a + k], recv_sem=recv2.at[4 * a + k],
                    device_id=(x, y, 1 - c), device_id_type=MESH))
        return passed


def carried_call(body, comm, *, name, grid, in_specs, out_specs, out_shape, scratch_shapes, args):
    sem = ("arbitrary",) * len(grid)
    params = pltpu.CompilerParams(dimension_semantics=sem, vmem_limit_bytes=VMEM_LIMIT)
    if comm is None:
        res = pl.pallas_call(body, name=name, grid=grid, in_specs=in_specs, out_specs=out_specs, out_shape=out_shape,
                             scratch_shapes=scratch_shapes, compiler_params=params)(*args)
        return res, []
    ni, no, nscr = len(in_specs), len(out_specs), len(scratch_shapes)
    ci, co = len(comm.ins), len(comm.out_shapes)

    def both(*refs):
        ins, refs = refs[:ni], refs[ni:]
        cins, refs = refs[:ci], refs[ci:]
        outs, refs = refs[:no], refs[no:]
        couts, refs = refs[:co], refs[co:]
        scr, sems = refs[:nscr], refs[nscr:]
        first = pl.program_id(0) == 0
        last = pl.program_id(0) == grid[0] - 1
        for d in range(1, len(grid)):
            first = first & (pl.program_id(d) == 0)
            last = last & (pl.program_id(d) == grid[d] - 1)

        @pl.when(first)
        def _():
            for cp in comm.copies(cins, couts, *sems):
                cp.start()

        body(*ins, *outs, *scr)

        @pl.when(last)
        def _():
            for cp in comm.copies(cins, couts, *sems):
                cp.wait()
            if hasattr(comm, "forwards"):
                passed = comm.forwards(cins, couts, *sems)
                for cp in passed:
                    cp.start()
                for cp in passed:
                    cp.wait()

    res = pl.pallas_call(
        both, name=name, grid=grid, in_specs=list(in_specs) + [ANY] * ci, out_specs=list(out_specs) + [ANY] * co,
        out_shape=list(out_shape) + comm.out_shapes, scratch_shapes=list(scratch_shapes) + comm.sems,
        compiler_params=params)(*args, *comm.ins)
    return res[:no], res[no:]


def gather_chips_via_sibling(shard):
    half = shard.shape[0] // 2

    def body(in_ref, out_ref, send1, recv1, send2, recv2, loc):
        x, y, c = _place()
        me = 2 * x + y
        mine = pl.ds(c * half, half)
        local = pltpu.make_async_copy(in_ref, out_ref.at[me], loc)
        local.start()
        chips = [(1 - x, y), (x, 1 - y), (1 - x, 1 - y)]
        first = [pltpu.make_async_remote_copy(
            src_ref=in_ref.at[mine], dst_ref=out_ref.at[me, mine], send_sem=send1.at[j], recv_sem=recv1.at[j],
            device_id=(px, py, c), device_id_type=MESH) for j, (px, py) in enumerate(chips)]
        for cp in first:
            cp.start()
        passed = []
        for j, (px, py) in enumerate(chips):
            first[j].wait_recv()
            got = out_ref.at[2 * px + py, mine]
            cp = pltpu.make_async_remote_copy(src_ref=got, dst_ref=got, send_sem=send2.at[j], recv_sem=recv2.at[j],
                                              device_id=(x, y, 1 - c), device_id_type=MESH)
            cp.start()
            passed.append(cp)
        for cp in first:
            cp.wait_send()
        for cp in passed:
            cp.wait()
        local.wait()

    return pl.pallas_call(
        body, name="gather_first", in_specs=[ANY], out_specs=ANY,
        out_shape=jax.ShapeDtypeStruct((4,) + shard.shape, shard.dtype),
        scratch_shapes=[pltpu.SemaphoreType.DMA((3,))] * 4 + [pltpu.SemaphoreType.DMA],
        compiler_params=pltpu.CompilerParams(has_side_effects=True))(shard)


def all_reduce_small(v):
    R, C = v.shape

    def body(v_ref, o_ref, slots, send, recv, send2, recv2):
        x, y, c = _place()
        me = 4 * x + 2 * y + c
        slots[me] = v_ref[...]
        chips = [(1 - x, y), (x, 1 - y), (1 - x, 1 - y)]
        first = [pltpu.make_async_remote_copy(
            src_ref=v_ref, dst_ref=slots.at[me], send_sem=send.at[j], recv_sem=recv.at[j],
            device_id=(px, py, c), device_id_type=MESH) for j, (px, py) in enumerate(chips)]
        for cp in first:
            cp.start()
        for cp in first:
            cp.wait()
        passed = []
        for k, (qx, qy) in enumerate([(x, y)] + chips):
            slot = slots.at[4 * qx + 2 * qy + c]
            passed.append(pltpu.make_async_remote_copy(
                src_ref=slot, dst_ref=slot, send_sem=send2.at[k], recv_sem=recv2.at[k],
                device_id=(x, y, 1 - c), device_id_type=MESH))
        for cp in passed:
            cp.start()
        for cp in passed:
            cp.wait()
        acc = slots[0]
        for d in range(1, 8):
            acc = acc + slots[d]
        o_ref[...] = acc

    vm = pl.BlockSpec(memory_space=pltpu.VMEM)
    return pl.pallas_call(
        body, name="all_reduce_small", in_specs=[vm], out_specs=vm,
        out_shape=jax.ShapeDtypeStruct((R, C), F32),
        scratch_shapes=[pltpu.VMEM((8, R, C), F32), pltpu.SemaphoreType.DMA((3,)), pltpu.SemaphoreType.DMA((3,)),
                        pltpu.SemaphoreType.DMA((4,)), pltpu.SemaphoreType.DMA((4,))],
        compiler_params=pltpu.CompilerParams(has_side_effects=True, vmem_limit_bytes=VMEM_LIMIT),
    )(v)


def _call(body, *, name, grid, in_specs, out_specs, out_shape, scratch_shapes=(), args):
    res, _ = carried_call(body, None, name=name, grid=grid, in_specs=in_specs, out_specs=out_specs,
                          out_shape=out_shape, scratch_shapes=list(scratch_shapes), args=args)
    return res


def mm_groups(a, w, bias, *, tm, name):
    T, K = a.shape
    G, _, N = w.shape

    def body(a_ref, w_ref, b_ref, o_ref):
        o_ref[...] = _dot(a_ref[...], w_ref[...]) + b_ref[...]

    return _call(
        body, name=name, grid=(G, T // tm),
        in_specs=[pl.BlockSpec((tm, K), lambda g, i: (i, 0)),
                  pl.BlockSpec((None, K, N), lambda g, i: (g, 0, 0)),
                  pl.BlockSpec((None, 1, N), lambda g, i: (g, 0, 0))],
        out_specs=[pl.BlockSpec((None, tm, N), lambda g, i: (g, i, 0))],
        out_shape=[jax.ShapeDtypeStruct((G, T, N), F32)], args=(a, w, bias))[0]


def mm_res_ln(a, w, bias, res, g, b, *, tm, name):
    T, K = a.shape
    N = w.shape[1]

    def body(a_ref, w_ref, bias_ref, res_ref, g_ref, b_ref, r_ref, y_ref, yb_ref):
        r = ALPHA * res_ref[...] + _dot(a_ref[...], w_ref[...]) + bias_ref[...]
        r_ref[...] = r
        xhat, _ = _ln_stats(r)
        y = xhat * g_ref[...] + b_ref[...]
        y_ref[...] = y
        yb_ref[...] = y.astype(BF16)

    row = lambda i: (i, 0)
    fix = lambda i: (0, 0)
    return _call(
        body, name=name, grid=(T // tm,),
        in_specs=[pl.BlockSpec((tm, K), row), pl.BlockSpec((K, N), fix), pl.BlockSpec((1, N), fix),
                  pl.BlockSpec((tm, N), row), pl.BlockSpec((1, N), fix), pl.BlockSpec((1, N), fix)],
        out_specs=[pl.BlockSpec((tm, N), row), pl.BlockSpec((tm, N), row), pl.BlockSpec((tm, N), row)],
        out_shape=[jax.ShapeDtypeStruct((T, N), F32), jax.ShapeDtypeStruct((T, N), F32),
                   jax.ShapeDtypeStruct((T, N), BF16)],
        args=(a, w, bias, res, g, b))


def ffn_fwd(x, w1, w2, g, b, *, tm, tf, name, comm=None):
    T, D = x.shape
    NC, _, FC = w1.shape
    F = NC * FC
    per = FC // tf
    nf = F // tf

    def body(x_ref, w1_ref, w2_ref, g_ref, b_ref, z_ref, r_ref, y_ref, yb_ref, acc_ref, xb_ref):
        f = pl.program_id(1)

        @pl.when(f == 0)
        def _():
            acc_ref[...] = jnp.zeros_like(acc_ref)
            xb_ref[...] = x_ref[...].astype(BF16)

        z = _dot(xb_ref[...], w1_ref[...])
        z_ref[...] = z.astype(BF16)
        h = jnp.square(jnp.maximum(z, 0.0)).astype(BF16)
        acc_ref[...] += _dot(h, w2_ref[...])

        @pl.when(f == nf - 1)
        def _():
            r = ALPHA * x_ref[...] + acc_ref[...]
            r_ref[...] = r
            xhat, _ = _ln_stats(r)
            y = xhat * g_ref[...] + b_ref[...]
            y_ref[...] = y
            yb_ref[...] = y.astype(BF16)

    return carried_call(
        body, comm, name=name, grid=(T // tm, nf),
        in_specs=[pl.BlockSpec((tm, D), lambda i, f: (i, 0)),
                  pl.BlockSpec((None, D, tf), lambda i, f: (f // per, 0, f % per)),
                  pl.BlockSpec((tf, D), lambda i, f: (f, 0)),
                  pl.BlockSpec((1, D), lambda i, f: (0, 0)),
                  pl.BlockSpec((1, D), lambda i, f: (0, 0))],
        out_specs=[pl.BlockSpec((tm, tf), lambda i, f: (i, f)),
                   pl.BlockSpec((tm, D), lambda i, f: (i, 0)),
                   pl.BlockSpec((tm, D), lambda i, f: (i, 0)),
                   pl.BlockSpec((tm, D), lambda i, f: (i, 0))],
        out_shape=[jax.ShapeDtypeStruct((T, F), BF16), jax.ShapeDtypeStruct((T, D), F32),
                   jax.ShapeDtypeStruct((T, D), F32), jax.ShapeDtypeStruct((T, D), BF16)],
        scratch_shapes=[pltpu.VMEM((tm, D), F32), pltpu.VMEM((tm, D), BF16)],
        args=(x, w1, w2, g, b))


def ln_bwd_mm(dy, r, g, w, *, tm, name):
    T, N = dy.shape
    Ko = w.shape[0]

    def body(dy_ref, r_ref, g_ref, w_ref, dr_ref, drb_ref, o_ref, s_ref):
        @pl.when(pl.program_id(0) == 0)
        def _():
            s_ref[...] = jnp.zeros_like(s_ref)

        sums = _ln_bwd_rows(dy_ref, r_ref, g_ref, dr_ref, drb_ref, tm)
        for k in range(3):
            s_ref[k:k + 1, :] += sums[k]
        o_ref[...] = _dot_nt(drb_ref[...], w_ref[...])

    row = lambda i: (i, 0)
    fix = lambda i: (0, 0)
    return _call(
        body, name=name, grid=(T // tm,),
        in_specs=[pl.BlockSpec((tm, N), row), pl.BlockSpec((tm, N), row), pl.BlockSpec((1, N), fix),
                  pl.BlockSpec((Ko, N), fix)],
        out_specs=[pl.BlockSpec((tm, N), row), pl.BlockSpec((tm, N), row), pl.BlockSpec((tm, Ko), row),
                   pl.BlockSpec((8, N), fix)],
        out_shape=[jax.ShapeDtypeStruct((T, N), F32), jax.ShapeDtypeStruct((T, N), BF16),
                   jax.ShapeDtypeStruct((T, Ko), F32), jax.ShapeDtypeStruct((8, N), F32)],
        args=(dy, r, g, w))


def ffn_bwd_dx(dy, r, g, z, w1, w2, *, tm, tf, name, comm=None):
    T, D = dy.shape
    NC, _, FC = w1.shape
    F = NC * FC
    per = FC // tf
    nf = F // tf

    def body(dy_ref, r_ref, g_ref, z_ref, w1_ref, w2_ref, dz_ref, dx_ref, drb_ref, s_ref, dr_scr, acc_ref):
        i = pl.program_id(0)
        f = pl.program_id(1)

        @pl.when((i == 0) & (f == 0))
        def _():
            s_ref[...] = jnp.zeros_like(s_ref)

        @pl.when(f == 0)
        def _():
            sums = _ln_bwd_rows(dy_ref, r_ref, g_ref, dr_scr, drb_ref, tm)
            acc_ref[...] = jnp.zeros_like(acc_ref)
            for k in range(2):
                s_ref[k:k + 1, :] += sums[k]

        dh = _dot_nt(drb_ref[...], w2_ref[...])
        dz = (dh * (2.0 * jnp.maximum(z_ref[...].astype(F32), 0.0))).astype(BF16)
        dz_ref[...] = dz
        acc_ref[...] += _dot_nt(dz, w1_ref[...])

        @pl.when(f == nf - 1)
        def _():
            dx_ref[...] = ALPHA * dr_scr[...] + acc_ref[...]

    return carried_call(
        body, comm, name=name, grid=(T // tm, nf),
        in_specs=[pl.BlockSpec((tm, D), lambda i, f: (i, 0)),
                  pl.BlockSpec((tm, D), lambda i, f: (i, 0)),
                  pl.BlockSpec((1, D), lambda i, f: (0, 0)),
                  pl.BlockSpec((tm, tf), lambda i, f: (i, f)),
                  pl.BlockSpec((None, D, tf), lambda i, f: (f // per, 0, f % per)),
                  pl.BlockSpec((tf, D), lambda i, f: (f, 0))],
        out_specs=[pl.BlockSpec((tm, tf), lambda i, f: (i, f)),
                   pl.BlockSpec((tm, D), lambda i, f: (i, 0)),
                   pl.BlockSpec((tm, D), lambda i, f: (i, 0)),
                   pl.BlockSpec((8, D), lambda i, f: (0, 0))],
        out_shape=[jax.ShapeDtypeStruct((T, F), BF16), jax.ShapeDtypeStruct((T, D), F32),
                   jax.ShapeDtypeStruct((T, D), BF16), jax.ShapeDtypeStruct((8, D), F32)],
        scratch_shapes=[pltpu.VMEM((tm, D), F32), pltpu.VMEM((tm, D), F32)],
        args=(dy, r, g, z, w1, w2))


def mm_tn(a, b, *, tm, tk, tn, relu2=False, name):
    T, K = a.shape
    G, _, N = b.shape
    nt = T // tm

    def body(a_ref, b_ref, o_ref, acc_ref):
        t = pl.program_id(3)

        @pl.when(t == 0)
        def _():
            acc_ref[...] = jnp.zeros_like(acc_ref)

        av = a_ref[...]
        if relu2:
            av = jnp.square(jnp.maximum(av.astype(F32), 0.0))
        acc_ref[...] += _dot_tn(av.astype(BF16), b_ref[...])

        @pl.when(t == nt - 1)
        def _():
            o_ref[...] = acc_ref[...].astype(BF16)

    return _call(
        body, name=name, grid=(G, K // tk, N // tn, nt),
        in_specs=[pl.BlockSpec((tm, tk), lambda g, k, n, t: (t, k)),
                  pl.BlockSpec((None, tm, tn), lambda g, k, n, t: (g, t, n))],
        out_specs=[pl.BlockSpec((None, None, tk, tn), lambda g, k, n, t: (g, n, k, 0))],
        out_shape=[jax.ShapeDtypeStruct((G, N // tn, K, tn), BF16)],
        scratch_shapes=[pltpu.VMEM((tk, tn), F32)], args=(a, b))[0]


def mm_nt_acc(dy, w, base, *, tm, name, comm=None):
    G, T, N = dy.shape
    K = w.shape[1]

    def body(dy_ref, w_ref, base_ref, o_ref):
        g = pl.program_id(1)

        @pl.when(g == 0)
        def _():
            o_ref[...] = ALPHA * base_ref[...]

        o_ref[...] += _dot_nt(dy_ref[...], w_ref[...])

    res, got = carried_call(
        body, comm, name=name, grid=(T // tm, G),
        in_specs=[pl.BlockSpec((None, tm, N), lambda i, g: (g, i, 0)),
                  pl.BlockSpec((None, K, N), lambda i, g: (g, 0, 0)),
                  pl.BlockSpec((tm, K), lambda i, g: (i, 0))],
        out_specs=[pl.BlockSpec((tm, K), lambda i, g: (i, 0))],
        out_shape=[jax.ShapeDtypeStruct((T, K), F32)], scratch_shapes=[], args=(dy, w, base))
    return res[0], got


def _split3(x):
    x1 = x.astype(BF16)
    r1 = x - x1.astype(F32)
    x2 = r1.astype(BF16)
    x3 = (r1 - x2.astype(F32)).astype(BF16)
    return x1, x2, x3


def _tri_dot(tri, x):
    x1, x2, x3 = _split3(x)
    return _dot(tri, x1) + _dot(tri, x2) + _dot(tri, x3)


def _gates(pq, fz, lb):
    sg = _sigmoid(fz)
    f = lb + (1.0 - lb) * sg
    logf = jnp.log(jnp.maximum(f, GATE_EPS))
    sq = _sigmoid(pq)
    return pq * sq, 1.0 - f, logf, f, sg, sq


def _staggered(gens):
    live = []
    waiting = list(gens)
    for gen in waiting:
        next(gen)
    while live or waiting:
        if waiting:
            live.append(waiting.pop(0))
        nxt = []
        for gen in live:
            try:
                next(gen)
                nxt.append(gen)
            except StopIteration:
                pass
        live = nxt


def _butterfly(ys, combine):
    span = 4
    while len(ys) > 1:
        ys = [combine(u, v, span) for u, v in zip(ys[0::2], ys[1::2])]
        span //= 2
    return ys[0]


def _rows_of_sums(xs):
    lands = _butterfly([[j] * 8 for j in range(8)],
                       lambda u, v, span: [u[r] if (r // span) % 2 else v[r] for r in range(8)])
    src = [None] * 8
    for r in range(8):
        src[lands[r]] = xs[r]
    row = lax.broadcasted_iota(jnp.int32, xs[0].shape, 0)

    def combine(u, v, span):
        return jnp.where((row // span) % 2 == 1, u + pltpu.roll(u, span, 0), v + pltpu.roll(v, 8 - span, 0))

    return _butterfly(src, combine)


def _block_diag_mask():
    ri = lax.broadcasted_iota(jnp.int32, (PAIR * HEAD, PAIR * HEAD), 0) // HEAD
    ci = lax.broadcasted_iota(jnp.int32, (PAIR * HEAD, PAIR * HEAD), 1) // HEAD
    return ri == ci


def _fill_off_diagonal(q_s, k_s, b_s, lhs, rhs):
    for i in range(1, CHUNK // SUB):
        lo = i * SUB
        ref = b_s[lo - 1:lo, :]
        qt = (q_s[lo:lo + SUB, :] * jnp.exp(b_s[lo:lo + SUB, :] - ref)).astype(BF16)
        kt = (k_s[0:lo, :] * jnp.exp(ref - b_s[0:lo, :])).astype(BF16)
        for h in range(PAIR):
            hl = slice(h * HEAD, (h + 1) * HEAD)
            lhs[h, lo:lo + SUB, (i - 1) * HEAD:i * HEAD] = qt[:, hl]
            rhs[h, 0:lo, (i - 1) * HEAD:i * HEAD] = kt[:, hl]


def hgrn_fwd(proj, lb, norm_g, *, rb, name, comm=None):
    _, T, D = proj.shape
    H = D // HEAD
    nb = T // rb
    nck = rb // CHUNK
    nsub = CHUNK // SUB
    W = PAIR * HEAD
    fam = CHUNK * SUB
    ntc = min(2 * TRIP_CHUNKS, nck)

    def body(pq_ref, fz_ref, pv_ref, pg_ref, lb_ref, ng_ref, o_ref, og_ref, st_ref, S, q_a, k_a, v_a, b_a, lhs_a, rhs_a,
             p_a):
        @pl.when(pl.program_id(1) == 0)
        def _():
            S[...] = jnp.zeros_like(S)

        @pl.when((pl.program_id(0) == 0) & (pl.program_id(1) == 0))
        def _():
            lhs_a[...] = jnp.zeros_like(lhs_a)
            rhs_a[...] = jnp.zeros_like(rhs_a)

        ri = lax.broadcasted_iota(jnp.int32, (CHUNK, CHUNK), 0)
        ci = lax.broadcasted_iota(jnp.int32, (CHUNK, CHUNK), 1)
        tri = (ci <= ri).astype(BF16)
        ones = jnp.ones((HEAD, HEAD), BF16)
        bd = _block_diag_mask()

        def chunk_stages(j, c):
            q_s, k_s, v_s, b_s, lhs, rhs, p_s = (r.at[j] for r in (q_a, k_a, v_a, b_a, lhs_a, rhs_a, p_a))
            base = c * CHUNK
            rows = pl.ds(pl.multiple_of(base, CHUNK), CHUNK)
            q, k, logf, _, _, _ = _gates(pq_ref[rows, :], fz_ref[rows, :], lb_ref[...])
            v = pv_ref[rows, :]
            b = _tri_dot(tri, logf)
            yield
            q_s[...] = q
            k_s[...] = k
            v_s[...] = v
            b_s[...] = b
            bl = b_s[CHUNK - 1:CHUNK, :]
            upd = _dot_tn(v.astype(BF16), (k * jnp.exp(bl - b)).astype(BF16))
            Sv = S[...]
            for h in range(PAIR):
                st_ref[h, c] = Sv[h * HEAD:(h + 1) * HEAD, h * HEAD:(h + 1) * HEAD]
            o_int = _dot_nt((q * jnp.exp(b)).astype(BF16), Sv.astype(BF16))
            S[...] = Sv * jnp.exp(bl) + jnp.where(bd, upd, 0.0)
            _fill_off_diagonal(q_s, k_s, b_s, lhs, rhs)
            a = [_dot_nt(lhs[h], rhs[h]) for h in range(PAIR)]
            yield

            def diag_products(blocks):
                for i in blocks:
                    lo = i * SUB
                    for s in range(SUB):
                        m = lax.broadcasted_iota(jnp.int32, (SUB, W), 0) >= s
                        at = (i * SUB + s) * SUB
                        e = jnp.exp(b_s[lo:lo + SUB, :] - b_s[lo + s:lo + s + 1, :])
                        p = jnp.where(m, q_s[lo:lo + SUB, :] * (k_s[lo + s:lo + s + 1, :] * e), 0.0).astype(BF16)
                        for h in range(PAIR):
                            p_s[h * fam + at:h * fam + at + SUB, :] = p[:, h * HEAD:(h + 1) * HEAD]

            diag_products(range(0, nsub // 2))
            yield
            off = [_dot(a[h].astype(BF16), v_s[:, h * HEAD:(h + 1) * HEAD].astype(BF16)) for h in range(PAIR)]
            diag_products(range(nsub // 2, nsub))
            yield
            rs = _dot(p_s[...], ones)
            yield
            for i in range(nsub):
                lo = i * SUB
                blk = pl.ds(pl.multiple_of(base + lo, SUB), SUB)
                for h in range(PAIR):
                    hl = slice(h * HEAD, (h + 1) * HEAD)
                    acc = o_int[lo:lo + SUB, hl] + off[h][lo:lo + SUB, :]
                    for s in range(SUB):
                        at = h * fam + (i * SUB + s) * SUB
                        acc = acc + rs[at:at + SUB, :] * v_s[lo + s:lo + s + 1, hl]
                    o_ref[blk, hl] = acc
                    rinv = lax.rsqrt(jnp.mean(acc * acc, axis=-1, keepdims=True) + RMS_EPS)
                    pg = pg_ref[blk, hl]
                    og_ref[blk, hl] = (acc * rinv * ng_ref[:, hl] * (pg * _sigmoid(pg))).astype(BF16)
            yield

        def trip(g, carry):
            _staggered([chunk_stages(j, g * ntc + j) for j in range(ntc)])
            return carry

        lax.fori_loop(0, nck // ntc, trip, 0)

    def grp(gi):
        return pl.BlockSpec((None, rb, W), lambda h, r: (gi, r, h))

    vec = pl.BlockSpec((1, W), lambda h, r: (0, h))
    return carried_call(
        body, comm, name=name, grid=(H // PAIR, nb),
        in_specs=[grp(0), grp(1), grp(2), grp(3), vec, vec],
        out_specs=[pl.BlockSpec((rb, W), lambda h, r: (r, h)),
                   pl.BlockSpec((rb, W), lambda h, r: (r, h)),
                   pl.BlockSpec((PAIR, nck, HEAD, HEAD), lambda h, r: (h, r, 0, 0))],
        out_shape=[jax.ShapeDtypeStruct((T, D), F32), jax.ShapeDtypeStruct((T, D), BF16),
                   jax.ShapeDtypeStruct((H, T // CHUNK, HEAD, HEAD), F32)],
        scratch_shapes=[pltpu.VMEM((W, W), F32)] + [pltpu.VMEM((ntc, CHUNK, W), F32)] * 4
        + [pltpu.VMEM((ntc, PAIR, CHUNK, (nsub - 1) * HEAD), BF16)] * 2 + [pltpu.VMEM((ntc, PAIR * fam, HEAD), BF16)],
        args=(proj, proj, proj, proj, lb, norm_g))


def hgrn_bwd(proj, o, dog, states, lb, norm_g, *, rb, name, comm=None):
    _, T, D = proj.shape
    H = D // HEAD
    nb = T // rb
    nck = rb // CHUNK
    nsub = CHUNK // SUB
    W = PAIR * HEAD
    fam = CHUNK * SUB
    ntc = min(TRIP_CHUNKS, nck)

    def body(pq_ref, fz_ref, pv_ref, pg_ref, o_ref, dog_ref, st_ref, lb_ref, ng_ref, dp_ref, s_ref,
             dS, *per_chunk):
        S0_a, lhs_a, rhs_a = per_chunk[0], per_chunk[10], per_chunk[11]

        @pl.when(pl.program_id(1) == 0)
        def _():
            dS[...] = jnp.zeros_like(dS)
            s_ref[...] = jnp.zeros_like(s_ref)

        @pl.when((pl.program_id(0) == 0) & (pl.program_id(1) == 0))
        def _():
            lhs_a[...] = jnp.zeros_like(lhs_a)
            rhs_a[...] = jnp.zeros_like(rhs_a)
            S0_a[...] = jnp.zeros_like(S0_a)

        ri = lax.broadcasted_iota(jnp.int32, (CHUNK, CHUNK), 0)
        ci = lax.broadcasted_iota(jnp.int32, (CHUNK, CHUNK), 1)
        tri = (ci <= ri).astype(BF16)
        triu = (ci >= ri).astype(BF16)
        below = (ri // SUB) > (ci // SUB)
        ones = jnp.ones((HEAD, HEAD), BF16)
        bd = _block_diag_mask()
        last_row = lax.broadcasted_iota(jnp.int32, (CHUNK, W), 0) == CHUNK - 1

        def chunk_stages(j, c):
            (S0, q_s, k_s, v_s, b_s, do_s, dq_s, dk_s, dv_s, cr_s, lhs, rhs, ke_s, qe_s, p_s) = (r.at[j] for r in per_chunk)
            rows = pl.ds(pl.multiple_of(c * CHUNK, CHUNK), CHUNK)
            lb_ = lb_ref[...]
            pq = pq_ref[rows, :]
            q, k, logf, f, sg, sq = _gates(pq, fz_ref[rows, :], lb_)
            v = pv_ref[rows, :]
            b = _tri_dot(tri, logf)
            dpg = []
            for h in range(PAIR):
                hl = slice(h * HEAD, (h + 1) * HEAD)
                oh = o_ref[rows, hl]
                dog_ = dog_ref[rows, hl]
                pg = pg_ref[rows, hl]
                ng = ng_ref[:, hl]
                spg = _sigmoid(pg)
                rinv = lax.rsqrt(jnp.mean(oh * oh, axis=-1, keepdims=True) + RMS_EPS)
                on = oh * rinv
                dpg.append(dog_ * (on * ng) * (spg * (1.0 + pg * (1.0 - spg))))
                don = dog_ * (pg * spg)
                s_ref[0:1, hl] += jnp.sum(don * on, axis=0, keepdims=True)
                dxn = don * ng
                do_s[:, hl] = rinv * (dxn - on * jnp.mean(dxn * on, axis=-1, keepdims=True))
                S0[hl, hl] = st_ref[h, c]
            yield
            q_s[...] = q
            k_s[...] = k
            v_s[...] = v
            b_s[...] = b
            do = do_s[...]
            dob = do.astype(BF16)
            vb = v.astype(BF16)
            eb = jnp.exp(b)
            bl = b_s[CHUNK - 1:CHUNK, :]
            ebl = jnp.exp(bl)
            ekk = jnp.exp(bl - b)
            upd = _dot_tn(dob, (q * eb).astype(BF16))
            S0v = S0[...]
            dSv = dS[...]
            dSb = dSv.astype(BF16)
            dq_s[...] = _dot(dob, S0v.astype(BF16)) * eb
            dk_state = _dot(vb, dSb) * ekk
            dk_s[...] = dk_state
            dv_s[...] = _dot_nt((k * ekk).astype(BF16), dSb)
            extra = jnp.sum(k * dk_state, axis=0, keepdims=True) + ebl * jnp.sum(S0v * dSv, axis=0, keepdims=True)
            dS[...] = dSv * ebl + jnp.where(bd, upd, 0.0)

            _fill_off_diagonal(q_s, k_s, b_s, lhs, rhs)
            at = [_dot_nt(rhs[h], lhs[h]) for h in range(PAIR)]
            daf = [jnp.where(below, _dot_nt(dob[:, h * HEAD:(h + 1) * HEAD], vb[:, h * HEAD:(h + 1) * HEAD]), 0.0)
                   for h in range(PAIR)]
            yield

            def diag_products(blocks):
                for i in blocks:
                    lo = i * SUB
                    for s in range(SUB):
                        m = lax.broadcasted_iota(jnp.int32, (SUB, W), 0) >= s
                        at_ = (i * SUB + s) * SUB
                        qi = q_s[lo:lo + SUB, :]
                        e = jnp.where(m, jnp.exp(b_s[lo:lo + SUB, :] - b_s[lo + s:lo + s + 1, :]), 0.0)
                        ke = k_s[lo + s:lo + s + 1, :] * e
                        ke_s[at_:at_ + SUB, :] = ke
                        qe_s[at_:at_ + SUB, :] = qi * e
                        pa = (qi * ke).astype(BF16)
                        pd = jnp.where(m, do_s[lo:lo + SUB, :] * v_s[lo + s:lo + s + 1, :], 0.0).astype(BF16)
                        for h in range(PAIR):
                            hl = slice(h * HEAD, (h + 1) * HEAD)
                            p_s[(2 * h) * fam + at_:(2 * h) * fam + at_ + SUB, :] = pa[:, hl]
                            p_s[(2 * h + 1) * fam + at_:(2 * h + 1) * fam + at_ + SUB, :] = pd[:, hl]

            diag_products(range(0, nsub // 2))
            yield
            dqb, dkb = [], []
            for h in range(PAIR):
                hl = slice(h * HEAD, (h + 1) * HEAD)
                dv_s[:, hl] += _dot(at[h].astype(BF16), dob[:, hl])
                dqb.append(_dot(daf[h].astype(BF16), rhs[h]))
                dkb.append(_dot(daf[h].T.astype(BF16), lhs[h]))
            diag_products(range(nsub // 2, nsub))
            yield
            rs = _dot(p_s[...], ones)
            cr_s[...] = jnp.zeros_like(cr_s)
            for i in range(1, nsub):
                lo = i * SUB
                ref = b_s[lo - 1:lo, :]
                eq = jnp.exp(b_s[lo:lo + SUB, :] - ref)
                ek = jnp.exp(ref - b_s[0:lo, :])
                qtf = q_s[lo:lo + SUB, :] * eq
                ktf = k_s[0:lo, :] * ek
                cb = slice((i - 1) * HEAD, i * HEAD)
                for h in range(PAIR):
                    hl = slice(h * HEAD, (h + 1) * HEAD)
                    dqi = dqb[h][lo:lo + SUB, cb]
                    dki = dkb[h][0:lo, cb]
                    dq_s[lo:lo + SUB, hl] += dqi * eq[:, hl]
                    dk_s[0:lo, hl] += dki * ek[:, hl]
                    cr_s[lo:lo + SUB, hl] += (lhs[h, lo:lo + SUB, cb].astype(F32) - qtf[:, hl]) * dqi
                    cr_s[0:lo, hl] -= (rhs[h, 0:lo, cb].astype(F32) - ktf[:, hl]) * dki
            yield
            for i in range(nsub):
                lo = i * SUB
                for h in range(PAIR):
                    hl = slice(h * HEAD, (h + 1) * HEAD)
                    doi = do_s[lo:lo + SUB, hl]
                    dqa = dq_s[lo:lo + SUB, hl]
                    xk, xv = [], []
                    for s in range(SUB):
                        at = (i * SUB + s) * SUB
                        acol = rs[(2 * h) * fam + at:(2 * h) * fam + at + SUB, :]
                        dacol = rs[(2 * h + 1) * fam + at:(2 * h + 1) * fam + at + SUB, :]
                        dqa = dqa + dacol * ke_s[at:at + SUB, hl]
                        pk = dacol * qe_s[at:at + SUB, hl]
                        pv = acol * doi
                        xk.append(pk[0:8, :] + pk[8:SUB, :])
                        xv.append(pv[0:8, :] + pv[8:SUB, :])
                    dq_s[lo:lo + SUB, hl] = dqa
                    for g8 in range(SUB // 8):
                        r8 = slice(lo + 8 * g8, lo + 8 * g8 + 8)
                        dk_s[r8, hl] += _rows_of_sums(xk[8 * g8:8 * g8 + 8])
                        dv_s[r8, hl] += _rows_of_sums(xv[8 * g8:8 * g8 + 8])

            dq = dq_s[...]
            dk = dk_s[...]
            db = q * dq - k * dk + cr_s[...] + jnp.where(last_row, extra, 0.0)
            dlogf = _tri_dot(triu, db)
            df = jnp.where(f > GATE_EPS, dlogf / jnp.maximum(f, GATE_EPS), 0.0) - dk
            s_ref[1:2, :] += jnp.sum(df * (1.0 - sg), axis=0, keepdims=True)
            dp_ref[0, rows, :] = (dq * (sq * (1.0 + pq * (1.0 - sq)))).astype(BF16)
            dp_ref[1, rows, :] = (df * (1.0 - lb_) * sg * (1.0 - sg)).astype(BF16)
            dp_ref[2, rows, :] = dv_s[...].astype(BF16)
            for h in range(PAIR):
                dp_ref[3, rows, h * HEAD:(h + 1) * HEAD] = dpg[h].astype(BF16)
            yield

        def trip(g, carry):
            _staggered([chunk_stages(j, nck - 1 - (g * ntc + j)) for j in range(ntc)])
            return carry

        lax.fori_loop(0, nck // ntc, trip, 0)

    def grp(gi):
        return pl.BlockSpec((None, rb, W), lambda h, r: (gi, nb - 1 - r, h))

    rowsp = pl.BlockSpec((rb, W), lambda h, r: (nb - 1 - r, h))
    vec = pl.BlockSpec((1, W), lambda h, r: (0, h))
    return carried_call(
        body, comm, name=name, grid=(H // PAIR, nb),
        in_specs=[grp(0), grp(1), grp(2), grp(3), rowsp, rowsp,
                  pl.BlockSpec((PAIR, nck, HEAD, HEAD), lambda h, r: (h, nb - 1 - r, 0, 0)), vec, vec],
        out_specs=[pl.BlockSpec((4, rb, W), lambda h, r: (0, nb - 1 - r, h)),
                   pl.BlockSpec((8, W), lambda h, r: (0, h))],
        out_shape=[jax.ShapeDtypeStruct((4, T, D), BF16), jax.ShapeDtypeStruct((8, D), F32)],
        scratch_shapes=[pltpu.VMEM((W, W), F32), pltpu.VMEM((ntc, W, W), F32)] + [pltpu.VMEM((ntc, CHUNK, W), F32)] * 9
        + [pltpu.VMEM((ntc, PAIR, CHUNK, (nsub - 1) * HEAD), BF16)] * 2 + [pltpu.VMEM((ntc, fam, W), F32)] * 2
        + [pltpu.VMEM((ntc, 2 * PAIR * fam, HEAD), BF16)],
        args=(proj, proj, proj, proj, o, dog, states, lb, norm_g))


def lb_fwd(logits):
    def body(l_ref, o_ref):
        l = l_ref[...]
        mx = jnp.max(l, axis=0, keepdims=True)
        e = jnp.exp(l - mx)
        sm = e / jnp.sum(e, axis=0, keepdims=True)
        o_ref[0:1, :] = jnp.zeros_like(sm[0:1, :])
        o_ref[1:2, :] = sm[1:2, :]

    return pl.pallas_call(body, name="lb_fwd", out_shape=jax.ShapeDtypeStruct(logits.shape, F32))(logits)


def lb_bwd(logits, dlb):
    def body(l_ref, d_ref, o_ref):
        l = l_ref[...]
        mx = jnp.max(l, axis=0, keepdims=True)
        e = jnp.exp(l - mx)
        sm = e / jnp.sum(e, axis=0, keepdims=True)
        inner = d_ref[1:2, :] * sm[1:2, :]
        o_ref[0:1, :] = sm[0:1, :] * (0.0 - inner)
        o_ref[1:2, :] = sm[1:2, :] * (d_ref[1:2, :] - inner)

    return pl.pallas_call(body, name="lb_bwd", out_shape=jax.ShapeDtypeStruct(logits.shape, F32))(logits, dlb)


def _shifted_copies(sh, rows):
    for b in range(1, 8):
        sh[b, 0:rows, :] = sh[0, b:b + rows, :]


def conv_fwd(u, vec, *, tm, name, comm=None):
    _, T, D = u.shape
    hb = tm // HALO
    nlc = D // HEAD

    def body(a_ref, gt_ref, ap_ref, gp_ref, w_ref, c_ref, v_ref, sh):
        i = pl.program_id(0)
        sh[0, HALO:HALO + tm, :] = a_ref[...] * _sigmoid(gt_ref[...])
        prev = ap_ref[...] * _sigmoid(gp_ref[...])
        sh[0, 0:HALO, :] = jnp.where(i > 0, prev, 0.0)
        _shifted_copies(sh, tm + HALO - 8)

        def rowblock(r, carry):
            r0 = r * CONV_RB
            for cl in range(nlc):
                ls = slice(cl * HEAD, (cl + 1) * HEAD)
                acc = jnp.zeros((CONV_RB, HEAD), F32) + w_ref[32:33, ls]
                for j in range(CONV_W):
                    o = j + 2
                    at = pl.ds(pl.multiple_of(r0 + o - o % 8, 8), CONV_RB)
                    acc = acc + w_ref[j:j + 1, ls] * sh[o % 8, at, ls]
                c_ref[pl.ds(pl.multiple_of(r0, CONV_RB), CONV_RB), ls] = acc
            return carry

        lax.fori_loop(0, tm // CONV_RB, rowblock, 0)
        xhat, _ = _ln_stats(c_ref[...])
        y = xhat * w_ref[33:34, :] + w_ref[34:35, :]
        v_ref[...] = (y * _sigmoid(y)).astype(BF16)

    cur = lambda gi: pl.BlockSpec((None, tm, D), lambda i: (gi, i, 0))
    prv = lambda gi: pl.BlockSpec((None, HALO, D), lambda i: (gi, jnp.maximum(i * hb - 1, 0), 0))
    return carried_call(
        body, comm, name=name, grid=(T // tm,),
        in_specs=[cur(0), cur(1), prv(0), prv(1), pl.BlockSpec((VEC_ROWS, D), lambda i: (0, 0))],
        out_specs=[pl.BlockSpec((tm, D), lambda i: (i, 0)), pl.BlockSpec((tm, D), lambda i: (i, 0))],
        out_shape=[jax.ShapeDtypeStruct((T, D), F32), jax.ShapeDtypeStruct((T, D), BF16)],
        scratch_shapes=[pltpu.VMEM((8, tm + HALO, D), F32)],
        args=(u, u, u, u, vec))


def conv_bwd(dv2, c, u, vec, *, tm, name, comm=None):
    _, T, D = u.shape
    hb = tm // HALO
    nt = T // tm
    nh = T // HALO
    nlc = D // HEAD
    acc_rows = {j: j for j in range(CONV_W)}
    acc_rows.update({36: CONV_W, 37: CONV_W + 1})

    def body(dv_ref, c_ref, dvn_ref, cn_ref, a_ref, gt_ref, ap_ref, gp_ref, w_ref, du_ref, s_ref, gsh, dsh, part):
        i = pl.program_id(0)

        @pl.when(i == 0)
        def _():
            s_ref[...] = jnp.zeros_like(s_ref)
            part[...] = jnp.zeros_like(part)

        gam = w_ref[33:34, :]
        bet = w_ref[34:35, :]

        def dconv(dv, cc):
            xhat, rstd = _ln_stats(cc)
            y = xhat * gam + bet
            sy = _sigmoid(y)
            dy = dv * (sy * (1.0 + y * (1.0 - sy)))
            return _ln_bwd(dy, xhat, rstd, gam), dy, xhat

        dc, dy, xhat = dconv(dv_ref[...], c_ref[...])
        dcn, _, _ = dconv(dvn_ref[...], cn_ref[...])
        dsh[0, 0:tm, :] = dc
        dsh[0, tm:tm + HALO, :] = jnp.where(i < nt - 1, dcn, 0.0)
        gsh[0, HALO:HALO + tm, :] = a_ref[...] * _sigmoid(gt_ref[...])
        gsh[0, 0:HALO, :] = jnp.where(i > 0, ap_ref[...] * _sigmoid(gp_ref[...]), 0.0)
        s_ref[32:33, :] += jnp.sum(dc, axis=0, keepdims=True)
        s_ref[33:34, :] += jnp.sum(dy * xhat, axis=0, keepdims=True)
        s_ref[34:35, :] += jnp.sum(dy, axis=0, keepdims=True)
        _shifted_copies(dsh, tm + HALO - 8)
        _shifted_copies(gsh, tm + HALO - 8)

        def fold8(x):
            acc = x[0:8, :]
            for g in range(1, CONV_RB // 8):
                acc = acc + x[8 * g:8 * g + 8, :]
            return acc

        for cl in range(nlc):
            ls = slice(cl * HEAD, (cl + 1) * HEAD)

            def rowblock(r, sums, ls=ls):
                r0 = r * CONV_RB
                rows = pl.ds(pl.multiple_of(r0, CONV_RB), CONV_RB)
                dcb = dsh[0, rows, ls]
                dglu = jnp.zeros((CONV_RB, HEAD), F32)
                new = []
                for j in range(CONV_W):
                    od = 30 - j
                    og = j + 2
                    atd = pl.ds(pl.multiple_of(r0 + od - od % 8, 8), CONV_RB)
                    atg = pl.ds(pl.multiple_of(r0 + og - og % 8, 8), CONV_RB)
                    dglu = dglu + w_ref[j:j + 1, ls] * dsh[od % 8, atd, ls]
                    new.append(sums[j] + fold8(dcb * gsh[og % 8, atg, ls]))
                a = a_ref[rows, ls]
                sgt = _sigmoid(gt_ref[rows, ls])
                da = (dglu * sgt).astype(BF16)
                dg = (dglu * a * sgt * (1.0 - sgt)).astype(BF16)
                du_ref[0, rows, ls] = da
                du_ref[1, rows, ls] = dg
                new.append(sums[CONV_W] + fold8(da.astype(F32)))
                new.append(sums[CONV_W + 1] + fold8(dg.astype(F32)))
                return tuple(new)

            zero = jnp.zeros((8, HEAD), F32)
            sums = lax.fori_loop(0, tm // CONV_RB, rowblock, (zero,) * (CONV_W + 2))
            for k in range(CONV_W + 2):
                part[8 * k:8 * k + 8, ls] += sums[k]

        @pl.when(i == nt - 1)
        def _():
            for row, k in acc_rows.items():
                s_ref[row:row + 1, :] = jnp.sum(part[8 * k:8 * k + 8, :], axis=0, keepdims=True)

    row = lambda i: (i, 0)
    nxt = lambda i: (jnp.minimum((i + 1) * hb, nh - 1), 0)
    cur = lambda gi: pl.BlockSpec((None, tm, D), lambda i: (gi, i, 0))
    prv = lambda gi: pl.BlockSpec((None, HALO, D), lambda i: (gi, jnp.maximum(i * hb - 1, 0), 0))
    fix = lambda i: (0, 0)
    return carried_call(
        body, comm, name=name, grid=(nt,),
        in_specs=[pl.BlockSpec((tm, D), row), pl.BlockSpec((tm, D), row),
                  pl.BlockSpec((HALO, D), nxt), pl.BlockSpec((HALO, D), nxt),
                  cur(0), cur(1), prv(0), prv(1), pl.BlockSpec((VEC_ROWS, D), fix)],
        out_specs=[pl.BlockSpec((2, tm, D), lambda i: (0, i, 0)), pl.BlockSpec((VEC_ROWS, D), fix)],
        out_shape=[jax.ShapeDtypeStruct((2, T, D), BF16), jax.ShapeDtypeStruct((VEC_ROWS, D), F32)],
        scratch_shapes=[pltpu.VMEM((8, tm + HALO, D), F32), pltpu.VMEM((8, tm + HALO, D), F32),
                        pltpu.VMEM((8 * (CONV_W + 2), D), F32)],
        args=(dv2, c, dv2, c, u, u, u, u, vec))


def loss_grad(y, target, *, tm):
    T, D = y.shape
    nt = T // tm

    def body(y_ref, t_ref, l_ref, d_ref, acc):
        i = pl.program_id(0)

        @pl.when(i == 0)
        def _():
            acc[...] = jnp.zeros_like(acc)

        e = y_ref[...] - t_ref[...]
        d_ref[...] = e * (1.0 / D)
        acc[...] += jnp.sum(e * e, axis=0, keepdims=True)

        @pl.when(i == nt - 1)
        def _():
            l_ref[...] = 0.5 * jnp.sum(acc[...], axis=1, keepdims=True) * (1.0 / D)

    row = lambda i: (i, 0)
    return _call(
        body, name="loss_grad", grid=(nt,),
        in_specs=[pl.BlockSpec((tm, D), row), pl.BlockSpec((tm, D), row)],
        out_specs=[pl.BlockSpec((1, 1), lambda i: (0, 0)), pl.BlockSpec((tm, D), row)],
        out_shape=[jax.ShapeDtypeStruct((1, 1), F32), jax.ShapeDtypeStruct((T, D), F32)],
        scratch_shapes=[pltpu.VMEM((1, D), F32)], args=(y, target))


def _rows_block(R, C, budget=1 << 20):
    tr = R
    while tr * C * 4 > budget and tr % 32 == 0:
        tr //= 2
    return tr


def adamw(w, g, m, v, *, name):
    R, C = w.shape
    tr = _rows_block(R, C)

    def body(w_ref, g_ref, m_ref, v_ref, d_ref, mo_ref, vo_ref):
        g_ = g_ref[...]
        mn = ADAM_B1 * m_ref[...] + (1.0 - ADAM_B1) * g_
        vn = ADAM_B2 * v_ref[...] + (1.0 - ADAM_B2) * jnp.square(g_)
        m_hat = mn / (1.0 - ADAM_B1 ** ADAM_STEP)
        v_hat = vn / (1.0 - ADAM_B2 ** ADAM_STEP)
        d_ref[...] = -ADAM_LR * (m_hat / (jnp.sqrt(v_hat) + ADAM_EPS) + ADAM_WD * w_ref[...])
        mo_ref[...] = mn
        vo_ref[...] = vn

    spec = pl.BlockSpec((tr, C), lambda i: (i, 0))
    sd = jax.ShapeDtypeStruct((R, C), F32)
    return _call(body, name=name, grid=(R // tr,), in_specs=[spec] * 4, out_specs=[spec] * 3, out_shape=[sd] * 3,
                 args=(w, g, m, v))


def sum_slots(slots, *, name):
    _, R, C = slots.shape
    tr = _rows_block(R, C, budget=1 << 19)

    def body(s_ref, o_ref):
        acc = s_ref[0].astype(F32)
        for d in range(1, 8):
            acc = acc + s_ref[d].astype(F32)
        o_ref[...] = acc

    return _call(body, name=name, grid=(R // tr,), in_specs=[pl.BlockSpec((8, tr, C), lambda i: (0, i, 0))],
                 out_specs=[pl.BlockSpec((tr, C), lambda i: (i, 0))], out_shape=[jax.ShapeDtypeStruct((R, C), F32)],
                 args=(slots,))[0]


def _adam_nd(w, g, m, v, name):
    shp = w.shape
    c = shp[-1]
    f2 = lambda a: a.reshape(-1, c)
    d, mn, vn = adamw(f2(w), f2(g), f2(m), f2(v), name=name)
    return d.reshape(shp), mn.reshape(shp), vn.reshape(shp)


def _reduced(slots, name):
    out = []
    for s in slots:
        c = s.shape[-1]
        out.append(sum_slots(s.reshape(8, -1, c), name=name).reshape(s.shape[1:]))
    return out


def kernel(x, ln_mix_g, ln_mix_b, ln_ffn_g, ln_ffn_b, ffn_w1, ffn_w2, a_w_in, a_lb_logits, a_norm_g, a_w_out, b_w_pw1, b_b_pw1, b_w_dw, b_b_dw, b_ln_g, b_ln_b, b_w_pw2, b_b_pw2, loss_target, m_ln_mix_g, m_ln_mix_b, m_ln_ffn_g, m_ln_ffn_b, m_ffn_w1, m_ffn_w2, m_a_w_in, m_a_lb_logits, m_a_norm_g, m_a_w_out, m_b_w_pw1, m_b_b_pw1, m_b_w_dw, m_b_b_dw, m_b_ln_g, m_b_ln_b, m_b_w_pw2, m_b_b_pw2, v_ln_mix_g, v_ln_mix_b, v_ln_ffn_g, v_ln_ffn_b, v_ffn_w1, v_ffn_w2, v_a_w_in, v_a_lb_logits, v_a_norm_g, v_a_w_out, v_b_w_pw1, v_b_b_pw1, v_b_w_dw, v_b_b_dw, v_b_ln_g, v_b_ln_b, v_b_w_pw2, v_b_b_pw2):
    names = ["ln_mix_g", "ln_mix_b", "ln_ffn_g", "ln_ffn_b", "ffn_w1", "ffn_w2", "a_w_in", "a_lb_logits", "a_norm_g",
             "a_w_out", "b_w_pw1", "b_b_pw1", "b_w_dw", "b_b_dw", "b_ln_g", "b_ln_b", "b_w_pw2", "b_b_pw2"]
    w = dict(zip(names, [ln_mix_g, ln_mix_b, ln_ffn_g, ln_ffn_b, ffn_w1, ffn_w2, a_w_in, a_lb_logits, a_norm_g, a_w_out,
                         b_w_pw1, b_b_pw1, b_w_dw, b_b_dw, b_ln_g, b_ln_b, b_w_pw2, b_b_pw2]))
    m = dict(zip(names, [m_ln_mix_g, m_ln_mix_b, m_ln_ffn_g, m_ln_ffn_b, m_ffn_w1, m_ffn_w2, m_a_w_in, m_a_lb_logits,
                         m_a_norm_g, m_a_w_out, m_b_w_pw1, m_b_b_pw1, m_b_w_dw, m_b_b_dw, m_b_ln_g, m_b_ln_b, m_b_w_pw2,
                         m_b_b_pw2]))
    v = dict(zip(names, [v_ln_mix_g, v_ln_mix_b, v_ln_ffn_g, v_ln_ffn_b, v_ffn_w1, v_ffn_w2, v_a_w_in, v_a_lb_logits,
                         v_a_norm_g, v_a_w_out, v_b_w_pw1, v_b_b_pw1, v_b_w_dw, v_b_b_dw, v_b_ln_g, v_b_ln_b, v_b_w_pw2,
                         v_b_b_pw2]))
    T, D = x.shape[1], x.shape[2]
    DS = D // 4
    F = 4 * ffn_w1.shape[2]
    chip = 2 * lax.axis_index("x") + lax.axis_index("y")
    tm = min(T, 512)
    tmw = min(T, 1024)
    tmc = min(T, 256)
    rb = min(T, 1024)
    tf = min(F // 4, 1024)
    xin, target = x[0], loss_target[0]

    def mix_shards(i):
        j = i // 2
        if i % 2 == 0:
            return [a_w_in[j].astype(BF16), a_w_out[j].astype(BF16)]
        vec = jnp.concatenate([b_w_dw[j], jnp.zeros((1, DS), F32), b_b_dw[j][None], b_ln_g[j][None], b_ln_b[j][None],
                               b_b_pw2[j][None], b_b_pw1[j].reshape(2, DS), jnp.zeros((2, DS), F32)], axis=0)
        return [b_w_pw1[j].astype(BF16), b_w_pw2[j].astype(BF16), vec]

    def ffn_shards(i):
        return [ffn_w1[i].astype(BF16), ffn_w2[i].astype(BF16)]

    def mix_weights(i, got):
        if i % 2 == 0:
            return {"w_in": got[0], "w_out": got[1].reshape(D, D)}
        pw1 = jnp.transpose(got[0].reshape(2, 2, D, D // 2), (0, 2, 1, 3)).reshape(2, D, D)
        vec = jnp.transpose(got[2], (1, 0, 2)).reshape(VEC_ROWS, D)
        return {"pw1": pw1, "pw2": got[1].reshape(D, D), "vec": vec,
                "b_pw1": got[2][:, 36:38, :].reshape(2, 1, D)}

    lb_all = lb_fwd(a_lb_logits)
    zeros_bias = jnp.zeros((1, D), F32)

    first = mix_shards(0)
    mixw = {"w_in": gather_chips_via_sibling(first[0])}
    saved = []
    h, hb = xin, xin.astype(BF16)
    for i in range(DEPTH):
        j = i // 2
        s = {"xb": hb, "mixw": mixw}
        gf = GatherChips(ffn_shards(i) + (first[1:] if i == 0 else []))
        if i % 2 == 0:
            s["proj"] = mm_groups(hb, mixw["w_in"], jnp.zeros((4, 1, D), F32), tm=tmw, name="a_in_proj")
            (s["o"], s["og"], s["st"]), got = hgrn_fwd(s["proj"], lb_all[j:j + 1], a_norm_g[j:j + 1], rb=rb,
                                                       name="hgrn_fwd", comm=gf)
            if i == 0:
                mixw["w_out"] = got[2].reshape(D, D)
            s["r1"], x1, s["x1b"] = mm_res_ln(s["og"], mixw["w_out"], zeros_bias, h, ln_mix_g[i:i + 1],
                                              ln_mix_b[i:i + 1], tm=tmw, name="a_out_ln")
        else:
            s["u"] = mm_groups(hb, mixw["pw1"], mixw["b_pw1"], tm=tmw, name="b_pw1")
            (s["c"], s["v2"]), got = conv_fwd(s["u"], mixw["vec"], tm=min(T, 2 * tmc), name="conv_fwd", comm=gf)
            s["r1"], x1, s["x1b"] = mm_res_ln(s["v2"], mixw["pw2"], mixw["vec"][35:36], h, ln_mix_g[i:i + 1],
                                              ln_mix_b[i:i + 1], tm=tmw, name="b_pw2_ln")
        s["w1"], s["w2"] = got[0], got[1].reshape(F, D)
        gm = GatherChips(mix_shards(i + 1)) if i + 1 < DEPTH else None
        (s["z"], s["r2"], h, hb), got = ffn_fwd(x1, s["w1"], s["w2"], ln_ffn_g[i:i + 1], ln_ffn_b[i:i + 1],
                                                tm=tm, tf=tf, name="ffn_fwd", comm=gm)
        if gm is not None:
            mixw = mix_weights(i + 1, got)
        saved.append(s)

    loss_part, dh = loss_grad(h, target, tm=tmw)
    loss = lax.psum(loss_part[0, 0], ("x", "y", "c"))

    gr = {k: [None] * DEPTH for k in ("ln_mix_g", "ln_mix_b", "ln_ffn_g", "ln_ffn_b", "ffn_w1", "ffn_w2")}
    for k in ("a_w_in", "a_w_out", "a_norm_g", "a_dlb", "b_w_pw1", "b_w_pw2", "b_vec", "b_b_pw2"):
        gr[k] = [None] * 2
    w_in_name = ("a_w_in", "b_w_pw1")
    w_out_name = ("a_w_out", "b_w_pw2")
    pending = None

    for i in reversed(range(DEPTH)):
        j = i // 2
        s = saved[i]
        mixw = s["mixw"]
        sm = ScatterPieces([pending[1]]) if pending is not None else None
        (dz, dx1, drb2, sums2), slots = ffn_bwd_dx(dh, s["r2"], ln_ffn_g[i:i + 1], s["z"], s["w1"], s["w2"],
                                                   tm=tm, tf=tf, name="ffn_bwd_dx", comm=sm)
        if pending is not None:
            gr[w_in_name[pending[0] % 2]][pending[0] // 2] = _reduced(slots, "sum_mix_grads")[0]
        gr["ln_ffn_g"][i], gr["ln_ffn_b"][i] = sums2[0], sums2[1]
        dw1 = mm_tn(s["x1b"], dz[None], tm=tmw, tk=D, tn=F // 4, name="ffn_dw1")[0]
        dw2 = mm_tn(s["z"], drb2[None], tm=tmw, tk=F // 4, tn=D, relu2=True, name="ffn_dw2")[0, 0]
        wmix = mixw["w_out"] if i % 2 == 0 else mixw["pw2"]
        dr1, drb1, dmo, sums1 = ln_bwd_mm(dx1, s["r1"], ln_mix_g[i:i + 1], wmix, tm=tmw, name="mix_ln_bwd")
        gr["ln_mix_g"][i], gr["ln_mix_b"][i] = sums1[0], sums1[1]
        if i % 2 == 0:
            dwo = mm_tn(s["og"], drb1[None], tm=tmw, tk=D, tn=D, name="a_dw_out")[0, 0].reshape(4, DS, D)
            sf = ScatterPieces([dw1, dw2.reshape(4, F // 4, D), dwo])
            (dproj, hs), slots = hgrn_bwd(s["proj"], s["o"], dmo, s["st"], lb_all[j:j + 1], a_norm_g[j:j + 1], rb=rb,
                                          name="hgrn_bwd", comm=sf)
            gr["a_norm_g"][j], gr["a_dlb"][j] = hs[0], hs[1]
            dwi = mm_tn(s["xb"], dproj, tm=tmw, tk=D, tn=D, name="a_dw_in")[:, 0]
            dy_in, w_in_t, dx_name = dproj, mixw["w_in"], "a_dx"
        else:
            dwo = mm_tn(s["v2"], drb1[None], tm=tmw, tk=D, tn=D, name="b_dw_pw2")[0, 0].reshape(4, DS, D)
            sf = ScatterPieces([dw1, dw2.reshape(4, F // 4, D), dwo])
            (du, cs), slots = conv_bwd(dmo, s["c"], s["u"], mixw["vec"], tm=tmc, name="conv_bwd", comm=sf)
            gr["b_vec"][j], gr["b_b_pw2"][j] = cs, sums1[2]
            dwi = mm_tn(s["xb"], du, tm=tmw, tk=D, tn=D // 2, name="b_dw_pw1").reshape(4, D, D // 2)
            dy_in, w_in_t, dx_name = du, mixw["pw1"], "b_dx"
        gr["ffn_w1"][i], gr["ffn_w2"][i], gr[w_out_name[i % 2]][j] = _reduced(slots, "sum_ffn_grads")
        dh, slots = mm_nt_acc(dy_in, w_in_t, dr1, tm=tmw, name=dx_name, comm=ScatterPiecesViaSibling([dwi]) if i == 0 else None)
        if i == 0:
            gr[w_in_name[0]][0] = _reduced(slots, "sum_mix_grads")[0]
        pending = (i, dwi)
    grad_x = dh[None]

    small = {k: jnp.stack(gr[k]) for k in ("ln_mix_g", "ln_mix_b", "ln_ffn_g", "ln_ffn_b", "a_norm_g", "b_vec", "b_b_pw2")}
    small["a_lb_logits"] = lb_bwd(a_lb_logits, jnp.stack(gr["a_dlb"]))
    small_names = ["ln_mix_g", "ln_mix_b", "ln_ffn_g", "ln_ffn_b", "a_lb_logits", "a_norm_g", "b_b_pw2", "b_vec"]
    rows = [small[k].reshape(-1, D) for k in small_names]
    counts = [r.shape[0] for r in rows]
    rows = [jnp.pad(r, ((0, (-r.shape[0]) % 8), (0, 0))) for r in rows]
    summed = all_reduce_small(jnp.concatenate(rows, axis=0))
    sm = {}
    off = 0
    for k, n, r in zip(small_names, counts, rows):
        sm[k] = summed[off:off + n]
        off += r.shape[0]
    bvec = sm["b_vec"].reshape(2, VEC_ROWS, D)

    def shard_cols(a):
        return lax.dynamic_slice_in_dim(a, chip * DS, DS, axis=a.ndim - 1)

    grads = {k: jnp.stack(gr[k]) for k in ("ffn_w1", "ffn_w2", "a_w_in", "a_w_out", "b_w_pw1", "b_w_pw2")}
    for k in ("ln_mix_g", "ln_mix_b", "ln_ffn_g", "ln_ffn_b", "a_lb_logits", "a_norm_g"):
        grads[k] = sm[k]
    grads["b_b_pw1"] = lax.dynamic_slice_in_dim(bvec[:, 36:38, :].reshape(2, 2 * D), chip * (D // 2), D // 2, axis=1)
    grads["b_w_dw"] = shard_cols(bvec[:, 0:CONV_W, :])
    grads["b_b_dw"] = shard_cols(bvec[:, 32, :])
    grads["b_ln_g"] = shard_cols(bvec[:, 33, :])
    grads["b_ln_b"] = shard_cols(bvec[:, 34, :])
    grads["b_b_pw2"] = shard_cols(sm["b_b_pw2"])

    delta, new_m, new_v = {}, {}, {}
    for k in names:
        delta[k], new_m[k], new_v[k] = _adam_nd(w[k], grads[k], m[k], v[k], "adamw_" + k)
    return (loss, grad_x, *[grads[k] for k in names], *[delta[k] for k in names],
            *[new_m[k] for k in names], *[new_v[k] for k in names])
```
